```python
import jax, jax.numpy as jnp
from jax import lax
import numpy as np

D_MODEL = 1024
BATCH = 4
SEQ = 4096
DEPTH = 1
DEC_BATCH = 128
DEC_SEQ = 4
PAST_LEN = 16384
PAGE_SIZE = 128

GLA_HEADS = 4
GLA_DK = D_MODEL // 2 // GLA_HEADS
GLA_DV = D_MODEL // GLA_HEADS
GATE_RANK = 16
GATE_TAU = 16.0
GLA_CHUNK = 64
SWA_HEADS = 16
SWA_KV_HEADS = 4
HEAD_DIM = D_MODEL // SWA_HEADS
SWA_GROUP = SWA_HEADS // SWA_KV_HEADS
WINDOW = 128
ROPE_THETA = 10000.0
N_EXPERTS = 32
TOP_K = 4
D_FF = D_MODEL
SWIGLU_ALPHA = 1.702
SWIGLU_LIMIT = 7.0
NORM_EPS = 1e-6

SPLITS = (GLA_HEADS * GLA_DK, GLA_HEADS * GLA_DK, GLA_HEADS * GLA_DV, GLA_HEADS * GLA_DV, GATE_RANK,
          SWA_HEADS * HEAD_DIM, SWA_KV_HEADS * HEAD_DIM, SWA_KV_HEADS * HEAD_DIM, D_MODEL, D_MODEL)
D_IN = sum(SPLITS)

kernel_name = "hybrid_gla_swa_sink_moe_adaln_step"


def rms_norm(x, g):
    xf = x.astype(jnp.float32)
    y = xf * lax.rsqrt(jnp.mean(xf * xf, axis=-1, keepdims=True) + NORM_EPS)
    return (y * g.astype(jnp.float32)).astype(x.dtype)


def rope(x, pos):
    half = HEAD_DIM // 2
    inv = ROPE_THETA ** (-jnp.arange(half, dtype=jnp.float32) / half)
    ang = pos.astype(jnp.float32)[:, None, None] * inv
    cos, sin = jnp.cos(ang), jnp.sin(ang)
    xf = x.astype(jnp.float32)
    x1, x2 = xf[..., :half], xf[..., half:]
    return jnp.concatenate([x1 * cos - x2 * sin, x2 * cos + x1 * sin], axis=-1).astype(x.dtype)


def ada_split(c, w_ada, b_ada):
    m = jax.nn.silu(c) @ w_ada + b_ada
    return jnp.split(m[:, None, :], 6, axis=-1)


def modulate(x, g, shift, scale):
    return rms_norm(x, g) * (1 + scale) + shift


def gla_chunk(S, q, k, v, lg):
    L = q.shape[2]
    b = jnp.cumsum(lg, axis=2)
    causal = jnp.tril(jnp.ones((L, L), dtype=bool))
    diff = b[:, :, :, None, :] - b[:, :, None, :, :]
    decay = jnp.exp(jnp.where(causal[:, :, None], diff, -jnp.inf))
    attn = jnp.einsum('bhtd,bhtsd,bhsd->bhts', q, decay, k)
    o = jnp.einsum('bhtd,bhde->bhte', q * jnp.exp(b), S) + jnp.einsum('bhts,bhse->bhte', attn, v)
    b_last = b[:, :, -1:, :]
    S_new = jnp.exp(b_last[:, :, 0, :])[..., None] * S + jnp.einsum('bhsd,bhse->bhde', k * jnp.exp(b_last - b), v)
    return S_new, o


def gla_mixer(q, k, v, r, glr, S0, chunk, w_gk2, b_gk, gla_norm_g):
    B, T, _ = q.shape
    f32 = jnp.float32

    def heads(t, d):
        return t.reshape(B, T, GLA_HEADS, d).transpose(0, 2, 1, 3).astype(f32)

    lg = jax.nn.log_sigmoid((glr @ w_gk2 + b_gk).astype(f32)) / GATE_TAU
    qh = heads(q, GLA_DK) * (GLA_DK ** -0.5)
    kh, vh, gh = heads(k, GLA_DK), heads(v, GLA_DV), heads(lg, GLA_DK)
    nc = T // chunk

    def to_chunks(t):
        return t.reshape(B, GLA_HEADS, nc, chunk, t.shape[-1]).transpose(2, 0, 1, 3, 4)

    S, o = lax.scan(lambda s, xs: gla_chunk(s, *xs), S0.astype(f32),
                    (to_chunks(qh), to_chunks(kh), to_chunks(vh), to_chunks(gh)))
    o = o.transpose(1, 0, 3, 2, 4).reshape(B, T, GLA_HEADS, GLA_DV)
    o = rms_norm(o, gla_norm_g) * jax.nn.silu(r.reshape(B, T, GLA_HEADS, GLA_DV).astype(f32))
    return o.reshape(B, T, GLA_HEADS * GLA_DV).astype(q.dtype), S


def sink_softmax(s, sink, mask):
    s = jnp.where(mask, s.astype(jnp.float32) * (HEAD_DIM ** -0.5), -jnp.inf)
    m = jnp.maximum(jnp.max(s, axis=-1, keepdims=True), sink)
    p = jnp.exp(s - m)
    return p / (jnp.sum(p, axis=-1, keepdims=True) + jnp.exp(sink - m))


def swa_prompt(q, k, v, sinks):
    B, T = q.shape[:2]
    nb = T // WINDOW
    qb = q.reshape(B, nb, WINDOW, SWA_KV_HEADS, SWA_GROUP, HEAD_DIM)
    kb = k.reshape(B, nb, WINDOW, SWA_KV_HEADS, HEAD_DIM)
    vb = v.reshape(B, nb, WINDOW, SWA_KV_HEADS, HEAD_DIM)

    def with_prev(t):
        prev = jnp.pad(t[:, :-1], [(0, 0), (1, 0)] + [(0, 0)] * (t.ndim - 2))
        return jnp.concatenate([prev, t], axis=2)

    kk, vv = with_prev(kb), with_prev(vb)
    blk = jnp.arange(nb)[:, None] * WINDOW
    qpos = blk + jnp.arange(WINDOW)[None, :]
    kpos = blk - WINDOW + jnp.arange(2 * WINDOW)[None, :]
    dq = qpos[:, :, None] - kpos[:, None, :]
    mask = (kpos[:, None, :] >= 0) & (dq >= 0) & (dq < WINDOW)
    s = jnp.einsum('bnqkgd,bnskd->bnkgqs', qb, kk)
    sink = sinks.astype(jnp.float32).reshape(SWA_KV_HEADS, SWA_GROUP)[..., None, None]
    p = sink_softmax(s, sink, mask[None, :, None, None])
    o = jnp.einsum('bnkgqs,bnskd->bnqkgd', p.astype(vv.dtype), vv)
    return o.reshape(B, T, SWA_HEADS * HEAD_DIM)


def swa_sample(q, k, v, k_buf, v_buf, sinks):
    DB, L = q.shape[:2]
    kk = jnp.concatenate([k_buf.astype(k.dtype), k], axis=1)
    vv = jnp.concatenate([v_buf.astype(v.dtype), v], axis=1)
    qpos = jnp.arange(L)
    kpos = jnp.arange(WINDOW + L) - WINDOW
    dq = qpos[:, None] - kpos[None, :]
    mask = (dq >= 0) & (dq < WINDOW)
    qg = q.reshape(DB, L, SWA_KV_HEADS, SWA_GROUP, HEAD_DIM)
    s = jnp.einsum('bqkgd,bskd->bkgqs', qg, kk)
    sink = sinks.astype(jnp.float32).reshape(SWA_KV_HEADS, SWA_GROUP)[..., None, None]
    p = sink_softmax(s, sink, mask)
    o = jnp.einsum('bkgqs,bskd->bqkgd', p.astype(vv.dtype), vv)
    return o.reshape(DB, L, SWA_HEADS * HEAD_DIM), kk[:, L:], vv[:, L:]


def token_mixer(h, pos, gla_state, k_buf, v_buf, chunk, w_in, w_gk2, b_gk, gla_norm_g,
                q_norm_g, k_norm_g, sinks, w_o):
    B, T, _ = h.shape
    points = [int(p) for p in np.cumsum(SPLITS)[:-1]]
    q_g, k_g, v_g, r_g, g_lr, q_s, k_s, v_s, a_g, a_s = jnp.split(h @ w_in, points, axis=-1)
    if gla_state is None:
        gla_state = jnp.zeros((B, GLA_HEADS, GLA_DK, GLA_DV), jnp.float32)
    o_gla, S = gla_mixer(q_g, k_g, v_g, r_g, g_lr, gla_state, chunk, w_gk2, b_gk, gla_norm_g)
    qs = rope(rms_norm(q_s.reshape(B, T, SWA_HEADS, HEAD_DIM), q_norm_g), pos)
    ks = rope(rms_norm(k_s.reshape(B, T, SWA_KV_HEADS, HEAD_DIM), k_norm_g), pos)
    vs = v_s.reshape(B, T, SWA_KV_HEADS, HEAD_DIM)
    if k_buf is None:
        o_swa = swa_prompt(qs, ks, vs, sinks)
        new_k, new_v = ks[:, -WINDOW:], vs[:, -WINDOW:]
    else:
        o_swa, new_k, new_v = swa_sample(qs, ks, vs, k_buf, v_buf, sinks)
    merged = jax.nn.sigmoid(a_g) * o_gla + jax.nn.sigmoid(a_s) * o_swa
    return merged @ w_o, S.astype(h.dtype), new_k, new_v


def moe(h, w_router, b_router, w_gu, b_gu, w_down, b_down):
    logits = (h @ w_router + b_router).astype(jnp.float32)
    top_v, top_i = lax.top_k(logits, TOP_K)
    gates = jax.nn.softmax(top_v, axis=-1)
    combine = jnp.sum(jax.nn.one_hot(top_i, N_EXPERTS, dtype=jnp.float32) * gates[..., None], axis=1)
    out = jnp.zeros(h.shape, jnp.float32)
    for e in range(N_EXPERTS):
        gu = h @ w_gu[e] + b_gu[e]
        gate = jnp.minimum(gu[:, :D_FF], SWIGLU_LIMIT)
        up = jnp.clip(gu[:, D_FF:], -SWIGLU_LIMIT, SWIGLU_LIMIT)
        act = (up + 1) * gate * jax.nn.sigmoid(SWIGLU_ALPHA * gate)
        out = out + combine[:, e:e + 1] * (act @ w_down[e] + b_down[e]).astype(jnp.float32)
    return out.astype(h.dtype)


def setup_inputs(seed: int = 0) -> dict:
    key = jax.random.key(seed)
    ks = jax.random.split(key, 26)
    n = jax.random.normal
    f32 = jnp.float32
    D = D_MODEL
    return {
        "x_prompt": n(ks[0], (BATCH, SEQ, D), f32),
        "x_sample": n(ks[1], (DEC_BATCH, DEC_SEQ, D), f32),
        "c_prompt": n(ks[2], (BATCH, D), f32),
        "c_sample": n(ks[3], (DEC_BATCH, D), f32),
        "state_gla": n(ks[4], (DEPTH, DEC_BATCH, GLA_HEADS, GLA_DK, GLA_DV), f32),
        "cache_swa_k": n(ks[5], (DEPTH, DEC_BATCH, WINDOW, SWA_KV_HEADS, HEAD_DIM), f32),
        "cache_swa_v": n(ks[6], (DEPTH, DEC_BATCH, WINDOW, SWA_KV_HEADS, HEAD_DIM), f32),
        "w_ada": n(ks[7], (DEPTH, D, 6 * D), f32) * D ** -0.5,
        "b_ada": n(ks[8], (DEPTH, 6 * D), f32) * 0.01,
        "norm1_g": 1.0 + 0.1 * n(ks[9], (DEPTH, D), f32),
        "norm2_g": 1.0 + 0.1 * n(ks[10], (DEPTH, D), f32),
        "w_in": n(ks[11], (DEPTH, D, D_IN), f32) * D ** -0.5,
        "w_gk2": n(ks[12], (DEPTH, GATE_RANK, GLA_HEADS * GLA_DK), f32) * GATE_RANK ** -0.5,
        "b_gk": 0.1 * n(ks[13], (DEPTH, GLA_HEADS * GLA_DK), f32),
        "gla_norm_g": 1.0 + 0.1 * n(ks[14], (DEPTH, GLA_DV), f32),
        "q_norm_g": 1.0 + 0.1 * n(ks[15], (DEPTH, HEAD_DIM), f32),
        "k_norm_g": 1.0 + 0.1 * n(ks[16], (DEPTH, HEAD_DIM), f32),
        "attn_sinks": n(ks[17], (DEPTH, SWA_HEADS), f32),
        "w_o": n(ks[18], (DEPTH, D, D), f32) * D ** -0.5,
        "w_router": n(ks[19], (DEPTH, D, N_EXPERTS), f32) * D ** -0.5,
        "b_router": 0.01 * n(ks[20], (DEPTH, N_EXPERTS), f32),
        "w_gate_up": n(ks[21], (DEPTH, N_EXPERTS, D, 2 * D_FF), f32) * D ** -0.5,
        "b_gate_up": 0.01 * n(ks[22], (DEPTH, N_EXPERTS, 2 * D_FF), f32),
        "w_down": n(ks[23], (DEPTH, N_EXPERTS, D_FF, D), f32) * D_FF ** -0.5,
        "b_down": 0.01 * n(ks[24], (DEPTH, N_EXPERTS, D), f32),
    }


def reference(x_prompt, x_sample, c_prompt, c_sample, state_gla, cache_swa_k, cache_swa_v,
              w_ada, b_ada, norm1_g, norm2_g, w_in, w_gk2, b_gk, gla_norm_g, q_norm_g, k_norm_g,
              attn_sinks, w_o, w_router, b_router, w_gate_up, b_gate_up, w_down, b_down):
    xp, xs = x_prompt, x_sample
    pos_p = jnp.arange(xp.shape[1])
    pos_s = PAST_LEN + jnp.arange(xs.shape[1])
    S_p, K_p, V_p, S_s, K_s, V_s = [], [], [], [], [], []
    for l in range(DEPTH):
        ap = ada_split(c_prompt, w_ada[l], b_ada[l])
        a_s = ada_split(c_sample, w_ada[l], b_ada[l])
        mix_w = (w_in[l], w_gk2[l], b_gk[l], gla_norm_g[l], q_norm_g[l], k_norm_g[l], attn_sinks[l], w_o[l])
        yp, sp, kp, vp = token_mixer(modulate(xp, norm1_g[l], ap[0], ap[1]), pos_p, None, None, None,
                                     GLA_CHUNK, *mix_w)
        ys, ss, ksm, vsm = token_mixer(modulate(xs, norm1_g[l], a_s[0], a_s[1]), pos_s, state_gla[l],
                                       cache_swa_k[l], cache_swa_v[l], xs.shape[1], *mix_w)
        xp = xp + ap[2] * yp
        xs = xs + a_s[2] * ys
        h2p = modulate(xp, norm2_g[l], ap[3], ap[4]).reshape(-1, D_MODEL)
        h2s = modulate(xs, norm2_g[l], a_s[3], a_s[4]).reshape(-1, D_MODEL)
        m = moe(jnp.concatenate([h2p, h2s], axis=0), w_router[l], b_router[l], w_gate_up[l], b_gate_up[l],
                w_down[l], b_down[l])
        n_p = h2p.shape[0]
        xp = xp + ap[5] * m[:n_p].reshape(xp.shape)
        xs = xs + a_s[5] * m[n_p:].reshape(xs.shape)
        S_p.append(sp); K_p.append(kp); V_p.append(vp)
        S_s.append(ss); K_s.append(ksm); V_s.append(vsm)
    return (xp, xs, jnp.stack(S_p), jnp.stack(K_p), jnp.stack(V_p), jnp.stack(S_s), jnp.stack(K_s), jnp.stack(V_s))
```

```python
import functools

import numpy as np
import jax
import jax.numpy as jnp
from jax import lax
from jax.experimental import pallas as pl
from jax.experimental.pallas import tpu as pltpu

F32 = jnp.float32
BF16 = jnp.bfloat16
HIGHEST = lax.Precision.HIGHEST

D_MODEL = 1024
PAST_LEN = 16384
GLA_HEADS = 4
GLA_DK = 128
GLA_DV = 256
GATE_RANK = 16
GATE_TAU = 16.0
SWA_HEADS = 16
SWA_KV_HEADS = 4
HEAD_DIM = 64
SWA_GROUP = SWA_HEADS // SWA_KV_HEADS
WINDOW = 128
ROPE_THETA = 10000.0
N_EXPERTS = 32
TOP_K = 4
D_FF = 1024
SWIGLU_ALPHA = 1.702
SWIGLU_LIMIT = 7.0
NORM_EPS = 1e-6

LANES = 128
SAMPLE_ROWS = 8
TOKEN_TILE = 512
GLA_CHUNK = 64
GLA_BLOCK = 256
VMEM_LIMIT = 56 * 1024 * 1024

COL_VG, COL_RG, COL_QS, COL_AG, COL_AS = 0, 1024, 2048, 3072, 4096
COL_QG, COL_KG, COL_KS, COL_VS = 5120, 5632, 6144, 6400
D_MAIN = 6656
PROJ_CHUNK = 512


def _dot(a, b, precision=None):
    return jnp.dot(a, b, preferred_element_type=F32, precision=precision)


def _dot_nt(a, b, precision=None):
    return lax.dot_general(a, b, (((1,), (1,)), ((), ())), preferred_element_type=F32, precision=precision)


def _dot_tn(a, b, precision=None):
    return lax.dot_general(a, b, (((0,), (0,)), ((), ())), preferred_element_type=F32, precision=precision)


def _params(*sem):
    return pltpu.CompilerParams(dimension_semantics=sem, vmem_limit_bytes=VMEM_LIMIT)


def _rms(x, g):
    return x * lax.rsqrt(jnp.mean(x * x, axis=-1, keepdims=True) + NORM_EPS) * g


def _log_sigmoid(x):
    return jnp.minimum(x, 0.0) - jnp.log(1.0 + jnp.exp(-jnp.abs(x)))


def _ada_kernel(c_ref, w_ref, b_ref, o_ref):
    c = c_ref[...]
    s = c * jax.nn.sigmoid(c)
    o_ref[...] = _dot(s, w_ref[...], HIGHEST) + b_ref[...]


def _ada(c_all, w_ada, b_ada):
    rows = c_all.shape[0]
    tn = 768
    return pl.pallas_call(
        _ada_kernel,
        grid=(6 * D_MODEL // tn,),
        in_specs=[pl.BlockSpec((rows, D_MODEL), lambda j: (0, 0)),
                  pl.BlockSpec((D_MODEL, tn), lambda j: (0, j)),
                  pl.BlockSpec((1, tn), lambda j: (0, j))],
        out_specs=pl.BlockSpec((rows, tn), lambda j: (0, j)),
        out_shape=jax.ShapeDtypeStruct((rows, 6 * D_MODEL), F32),
        compiler_params=_params("parallel"),
        name="ada",
    )(c_all, w_ada, b_ada.reshape(1, -1))


def _inproj_kernel(x_ref, shift_ref, scale_ref, g_ref, w_ref, wg_ref, o_ref, og_ref):
    h = _rms(x_ref[...], g_ref[...]) * (1.0 + scale_ref[...]) + shift_ref[...]
    hb = h.astype(BF16)
    for j in range(D_MAIN // PROJ_CHUNK):
        sl = slice(j * PROJ_CHUNK, (j + 1) * PROJ_CHUNK)
        o_ref[:, sl] = _dot(hb, w_ref[:, sl]).astype(o_ref.dtype)
    og_ref[...] = _dot(hb, wg_ref[...])


def _mod_spec(per_token, tiles_per_seq):
    if per_token:
        return pl.BlockSpec((TOKEN_TILE, D_MODEL), lambda i: (i, 0))
    return pl.BlockSpec((None, 1, D_MODEL), lambda i: (i // tiles_per_seq, 0, 0))


def _inproj(x, shift, scale, g, w_main, w_glr, out_dtype, per_token, tiles_per_seq):
    n = x.shape[0]
    mod = _mod_spec(per_token, tiles_per_seq)
    const = lambda i: (0, 0)
    return pl.pallas_call(
        _inproj_kernel,
        grid=(n // TOKEN_TILE,),
        in_specs=[pl.BlockSpec((TOKEN_TILE, D_MODEL), lambda i: (i, 0)), mod, mod,
                  pl.BlockSpec((1, D_MODEL), const),
                  pl.BlockSpec((D_MODEL, D_MAIN), const, pipeline_mode=pl.Buffered(1)),
                  pl.BlockSpec((D_MODEL, LANES), const, pipeline_mode=pl.Buffered(1))],
        out_specs=[pl.BlockSpec((TOKEN_TILE, D_MAIN), lambda i: (i, 0)),
                   pl.BlockSpec((TOKEN_TILE, LANES), lambda i: (i, 0))],
        out_shape=[jax.ShapeDtypeStruct((n, D_MAIN), out_dtype),
                   jax.ShapeDtypeStruct((n, LANES), F32)],
        compiler_params=_params("parallel"),
        name="inproj",
    )(x, shift, scale, g, w_main, w_glr)


def _gla_log_gate(glr, wgk, bgk):
    return _log_sigmoid(_dot(glr, wgk, HIGHEST) + bgk) * (1.0 / GATE_TAU)


def _gla_out(o, r, g):
    r = r.astype(F32)
    return _rms(o, g) * (r * jax.nn.sigmoid(r))


def _gla_prompt_kernel(q_ref, k_ref, v_ref, r_ref, glr_ref, wgk_ref, bgk_ref, g_ref,
                       o_ref, s_ref, st_ref):
    c = pl.program_id(2)

    @pl.when(c == 0)
    def _():
        st_ref[...] = jnp.zeros_like(st_ref)

    L = GLA_CHUNK
    row = lax.broadcasted_iota(jnp.int32, (L, L), 0)
    col = lax.broadcasted_iota(jnp.int32, (L, L), 1)
    causal = col <= row
    tril = causal.astype(F32)
    lg_all = _gla_log_gate(glr_ref[...], wgk_ref[...], bgk_ref[...])
    for i in range(GLA_BLOCK // L):
        sl = slice(i * L, (i + 1) * L)
        q = q_ref[sl, :].astype(F32) * (GLA_DK ** -0.5)
        k = k_ref[sl, :].astype(F32)
        v = v_ref[sl, :]
        b = _dot(tril, lg_all[sl, :], HIGHEST)
        b_last = b[L - 1:L, :]
        qe = (q * jnp.exp(b)).astype(BF16)
        ke = (k * jnp.exp(-b)).astype(BF16)
        kd = (k * jnp.exp(b_last - b)).astype(BF16)
        attn = jnp.where(causal, _dot_nt(qe, ke), 0.0).astype(BF16)
        st = st_ref[...]
        o = _dot_nt(qe, st.astype(BF16)) + _dot(attn, v)
        st_ref[...] = st * jnp.exp(b_last) + _dot_tn(v, kd)
        o_ref[sl, :] = _gla_out(o, r_ref[sl, :], g_ref[...]).astype(o_ref.dtype)

    @pl.when(c == pl.num_programs(2) - 1)
    def _():
        s_ref[...] = st_ref[...].T


def _gla_prompt(yp, glr, wgk, bgk, gnorm, batch, seq):
    nb = seq // GLA_BLOCK
    tok = lambda b, h, c: b * nb + c
    return pl.pallas_call(
        _gla_prompt_kernel,
        grid=(batch, GLA_HEADS, nb),
        in_specs=[pl.BlockSpec((GLA_BLOCK, GLA_DK), lambda b, h, c: (tok(b, h, c), COL_QG // GLA_DK + h)),
                  pl.BlockSpec((GLA_BLOCK, GLA_DK), lambda b, h, c: (tok(b, h, c), COL_KG // GLA_DK + h)),
                  pl.BlockSpec((GLA_BLOCK, GLA_DV), lambda b, h, c: (tok(b, h, c), COL_VG // GLA_DV + h)),
                  pl.BlockSpec((GLA_BLOCK, GLA_DV), lambda b, h, c: (tok(b, h, c), COL_RG // GLA_DV + h)),
                  pl.BlockSpec((GLA_BLOCK, LANES), lambda b, h, c: (tok(b, h, c), 0)),
                  pl.BlockSpec((LANES, GLA_DK), lambda b, h, c: (0, h)),
                  pl.BlockSpec((1, GLA_DK), lambda b, h, c: (0, h)),
                  pl.BlockSpec((1, GLA_DV), lambda b, h, c: (0, 0))],
        out_specs=[pl.BlockSpec((GLA_BLOCK, GLA_DV), lambda b, h, c: (tok(b, h, c), h)),
                   pl.BlockSpec((None, None, GLA_DK, GLA_DV), lambda b, h, c: (b, h, 0, 0))],
        out_shape=[jax.ShapeDtypeStruct((batch * seq, GLA_HEADS * GLA_DV), BF16),
                   jax.ShapeDtypeStruct((batch, GLA_HEADS, GLA_DK, GLA_DV), F32)],
        scratch_shapes=[pltpu.VMEM((GLA_DV, GLA_DK), F32)],
        compiler_params=_params("parallel", "parallel", "arbitrary"),
        name="gla_prompt",
    )(yp, yp, yp, yp, glr, wgk, bgk, gnorm)


def _gla_sample_kernel(q_ref, k_ref, v_ref, r_ref, glr_ref, wgk_ref, bgk_ref, g_ref, s0_ref,
                       o_ref, s_ref, *, n_real):
    R = SAMPLE_ROWS
    row = lax.broadcasted_iota(jnp.int32, (R, R), 0)
    col = lax.broadcasted_iota(jnp.int32, (R, R), 1)
    causal = col <= row
    tril = causal.astype(F32)
    real = lax.broadcasted_iota(jnp.int32, (R, 1), 0) < n_real
    last_row = (lax.broadcasted_iota(jnp.int32, (R, GLA_DV), 0) == R - 1).astype(F32)
    glr = glr_ref[...]
    for h in range(GLA_HEADS):
        dk = slice(h * GLA_DK, (h + 1) * GLA_DK)
        dv = slice(h * GLA_DV, (h + 1) * GLA_DV)
        lg = jnp.where(real, _gla_log_gate(glr, wgk_ref[:, dk], bgk_ref[:, dk]), 0.0)
        b = _dot(tril, lg, HIGHEST)
        b_last = b[R - 1:R, :]
        q = q_ref[:, dk] * (GLA_DK ** -0.5)
        k = jnp.where(real, k_ref[:, dk], 0.0)
        v = v_ref[:, dv]
        qe = q * jnp.exp(b)
        ke = k * jnp.exp(-b)
        kd = k * jnp.exp(b_last - b)
        attn = jnp.where(causal, _dot_nt(qe, ke), 0.0)
        s0 = s0_ref[h]
        o = _dot(qe, s0) + _dot(attn, v)
        o_ref[:, dv] = _gla_out(o, r_ref[:, dv], g_ref[...])
        b_last_cols = _dot_tn(b, last_row, HIGHEST)
        s_ref[h] = s0 * jnp.exp(b_last_cols) + _dot_tn(kd, v)


def _gla_sample(ys, glr, wgk, bgk, gnorm, state, n_seq, n_real):
    R = SAMPLE_ROWS
    hk, hv = GLA_HEADS * GLA_DK, GLA_HEADS * GLA_DV
    st_spec = pl.BlockSpec((None, GLA_HEADS, GLA_DK, GLA_DV), lambda b: (b, 0, 0, 0))
    return pl.pallas_call(
        functools.partial(_gla_sample_kernel, n_real=n_real),
        grid=(n_seq,),
        in_specs=[pl.BlockSpec((R, hk), lambda b: (b, COL_QG // hk)),
                  pl.BlockSpec((R, hk), lambda b: (b, COL_KG // hk)),
                  pl.BlockSpec((R, hv), lambda b: (b, COL_VG // hv)),
                  pl.BlockSpec((R, hv), lambda b: (b, COL_RG // hv)),
                  pl.BlockSpec((R, LANES), lambda b: (b, 0)),
                  pl.BlockSpec((LANES, hk), lambda b: (0, 0)),
                  pl.BlockSpec((1, hk), lambda b: (0, 0)),
                  pl.BlockSpec((1, GLA_DV), lambda b: (0, 0)),
                  st_spec],
        out_specs=[pl.BlockSpec((R, hv), lambda b: (b, 0)), st_spec],
        out_shape=[jax.ShapeDtypeStruct((n_seq * R, hv), F32),
                   jax.ShapeDtypeStruct((n_seq, GLA_HEADS, GLA_DK, GLA_DV), F32)],
        compiler_params=_params("parallel"),
        name="gla_sample",
    )(ys, ys, ys, ys, glr, wgk, bgk, gnorm, state)


def _lane_lower(shape):
    return lax.broadcasted_iota(jnp.int32, shape, len(shape) - 1) % LANES < HEAD_DIM


def _head_norm_rope(x, g, cos, sin, seg):
    sq = x * x
    hi = sq.astype(BF16)
    lo = (sq - hi.astype(F32)).astype(BF16)
    ss = _dot(hi, seg) + _dot(lo, seg)
    y = x * lax.rsqrt(ss * (1.0 / HEAD_DIM) + NORM_EPS) * g
    half = HEAD_DIM // 2
    lane = lax.broadcasted_iota(jnp.int32, y.shape, 1)
    rot = jnp.where(lane % HEAD_DIM < half, pltpu.roll(y, LANES - half, 1), pltpu.roll(y, half, 1))
    return y * cos + rot * sin


def _both_halves(blk, half):
    sw = pltpu.roll(blk, HEAD_DIM, 1)
    lower = _lane_lower(blk.shape)
    return jnp.where(lower, blk, sw) if half == 0 else jnp.where(lower, sw, blk)


def _stack_heads(q_blocks):
    parts = []
    for qb in q_blocks:
        lower = _lane_lower(qb.shape)
        zero = jnp.zeros_like(qb)
        parts += [jnp.where(lower, qb, zero), jnp.where(lower, zero, qb)]
    return jnp.concatenate(parts, axis=0)


def _swa_prompt_kernel(sink_ref, q_ref, k_ref, v_ref, cos_ref, sin_ref, gq_ref, gk_ref, seg_ref,
                       o_ref, ko_ref, vo_ref, kprev_ref, vprev_ref):
    n = pl.program_id(1)
    W = WINDOW

    @pl.when(n == 0)
    def _():
        kprev_ref[...] = jnp.zeros_like(kprev_ref)
        vprev_ref[...] = jnp.zeros_like(vprev_ref)

    cos, sin, seg = cos_ref[...], sin_ref[...], seg_ref[...]
    k_cur = jnp.concatenate(
        [_head_norm_rope(k_ref[:, j * LANES:(j + 1) * LANES].astype(F32), gk_ref[...], cos, sin, seg)
         for j in range(SWA_KV_HEADS * HEAD_DIM // LANES)], axis=1)
    v_cur = v_ref[...].astype(F32)
    ko_ref[...] = k_cur
    vo_ref[...] = v_cur
    k2 = jnp.concatenate([kprev_ref[...], k_cur], axis=0)
    v2 = jnp.concatenate([vprev_ref[...], v_cur], axis=0)
    kprev_ref[...] = k_cur
    vprev_ref[...] = v_cur

    qi = lax.broadcasted_iota(jnp.int32, (W, 2 * W), 0)
    ki = lax.broadcasted_iota(jnp.int32, (W, 2 * W), 1)
    dq = qi + W - ki
    mask = (dq >= 0) & (dq < W) & ((ki >= W) | (n > 0))
    for kh in range(SWA_KV_HEADS):
        blk = slice((kh // 2) * LANES, (kh // 2 + 1) * LANES)
        kb = _both_halves(k2[:, blk], kh % 2).astype(BF16)
        vb = _both_halves(v2[:, blk], kh % 2).astype(BF16)
        qblocks = []
        for j in range(2):
            c0 = (2 * kh + j) * LANES
            qn = _head_norm_rope(q_ref[:, c0:c0 + LANES].astype(F32), gq_ref[...], cos, sin, seg)
            qblocks.append(qn * (HEAD_DIM ** -0.5))
        qs = _stack_heads(qblocks).astype(BF16)
        s = _dot_nt(qs, kb)
        outs = []
        for g in range(SWA_GROUP):
            sg = jnp.where(mask, s[g * W:(g + 1) * W, :], -jnp.inf)
            sink = sink_ref[kh * SWA_GROUP + g]
            m = jnp.maximum(jnp.max(sg, axis=-1, keepdims=True), sink)
            p = jnp.exp(sg - m)
            denom = jnp.sum(p, axis=-1, keepdims=True) + jnp.exp(sink - m)
            outs.append(_dot(p.astype(BF16), vb) / denom)
        lower = _lane_lower((W, LANES))
        for j in range(2):
            c0 = (2 * kh + j) * LANES
            o_ref[:, c0:c0 + LANES] = jnp.where(lower, outs[2 * j], outs[2 * j + 1]).astype(o_ref.dtype)


def _swa_prompt(yp, sinks, cos, sin, gq, gk, seg, batch, seq):
    nb = seq // WINDOW
    kvw = SWA_KV_HEADS * HEAD_DIM
    tok = lambda b, n: b * nb + n
    const = lambda b, n: (0, 0)
    cache_spec = pl.BlockSpec((None, WINDOW, kvw), lambda b, n: (b, 0, 0))
    return pl.pallas_call(
        _swa_prompt_kernel,
        grid=(batch, nb),
        in_specs=[pl.BlockSpec(memory_space=pltpu.SMEM),
                  pl.BlockSpec((WINDOW, D_MODEL), lambda b, n: (tok(b, n), COL_QS // D_MODEL)),
                  pl.BlockSpec((WINDOW, kvw), lambda b, n: (tok(b, n), COL_KS // kvw)),
                  pl.BlockSpec((WINDOW, kvw), lambda b, n: (tok(b, n), COL_VS // kvw)),
                  pl.BlockSpec((WINDOW, LANES), lambda b, n: (n, 0)),
                  pl.BlockSpec((WINDOW, LANES), lambda b, n: (n, 0)),
                  pl.BlockSpec((1, LANES), const), pl.BlockSpec((1, LANES), const),
                  pl.BlockSpec((LANES, LANES), const)],
        out_specs=[pl.BlockSpec((WINDOW, D_MODEL), lambda b, n: (tok(b, n), 0)), cache_spec, cache_spec],
        out_shape=[jax.ShapeDtypeStruct((batch * seq, D_MODEL), BF16),
                   jax.ShapeDtypeStruct((batch, WINDOW, kvw), F32),
                   jax.ShapeDtypeStruct((batch, WINDOW, kvw), F32)],
        scratch_shapes=[pltpu.VMEM((WINDOW, kvw), F32), pltpu.VMEM((WINDOW, kvw), F32)],
        compiler_params=_params("parallel", "arbitrary"),
        name="swa_prompt",
    )(sinks, yp, yp, yp, cos, sin, gq, gk, seg)


def _shift_cache(cache, new, n_real):
    R = SAMPLE_ROWS
    rolled = pltpu.roll(cache, WINDOW - n_real, 0)
    tail_new = pltpu.roll(new, R - n_real, 0)
    row = lax.broadcasted_iota(jnp.int32, (R, cache.shape[1]), 0)
    tail = jnp.where(row < R - n_real, rolled[WINDOW - R:, :], tail_new)
    return jnp.concatenate([rolled[:WINDOW - R, :], tail], axis=0)


def _swa_sample_kernel(sink_ref, q_ref, k_ref, v_ref, kc_ref, vc_ref, cos_ref, sin_ref, gq_ref, gk_ref,
                       seg_ref, o_ref, ko_ref, vo_ref, *, n_real):
    R, W = SAMPLE_ROWS, WINDOW
    cos, sin, seg = cos_ref[...], sin_ref[...], seg_ref[...]
    k_new = jnp.concatenate(
        [_head_norm_rope(k_ref[:, j * LANES:(j + 1) * LANES], gk_ref[...], cos, sin, seg)
         for j in range(SWA_KV_HEADS * HEAD_DIM // LANES)], axis=1)
    v_new = v_ref[...]
    kc, vc = kc_ref[...], vc_ref[...]
    ko_ref[...] = _shift_cache(kc, k_new, n_real)
    vo_ref[...] = _shift_cache(vc, v_new, n_real)

    t_c = lax.broadcasted_iota(jnp.int32, (R, W), 0)
    c_c = lax.broadcasted_iota(jnp.int32, (R, W), 1)
    mask_c = c_c > t_c
    t_n = lax.broadcasted_iota(jnp.int32, (R, R), 0)
    s_n = lax.broadcasted_iota(jnp.int32, (R, R), 1)
    mask_n = s_n <= t_n
    for kh in range(SWA_KV_HEADS):
        blk = slice((kh // 2) * LANES, (kh // 2 + 1) * LANES)
        kcb, vcb = _both_halves(kc[:, blk], kh % 2), _both_halves(vc[:, blk], kh % 2)
        knb, vnb = _both_halves(k_new[:, blk], kh % 2), _both_halves(v_new[:, blk], kh % 2)
        qblocks = []
        for j in range(2):
            c0 = (2 * kh + j) * LANES
            qn = _head_norm_rope(q_ref[:, c0:c0 + LANES], gq_ref[...], cos, sin, seg)
            qblocks.append(qn * (HEAD_DIM ** -0.5))
        qs = _stack_heads(qblocks)
        sc_all = _dot_nt(qs, kcb)
        sn_all = _dot_nt(qs, knb)
        outs = []
        for g in range(SWA_GROUP):
            sc = jnp.where(mask_c, sc_all[g * R:(g + 1) * R, :], -jnp.inf)
            sn = jnp.where(mask_n, sn_all[g * R:(g + 1) * R, :], -jnp.inf)
            sink = sink_ref[kh * SWA_GROUP + g]
            m = jnp.maximum(jnp.maximum(jnp.max(sc, axis=-1, keepdims=True),
                                        jnp.max(sn, axis=-1, keepdims=True)), sink)
            pc, pn = jnp.exp(sc - m), jnp.exp(sn - m)
            denom = (jnp.sum(pc, axis=-1, keepdims=True) + jnp.sum(pn, axis=-1, keepdims=True)
                     + jnp.exp(sink - m))
            outs.append((_dot(pc, vcb) + _dot(pn, vnb)) / denom)
        lower = _lane_lower((R, LANES))
        for j in range(2):
            c0 = (2 * kh + j) * LANES
            o_ref[:, c0:c0 + LANES] = jnp.where(lower, outs[2 * j], outs[2 * j + 1])


def _swa_sample(ys, sinks, kcache, vcache, cos, sin, gq, gk, seg, n_seq, n_real):
    R = SAMPLE_ROWS
    kvw = SWA_KV_HEADS * HEAD_DIM
    const = lambda b: (0, 0)
    cache_spec = pl.BlockSpec((None, WINDOW, kvw), lambda b: (b, 0, 0))
    return pl.pallas_call(
        functools.partial(_swa_sample_kernel, n_real=n_real),
        grid=(n_seq,),
        in_specs=[pl.BlockSpec(memory_space=pltpu.SMEM),
                  pl.BlockSpec((R, D_MODEL), lambda b: (b, COL_QS // D_MODEL)),
                  pl.BlockSpec((R, kvw), lambda b: (b, COL_KS // kvw)),
                  pl.BlockSpec((R, kvw), lambda b: (b, COL_VS // kvw)),
                  cache_spec, cache_spec,
                  pl.BlockSpec((R, LANES), const), pl.BlockSpec((R, LANES), const),
                  pl.BlockSpec((1, LANES), const), pl.BlockSpec((1, LANES), const),
                  pl.BlockSpec((LANES, LANES), const)],
        out_specs=[pl.BlockSpec((R, D_MODEL), lambda b: (b, 0)), cache_spec, cache_spec],
        out_shape=[jax.ShapeDtypeStruct((n_seq * R, D_MODEL), F32),
                   jax.ShapeDtypeStruct((n_seq, WINDOW, kvw), F32),
                   jax.ShapeDtypeStruct((n_seq, WINDOW, kvw), F32)],
        compiler_params=_params("parallel"),
        name="swa_sample",
    )(sinks, ys, ys, ys, kcache, vcache, cos, sin, gq, gk, seg)


def _post_kernel(ag_ref, as_ref, og_ref, os_ref, x_ref, gate_ref, shift_ref, scale_ref, g2_ref,
                 wo_ref, wr_ref, br_ref, x1_ref, h2_ref, comb_ref):
    merged = (jax.nn.sigmoid(ag_ref[...].astype(F32)) * og_ref[...].astype(F32)
              + jax.nn.sigmoid(as_ref[...].astype(F32)) * os_ref[...].astype(F32))
    y = _dot(merged.astype(BF16), wo_ref[...])
    x1 = x_ref[...] + gate_ref[...] * y
    x1_ref[...] = x1
    h2 = _rms(x1, g2_ref[...]) * (1.0 + scale_ref[...]) + shift_ref[...]
    h2_ref[...] = h2.astype(BF16)

    logits = _dot(h2, wr_ref[...], HIGHEST) + br_ref[...]
    lane = lax.broadcasted_iota(jnp.int32, logits.shape, 1).astype(F32)
    work = logits
    vals, hots = [], []
    for _ in range(TOP_K):
        m = jnp.max(work, axis=-1, keepdims=True)
        idx = jnp.min(jnp.where(work == m, lane, float(LANES)), axis=-1, keepdims=True)
        hot = lane == idx
        vals.append(m)
        hots.append(hot)
        work = jnp.where(hot, -jnp.inf, work)
    exps = [jnp.exp(v - vals[0]) for v in vals]
    denom = exps[0] + exps[1] + exps[2] + exps[3]
    comb = jnp.zeros_like(logits)
    for e, hot in zip(exps, hots):
        comb = jnp.where(hot, e / denom, comb)
    comb_ref[...] = comb


def _post(y_all, o_gla, o_swa, x, gate, shift, scale, g2, wo, wr, br, per_token, tiles_per_seq):
    n = x.shape[0]
    mod = _mod_spec(per_token, tiles_per_seq)
    row = lambda i: (i, 0)
    const = lambda i: (0, 0)
    wide = pl.BlockSpec((TOKEN_TILE, D_MODEL), row)
    return pl.pallas_call(
        _post_kernel,
        grid=(n // TOKEN_TILE,),
        in_specs=[pl.BlockSpec((TOKEN_TILE, D_MODEL), lambda i: (i, COL_AG // D_MODEL)),
                  pl.BlockSpec((TOKEN_TILE, D_MODEL), lambda i: (i, COL_AS // D_MODEL)),
                  wide, wide, wide, mod, mod, mod,
                  pl.BlockSpec((1, D_MODEL), const),
                  pl.BlockSpec((D_MODEL, D_MODEL), const),
                  pl.BlockSpec((D_MODEL, LANES), const),
                  pl.BlockSpec((1, LANES), const)],
        out_specs=[wide, wide, pl.BlockSpec((TOKEN_TILE, LANES), row)],
        out_shape=[jax.ShapeDtypeStruct((n, D_MODEL), F32),
                   jax.ShapeDtypeStruct((n, D_MODEL), BF16),
                   jax.ShapeDtypeStruct((n, LANES), F32)],
        compiler_params=_params("parallel"),
        name="post",
    )(y_all, y_all, o_gla, o_swa, x, gate, shift, scale, g2, wo, wr, br)


def _moe_kernel(h_ref, comb_ref, wgu_ref, bgu_ref, wd_ref, bd_ref, o_ref, acc_ref):
    e = pl.program_id(1)

    @pl.when(e == 0)
    def _():
        acc_ref[...] = jnp.zeros_like(acc_ref)

    gu = _dot(h_ref[...], wgu_ref[...]) + bgu_ref[...]
    gate = jnp.minimum(gu[:, :D_FF], SWIGLU_LIMIT)
    up = jnp.clip(gu[:, D_FF:], -SWIGLU_LIMIT, SWIGLU_LIMIT)
    act = (up + 1.0) * gate * jax.nn.sigmoid(SWIGLU_ALPHA * gate)
    y = _dot(act.astype(BF16), wd_ref[...]) + bd_ref[...]
    comb = comb_ref[...]
    lane = lax.broadcasted_iota(jnp.int32, comb.shape, 1)
    c = jnp.sum(jnp.where(lane == e, comb, 0.0), axis=-1, keepdims=True)
    acc_ref[...] += c * y

    @pl.when(e == pl.num_programs(1) - 1)
    def _():
        o_ref[...] = acc_ref[...]


def _moe(h2, comb, wgu, bgu, wd, bd):
    n = h2.shape[0]
    return pl.pallas_call(
        _moe_kernel,
        grid=(n // TOKEN_TILE, N_EXPERTS),
        in_specs=[pl.BlockSpec((TOKEN_TILE, D_MODEL), lambda i, e: (i, 0)),
                  pl.BlockSpec((TOKEN_TILE, LANES), lambda i, e: (i, 0)),
                  pl.BlockSpec((None, D_MODEL, 2 * D_FF), lambda i, e: (e, 0, 0)),
                  pl.BlockSpec((None, 1, 2 * D_FF), lambda i, e: (e, 0, 0)),
                  pl.BlockSpec((None, D_FF, D_MODEL), lambda i, e: (e, 0, 0)),
                  pl.BlockSpec((None, 1, D_MODEL), lambda i, e: (e, 0, 0))],
        out_specs=pl.BlockSpec((TOKEN_TILE, D_MODEL), lambda i, e: (i, 0)),
        out_shape=jax.ShapeDtypeStruct((n, D_MODEL), F32),
        scratch_shapes=[pltpu.VMEM((TOKEN_TILE, D_MODEL), F32)],
        compiler_params=_params("parallel", "arbitrary"),
        name="moe",
    )(h2, comb, wgu, bgu, wd, bd)


def _final_kernel(x_ref, gate_ref, m_ref, o_ref):
    o_ref[...] = x_ref[...] + gate_ref[...] * m_ref[...]


def _final(x1, gate, m, row_offset_tiles, per_token, tiles_per_seq):
    n = x1.shape[0]
    wide = pl.BlockSpec((TOKEN_TILE, D_MODEL), lambda i: (i, 0))
    return pl.pallas_call(
        _final_kernel,
        grid=(n // TOKEN_TILE,),
        in_specs=[wide, _mod_spec(per_token, tiles_per_seq),
                  pl.BlockSpec((TOKEN_TILE, D_MODEL), lambda i: (i + row_offset_tiles, 0))],
        out_specs=wide,
        out_shape=jax.ShapeDtypeStruct((n, D_MODEL), F32),
        compiler_params=_params("parallel"),
        name="final",
    )(x1, gate, m)


def _rope_tables(pos):
    half = HEAD_DIM // 2
    inv = ROPE_THETA ** (-jnp.arange(half, dtype=F32) / half)
    ang = pos.astype(F32)[:, None] * inv
    cos, sin = jnp.cos(ang), jnp.sin(ang)
    reps = LANES // HEAD_DIM
    return (jnp.tile(jnp.concatenate([cos, cos], axis=1), (1, reps)),
            jnp.tile(jnp.concatenate([-sin, sin], axis=1), (1, reps)))


def kernel(x_prompt, x_sample, c_prompt, c_sample, state_gla, cache_swa_k, cache_swa_v, w_ada, b_ada,
           norm1_g, norm2_g, w_in, w_gk2, b_gk, gla_norm_g, q_norm_g, k_norm_g, attn_sinks, w_o,
           w_router, b_router, w_gate_up, b_gate_up, w_down, b_down):
    batch, seq, d = x_prompt.shape
    n_seq, n_real, _ = x_sample.shape
    depth = w_in.shape[0]
    assert depth == 1 and d == D_MODEL and n_real <= SAMPLE_ROWS
    assert seq % TOKEN_TILE == 0 and (n_seq * SAMPLE_ROWS) % TOKEN_TILE == 0
    R = SAMPLE_ROWS
    kvw = SWA_KV_HEADS * HEAD_DIM
    tiles_per_seq = seq // TOKEN_TILE

    qg, kg, vg, rg, glr_w, qs, ks, vs, ag, as_ = jnp.split(
        w_in[0], [512, 1024, 2048, 3072, 3088, 4112, 4368, 4624, 5648], axis=1)
    w_main = jnp.concatenate([vg, rg, qs, ag, as_, qg, kg, ks, vs], axis=1).astype(BF16)
    w_glr = jnp.pad(glr_w, ((0, 0), (0, LANES - GATE_RANK))).astype(BF16)
    wgk = jnp.pad(w_gk2[0], ((0, LANES - GATE_RANK), (0, 0)))
    bgk = b_gk[0].reshape(1, -1)
    gnorm = gla_norm_g[0].reshape(1, -1)
    gq = jnp.tile(q_norm_g[0], LANES // HEAD_DIM).reshape(1, LANES)
    gk = jnp.tile(k_norm_g[0], LANES // HEAD_DIM).reshape(1, LANES)
    seg = jnp.asarray(np.kron(np.eye(LANES // HEAD_DIM), np.ones((HEAD_DIM, HEAD_DIM))), BF16)
    sinks = attn_sinks[0]
    wo = w_o[0].astype(BF16)
    wr = jnp.pad(w_router[0], ((0, 0), (0, LANES - N_EXPERTS)))
    br = jnp.pad(b_router[0], (0, LANES - N_EXPERTS), constant_values=-1e30).reshape(1, LANES)
    wgu = w_gate_up[0].astype(BF16)
    bgu = b_gate_up[0].reshape(N_EXPERTS, 1, 2 * D_FF)
    wd = w_down[0].astype(BF16)
    bd = b_down[0].reshape(N_EXPERTS, 1, D_MODEL)
    g1 = norm1_g[0].reshape(1, -1)
    g2 = norm2_g[0].reshape(1, -1)

    n_c = batch + n_seq
    c_all = jnp.pad(jnp.concatenate([c_prompt, c_sample], axis=0), ((0, -n_c % 8), (0, 0)))
    m_all = _ada(c_all, w_ada[0], b_ada[0])
    mp = [m_all[:batch, i * d:(i + 1) * d].reshape(batch, 1, d) for i in range(6)]
    ms = [jnp.repeat(m_all[batch:n_c, i * d:(i + 1) * d], R, axis=0) for i in range(6)]

    xp = x_prompt.reshape(batch * seq, d)
    xs = jnp.pad(x_sample, ((0, 0), (0, R - n_real), (0, 0))).reshape(n_seq * R, d)
    cos_p, sin_p = _rope_tables(jnp.arange(seq))
    cos_s, sin_s = _rope_tables(PAST_LEN + jnp.arange(R))

    yp, glr_p = _inproj(xp, mp[0], mp[1], g1, w_main, w_glr, BF16, False, tiles_per_seq)
    ys, glr_s = _inproj(xs, ms[0], ms[1], g1, w_main, w_glr, F32, True, 1)
    og_p, st_p = _gla_prompt(yp, glr_p, wgk, bgk, gnorm, batch, seq)
    og_s, st_s = _gla_sample(ys, glr_s, wgk, bgk, gnorm, state_gla[0], n_seq, n_real)
    os_p, kc_p, vc_p = _swa_prompt(yp, sinks, cos_p, sin_p, gq, gk, seg, batch, seq)
    os_s, kc_s, vc_s = _swa_sample(ys, sinks, cache_swa_k[0].reshape(n_seq, WINDOW, kvw),
                                   cache_swa_v[0].reshape(n_seq, WINDOW, kvw),
                                   cos_s, sin_s, gq, gk, seg, n_seq, n_real)
    x1_p, h2_p, cb_p = _post(yp, og_p, os_p, xp, mp[2], mp[3], mp[4], g2, wo, wr, br, False, tiles_per_seq)
    x1_s, h2_s, cb_s = _post(ys, og_s, os_s, xs, ms[2], ms[3], ms[4], g2, wo, wr, br, True, 1)

    m = _moe(jnp.concatenate([h2_p, h2_s], axis=0), jnp.concatenate([cb_p, cb_s], axis=0), wgu, bgu, wd, bd)
    out_p = _final(x1_p, mp[5], m, 0, False, tiles_per_seq)
    out_s = _final(x1_s, ms[5], m, batch * seq // TOKEN_TILE, True, 1)

    cache_shape = (WINDOW, SWA_KV_HEADS, HEAD_DIM)
    return (out_p.reshape(batch, seq, d),
            out_s.reshape(n_seq, R, d)[:, :n_real],
            st_p[None],
            kc_p.reshape(1, batch, *cache_shape),
            vc_p.reshape(1, batch, *cache_shape),
            st_s[None],
            kc_s.reshape(1, n_seq, *cache_shape),
            vc_s.reshape(1, n_seq, *cache_shape))
```

```python
import functools

import numpy as np
import jax
import jax.numpy as jnp
from jax import lax
from jax.experimental import pallas as pl
from jax.experimental.pallas import tpu as pltpu

F32 = jnp.float32
BF16 = jnp.bfloat16
HIGHEST = lax.Precision.HIGHEST

D_MODEL = 1024
PAST_LEN = 16384
GLA_HEADS = 4
GLA_DK = 128
GLA_DV = 256
GATE_RANK = 16
GATE_TAU = 16.0
SWA_HEADS = 16
SWA_KV_HEADS = 4
HEAD_DIM = 64
SWA_GROUP = SWA_HEADS // SWA_KV_HEADS
WINDOW = 128
ROPE_THETA = 10000.0
N_EXPERTS = 32
TOP_K = 4
D_FF = 1024
SWIGLU_ALPHA = 1.702
SWIGLU_LIMIT = 7.0
NORM_EPS = 1e-6

LANES = 128
SAMPLE_ROWS = 8
TOKEN_TILE = 512
GLA_CHUNK = 64
GLA_BLOCK = 256
VMEM_LIMIT = 56 * 1024 * 1024
ROW_UNIT = 16
EXPERT_TILE = 256
LOCAL_ROWS = -(-(TOP_K * TOKEN_TILE + N_EXPERTS * (ROW_UNIT - 1)) // TOKEN_TILE) * TOKEN_TILE

COL_VG, COL_RG, COL_QS, COL_AG, COL_AS = 0, 1024, 2048, 3072, 4096
COL_QG, COL_KG, COL_KS, COL_VS = 5120, 5632, 6144, 6400
D_MAIN = 6656
PROJ_CHUNK = 512


def _dot(a, b, precision=None):
    return jnp.dot(a, b, preferred_element_type=F32, precision=precision)


def _dot_nt(a, b, precision=None):
    return lax.dot_general(a, b, (((1,), (1,)), ((), ())), preferred_element_type=F32, precision=precision)


def _dot_tn(a, b, precision=None):
    return lax.dot_general(a, b, (((0,), (0,)), ((), ())), preferred_element_type=F32, precision=precision)


def _params(*sem):
    return pltpu.CompilerParams(dimension_semantics=sem, vmem_limit_bytes=VMEM_LIMIT)


def _rms(x, g):
    return x * lax.rsqrt(jnp.mean(x * x, axis=-1, keepdims=True) + NORM_EPS) * g


def _log_sigmoid(x):
    return jnp.minimum(x, 0.0) - jnp.log(1.0 + jnp.exp(-jnp.abs(x)))


def _ada_kernel(c_ref, w_ref, b_ref, o_ref):
    c = c_ref[...]
    s = c * jax.nn.sigmoid(c)
    o_ref[...] = _dot(s, w_ref[...], HIGHEST) + b_ref[...]


def _ada(c_all, w_ada, b_ada):
    rows = c_all.shape[0]
    tn = 768
    return pl.pallas_call(
        _ada_kernel,
        grid=(6 * D_MODEL // tn,),
        in_specs=[pl.BlockSpec((rows, D_MODEL), lambda j: (0, 0)),
                  pl.BlockSpec((D_MODEL, tn), lambda j: (0, j)),
                  pl.BlockSpec((1, tn), lambda j: (0, j))],
        out_specs=pl.BlockSpec((rows, tn), lambda j: (0, j)),
        out_shape=jax.ShapeDtypeStruct((rows, 6 * D_MODEL), F32),
        compiler_params=_params("parallel"),
        name="ada",
    )(c_all, w_ada, b_ada.reshape(1, -1))


def _inproj_kernel(x_ref, shift_ref, scale_ref, g_ref, w_ref, wg_ref, o_ref, og_ref):
    h = _rms(x_ref[...], g_ref[...]) * (1.0 + scale_ref[...]) + shift_ref[...]
    hb = h.astype(BF16)
    for j in range(D_MAIN // PROJ_CHUNK):
        sl = slice(j * PROJ_CHUNK, (j + 1) * PROJ_CHUNK)
        o_ref[:, sl] = _dot(hb, w_ref[:, sl]).astype(o_ref.dtype)
    og_ref[...] = _dot(hb, wg_ref[...])


def _mod_spec(per_token, tiles_per_seq):
    if per_token:
        return pl.BlockSpec((TOKEN_TILE, D_MODEL), lambda i: (i, 0))
    return pl.BlockSpec((None, 1, D_MODEL), lambda i: (i // tiles_per_seq, 0, 0))


def _inproj(x, shift, scale, g, w_main, w_glr, out_dtype, per_token, tiles_per_seq):
    n = x.shape[0]
    mod = _mod_spec(per_token, tiles_per_seq)
    const = lambda i: (0, 0)
    return pl.pallas_call(
        _inproj_kernel,
        grid=(n // TOKEN_TILE,),
        in_specs=[pl.BlockSpec((TOKEN_TILE, D_MODEL), lambda i: (i, 0)), mod, mod,
                  pl.BlockSpec((1, D_MODEL), const),
                  pl.BlockSpec((D_MODEL, D_MAIN), const, pipeline_mode=pl.Buffered(1)),
                  pl.BlockSpec((D_MODEL, LANES), const, pipeline_mode=pl.Buffered(1))],
        out_specs=[pl.BlockSpec((TOKEN_TILE, D_MAIN), lambda i: (i, 0)),
                   pl.BlockSpec((TOKEN_TILE, LANES), lambda i: (i, 0))],
        out_shape=[jax.ShapeDtypeStruct((n, D_MAIN), out_dtype),
                   jax.ShapeDtypeStruct((n, LANES), F32)],
        compiler_params=_params("parallel"),
        name="inproj",
    )(x, shift, scale, g, w_main, w_glr)


def _gla_log_gate(glr, wgk, bgk):
    return _log_sigmoid(_dot(glr, wgk, HIGHEST) + bgk) * (1.0 / GATE_TAU)


def _gla_out(o, r, g):
    r = r.astype(F32)
    return _rms(o, g) * (r * jax.nn.sigmoid(r))


def _gla_prompt_kernel(q_ref, k_ref, v_ref, r_ref, glr_ref, wgk_ref, bgk_ref, g_ref,
                       o_ref, s_ref, st_ref):
    c = pl.program_id(2)

    @pl.when(c == 0)
    def _():
        st_ref[...] = jnp.zeros_like(st_ref)

    L = GLA_CHUNK
    row = lax.broadcasted_iota(jnp.int32, (L, L), 0)
    col = lax.broadcasted_iota(jnp.int32, (L, L), 1)
    causal = col <= row
    tril = causal.astype(F32)
    lg_all = _gla_log_gate(glr_ref[...], wgk_ref[...], bgk_ref[...])
    for i in range(GLA_BLOCK // L):
        sl = slice(i * L, (i + 1) * L)
        q = q_ref[sl, :].astype(F32) * (GLA_DK ** -0.5)
        k = k_ref[sl, :].astype(F32)
        v = v_ref[sl, :]
        b = _dot(tril, lg_all[sl, :], HIGHEST)
        b_last = b[L - 1:L, :]
        qe = (q * jnp.exp(b)).astype(BF16)
        ke = (k * jnp.exp(-b)).astype(BF16)
        kd = (k * jnp.exp(b_last - b)).astype(BF16)
        attn = jnp.where(causal, _dot_nt(qe, ke), 0.0).astype(BF16)
        st = st_ref[...]
        o = _dot_nt(qe, st.astype(BF16)) + _dot(attn, v)
        st_ref[...] = st * jnp.exp(b_last) + _dot_tn(v, kd)
        o_ref[sl, :] = _gla_out(o, r_ref[sl, :], g_ref[...]).astype(o_ref.dtype)

    @pl.when(c == pl.num_programs(2) - 1)
    def _():
        s_ref[...] = st_ref[...].T


def _gla_prompt(yp, glr, wgk, bgk, gnorm, batch, seq):
    nb = seq // GLA_BLOCK
    tok = lambda b, h, c: b * nb + c
    return pl.pallas_call(
        _gla_prompt_kernel,
        grid=(batch, GLA_HEADS, nb),
        in_specs=[pl.BlockSpec((GLA_BLOCK, GLA_DK), lambda b, h, c: (tok(b, h, c), COL_QG // GLA_DK + h)),
                  pl.BlockSpec((GLA_BLOCK, GLA_DK), lambda b, h, c: (tok(b, h, c), COL_KG // GLA_DK + h)),
                  pl.BlockSpec((GLA_BLOCK, GLA_DV), lambda b, h, c: (tok(b, h, c), COL_VG // GLA_DV + h)),
                  pl.BlockSpec((GLA_BLOCK, GLA_DV), lambda b, h, c: (tok(b, h, c), COL_RG // GLA_DV + h)),
                  pl.BlockSpec((GLA_BLOCK, LANES), lambda b, h, c: (tok(b, h, c), 0)),
                  pl.BlockSpec((LANES, GLA_DK), lambda b, h, c: (0, h)),
                  pl.BlockSpec((1, GLA_DK), lambda b, h, c: (0, h)),
                  pl.BlockSpec((1, GLA_DV), lambda b, h, c: (0, 0))],
        out_specs=[pl.BlockSpec((GLA_BLOCK, GLA_DV), lambda b, h, c: (tok(b, h, c), h)),
                   pl.BlockSpec((None, None, GLA_DK, GLA_DV), lambda b, h, c: (b, h, 0, 0))],
        out_shape=[jax.ShapeDtypeStruct((batch * seq, GLA_HEADS * GLA_DV), BF16),
                   jax.ShapeDtypeStruct((batch, GLA_HEADS, GLA_DK, GLA_DV), F32)],
        scratch_shapes=[pltpu.VMEM((GLA_DV, GLA_DK), F32)],
        compiler_params=_params("parallel", "parallel", "arbitrary"),
        name="gla_prompt",
    )(yp, yp, yp, yp, glr, wgk, bgk, gnorm)


def _gla_sample_kernel(q_ref, k_ref, v_ref, r_ref, glr_ref, wgk_ref, bgk_ref, g_ref, s0_ref,
                       o_ref, s_ref, *, n_real):
    R = SAMPLE_ROWS
    row = lax.broadcasted_iota(jnp.int32, (R, R), 0)
    col = lax.broadcasted_iota(jnp.int32, (R, R), 1)
    causal = col <= row
    tril = causal.astype(F32)
    real = lax.broadcasted_iota(jnp.int32, (R, 1), 0) < n_real
    last_row = (lax.broadcasted_iota(jnp.int32, (R, GLA_DV), 0) == R - 1).astype(F32)
    glr = glr_ref[...]
    for h in range(GLA_HEADS):
        dk = slice(h * GLA_DK, (h + 1) * GLA_DK)
        dv = slice(h * GLA_DV, (h + 1) * GLA_DV)
        lg = jnp.where(real, _gla_log_gate(glr, wgk_ref[:, dk], bgk_ref[:, dk]), 0.0)
        b = _dot(tril, lg, HIGHEST)
        b_last = b[R - 1:R, :]
        q = q_ref[:, dk] * (GLA_DK ** -0.5)
        k = jnp.where(real, k_ref[:, dk], 0.0)
        v = v_ref[:, dv]
        qe = q * jnp.exp(b)
        ke = k * jnp.exp(-b)
        kd = k * jnp.exp(b_last - b)
        attn = jnp.where(causal, _dot_nt(qe, ke), 0.0)
        s0 = s0_ref[h]
        o = _dot(qe, s0) + _dot(attn, v)
        o_ref[:, dv] = _gla_out(o, r_ref[:, dv], g_ref[...])
        b_last_cols = _dot_tn(b, last_row, HIGHEST)
        s_ref[h] = s0 * jnp.exp(b_last_cols) + _dot_tn(kd, v)


def _gla_sample(ys, glr, wgk, bgk, gnorm, state, n_seq, n_real):
    R = SAMPLE_ROWS
    hk, hv = GLA_HEADS * GLA_DK, GLA_HEADS * GLA_DV
    st_spec = pl.BlockSpec((None, GLA_HEADS, GLA_DK, GLA_DV), lambda b: (b, 0, 0, 0))
    return pl.pallas_call(
        functools.partial(_gla_sample_kernel, n_real=n_real),
        grid=(n_seq,),
        in_specs=[pl.BlockSpec((R, hk), lambda b: (b, COL_QG // hk)),
                  pl.BlockSpec((R, hk), lambda b: (b, COL_KG // hk)),
                  pl.BlockSpec((R, hv), lambda b: (b, COL_VG // hv)),
                  pl.BlockSpec((R, hv), lambda b: (b, COL_RG // hv)),
                  pl.BlockSpec((R, LANES), lambda b: (b, 0)),
                  pl.BlockSpec((LANES, hk), lambda b: (0, 0)),
                  pl.BlockSpec((1, hk), lambda b: (0, 0)),
                  pl.BlockSpec((1, GLA_DV), lambda b: (0, 0)),
                  st_spec],
        out_specs=[pl.BlockSpec((R, hv), lambda b: (b, 0)), st_spec],
        out_shape=[jax.ShapeDtypeStruct((n_seq * R, hv), F32),
                   jax.ShapeDtypeStruct((n_seq, GLA_HEADS, GLA_DK, GLA_DV), F32)],
        compiler_params=_params("parallel"),
        name="gla_sample",
    )(ys, ys, ys, ys, glr, wgk, bgk, gnorm, state)


def _lane_lower(shape):
    return lax.broadcasted_iota(jnp.int32, shape, len(shape) - 1) % LANES < HEAD_DIM


def _head_norm_rope(x, g, cos, sin, seg):
    sq = x * x
    hi = sq.astype(BF16)
    lo = (sq - hi.astype(F32)).astype(BF16)
    ss = _dot(hi, seg) + _dot(lo, seg)
    y = x * lax.rsqrt(ss * (1.0 / HEAD_DIM) + NORM_EPS) * g
    half = HEAD_DIM // 2
    lane = lax.broadcasted_iota(jnp.int32, y.shape, 1)
    rot = jnp.where(lane % HEAD_DIM < half, pltpu.roll(y, LANES - half, 1), pltpu.roll(y, half, 1))
    return y * cos + rot * sin


def _both_halves(blk, half):
    sw = pltpu.roll(blk, HEAD_DIM, 1)
    lower = _lane_lower(blk.shape)
    return jnp.where(lower, blk, sw) if half == 0 else jnp.where(lower, sw, blk)


def _stack_heads(q_blocks):
    parts = []
    for qb in q_blocks:
        lower = _lane_lower(qb.shape)
        zero = jnp.zeros_like(qb)
        parts += [jnp.where(lower, qb, zero), jnp.where(lower, zero, qb)]
    return jnp.concatenate(parts, axis=0)


def _swa_prompt_kernel(sink_ref, q_ref, k_ref, v_ref, cos_ref, sin_ref, gq_ref, gk_ref, seg_ref,
                       o_ref, ko_ref, vo_ref, kprev_ref, vprev_ref):
    n = pl.program_id(1)
    W = WINDOW

    @pl.when(n == 0)
    def _():
        kprev_ref[...] = jnp.zeros_like(kprev_ref)
        vprev_ref[...] = jnp.zeros_like(vprev_ref)

    cos, sin, seg = cos_ref[...], sin_ref[...], seg_ref[...]
    k_cur = jnp.concatenate(
        [_head_norm_rope(k_ref[:, j * LANES:(j + 1) * LANES].astype(F32), gk_ref[...], cos, sin, seg)
         for j in range(SWA_KV_HEADS * HEAD_DIM // LANES)], axis=1)
    v_cur = v_ref[...].astype(F32)
    ko_ref[...] = k_cur
    vo_ref[...] = v_cur
    k2 = jnp.concatenate([kprev_ref[...], k_cur], axis=0)
    v2 = jnp.concatenate([vprev_ref[...], v_cur], axis=0)
    kprev_ref[...] = k_cur
    vprev_ref[...] = v_cur

    qi = lax.broadcasted_iota(jnp.int32, (W, 2 * W), 0)
    ki = lax.broadcasted_iota(jnp.int32, (W, 2 * W), 1)
    dq = qi + W - ki
    mask = (dq >= 0) & (dq < W) & ((ki >= W) | (n > 0))
    for kh in range(SWA_KV_HEADS):
        blk = slice((kh // 2) * LANES, (kh // 2 + 1) * LANES)
        kb = _both_halves(k2[:, blk], kh % 2).astype(BF16)
        vb = _both_halves(v2[:, blk], kh % 2).astype(BF16)
        qblocks = []
        for j in range(2):
            c0 = (2 * kh + j) * LANES
            qn = _head_norm_rope(q_ref[:, c0:c0 + LANES].astype(F32), gq_ref[...], cos, sin, seg)
            qblocks.append(qn * (HEAD_DIM ** -0.5))
        qs = _stack_heads(qblocks).astype(BF16)
        s = _dot_nt(qs, kb)
        outs = []
        for g in range(SWA_GROUP):
            sg = jnp.where(mask, s[g * W:(g + 1) * W, :], -jnp.inf)
            sink = sink_ref[kh * SWA_GROUP + g]
            m = jnp.maximum(jnp.max(sg, axis=-1, keepdims=True), sink)
            p = jnp.exp(sg - m)
            denom = jnp.sum(p, axis=-1, keepdims=True) + jnp.exp(sink - m)
            outs.append(_dot(p.astype(BF16), vb) / denom)
        lower = _lane_lower((W, LANES))
        for j in range(2):
            c0 = (2 * kh + j) * LANES
            o_ref[:, c0:c0 + LANES] = jnp.where(lower, outs[2 * j], outs[2 * j + 1]).astype(o_ref.dtype)


def _swa_prompt(yp, sinks, cos, sin, gq, gk, seg, batch, seq):
    nb = seq // WINDOW
    kvw = SWA_KV_HEADS * HEAD_DIM
    tok = lambda b, n: b * nb + n
    const = lambda b, n: (0, 0)
    cache_spec = pl.BlockSpec((None, WINDOW, kvw), lambda b, n: (b, 0, 0))
    return pl.pallas_call(
        _swa_prompt_kernel,
        grid=(batch, nb),
        in_specs=[pl.BlockSpec(memory_space=pltpu.SMEM),
                  pl.BlockSpec((WINDOW, D_MODEL), lambda b, n: (tok(b, n), COL_QS // D_MODEL)),
                  pl.BlockSpec((WINDOW, kvw), lambda b, n: (tok(b, n), COL_KS // kvw)),
                  pl.BlockSpec((WINDOW, kvw), lambda b, n: (tok(b, n), COL_VS // kvw)),
                  pl.BlockSpec((WINDOW, LANES), lambda b, n: (n, 0)),
                  pl.BlockSpec((WINDOW, LANES), lambda b, n: (n, 0)),
                  pl.BlockSpec((1, LANES), const), pl.BlockSpec((1, LANES), const),
                  pl.BlockSpec((LANES, LANES), const)],
        out_specs=[pl.BlockSpec((WINDOW, D_MODEL), lambda b, n: (tok(b, n), 0)), cache_spec, cache_spec],
        out_shape=[jax.ShapeDtypeStruct((batch * seq, D_MODEL), BF16),
                   jax.ShapeDtypeStruct((batch, WINDOW, kvw), F32),
                   jax.ShapeDtypeStruct((batch, WINDOW, kvw), F32)],
        scratch_shapes=[pltpu.VMEM((WINDOW, kvw), F32), pltpu.VMEM((WINDOW, kvw), F32)],
        compiler_params=_params("parallel", "arbitrary"),
        name="swa_prompt",
    )(sinks, yp, yp, yp, cos, sin, gq, gk, seg)


def _shift_cache(cache, new, n_real):
    R = SAMPLE_ROWS
    rolled = pltpu.roll(cache, WINDOW - n_real, 0)
    tail_new = pltpu.roll(new, R - n_real, 0)
    row = lax.broadcasted_iota(jnp.int32, (R, cache.shape[1]), 0)
    tail = jnp.where(row < R - n_real, rolled[WINDOW - R:, :], tail_new)
    return jnp.concatenate([rolled[:WINDOW - R, :], tail], axis=0)


def _swa_sample_kernel(sink_ref, q_ref, k_ref, v_ref, kc_ref, vc_ref, cos_ref, sin_ref, gq_ref, gk_ref,
                       seg_ref, o_ref, ko_ref, vo_ref, *, n_real):
    R, W = SAMPLE_ROWS, WINDOW
    cos, sin, seg = cos_ref[...], sin_ref[...], seg_ref[...]
    k_new = jnp.concatenate(
        [_head_norm_rope(k_ref[:, j * LANES:(j + 1) * LANES], gk_ref[...], cos, sin, seg)
         for j in range(SWA_KV_HEADS * HEAD_DIM // LANES)], axis=1)
    v_new = v_ref[...]
    kc, vc = kc_ref[...], vc_ref[...]
    ko_ref[...] = _shift_cache(kc, k_new, n_real)
    vo_ref[...] = _shift_cache(vc, v_new, n_real)

    t_c = lax.broadcasted_iota(jnp.int32, (R, W), 0)
    c_c = lax.broadcasted_iota(jnp.int32, (R, W), 1)
    mask_c = c_c > t_c
    t_n = lax.broadcasted_iota(jnp.int32, (R, R), 0)
    s_n = lax.broadcasted_iota(jnp.int32, (R, R), 1)
    mask_n = s_n <= t_n
    for kh in range(SWA_KV_HEADS):
        blk = slice((kh // 2) * LANES, (kh // 2 + 1) * LANES)
        kcb, vcb = _both_halves(kc[:, blk], kh % 2), _both_halves(vc[:, blk], kh % 2)
        knb, vnb = _both_halves(k_new[:, blk], kh % 2), _both_halves(v_new[:, blk], kh % 2)
        qblocks = []
        for j in range(2):
            c0 = (2 * kh + j) * LANES
            qn = _head_norm_rope(q_ref[:, c0:c0 + LANES], gq_ref[...], cos, sin, seg)
            qblocks.append(qn * (HEAD_DIM ** -0.5))
        qs = _stack_heads(qblocks)
        sc_all = _dot_nt(qs, kcb)
        sn_all = _dot_nt(qs, knb)
        outs = []
        for g in range(SWA_GROUP):
            sc = jnp.where(mask_c, sc_all[g * R:(g + 1) * R, :], -jnp.inf)
            sn = jnp.where(mask_n, sn_all[g * R:(g + 1) * R, :], -jnp.inf)
            sink = sink_ref[kh * SWA_GROUP + g]
            m = jnp.maximum(jnp.maximum(jnp.max(sc, axis=-1, keepdims=True),
                                        jnp.max(sn, axis=-1, keepdims=True)), sink)
            pc, pn = jnp.exp(sc - m), jnp.exp(sn - m)
            denom = (jnp.sum(pc, axis=-1, keepdims=True) + jnp.sum(pn, axis=-1, keepdims=True)
                     + jnp.exp(sink - m))
            outs.append((_dot(pc, vcb) + _dot(pn, vnb)) / denom)
        lower = _lane_lower((R, LANES))
        for j in range(2):
            c0 = (2 * kh + j) * LANES
            o_ref[:, c0:c0 + LANES] = jnp.where(lower, outs[2 * j], outs[2 * j + 1])


def _swa_sample(ys, sinks, kcache, vcache, cos, sin, gq, gk, seg, n_seq, n_real):
    R = SAMPLE_ROWS
    kvw = SWA_KV_HEADS * HEAD_DIM
    const = lambda b: (0, 0)
    cache_spec = pl.BlockSpec((None, WINDOW, kvw), lambda b: (b, 0, 0))
    return pl.pallas_call(
        functools.partial(_swa_sample_kernel, n_real=n_real),
        grid=(n_seq,),
        in_specs=[pl.BlockSpec(memory_space=pltpu.SMEM),
                  pl.BlockSpec((R, D_MODEL), lambda b: (b, COL_QS // D_MODEL)),
                  pl.BlockSpec((R, kvw), lambda b: (b, COL_KS // kvw)),
                  pl.BlockSpec((R, kvw), lambda b: (b, COL_VS // kvw)),
                  cache_spec, cache_spec,
                  pl.BlockSpec((R, LANES), const), pl.BlockSpec((R, LANES), const),
                  pl.BlockSpec((1, LANES), const), pl.BlockSpec((1, LANES), const),
                  pl.BlockSpec((LANES, LANES), const)],
        out_specs=[pl.BlockSpec((R, D_MODEL), lambda b: (b, 0)), cache_spec, cache_spec],
        out_shape=[jax.ShapeDtypeStruct((n_seq * R, D_MODEL), F32),
                   jax.ShapeDtypeStruct((n_seq, WINDOW, kvw), F32),
                   jax.ShapeDtypeStruct((n_seq, WINDOW, kvw), F32)],
        compiler_params=_params("parallel"),
        name="swa_sample",
    )(sinks, ys, ys, ys, kcache, vcache, cos, sin, gq, gk, seg)


def _post_kernel(ag_ref, as_ref, og_ref, os_ref, x_ref, gate_ref, shift_ref, scale_ref, g2_ref,
                 wo_ref, wr_ref, br_ref, x1_ref, h2_ref, slot_ref, gatek_ref, cnt_ref, *, n_valid):
    merged =(jax.nn.sigmoid(ag_ref[...].astype(F32)) * og_ref[...].astype(F32)
              + jax.nn.sigmoid(as_ref[...].astype(F32)) * os_ref[...].astype(F32))
    y = _dot(merged.astype(BF16), wo_ref[...])
    x1 = x_ref[...] + gate_ref[...] * y
    x1_ref[...] = x1
    h2 = _rms(x1, g2_ref[...]) * (1.0 + scale_ref[...]) + shift_ref[...]
    h2_ref[...] = h2.astype(BF16)

    logits = _dot(h2, wr_ref[...], HIGHEST) + br_ref[...]
    lane_i = lax.broadcasted_iota(jnp.int32, logits.shape, 1)
    lane = lane_i.astype(F32)
    work = logits
    vals, hots = [], []
    for _ in range(TOP_K):
        m = jnp.max(work, axis=-1, keepdims=True)
        idx = jnp.min(jnp.where(work == m, lane, float(LANES)), axis=-1, keepdims=True)
        hot = lane == idx
        vals.append(m)
        hots.append(hot)
        work = jnp.where(hot, -jnp.inf, work)
    exps = [jnp.exp(v - vals[0]) for v in vals]
    denom = exps[0] + exps[1] + exps[2] + exps[3]

    tm = logits.shape[0]
    valid = lax.broadcasted_iota(jnp.int32, (tm, 1), 0) % SAMPLE_ROWS < n_valid
    sel = jnp.zeros_like(logits)
    for hot in hots:
        sel = jnp.where(hot, 1.0, sel)
    sel = jnp.where(valid, sel, 0.0)
    earlier = (lax.broadcasted_iota(jnp.int32, (tm, tm), 1)
               < lax.broadcasted_iota(jnp.int32, (tm, tm), 0))
    rank = _dot(jnp.where(earlier, 1.0, 0.0).astype(BF16), sel.astype(BF16))
    cnt = jnp.sum(sel, axis=0, keepdims=True)
    cnt_pad = jnp.floor((cnt + (ROW_UNIT - 1.0)) * (1.0 / ROW_UNIT)) * ROW_UNIT
    below = (lax.broadcasted_iota(jnp.int32, (LANES, LANES), 0)
             < lax.broadcasted_iota(jnp.int32, (LANES, LANES), 1))
    seg_start = _dot(jnp.broadcast_to(cnt_pad, (8, LANES)), jnp.where(below, 1.0, 0.0), HIGHEST)[0:1]
    pos = seg_start + rank
    slots = jnp.full_like(logits, -1.0)
    gates = jnp.zeros_like(logits)
    for k in range(TOP_K):
        s_k = jnp.sum(jnp.where(hots[k], pos, 0.0), axis=-1, keepdims=True)
        slots = jnp.where(lane_i == k, s_k, slots)
        gates = jnp.where(lane_i == k, exps[k] / denom, gates)
    slot_ref[...] = jnp.where(valid, slots, -1.0)
    gatek_ref[...] = gates
    cnt_ref[...] = cnt


def _post(y_all, o_gla, o_swa, x, gate, shift, scale, g2, wo, wr, br, per_token, tiles_per_seq, n_valid):
    n = x.shape[0]
    mod = _mod_spec(per_token, tiles_per_seq)
    row = lambda i: (i, 0)
    const = lambda i: (0, 0)
    wide = pl.BlockSpec((TOKEN_TILE, D_MODEL), row)
    narrow = pl.BlockSpec((TOKEN_TILE, LANES), row)
    return pl.pallas_call(
        functools.partial(_post_kernel, n_valid=n_valid),
        grid=(n // TOKEN_TILE,),
        in_specs=[pl.BlockSpec((TOKEN_TILE, D_MODEL), lambda i: (i, COL_AG // D_MODEL)),
                  pl.BlockSpec((TOKEN_TILE, D_MODEL), lambda i: (i, COL_AS // D_MODEL)),
                  wide, wide, wide, mod, mod, mod,
                  pl.BlockSpec((1, D_MODEL), const),
                  pl.BlockSpec((D_MODEL, D_MODEL), const),
                  pl.BlockSpec((D_MODEL, LANES), const),
                  pl.BlockSpec((1, LANES), const)],
        out_specs=[wide, wide, narrow, narrow, pl.BlockSpec((None, 1, LANES), lambda i: (i, 0, 0))],
        out_shape=[jax.ShapeDtypeStruct((n, D_MODEL), F32),
                   jax.ShapeDtypeStruct((n, D_MODEL), BF16),
                   jax.ShapeDtypeStruct((n, LANES), F32),
                   jax.ShapeDtypeStruct((n, LANES), F32),
                   jax.ShapeDtypeStruct((n // TOKEN_TILE, 1, LANES), F32)],
        compiler_params=_params("parallel"),
        name="post",
    )(y_all, y_all, o_gla, o_swa, x, gate, shift, scale, g2, wo, wr, br)


def _slot_matrix(slot_cols, weights, chunk):
    tm = slot_cols[0].shape[0]
    j = lax.broadcasted_iota(jnp.int32, (tm, tm), 1) + chunk * tm
    out = jnp.zeros((tm, tm), F32)
    for s, w in zip(slot_cols, weights):
        out = out + jnp.where(s == j, w, 0.0)
    return out.astype(BF16)


def _dispatch_kernel(nu_ref, dst_ref, ntail_ref, tail_ref, h_ref, slot_ref, xg_ref,
                     sorted_ref, zero_ref, sem, tail_sem):
    t = pl.program_id(0)
    tm = h_ref.shape[0]
    slots = slot_ref[...].astype(jnp.int32)
    slot_cols = [slots[:, k:k + 1] for k in range(TOP_K)]
    h = h_ref[...]
    for c in range(LOCAL_ROWS // tm):
        onehot = _slot_matrix(slot_cols, [1.0] * TOP_K, c)
        sorted_ref[c * tm:(c + 1) * tm, :] = _dot_tn(onehot, h).astype(BF16)

    def unit_copy(i):
        src = pl.multiple_of(i * ROW_UNIT, ROW_UNIT)
        dst = pl.multiple_of(dst_ref[t, i] * ROW_UNIT, ROW_UNIT)
        return pltpu.make_async_copy(sorted_ref.at[pl.ds(src, ROW_UNIT), :],
                                     xg_ref.at[pl.ds(dst, ROW_UNIT), :], sem)

    n_units = nu_ref[t]
    lax.fori_loop(0, n_units, lambda i, c: (unit_copy(i).start(), c)[1], 0)

    def tail_copy(i):
        dst = pl.multiple_of(tail_ref[i] * ROW_UNIT, ROW_UNIT)
        return pltpu.make_async_copy(zero_ref, xg_ref.at[pl.ds(dst, ROW_UNIT), :], tail_sem)

    @pl.when(t == pl.num_programs(0) - 1)
    def _():
        zero_ref[...] = jnp.zeros_like(zero_ref)
        n_tail = ntail_ref[0]
        lax.fori_loop(0, n_tail, lambda i, c: (tail_copy(i).start(), c)[1], 0)
        lax.fori_loop(0, n_tail, lambda i, c: (tail_copy(i).wait(), c)[1], 0)

    lax.fori_loop(0, n_units, lambda i, c: (unit_copy(i).wait(), c)[1], 0)


def _dispatch(h2, slots, n_units, unit_dst, n_tail, tail_dst, rows_max):
    n = h2.shape[0]
    return pl.pallas_call(
        _dispatch_kernel,
        grid_spec=pltpu.PrefetchScalarGridSpec(
            num_scalar_prefetch=4,
            grid=(n // TOKEN_TILE,),
            in_specs=[pl.BlockSpec((TOKEN_TILE, D_MODEL), lambda t, *_: (t, 0)),
                      pl.BlockSpec((TOKEN_TILE, LANES), lambda t, *_: (t, 0))],
            out_specs=pl.BlockSpec(memory_space=pl.ANY),
            scratch_shapes=[pltpu.VMEM((LOCAL_ROWS, D_MODEL), BF16),
                            pltpu.VMEM((ROW_UNIT, D_MODEL), BF16),
                            pltpu.SemaphoreType.DMA, pltpu.SemaphoreType.DMA]),
        out_shape=jax.ShapeDtypeStruct((rows_max, D_MODEL), BF16),
        compiler_params=_params("arbitrary"),
        name="dispatch",
    )(n_units, unit_dst, n_tail, tail_dst, h2, slots)


def _expert_kernel(te_ref, nused_ref, x_ref, wgu_ref, bgu_ref, wd_ref, bd_ref, y_ref, wgu_bf, wd_bf):
    i = pl.program_id(0)

    @pl.when(i < nused_ref[0])
    def _():
        @pl.when((i == 0) | (te_ref[i] != te_ref[jnp.maximum(i - 1, 0)]))
        def _():
            wgu_bf[...] = wgu_ref[...].astype(BF16)
            wd_bf[...] = wd_ref[...].astype(BF16)

        gu = _dot(x_ref[...], wgu_bf[...]) + bgu_ref[...]
        gate = jnp.minimum(gu[:, :D_FF], SWIGLU_LIMIT)
        up = jnp.clip(gu[:, D_FF:], -SWIGLU_LIMIT, SWIGLU_LIMIT)
        act = (up + 1.0) * gate * jax.nn.sigmoid(SWIGLU_ALPHA * gate)
        y_ref[...] = (_dot(act.astype(BF16), wd_bf[...]) + bd_ref[...]).astype(y_ref.dtype)


def _experts(xg, tile_expert, n_used, wgu, bgu, wd, bd):
    rows_max = xg.shape[0]
    row = lambda i, te, nu: (jnp.minimum(i, nu[0] - 1), 0)
    exp = lambda i, te, nu: (te[jnp.minimum(i, nu[0] - 1)], 0, 0)
    return pl.pallas_call(
        _expert_kernel,
        grid_spec=pltpu.PrefetchScalarGridSpec(
            num_scalar_prefetch=2,
            grid=(rows_max // EXPERT_TILE,),
            in_specs=[pl.BlockSpec((EXPERT_TILE, D_MODEL), row),
                      pl.BlockSpec((None, D_MODEL, 2 * D_FF), exp),
                      pl.BlockSpec((None, 1, 2 * D_FF), exp),
                      pl.BlockSpec((None, D_FF, D_MODEL), exp),
                      pl.BlockSpec((None, 1, D_MODEL), exp)],
            out_specs=pl.BlockSpec((EXPERT_TILE, D_MODEL), row),
            scratch_shapes=[pltpu.VMEM((D_MODEL, 2 * D_FF), BF16), pltpu.VMEM((D_FF, D_MODEL), BF16)]),
        out_shape=jax.ShapeDtypeStruct((rows_max, D_MODEL), BF16),
        compiler_params=_params("arbitrary"),
        name="experts",
    )(tile_expert, n_used, xg, wgu, bgu, wd, bd)


def _combine_kernel(nu_ref, src_ref, slot_ref, gatek_ref, x_ref, gmlp_ref, y_ref, o_ref, ys_ref, sem,
                    *, tile_offset):
    t = pl.program_id(0) + tile_offset
    tm = x_ref.shape[0]

    def unit_copy(i):
        src = pl.multiple_of(src_ref[t, i] * ROW_UNIT, ROW_UNIT)
        dst = pl.multiple_of(i * ROW_UNIT, ROW_UNIT)
        return pltpu.make_async_copy(y_ref.at[pl.ds(src, ROW_UNIT), :],
                                     ys_ref.at[pl.ds(dst, ROW_UNIT), :], sem)

    n_units = nu_ref[t]
    lax.fori_loop(0, n_units, lambda i, c: (unit_copy(i).start(), c)[1], 0)

    def zero_unit(i, c):
        ys_ref[pl.ds(pl.multiple_of(i * ROW_UNIT, ROW_UNIT), ROW_UNIT), :] = jnp.zeros(
            (ROW_UNIT, D_MODEL), ys_ref.dtype)
        return c

    lax.fori_loop(n_units, LOCAL_ROWS // ROW_UNIT, zero_unit, 0)
    lax.fori_loop(0, n_units, lambda i, c: (unit_copy(i).wait(), c)[1], 0)

    slots = slot_ref[...].astype(jnp.int32)
    gates = gatek_ref[...]
    slot_cols = [slots[:, k:k + 1] for k in range(TOP_K)]
    gate_cols = [gates[:, k:k + 1] for k in range(TOP_K)]
    acc = jnp.zeros((tm, D_MODEL), F32)
    for c in range(LOCAL_ROWS // tm):
        acc = acc + _dot(_slot_matrix(slot_cols, gate_cols, c), ys_ref[c * tm:(c + 1) * tm, :])
    o_ref[...] = x_ref[...] + gmlp_ref[...] * acc


def _combine(y, slots, gates, x1, gmlp, n_units, unit_src, tile_offset, per_token, tiles_per_seq):
    n = x1.shape[0]
    if per_token:
        mod = pl.BlockSpec((TOKEN_TILE, D_MODEL), lambda i, *_: (i, 0))
    else:
        mod = pl.BlockSpec((None, 1, D_MODEL), lambda i, *_: (i // tiles_per_seq, 0, 0))
    wide = pl.BlockSpec((TOKEN_TILE, D_MODEL), lambda i, *_: (i, 0))
    narrow = pl.BlockSpec((TOKEN_TILE, LANES), lambda i, *_: (i + tile_offset, 0))
    return pl.pallas_call(
        functools.partial(_combine_kernel, tile_offset=tile_offset),
        grid_spec=pltpu.PrefetchScalarGridSpec(
            num_scalar_prefetch=2,
            grid=(n // TOKEN_TILE,),
            in_specs=[narrow, narrow, wide, mod, pl.BlockSpec(memory_space=pl.ANY)],
            out_specs=wide,
            scratch_shapes=[pltpu.VMEM((LOCAL_ROWS, D_MODEL), BF16), pltpu.SemaphoreType.DMA]),
        out_shape=jax.ShapeDtypeStruct((n, D_MODEL), F32),
        compiler_params=_params("arbitrary"),
        name="combine",
    )(n_units, unit_src, slots, gates, x1, gmlp, y)


def _route_tables(cnt, rows_max):
    n_tiles = cnt.shape[0]
    units = (cnt + ROW_UNIT - 1) // ROW_UNIT
    group_units = jnp.sum(units, axis=0)
    upt = EXPERT_TILE // ROW_UNIT
    group_pad = (group_units + upt - 1) // upt * upt
    group_end = jnp.cumsum(group_pad)
    group_start = group_end - group_pad
    seg_start = group_start[None, :] + jnp.cumsum(units, axis=0) - units
    local_end = jnp.cumsum(units, axis=1)
    local_start = local_end - units
    n_units = local_end[:, -1].astype(jnp.int32)
    i = jnp.arange(LOCAL_ROWS // ROW_UNIT, dtype=jnp.int32)
    e_of = jnp.minimum(jnp.sum(i[None, :, None] >= local_end[:, None, :], axis=-1), N_EXPERTS - 1)
    unit_hbm = (jnp.take_along_axis(seg_start, e_of, axis=1)
                + i[None, :] - jnp.take_along_axis(local_start, e_of, axis=1)).astype(jnp.int32)
    unit_hbm = jnp.clip(unit_hbm, 0, rows_max // ROW_UNIT - 1)

    n_tail_e = group_pad - group_units
    j = jnp.arange(N_EXPERTS * upt, dtype=jnp.int32)
    tail_end = jnp.cumsum(n_tail_e)
    te_of = jnp.minimum(jnp.sum(j[:, None] >= tail_end[None, :], axis=-1), N_EXPERTS - 1)
    tail_hbm = (group_start + group_units)[te_of] + j - (tail_end - n_tail_e)[te_of]
    tail_hbm = jnp.clip(tail_hbm, 0, rows_max // ROW_UNIT - 1).astype(jnp.int32)
    n_tail = tail_end[-1:].astype(jnp.int32)

    r = jnp.arange(rows_max // EXPERT_TILE, dtype=jnp.int32)
    tile_expert = jnp.minimum(jnp.sum(r[:, None] * upt >= group_end[None, :], axis=-1),
                              N_EXPERTS - 1).astype(jnp.int32)
    n_used = (group_end[-1:] // upt).astype(jnp.int32)
    del n_tiles
    return n_units, unit_hbm, n_tail, tail_hbm, tile_expert, n_used


def _rope_tables(pos):
    half = HEAD_DIM // 2
    inv = ROPE_THETA ** (-jnp.arange(half, dtype=F32) / half)
    ang = pos.astype(F32)[:, None] * inv
    cos, sin = jnp.cos(ang), jnp.sin(ang)
    reps = LANES // HEAD_DIM
    return (jnp.tile(jnp.concatenate([cos, cos], axis=1), (1, reps)),
            jnp.tile(jnp.concatenate([-sin, sin], axis=1), (1, reps)))


def kernel(x_prompt, x_sample, c_prompt, c_sample, state_gla, cache_swa_k, cache_swa_v, w_ada, b_ada,
           norm1_g, norm2_g, w_in, w_gk2, b_gk, gla_norm_g, q_norm_g, k_norm_g, attn_sinks, w_o,
           w_router, b_router, w_gate_up, b_gate_up, w_down, b_down):
    batch, seq, d = x_prompt.shape
    n_seq, n_real, _ = x_sample.shape
    depth = w_in.shape[0]
    assert depth == 1 and d == D_MODEL and n_real <= SAMPLE_ROWS
    assert seq % TOKEN_TILE == 0 and (n_seq * SAMPLE_ROWS) % TOKEN_TILE == 0
    R = SAMPLE_ROWS
    kvw = SWA_KV_HEADS * HEAD_DIM
    tiles_per_seq = seq // TOKEN_TILE

    qg, kg, vg, rg, glr_w, qs, ks, vs, ag, as_ = jnp.split(
        w_in[0], [512, 1024, 2048, 3072, 3088, 4112, 4368, 4624, 5648], axis=1)
    w_main = jnp.concatenate([vg, rg, qs, ag, as_, qg, kg, ks, vs], axis=1).astype(BF16)
    w_glr = jnp.pad(glr_w, ((0, 0), (0, LANES - GATE_RANK))).astype(BF16)
    wgk = jnp.pad(w_gk2[0], ((0, LANES - GATE_RANK), (0, 0)))
    bgk = b_gk[0].reshape(1, -1)
    gnorm = gla_norm_g[0].reshape(1, -1)
    gq = jnp.tile(q_norm_g[0], LANES // HEAD_DIM).reshape(1, LANES)
    gk = jnp.tile(k_norm_g[0], LANES // HEAD_DIM).reshape(1, LANES)
    seg = jnp.asarray(np.kron(np.eye(LANES // HEAD_DIM), np.ones((HEAD_DIM, HEAD_DIM))), BF16)
    sinks = attn_sinks[0]
    wo = w_o[0].astype(BF16)
    wr = jnp.pad(w_router[0], ((0, 0), (0, LANES - N_EXPERTS)))
    br = jnp.pad(b_router[0], (0, LANES - N_EXPERTS), constant_values=-1e30).reshape(1, LANES)
    bgu = b_gate_up[0].reshape(N_EXPERTS, 1, 2 * D_FF)
    bd = b_down[0].reshape(N_EXPERTS, 1, D_MODEL)
    g1 = norm1_g[0].reshape(1, -1)
    g2 = norm2_g[0].reshape(1, -1)

    n_c = batch + n_seq
    c_all = jnp.pad(jnp.concatenate([c_prompt, c_sample], axis=0), ((0, -n_c % 8), (0, 0)))
    m_all = _ada(c_all, w_ada[0], b_ada[0])
    mp = [m_all[:batch, i * d:(i + 1) * d].reshape(batch, 1, d) for i in range(6)]
    ms = [jnp.repeat(m_all[batch:n_c, i * d:(i + 1) * d], R, axis=0) for i in range(6)]

    xp = x_prompt.reshape(batch * seq, d)
    xs = jnp.pad(x_sample, ((0, 0), (0, R - n_real), (0, 0))).reshape(n_seq * R, d)
    cos_p, sin_p = _rope_tables(jnp.arange(seq))
    cos_s, sin_s = _rope_tables(PAST_LEN + jnp.arange(R))

    yp, glr_p = _inproj(xp, mp[0], mp[1], g1, w_main, w_glr, BF16, False, tiles_per_seq)
    ys, glr_s = _inproj(xs, ms[0], ms[1], g1, w_main, w_glr, F32, True, 1)
    og_p, st_p = _gla_prompt(yp, glr_p, wgk, bgk, gnorm, batch, seq)
    og_s, st_s = _gla_sample(ys, glr_s, wgk, bgk, gnorm, state_gla[0], n_seq, n_real)
    os_p, kc_p, vc_p = _swa_prompt(yp, sinks, cos_p, sin_p, gq, gk, seg, batch, seq)
    os_s, kc_s, vc_s = _swa_sample(ys, sinks, cache_swa_k[0].reshape(n_seq, WINDOW, kvw),
                                   cache_swa_v[0].reshape(n_seq, WINDOW, kvw),
                                   cos_s, sin_s, gq, gk, seg, n_seq, n_real)
    x1_p, h2_p, sl_p, gt_p, cnt_p = _post(yp, og_p, os_p, xp, mp[2], mp[3], mp[4], g2, wo, wr, br,
                                          False, tiles_per_seq, R)
    x1_s, h2_s, sl_s, gt_s, cnt_s = _post(ys, og_s, os_s, xs, ms[2], ms[3], ms[4], g2, wo, wr, br,
                                          True, 1, n_real)

    h2 = jnp.concatenate([h2_p, h2_s], axis=0)
    slots = jnp.concatenate([sl_p, sl_s], axis=0)
    gates = jnp.concatenate([gt_p, gt_s], axis=0)
    cnt = jnp.concatenate([cnt_p, cnt_s], axis=0)[:, 0, :N_EXPERTS].astype(jnp.int32)
    n_tiles = cnt.shape[0]
    rows_bound = (TOP_K * (batch * seq + n_seq * n_real) + n_tiles * N_EXPERTS * (ROW_UNIT - 1)
                  + N_EXPERTS * (EXPERT_TILE - 1))
    rows_max = -(-rows_bound // EXPERT_TILE) * EXPERT_TILE
    n_units, unit_hbm, n_tail, tail_hbm, tile_expert, n_used = _route_tables(cnt, rows_max)
    xg = _dispatch(h2, slots, n_units, unit_hbm, n_tail, tail_hbm, rows_max)
    yg = _experts(xg, tile_expert, n_used, w_gate_up[0], bgu, w_down[0], bd)
    p_tiles = batch * seq // TOKEN_TILE
    out_p = _combine(yg, slots, gates, x1_p, mp[5], n_units, unit_hbm, 0, False, tiles_per_seq)
    out_s = _combine(yg, slots, gates, x1_s, ms[5], n_units, unit_hbm, p_tiles, True, 1)

    cache_shape = (WINDOW, SWA_KV_HEADS, HEAD_DIM)
    return (out_p.reshape(batch, seq, d),
            out_s.reshape(n_seq, R, d)[:, :n_real],
            st_p[None],
            kc_p.reshape(1, batch, *cache_shape),
            vc_p.reshape(1, batch, *cache_shape),
            st_s[None],
            kc_s.reshape(1, n_seq, *cache_shape),
            vc_s.reshape(1, n_seq, *cache_shape))
```

```python
import functools

import numpy as np
import jax
import jax.numpy as jnp
from jax import lax
from jax.experimental import pallas as pl
from jax.experimental.pallas import tpu as pltpu

F32 = jnp.float32
BF16 = jnp.bfloat16
HIGHEST = lax.Precision.HIGHEST

D_MODEL = 1024
PAST_LEN = 16384
GLA_HEADS = 4
GLA_DK = 128
GLA_DV = 256
GATE_RANK = 16
GATE_TAU = 16.0
SWA_HEADS = 16
SWA_KV_HEADS = 4
HEAD_DIM = 64
SWA_GROUP = SWA_HEADS // SWA_KV_HEADS
WINDOW = 128
ROPE_THETA = 10000.0
N_EXPERTS = 32
TOP_K = 4
D_FF = 1024
SWIGLU_ALPHA = 1.702
SWIGLU_LIMIT = 7.0
NORM_EPS = 1e-6

LANES = 128
SAMPLE_ROWS = 8
SAMPLE_SEQS = 16
TOKEN_TILE = 512
GLA_CHUNK = 64
GLA_BLOCK = 256
VMEM_LIMIT = 56 * 1024 * 1024
ROW_UNIT = 16
EXPERT_TILE = 512
EXPERT_PART = 256
LOCAL_ROWS = -(-(TOP_K * TOKEN_TILE + N_EXPERTS * (ROW_UNIT - 1)) // TOKEN_TILE) * TOKEN_TILE

COL_VG, COL_RG, COL_QS, COL_AG, COL_AS = 0, 1024, 2048, 3072, 4096
COL_QG, COL_KG, COL_KS, COL_VS = 5120, 5632, 6144, 6400
D_MAIN = 6656
PROJ_CHUNK = 512


def _dot(a, b, precision=None):
    return jnp.dot(a, b, preferred_element_type=F32, precision=precision)


def _dot_nt(a, b, precision=None):
    return lax.dot_general(a, b, (((1,), (1,)), ((), ())), preferred_element_type=F32, precision=precision)


def _dot_tn(a, b, precision=None):
    return lax.dot_general(a, b, (((0,), (0,)), ((), ())), preferred_element_type=F32, precision=precision)


def _params(*sem):
    return pltpu.CompilerParams(dimension_semantics=sem, vmem_limit_bytes=VMEM_LIMIT)


def _rms(x, g):
    return x * lax.rsqrt(jnp.mean(x * x, axis=-1, keepdims=True) + NORM_EPS) * g


def _log_sigmoid(x):
    return jnp.minimum(x, 0.0) - jnp.log(1.0 + jnp.exp(-jnp.abs(x)))


def _ada_kernel(c_ref, w_ref, b_ref, o_ref):
    c = c_ref[...]
    s = c * jax.nn.sigmoid(c)
    o_ref[...] = _dot(s, w_ref[...], HIGHEST) + b_ref[...]


def _ada(c_all, w_ada, b_ada):
    rows = c_all.shape[0]
    tn = 768
    return pl.pallas_call(
        _ada_kernel,
        grid=(6 * D_MODEL // tn,),
        in_specs=[pl.BlockSpec((rows, D_MODEL), lambda j: (0, 0)),
                  pl.BlockSpec((D_MODEL, tn), lambda j: (0, j)),
                  pl.BlockSpec((1, tn), lambda j: (0, j))],
        out_specs=pl.BlockSpec((rows, tn), lambda j: (0, j)),
        out_shape=jax.ShapeDtypeStruct((rows, 6 * D_MODEL), F32),
        compiler_params=_params("parallel"),
        name="ada",
    )(c_all, w_ada, b_ada.reshape(1, -1))


def _inproj_kernel(x_ref, shift_ref, scale_ref, g_ref, w_ref, wg_ref, o_ref, og_ref):
    h = _rms(x_ref[...], g_ref[...]) * (1.0 + scale_ref[...]) + shift_ref[...]
    hb = h.astype(BF16)
    for j in range(D_MAIN // PROJ_CHUNK):
        sl = slice(j * PROJ_CHUNK, (j + 1) * PROJ_CHUNK)
        o_ref[:, sl] = _dot(hb, w_ref[:, sl]).astype(o_ref.dtype)
    og_ref[...] = _dot(hb, wg_ref[...])


def _mod_spec(per_token, tiles_per_seq):
    if per_token:
        return pl.BlockSpec((TOKEN_TILE, D_MODEL), lambda i: (i, 0))
    return pl.BlockSpec((None, 1, D_MODEL), lambda i: (i // tiles_per_seq, 0, 0))


def _inproj(x, shift, scale, g, w_main, w_glr, out_dtype, per_token, tiles_per_seq):
    n = x.shape[0]
    mod = _mod_spec(per_token, tiles_per_seq)
    const = lambda i: (0, 0)
    return pl.pallas_call(
        _inproj_kernel,
        grid=(n // TOKEN_TILE,),
        in_specs=[pl.BlockSpec((TOKEN_TILE, D_MODEL), lambda i: (i, 0)), mod, mod,
                  pl.BlockSpec((1, D_MODEL), const),
                  pl.BlockSpec((D_MODEL, D_MAIN), const, pipeline_mode=pl.Buffered(1)),
                  pl.BlockSpec((D_MODEL, LANES), const, pipeline_mode=pl.Buffered(1))],
        out_specs=[pl.BlockSpec((TOKEN_TILE, D_MAIN), lambda i: (i, 0)),
                   pl.BlockSpec((TOKEN_TILE, LANES), lambda i: (i, 0))],
        out_shape=[jax.ShapeDtypeStruct((n, D_MAIN), out_dtype),
                   jax.ShapeDtypeStruct((n, LANES), F32)],
        compiler_params=_params("parallel"),
        name="inproj",
    )(x, shift, scale, g, w_main, w_glr)


def _gla_log_gate(glr, wgk, bgk):
    return _log_sigmoid(_dot(glr, wgk, HIGHEST) + bgk) * (1.0 / GATE_TAU)


def _gla_out(o, r, g):
    r = r.astype(F32)
    return _rms(o, g) * (r * jax.nn.sigmoid(r))


def _gla_prompt_kernel(q_ref, k_ref, v_ref, r_ref, glr_ref, wgk_ref, bgk_ref, g_ref,
                       o_ref, s_ref, st_ref):
    c = pl.program_id(2)

    @pl.when(c == 0)
    def _():
        st_ref[...] = jnp.zeros_like(st_ref)

    L = GLA_CHUNK
    row = lax.broadcasted_iota(jnp.int32, (L, L), 0)
    col = lax.broadcasted_iota(jnp.int32, (L, L), 1)
    causal = col <= row
    tril = causal.astype(F32)
    lg_all = _gla_log_gate(glr_ref[...], wgk_ref[...], bgk_ref[...])
    for i in range(GLA_BLOCK // L):
        sl = slice(i * L, (i + 1) * L)
        q = q_ref[sl, :].astype(F32) * (GLA_DK ** -0.5)
        k = k_ref[sl, :].astype(F32)
        v = v_ref[sl, :]
        b = _dot(tril, lg_all[sl, :], HIGHEST)
        b_last = b[L - 1:L, :]
        qe = (q * jnp.exp(b)).astype(BF16)
        ke = (k * jnp.exp(-b)).astype(BF16)
        kd = (k * jnp.exp(b_last - b)).astype(BF16)
        attn = jnp.where(causal, _dot_nt(qe, ke), 0.0).astype(BF16)
        st = st_ref[...]
        o = _dot_nt(qe, st.astype(BF16)) + _dot(attn, v)
        st_ref[...] = st * jnp.exp(b_last) + _dot_tn(v, kd)
        o_ref[sl, :] = _gla_out(o, r_ref[sl, :], g_ref[...]).astype(o_ref.dtype)

    @pl.when(c == pl.num_programs(2) - 1)
    def _():
        s_ref[...] = st_ref[...].T


def _gla_prompt(yp, glr, wgk, bgk, gnorm, batch, seq):
    nb = seq // GLA_BLOCK
    tok = lambda b, h, c: b * nb + c
    return pl.pallas_call(
        _gla_prompt_kernel,
        grid=(batch, GLA_HEADS, nb),
        in_specs=[pl.BlockSpec((GLA_BLOCK, GLA_DK), lambda b, h, c: (tok(b, h, c), COL_QG // GLA_DK + h)),
                  pl.BlockSpec((GLA_BLOCK, GLA_DK), lambda b, h, c: (tok(b, h, c), COL_KG // GLA_DK + h)),
                  pl.BlockSpec((GLA_BLOCK, GLA_DV), lambda b, h, c: (tok(b, h, c), COL_VG // GLA_DV + h)),
                  pl.BlockSpec((GLA_BLOCK, GLA_DV), lambda b, h, c: (tok(b, h, c), COL_RG // GLA_DV + h)),
                  pl.BlockSpec((GLA_BLOCK, LANES), lambda b, h, c: (tok(b, h, c), 0)),
                  pl.BlockSpec((LANES, GLA_DK), lambda b, h, c: (0, h)),
                  pl.BlockSpec((1, GLA_DK), lambda b, h, c: (0, h)),
                  pl.BlockSpec((1, GLA_DV), lambda b, h, c: (0, 0))],
        out_specs=[pl.BlockSpec((GLA_BLOCK, GLA_DV), lambda b, h, c: (tok(b, h, c), h)),
                   pl.BlockSpec((None, None, GLA_DK, GLA_DV), lambda b, h, c: (b, h, 0, 0))],
        out_shape=[jax.ShapeDtypeStruct((batch * seq, GLA_HEADS * GLA_DV), BF16),
                   jax.ShapeDtypeStruct((batch, GLA_HEADS, GLA_DK, GLA_DV), F32)],
        scratch_shapes=[pltpu.VMEM((GLA_DV, GLA_DK), F32)],
        compiler_params=_params("parallel", "parallel", "arbitrary"),
        name="gla_prompt",
    )(yp, yp, yp, yp, glr, wgk, bgk, gnorm)


def _gla_sample_kernel(q_ref, k_ref, v_ref, r_ref, glr_ref, wgk_ref, bgk_ref, g_ref, s0_ref,
                       o_ref, s_ref, *, n_real):
    R = SAMPLE_ROWS
    rows = q_ref.shape[0]
    row = lax.broadcasted_iota(jnp.int32, (rows, rows), 0)
    col = lax.broadcasted_iota(jnp.int32, (rows, rows), 1)
    same = (row // R) == (col // R)
    causal = same & (col <= row)
    real = lax.broadcasted_iota(jnp.int32, (rows, 1), 0) % R < n_real
    lg = jnp.where(real, _gla_log_gate(glr_ref[...], wgk_ref[...], bgk_ref[...]), 0.0)
    b = _dot(jnp.where(causal, 1.0, 0.0), lg, HIGHEST)
    b_last = _dot(jnp.where(same, 1.0, 0.0), lg, HIGHEST)
    q = q_ref[...] * (GLA_DK ** -0.5)
    k = jnp.where(real, k_ref[...], 0.0)
    v = v_ref[...]
    qe = q * jnp.exp(b)
    ke = k * jnp.exp(-b)
    kd = k * jnp.exp(b_last - b)
    for h in range(GLA_HEADS):
        dk = slice(h * GLA_DK, (h + 1) * GLA_DK)
        dv = slice(h * GLA_DV, (h + 1) * GLA_DV)
        attn = jnp.where(causal, _dot_nt(qe[:, dk], ke[:, dk]), 0.0)
        o_intra = _dot(attn, v[:, dv])
        decay_t = jnp.exp(b_last[:, dk]).T
        outs = []
        for s in range(rows // R):
            sl = slice(s * R, (s + 1) * R)
            s0 = s0_ref[s, h]
            outs.append(_dot(qe[sl, dk], s0) + o_intra[sl, :])
            s_ref[s, h] = s0 * decay_t[:, s * R:s * R + 1] + _dot_tn(kd[sl, dk], v[sl, dv])
        o_ref[:, dv] = _gla_out(jnp.concatenate(outs, axis=0), r_ref[:, dv], g_ref[...])


def _gla_sample(ys, glr, wgk, bgk, gnorm, state, n_seq, n_real):
    R = SAMPLE_SEQS * SAMPLE_ROWS
    hk, hv = GLA_HEADS * GLA_DK, GLA_HEADS * GLA_DV
    st_spec = pl.BlockSpec((SAMPLE_SEQS, GLA_HEADS, GLA_DK, GLA_DV), lambda b: (b, 0, 0, 0))
    return pl.pallas_call(
        functools.partial(_gla_sample_kernel, n_real=n_real),
        grid=(n_seq // SAMPLE_SEQS,),
        in_specs=[pl.BlockSpec((R, hk), lambda b: (b, COL_QG // hk)),
                  pl.BlockSpec((R, hk), lambda b: (b, COL_KG // hk)),
                  pl.BlockSpec((R, hv), lambda b: (b, COL_VG // hv)),
                  pl.BlockSpec((R, hv), lambda b: (b, COL_RG // hv)),
                  pl.BlockSpec((R, LANES), lambda b: (b, 0)),
                  pl.BlockSpec((LANES, hk), lambda b: (0, 0)),
                  pl.BlockSpec((1, hk), lambda b: (0, 0)),
                  pl.BlockSpec((1, GLA_DV), lambda b: (0, 0)),
                  st_spec],
        out_specs=[pl.BlockSpec((R, hv), lambda b: (b, 0)), st_spec],
        out_shape=[jax.ShapeDtypeStruct((n_seq * SAMPLE_ROWS, hv), F32),
                   jax.ShapeDtypeStruct((n_seq, GLA_HEADS, GLA_DK, GLA_DV), F32)],
        compiler_params=_params("parallel"),
        name="gla_sample",
    )(ys, ys, ys, ys, glr, wgk, bgk, gnorm, state)


def _lane_lower(shape):
    return lax.broadcasted_iota(jnp.int32, shape, len(shape) - 1) % LANES < HEAD_DIM


def _head_norm_rope(x, g, cos, sin, seg):
    sq = x * x
    hi = sq.astype(BF16)
    lo = (sq - hi.astype(F32)).astype(BF16)
    ss = _dot(hi, seg) + _dot(lo, seg)
    y = x * lax.rsqrt(ss * (1.0 / HEAD_DIM) + NORM_EPS) * g
    half = HEAD_DIM // 2
    lane = lax.broadcasted_iota(jnp.int32, y.shape, 1)
    rot = jnp.where(lane % HEAD_DIM < half, pltpu.roll(y, LANES - half, 1), pltpu.roll(y, half, 1))
    return y * cos + rot * sin


def _both_halves(blk, half):
    sw = pltpu.roll(blk, HEAD_DIM, 1)
    lower = _lane_lower(blk.shape)
    return jnp.where(lower, blk, sw) if half == 0 else jnp.where(lower, sw, blk)


def _stack_heads(q_blocks):
    parts = []
    for qb in q_blocks:
        lower = _lane_lower(qb.shape)
        zero = jnp.zeros_like(qb)
        parts += [jnp.where(lower, qb, zero), jnp.where(lower, zero, qb)]
    return jnp.concatenate(parts, axis=0)


def _swa_prompt_kernel(sink_ref, q_ref, k_ref, v_ref, cos_ref, sin_ref, gq_ref, gk_ref, seg_ref,
                       o_ref, ko_ref, vo_ref, kprev_ref, vprev_ref):
    n = pl.program_id(1)
    W = WINDOW

    @pl.when(n == 0)
    def _():
        kprev_ref[...] = jnp.zeros_like(kprev_ref)
        vprev_ref[...] = jnp.zeros_like(vprev_ref)

    cos, sin, seg = cos_ref[...], sin_ref[...], seg_ref[...]
    k_cur = jnp.concatenate(
        [_head_norm_rope(k_ref[:, j * LANES:(j + 1) * LANES].astype(F32), gk_ref[...], cos, sin, seg)
         for j in range(SWA_KV_HEADS * HEAD_DIM // LANES)], axis=1)
    v_cur = v_ref[...].astype(F32)
    ko_ref[...] = k_cur
    vo_ref[...] = v_cur
    k2 = jnp.concatenate([kprev_ref[...], k_cur], axis=0)
    v2 = jnp.concatenate([vprev_ref[...], v_cur], axis=0)
    kprev_ref[...] = k_cur
    vprev_ref[...] = v_cur

    qi = lax.broadcasted_iota(jnp.int32, (W, 2 * W), 0)
    ki = lax.broadcasted_iota(jnp.int32, (W, 2 * W), 1)
    dq = qi + W - ki
    mask = (dq >= 0) & (dq < W) & ((ki >= W) | (n > 0))
    for kh in range(SWA_KV_HEADS):
        blk = slice((kh // 2) * LANES, (kh // 2 + 1) * LANES)
        kb = _both_halves(k2[:, blk], kh % 2).astype(BF16)
        vb = _both_halves(v2[:, blk], kh % 2).astype(BF16)
        qblocks = []
        for j in range(2):
            c0 = (2 * kh + j) * LANES
            qn = _head_norm_rope(q_ref[:, c0:c0 + LANES].astype(F32), gq_ref[...], cos, sin, seg)
            qblocks.append(qn * (HEAD_DIM ** -0.5))
        qs = _stack_heads(qblocks).astype(BF16)
        s = _dot_nt(qs, kb)
        outs = []
        for g in range(SWA_GROUP):
            sg = jnp.where(mask, s[g * W:(g + 1) * W, :], -jnp.inf)
            sink = sink_ref[kh * SWA_GROUP + g]
            m = jnp.maximum(jnp.max(sg, axis=-1, keepdims=True), sink)
            p = jnp.exp(sg - m)
            denom = jnp.sum(p, axis=-1, keepdims=True) + jnp.exp(sink - m)
            outs.append(_dot(p.astype(BF16), vb) / denom)
        lower = _lane_lower((W, LANES))
        for j in range(2):
            c0 = (2 * kh + j) * LANES
            o_ref[:, c0:c0 + LANES] = jnp.where(lower, outs[2 * j], outs[2 * j + 1]).astype(o_ref.dtype)


def _swa_prompt(yp, sinks, cos, sin, gq, gk, seg, batch, seq):
    nb = seq // WINDOW
    kvw = SWA_KV_HEADS * HEAD_DIM
    tok = lambda b, n: b * nb + n
    const = lambda b, n: (0, 0)
    cache_spec = pl.BlockSpec((None, WINDOW, kvw), lambda b, n: (b, 0, 0))
    return pl.pallas_call(
        _swa_prompt_kernel,
        grid=(batch, nb),
        in_specs=[pl.BlockSpec(memory_space=pltpu.SMEM),
                  pl.BlockSpec((WINDOW, D_MODEL), lambda b, n: (tok(b, n), COL_QS // D_MODEL)),
                  pl.BlockSpec((WINDOW, kvw), lambda b, n: (tok(b, n), COL_KS // kvw)),
                  pl.BlockSpec((WINDOW, kvw), lambda b, n: (tok(b, n), COL_VS // kvw)),
                  pl.BlockSpec((WINDOW, LANES), lambda b, n: (n, 0)),
                  pl.BlockSpec((WINDOW, LANES), lambda b, n: (n, 0)),
                  pl.BlockSpec((1, LANES), const), pl.BlockSpec((1, LANES), const),
                  pl.BlockSpec((LANES, LANES), const)],
        out_specs=[pl.BlockSpec((WINDOW, D_MODEL), lambda b, n: (tok(b, n), 0)), cache_spec, cache_spec],
        out_shape=[jax.ShapeDtypeStruct((batch * seq, D_MODEL), BF16),
                   jax.ShapeDtypeStruct((batch, WINDOW, kvw), F32),
                   jax.ShapeDtypeStruct((batch, WINDOW, kvw), F32)],
        scratch_shapes=[pltpu.VMEM((WINDOW, kvw), F32), pltpu.VMEM((WINDOW, kvw), F32)],
        compiler_params=_params("parallel", "arbitrary"),
        name="swa_prompt",
    )(sinks, yp, yp, yp, cos, sin, gq, gk, seg)


def _shift_cache(cache, new, n_real):
    R = SAMPLE_ROWS
    rolled = pltpu.roll(cache, WINDOW - n_real, 0)
    tail_new = pltpu.roll(new, R - n_real, 0)
    row = lax.broadcasted_iota(jnp.int32, (R, cache.shape[1]), 0)
    tail = jnp.where(row < R - n_real, rolled[WINDOW - R:, :], tail_new)
    return jnp.concatenate([rolled[:WINDOW - R, :], tail], axis=0)


def _swa_sample_kernel(q_ref, k_ref, v_ref, kc_ref, vc_ref, cos_ref, sin_ref, gq_ref, gk_ref,
                       seg_ref, sink_ref, o_ref, ko_ref, vo_ref, *, n_real):
    R, W = SAMPLE_ROWS, WINDOW
    kvw = SWA_KV_HEADS * HEAD_DIM
    cos, sin, seg = cos_ref[...], sin_ref[...], seg_ref[...]
    k_new = jnp.concatenate(
        [_head_norm_rope(k_ref[:, j * LANES:(j + 1) * LANES], gk_ref[...], cos, sin, seg)
         for j in range(kvw // LANES)], axis=1)
    v_new = v_ref[...]
    q_pairs = [_head_norm_rope(q_ref[:, j * LANES:(j + 1) * LANES], gq_ref[...], cos, sin, seg)
               * (HEAD_DIM ** -0.5) for j in range(SWA_HEADS // 2)]

    hr = SWA_HEADS * R
    t_c = lax.broadcasted_iota(jnp.int32, (hr, W), 0) % R
    mask_c = lax.broadcasted_iota(jnp.int32, (hr, W), 1) > t_c
    t_n = lax.broadcasted_iota(jnp.int32, (hr, R), 0) % R
    mask_n = lax.broadcasted_iota(jnp.int32, (hr, R), 1) <= t_n
    sink = sink_ref[...]
    lower = _lane_lower((R, LANES))
    zeros = jnp.zeros((R, LANES), F32)
    for s in range(q_ref.shape[0] // R):
        sl = slice(s * R, (s + 1) * R)
        kc, vc = kc_ref[s], vc_ref[s]
        kn, vn = k_new[sl, :], v_new[sl, :]
        ko_ref[s] = _shift_cache(kc, kn, n_real)
        vo_ref[s] = _shift_cache(vc, vn, n_real)
        q_rows = []
        for h in range(SWA_HEADS):
            kh = h // SWA_GROUP
            x = q_pairs[h // 2][sl, :]
            if h % 2 != kh % 2:
                x = pltpu.roll(x, HEAD_DIM, 1)
            x = jnp.where(lower, x, zeros) if kh % 2 == 0 else jnp.where(lower, zeros, x)
            q_rows.append(jnp.concatenate([x, zeros] if kh // 2 == 0 else [zeros, x], axis=1))
        qbd = jnp.concatenate(q_rows, axis=0)
        sc = jnp.where(mask_c, _dot_nt(qbd, kc), -jnp.inf)
        sn = jnp.where(mask_n, _dot_nt(qbd, kn), -jnp.inf)
        m = jnp.maximum(jnp.maximum(jnp.max(sc, axis=-1, keepdims=True),
                                    jnp.max(sn, axis=-1, keepdims=True)), sink)
        pc, pn = jnp.exp(sc - m), jnp.exp(sn - m)
        denom = (jnp.sum(pc, axis=-1, keepdims=True) + jnp.sum(pn, axis=-1, keepdims=True)
                 + jnp.exp(sink - m))
        o = (_dot(pc, vc) + _dot(pn, vn)) / denom
        for j in range(SWA_HEADS // 2):
            halves = []
            for h in (2 * j, 2 * j + 1):
                kh = h // SWA_GROUP
                y = o[h * R:(h + 1) * R, (kh // 2) * LANES:(kh // 2 + 1) * LANES]
                halves.append(pltpu.roll(y, HEAD_DIM, 1) if h % 2 != kh % 2 else y)
            o_ref[sl, j * LANES:(j + 1) * LANES] = jnp.where(lower, halves[0], halves[1])


def _swa_sample(ys, sink_rows, kcache, vcache, cos, sin, gq, gk, seg, n_seq, n_real):
    R = SAMPLE_SEQS * SAMPLE_ROWS
    kvw = SWA_KV_HEADS * HEAD_DIM
    const = lambda b: (0, 0)
    cache_spec = pl.BlockSpec((SAMPLE_SEQS, WINDOW, kvw), lambda b: (b, 0, 0))
    return pl.pallas_call(
        functools.partial(_swa_sample_kernel, n_real=n_real),
        grid=(n_seq // SAMPLE_SEQS,),
        in_specs=[pl.BlockSpec((R, D_MODEL), lambda b: (b, COL_QS // D_MODEL)),
                  pl.BlockSpec((R, kvw), lambda b: (b, COL_KS // kvw)),
                  pl.BlockSpec((R, kvw), lambda b: (b, COL_VS // kvw)),
                  cache_spec, cache_spec,
                  pl.BlockSpec((R, LANES), const), pl.BlockSpec((R, LANES), const),
                  pl.BlockSpec((1, LANES), const), pl.BlockSpec((1, LANES), const),
                  pl.BlockSpec((LANES, LANES), const),
                  pl.BlockSpec((SWA_HEADS * SAMPLE_ROWS, 1), const)],
        out_specs=[pl.BlockSpec((R, D_MODEL), lambda b: (b, 0)), cache_spec, cache_spec],
        out_shape=[jax.ShapeDtypeStruct((n_seq * SAMPLE_ROWS, D_MODEL), F32),
                   jax.ShapeDtypeStruct((n_seq, WINDOW, kvw), F32),
                   jax.ShapeDtypeStruct((n_seq, WINDOW, kvw), F32)],
        compiler_params=_params("parallel"),
        name="swa_sample",
    )(ys, ys, ys, kcache, vcache, cos, sin, gq, gk, seg, sink_rows)


def _post_kernel(ag_ref, as_ref, og_ref, os_ref, x_ref, gate_ref, shift_ref, scale_ref, g2_ref,
                 wo_ref, wr_ref, br_ref, x1_ref, h2_ref, slot_ref, gatek_ref, cnt_ref, *, n_valid):
    merged =(jax.nn.sigmoid(ag_ref[...].astype(F32)) * og_ref[...].astype(F32)
              + jax.nn.sigmoid(as_ref[...].astype(F32)) * os_ref[...].astype(F32))
    y = _dot(merged.astype(BF16), wo_ref[...])
    x1 = x_ref[...] + gate_ref[...] * y
    x1_ref[...] = x1
    h2 = _rms(x1, g2_ref[...]) * (1.0 + scale_ref[...]) + shift_ref[...]
    h2_ref[...] = h2.astype(BF16)

    logits = _dot(h2, wr_ref[...], HIGHEST) + br_ref[...]
    lane_i = lax.broadcasted_iota(jnp.int32, logits.shape, 1)
    lane = lane_i.astype(F32)
    work = logits
    vals, hots = [], []
    for _ in range(TOP_K):
        m = jnp.max(work, axis=-1, keepdims=True)
        idx = jnp.min(jnp.where(work == m, lane, float(LANES)), axis=-1, keepdims=True)
        hot = lane == idx
        vals.append(m)
        hots.append(hot)
        work = jnp.where(hot, -jnp.inf, work)
    exps = [jnp.exp(v - vals[0]) for v in vals]
    denom = exps[0] + exps[1] + exps[2] + exps[3]

    tm = logits.shape[0]
    valid = lax.broadcasted_iota(jnp.int32, (tm, 1), 0) % SAMPLE_ROWS < n_valid
    sel = jnp.zeros_like(logits)
    for hot in hots:
        sel = jnp.where(hot, 1.0, sel)
    sel = jnp.where(valid, sel, 0.0)
    earlier = (lax.broadcasted_iota(jnp.int32, (tm, tm), 1)
               < lax.broadcasted_iota(jnp.int32, (tm, tm), 0))
    rank = _dot(jnp.where(earlier, 1.0, 0.0).astype(BF16), sel.astype(BF16))
    cnt = jnp.sum(sel, axis=0, keepdims=True)
    cnt_pad = jnp.floor((cnt + (ROW_UNIT - 1.0)) * (1.0 / ROW_UNIT)) * ROW_UNIT
    below = (lax.broadcasted_iota(jnp.int32, (LANES, LANES), 0)
             < lax.broadcasted_iota(jnp.int32, (LANES, LANES), 1))
    seg_start = _dot(jnp.broadcast_to(cnt_pad, (8, LANES)), jnp.where(below, 1.0, 0.0), HIGHEST)[0:1]
    pos = seg_start + rank
    slots = jnp.full_like(logits, -1.0)
    gates = jnp.zeros_like(logits)
    for k in range(TOP_K):
        s_k = jnp.sum(jnp.where(hots[k], pos, 0.0), axis=-1, keepdims=True)
        slots = jnp.where(lane_i == k, s_k, slots)
        gates = jnp.where(lane_i == k, exps[k] / denom, gates)
    slot_ref[...] = jnp.where(valid, slots, -1.0)
    gatek_ref[...] = gates
    cnt_ref[...] = cnt


def _post(y_all, o_gla, o_swa, x, gate, shift, scale, g2, wo, wr, br, per_token, tiles_per_seq, n_valid):
    n = x.shape[0]
    mod = _mod_spec(per_token, tiles_per_seq)
    row = lambda i: (i, 0)
    const = lambda i: (0, 0)
    wide = pl.BlockSpec((TOKEN_TILE, D_MODEL), row)
    narrow = pl.BlockSpec((TOKEN_TILE, LANES), row)
    return pl.pallas_call(
        functools.partial(_post_kernel, n_valid=n_valid),
        grid=(n // TOKEN_TILE,),
        in_specs=[pl.BlockSpec((TOKEN_TILE, D_MODEL), lambda i: (i, COL_AG // D_MODEL)),
                  pl.BlockSpec((TOKEN_TILE, D_MODEL), lambda i: (i, COL_AS // D_MODEL)),
                  wide, wide, wide, mod, mod, mod,
                  pl.BlockSpec((1, D_MODEL), const),
                  pl.BlockSpec((D_MODEL, D_MODEL), const),
                  pl.BlockSpec((D_MODEL, LANES), const),
                  pl.BlockSpec((1, LANES), const)],
        out_specs=[wide, wide, narrow, narrow, pl.BlockSpec((None, 1, LANES), lambda i: (i, 0, 0))],
        out_shape=[jax.ShapeDtypeStruct((n, D_MODEL), F32),
                   jax.ShapeDtypeStruct((n, D_MODEL), BF16),
                   jax.ShapeDtypeStruct((n, LANES), F32),
                   jax.ShapeDtypeStruct((n, LANES), F32),
                   jax.ShapeDtypeStruct((n // TOKEN_TILE, 1, LANES), F32)],
        compiler_params=_params("parallel"),
        name="post",
    )(y_all, y_all, o_gla, o_swa, x, gate, shift, scale, g2, wo, wr, br)


def _slot_matrix(slot_cols, weights, chunk):
    tm = slot_cols[0].shape[0]
    j = lax.broadcasted_iota(jnp.int32, (tm, tm), 1) + chunk * tm
    out = jnp.zeros((tm, tm), F32)
    for s, w in zip(slot_cols, weights):
        out = out + jnp.where(s == j, w, 0.0)
    return out.astype(BF16)


def _dispatch_kernel(nu_ref, dst_ref, ntail_ref, tail_ref, h_ref, slot_ref, xg_ref,
                     sorted_ref, zero_ref, sem, tail_sem):
    t = pl.program_id(0)
    last = pl.num_programs(0) - 1
    buf = t % 2
    tm = h_ref.shape[0]
    slots = slot_ref[...].astype(jnp.int32)
    slot_cols = [slots[:, k:k + 1] for k in range(TOP_K)]
    h = h_ref[...]
    for c in range(LOCAL_ROWS // tm):
        onehot = _slot_matrix(slot_cols, [1.0] * TOP_K, c)
        sorted_ref[buf, c * tm:(c + 1) * tm, :] = _dot_tn(onehot, h).astype(BF16)

    def unit_copy(tile, b, i):
        src = pl.multiple_of(i * ROW_UNIT, ROW_UNIT)
        dst = pl.multiple_of(dst_ref[tile, i] * ROW_UNIT, ROW_UNIT)
        return pltpu.make_async_copy(sorted_ref.at[b, pl.ds(src, ROW_UNIT), :],
                                     xg_ref.at[pl.ds(dst, ROW_UNIT), :], sem.at[b])

    def start_all(tile, b):
        lax.fori_loop(0, nu_ref[tile], lambda i, c: (unit_copy(tile, b, i).start(), c)[1], 0)

    def wait_all(tile, b):
        lax.fori_loop(0, nu_ref[tile], lambda i, c: (unit_copy(tile, b, i).wait(), c)[1], 0)

    @pl.when(t > 0)
    def _():
        wait_all(t - 1, 1 - buf)

    start_all(t, buf)

    def tail_copy(i):
        dst = pl.multiple_of(tail_ref[i] * ROW_UNIT, ROW_UNIT)
        return pltpu.make_async_copy(zero_ref, xg_ref.at[pl.ds(dst, ROW_UNIT), :], tail_sem)

    @pl.when(t == last)
    def _():
        zero_ref[...] = jnp.zeros_like(zero_ref)
        n_tail = ntail_ref[0]
        lax.fori_loop(0, n_tail, lambda i, c: (tail_copy(i).start(), c)[1], 0)
        lax.fori_loop(0, n_tail, lambda i, c: (tail_copy(i).wait(), c)[1], 0)
        wait_all(t, buf)


def _dispatch(h2, slots, n_units, unit_dst, n_tail, tail_dst, rows_max):
    n = h2.shape[0]
    return pl.pallas_call(
        _dispatch_kernel,
        grid_spec=pltpu.PrefetchScalarGridSpec(
            num_scalar_prefetch=4,
            grid=(n // TOKEN_TILE,),
            in_specs=[pl.BlockSpec((TOKEN_TILE, D_MODEL), lambda t, *_: (t, 0)),
                      pl.BlockSpec((TOKEN_TILE, LANES), lambda t, *_: (t, 0))],
            out_specs=pl.BlockSpec(memory_space=pl.ANY),
            scratch_shapes=[pltpu.VMEM((2, LOCAL_ROWS, D_MODEL), BF16),
                            pltpu.VMEM((ROW_UNIT, D_MODEL), BF16),
                            pltpu.SemaphoreType.DMA((2,)), pltpu.SemaphoreType.DMA]),
        out_shape=jax.ShapeDtypeStruct((rows_max, D_MODEL), BF16),
        compiler_params=_params("arbitrary"),
        name="dispatch",
    )(n_units, unit_dst, n_tail, tail_dst, h2, slots)


def _expert_kernel(tg_ref, ge_ref, ng_ref, nused_ref, rows_ref, x_ref, bgu_ref, bd_ref, wgu_hbm, wd_hbm,
                   y_ref, wgu_f32, wd_f32, wgu_bf, wd_bf, sem):
    i = pl.program_id(0)

    def fetch(g, b):
        e = ge_ref[g]
        return (pltpu.make_async_copy(wgu_hbm.at[e], wgu_f32.at[b], sem.at[0, b]),
                pltpu.make_async_copy(wd_hbm.at[e], wd_f32.at[b], sem.at[1, b]))

    @pl.when(i == 0)
    def _():
        for cp in fetch(0, 0):
            cp.start()

    @pl.when(i < nused_ref[0])
    def _():
        g = tg_ref[i]
        b = g % 2

        @pl.when((i == 0) | (g != tg_ref[jnp.maximum(i - 1, 0)]))
        def _():
            @pl.when(g + 1 < ng_ref[0])
            def _():
                for cp in fetch(g + 1, 1 - b):
                    cp.start()

            for cp in fetch(g, b):
                cp.wait()
            wgu_bf[...] = wgu_f32[b].astype(BF16)
            wd_bf[...] = wd_f32[b].astype(BF16)

        for part in range(EXPERT_TILE // EXPERT_PART):
            @pl.when(rows_ref[i] > part * EXPERT_PART)
            def _():
                sl = slice(part * EXPERT_PART, (part + 1) * EXPERT_PART)
                gu = _dot(x_ref[sl, :], wgu_bf[...]) + bgu_ref[...]
                gate = jnp.minimum(gu[:, :D_FF], SWIGLU_LIMIT)
                up = jnp.clip(gu[:, D_FF:], -SWIGLU_LIMIT, SWIGLU_LIMIT)
                act = (up + 1.0) * gate * jax.nn.sigmoid(SWIGLU_ALPHA * gate)
                y_ref[sl, :] = (_dot(act.astype(BF16), wd_bf[...]) + bd_ref[...]).astype(y_ref.dtype)


def _experts(xg, tile_group, group_expert, n_groups, n_used, tile_rows, wgu, bgu, wd, bd):
    rows_max = xg.shape[0]
    used = lambda i, nu: jnp.maximum(jnp.minimum(i, nu[0] - 1), 0)
    row = lambda i, tg, ge, ng, nu, tr: (used(i, nu), 0)
    exp = lambda i, tg, ge, ng, nu, tr: (ge[tg[used(i, nu)]], 0, 0)
    return pl.pallas_call(
        _expert_kernel,
        grid_spec=pltpu.PrefetchScalarGridSpec(
            num_scalar_prefetch=5,
            grid=(rows_max // EXPERT_TILE,),
            in_specs=[pl.BlockSpec((EXPERT_TILE, D_MODEL), row),
                      pl.BlockSpec((None, 1, 2 * D_FF), exp),
                      pl.BlockSpec((None, 1, D_MODEL), exp),
                      pl.BlockSpec(memory_space=pl.ANY),
                      pl.BlockSpec(memory_space=pl.ANY)],
            out_specs=pl.BlockSpec((EXPERT_TILE, D_MODEL), row),
            scratch_shapes=[pltpu.VMEM((2, D_MODEL, 2 * D_FF), F32), pltpu.VMEM((2, D_FF, D_MODEL), F32),
                            pltpu.VMEM((D_MODEL, 2 * D_FF), BF16), pltpu.VMEM((D_FF, D_MODEL), BF16),
                            pltpu.SemaphoreType.DMA((2, 2))]),
        out_shape=jax.ShapeDtypeStruct((rows_max, D_MODEL), BF16),
        compiler_params=_params("arbitrary"),
        name="experts",
    )(tile_group, group_expert, n_groups, n_used, tile_rows, xg, bgu, bd, wgu, wd)


def _combine_kernel(nu_ref, src_ref, slot_ref, gatek_ref, x_ref, gmlp_ref, y_ref, o_ref, ys_ref, sem,
                    *, tile_offset):
    j = pl.program_id(0)
    t = j + tile_offset
    buf = j % 2
    tm = x_ref.shape[0]

    def unit_copy(tile, b, i):
        src = pl.multiple_of(src_ref[tile, i] * ROW_UNIT, ROW_UNIT)
        dst = pl.multiple_of(i * ROW_UNIT, ROW_UNIT)
        return pltpu.make_async_copy(y_ref.at[pl.ds(src, ROW_UNIT), :],
                                     ys_ref.at[b, pl.ds(dst, ROW_UNIT), :], sem.at[b])

    def fetch(tile, b):
        n_units = nu_ref[tile]
        lax.fori_loop(0, n_units, lambda i, c: (unit_copy(tile, b, i).start(), c)[1], 0)

        def zero_unit(i, c):
            ys_ref[b, pl.ds(pl.multiple_of(i * ROW_UNIT, ROW_UNIT), ROW_UNIT), :] = jnp.zeros(
                (ROW_UNIT, D_MODEL), ys_ref.dtype)
            return c

        lax.fori_loop(n_units, LOCAL_ROWS // ROW_UNIT, zero_unit, 0)

    @pl.when(j == 0)
    def _():
        fetch(t, buf)

    @pl.when(j + 1 < pl.num_programs(0))
    def _():
        fetch(t + 1, 1 - buf)

    lax.fori_loop(0, nu_ref[t], lambda i, c: (unit_copy(t, buf, i).wait(), c)[1], 0)

    slots = slot_ref[...].astype(jnp.int32)
    gates = gatek_ref[...]
    slot_cols = [slots[:, k:k + 1] for k in range(TOP_K)]
    gate_cols = [gates[:, k:k + 1] for k in range(TOP_K)]
    acc = jnp.zeros((tm, D_MODEL), F32)
    for c in range(LOCAL_ROWS // tm):
        acc = acc + _dot(_slot_matrix(slot_cols, gate_cols, c), ys_ref[buf, c * tm:(c + 1) * tm, :])
    o_ref[...] = x_ref[...] + gmlp_ref[...] * acc


def _combine(y, slots, gates, x1, gmlp, n_units, unit_src, tile_offset, per_token, tiles_per_seq):
    n = x1.shape[0]
    if per_token:
        mod = pl.BlockSpec((TOKEN_TILE, D_MODEL), lambda i, *_: (i, 0))
    else:
        mod = pl.BlockSpec((None, 1, D_MODEL), lambda i, *_: (i // tiles_per_seq, 0, 0))
    wide = pl.BlockSpec((TOKEN_TILE, D_MODEL), lambda i, *_: (i, 0))
    narrow = pl.BlockSpec((TOKEN_TILE, LANES), lambda i, *_: (i + tile_offset, 0))
    return pl.pallas_call(
        functools.partial(_combine_kernel, tile_offset=tile_offset),
        grid_spec=pltpu.PrefetchScalarGridSpec(
            num_scalar_prefetch=2,
            grid=(n // TOKEN_TILE,),
            in_specs=[narrow, narrow, wide, mod, pl.BlockSpec(memory_space=pl.ANY)],
            out_specs=wide,
            scratch_shapes=[pltpu.VMEM((2, LOCAL_ROWS, D_MODEL), BF16), pltpu.SemaphoreType.DMA((2,))]),
        out_shape=jax.ShapeDtypeStruct((n, D_MODEL), F32),
        compiler_params=_params("arbitrary"),
        name="combine",
    )(n_units, unit_src, slots, gates, x1, gmlp, y)


def _route_tables(cnt, rows_max):
    units = (cnt + ROW_UNIT - 1) // ROW_UNIT
    group_units = jnp.sum(units, axis=0)
    upt = EXPERT_TILE // ROW_UNIT
    group_pad = (group_units + upt - 1) // upt * upt
    group_end = jnp.cumsum(group_pad)
    group_start = group_end - group_pad
    seg_start = group_start[None, :] + jnp.cumsum(units, axis=0) - units
    local_end = jnp.cumsum(units, axis=1)
    local_start = local_end - units
    n_units = local_end[:, -1].astype(jnp.int32)
    i = jnp.arange(LOCAL_ROWS // ROW_UNIT, dtype=jnp.int32)
    e_of = jnp.minimum(jnp.sum(i[None, :, None] >= local_end[:, None, :], axis=-1), N_EXPERTS - 1)
    unit_hbm = (jnp.take_along_axis(seg_start, e_of, axis=1)
                + i[None, :] - jnp.take_along_axis(local_start, e_of, axis=1)).astype(jnp.int32)
    unit_hbm = jnp.clip(unit_hbm, 0, rows_max // ROW_UNIT - 1)

    n_tail_e = group_pad - group_units
    j = jnp.arange(N_EXPERTS * upt, dtype=jnp.int32)
    tail_end = jnp.cumsum(n_tail_e)
    te_of = jnp.minimum(jnp.sum(j[:, None] >= tail_end[None, :], axis=-1), N_EXPERTS - 1)
    tail_hbm = (group_start + group_units)[te_of] + j - (tail_end - n_tail_e)[te_of]
    tail_hbm = jnp.clip(tail_hbm, 0, rows_max // ROW_UNIT - 1).astype(jnp.int32)
    n_tail = tail_end[-1:].astype(jnp.int32)

    r = jnp.arange(rows_max // EXPERT_TILE, dtype=jnp.int32)
    tile_expert = jnp.minimum(jnp.sum(r[:, None] * upt >= group_end[None, :], axis=-1), N_EXPERTS - 1)
    n_used = (group_end[-1:] // upt).astype(jnp.int32)
    tile_rows = jnp.clip(((group_start + group_units)[tile_expert] - r * upt) * ROW_UNIT,
                         0, EXPERT_TILE).astype(jnp.int32)
    nonempty = group_units > 0
    group_of_expert = jnp.cumsum(nonempty) - 1
    tile_group = group_of_expert[tile_expert].astype(jnp.int32)
    group_expert = jnp.argsort(jnp.logical_not(nonempty), stable=True).astype(jnp.int32)
    n_groups = jnp.sum(nonempty)[None].astype(jnp.int32)
    return (n_units, unit_hbm, n_tail, tail_hbm,
            tile_group, group_expert, n_groups, n_used, tile_rows)


def _rope_tables(pos):
    half = HEAD_DIM // 2
    inv = ROPE_THETA ** (-jnp.arange(half, dtype=F32) / half)
    ang = pos.astype(F32)[:, None] * inv
    cos, sin = jnp.cos(ang), jnp.sin(ang)
    reps = LANES // HEAD_DIM
    return (jnp.tile(jnp.concatenate([cos, cos], axis=1), (1, reps)),
            jnp.tile(jnp.concatenate([-sin, sin], axis=1), (1, reps)))


def kernel(x_prompt, x_sample, c_prompt, c_sample, state_gla, cache_swa_k, cache_swa_v, w_ada, b_ada,
           norm1_g, norm2_g, w_in, w_gk2, b_gk, gla_norm_g, q_norm_g, k_norm_g, attn_sinks, w_o,
           w_router, b_router, w_gate_up, b_gate_up, w_down, b_down):
    batch, seq, d = x_prompt.shape
    n_seq, n_real, _ = x_sample.shape
    depth = w_in.shape[0]
    assert depth == 1 and d == D_MODEL and n_real <= SAMPLE_ROWS
    assert seq % TOKEN_TILE == 0 and (n_seq * SAMPLE_ROWS) % TOKEN_TILE == 0 and n_seq % SAMPLE_SEQS == 0
    R = SAMPLE_ROWS
    kvw = SWA_KV_HEADS * HEAD_DIM
    tiles_per_seq = seq // TOKEN_TILE

    qg, kg, vg, rg, glr_w, qs, ks, vs, ag, as_ = jnp.split(
        w_in[0], [512, 1024, 2048, 3072, 3088, 4112, 4368, 4624, 5648], axis=1)
    w_main = jnp.concatenate([vg, rg, qs, ag, as_, qg, kg, ks, vs], axis=1).astype(BF16)
    w_glr = jnp.pad(glr_w, ((0, 0), (0, LANES - GATE_RANK))).astype(BF16)
    wgk = jnp.pad(w_gk2[0], ((0, LANES - GATE_RANK), (0, 0)))
    bgk = b_gk[0].reshape(1, -1)
    gnorm = gla_norm_g[0].reshape(1, -1)
    gq = jnp.tile(q_norm_g[0], LANES // HEAD_DIM).reshape(1, LANES)
    gk = jnp.tile(k_norm_g[0], LANES // HEAD_DIM).reshape(1, LANES)
    seg = jnp.asarray(np.kron(np.eye(LANES // HEAD_DIM), np.ones((HEAD_DIM, HEAD_DIM))), BF16)
    sinks = attn_sinks[0]
    wo = w_o[0].astype(BF16)
    wr = jnp.pad(w_router[0], ((0, 0), (0, LANES - N_EXPERTS)))
    br = jnp.pad(b_router[0], (0, LANES - N_EXPERTS), constant_values=-1e30).reshape(1, LANES)
    bgu = b_gate_up[0].reshape(N_EXPERTS, 1, 2 * D_FF)
    bd = b_down[0].reshape(N_EXPERTS, 1, D_MODEL)
    g1 = norm1_g[0].reshape(1, -1)
    g2 = norm2_g[0].reshape(1, -1)

    n_c = batch + n_seq
    c_all = jnp.pad(jnp.concatenate([c_prompt, c_sample], axis=0), ((0, -n_c % 8), (0, 0)))
    m_all = _ada(c_all, w_ada[0], b_ada[0])
    mp = [m_all[:batch, i * d:(i + 1) * d].reshape(batch, 1, d) for i in range(6)]
    ms = [jnp.repeat(m_all[batch:n_c, i * d:(i + 1) * d], R, axis=0) for i in range(6)]

    xp = x_prompt.reshape(batch * seq, d)
    xs = jnp.pad(x_sample, ((0, 0), (0, R - n_real), (0, 0))).reshape(n_seq * R, d)
    cos_p, sin_p = _rope_tables(jnp.arange(seq))
    cos_s, sin_s = _rope_tables(PAST_LEN + jnp.tile(jnp.arange(R), SAMPLE_SEQS))
    sink_rows = jnp.repeat(sinks, R).reshape(SWA_HEADS * R, 1)

    yp, glr_p = _inproj(xp, mp[0], mp[1], g1, w_main, w_glr, BF16, False, tiles_per_seq)
    ys, glr_s = _inproj(xs, ms[0], ms[1], g1, w_main, w_glr, F32, True, 1)
    og_p, st_p = _gla_prompt(yp, glr_p, wgk, bgk, gnorm, batch, seq)
    og_s, st_s = _gla_sample(ys, glr_s, wgk, bgk, gnorm, state_gla[0], n_seq, n_real)
    os_p, kc_p, vc_p = _swa_prompt(yp, sinks, cos_p, sin_p, gq, gk, seg, batch, seq)
    os_s, kc_s, vc_s = _swa_sample(ys, sink_rows, cache_swa_k[0].reshape(n_seq, WINDOW, kvw),
                                   cache_swa_v[0].reshape(n_seq, WINDOW, kvw),
                                   cos_s, sin_s, gq, gk, seg, n_seq, n_real)
    x1_p, h2_p, sl_p, gt_p, cnt_p = _post(yp, og_p, os_p, xp, mp[2], mp[3], mp[4], g2, wo, wr, br,
                                          False, tiles_per_seq, R)
    x1_s, h2_s, sl_s, gt_s, cnt_s = _post(ys, og_s, os_s, xs, ms[2], ms[3], ms[4], g2, wo, wr, br,
                                          True, 1, n_real)

    h2 = jnp.concatenate([h2_p, h2_s], axis=0)
    slots = jnp.concatenate([sl_p, sl_s], axis=0)
    gates = jnp.concatenate([gt_p, gt_s], axis=0)
    cnt = jnp.concatenate([cnt_p, cnt_s], axis=0)[:, 0, :N_EXPERTS].astype(jnp.int32)
    n_tiles = cnt.shape[0]
    rows_bound = (TOP_K * (batch * seq + n_seq * n_real) + n_tiles * N_EXPERTS * (ROW_UNIT - 1)
                  + N_EXPERTS * (EXPERT_TILE - 1))
    rows_max = -(-rows_bound // EXPERT_TILE) * EXPERT_TILE
    (n_units, unit_hbm, n_tail, tail_hbm,
     tile_group, group_expert, n_groups, n_used, tile_rows) = _route_tables(cnt, rows_max)
    xg = _dispatch(h2, slots, n_units, unit_hbm, n_tail, tail_hbm, rows_max)
    yg = _experts(xg, tile_group, group_expert, n_groups, n_used, tile_rows,
                  w_gate_up[0], bgu, w_down[0], bd)
    p_tiles = batch * seq // TOKEN_TILE
    out_p = _combine(yg, slots, gates, x1_p, mp[5], n_units, unit_hbm, 0, False, tiles_per_seq)
    out_s = _combine(yg, slots, gates, x1_s, ms[5], n_units, unit_hbm, p_tiles, True, 1)

    cache_shape = (WINDOW, SWA_KV_HEADS, HEAD_DIM)
    return (out_p.reshape(batch, seq, d),
            out_s.reshape(n_seq, R, d)[:, :n_real],
            st_p[None],
            kc_p.reshape(1, batch, *cache_shape),
            vc_p.reshape(1, batch, *cache_shape),
            st_s[None],
            kc_s.reshape(1, n_seq, *cache_shape),
            vc_s.reshape(1, n_seq, *cache_shape))
```

```python
import functools

import numpy as np
import jax
import jax.numpy as jnp
from jax import lax
from jax.experimental import pallas as pl
from jax.experimental.pallas import tpu as pltpu

F32 = jnp.float32
BF16 = jnp.bfloat16
HIGHEST = lax.Precision.HIGHEST

D_MODEL = 1024
PAST_LEN = 16384
GLA_HEADS = 4
GLA_DK = 128
GLA_DV = 256
GATE_RANK = 16
GATE_TAU = 16.0
SWA_HEADS = 16
SWA_KV_HEADS = 4
HEAD_DIM = 64
SWA_GROUP = SWA_HEADS // SWA_KV_HEADS
WINDOW = 128
ROPE_THETA = 10000.0
N_EXPERTS = 32
TOP_K = 4
D_FF = 1024
SWIGLU_ALPHA = 1.702
SWIGLU_LIMIT = 7.0
NORM_EPS = 1e-6

LANES = 128
SUBLANES = 8
SAMPLE_ROWS = 8
SAMPLE_SEQS = 16
TOKEN_TILE = 512
GLA_CHUNK = 64
GLA_BLOCK = 256
VMEM_LIMIT = 56 * 1024 * 1024
ROW_UNIT = 16
EXPERT_TILE = 512
EXPERT_PART = 256
LOCAL_ROWS = -(-(TOP_K * TOKEN_TILE + N_EXPERTS * (ROW_UNIT - 1)) // TOKEN_TILE) * TOKEN_TILE

COL_VG, COL_RG, COL_QS, COL_AG, COL_AS = 0, 1024, 2048, 3072, 4096
COL_QG, COL_KG, COL_KS, COL_VS = 5120, 5632, 6144, 6400
D_MAIN = 6656
PROJ_CHUNK = 512


def _dot(a, b, precision=None):
    return jnp.dot(a, b, preferred_element_type=F32, precision=precision)


def _dot_nt(a, b, precision=None):
    return lax.dot_general(a, b, (((1,), (1,)), ((), ())), preferred_element_type=F32, precision=precision)


def _dot_tn(a, b, precision=None):
    return lax.dot_general(a, b, (((0,), (0,)), ((), ())), preferred_element_type=F32, precision=precision)


def _dot_sum(sel, x):
    hi = x.astype(BF16)
    r1 = x - hi.astype(F32)
    mid = r1.astype(BF16)
    lo = (r1 - mid.astype(F32)).astype(BF16)
    return _dot(sel, hi) + _dot(sel, mid) + _dot(sel, lo)


def _params(*sem):
    return pltpu.CompilerParams(dimension_semantics=sem, vmem_limit_bytes=VMEM_LIMIT)


def _rms(x, g):
    return x * lax.rsqrt(jnp.mean(x * x, axis=-1, keepdims=True) + NORM_EPS) * g


def _log_sigmoid(x):
    return jnp.minimum(x, 0.0) - jnp.log(1.0 + jnp.exp(-jnp.abs(x)))


def _ada_kernel(c_ref, w_ref, b_ref, o_ref):
    c = c_ref[...]
    s = c * jax.nn.sigmoid(c)
    o_ref[...] = _dot(s, w_ref[...], HIGHEST) + b_ref[...]


def _ada(c_all, w_ada, b_ada):
    rows = c_all.shape[0]
    tn = 768
    return pl.pallas_call(
        _ada_kernel,
        grid=(6 * D_MODEL // tn,),
        in_specs=[pl.BlockSpec((rows, D_MODEL), lambda j: (0, 0)),
                  pl.BlockSpec((D_MODEL, tn), lambda j: (0, j)),
                  pl.BlockSpec((1, tn), lambda j: (0, j))],
        out_specs=pl.BlockSpec((rows, tn), lambda j: (0, j)),
        out_shape=jax.ShapeDtypeStruct((rows, 6 * D_MODEL), F32),
        compiler_params=_params("parallel"),
        name="ada",
    )(c_all, w_ada, b_ada.reshape(1, -1))


def _inproj_kernel(x_ref, shift_ref, scale_ref, g_ref, w_ref, wg_ref, o_ref, og_ref):
    h = _rms(x_ref[...], g_ref[...]) * (1.0 + scale_ref[...]) + shift_ref[...]
    hb = h.astype(BF16)
    for j in range(D_MAIN // PROJ_CHUNK):
        sl = slice(j * PROJ_CHUNK, (j + 1) * PROJ_CHUNK)
        o_ref[:, sl] = _dot(hb, w_ref[:, sl]).astype(o_ref.dtype)
    og_ref[...] = _dot(hb, wg_ref[...])


def _mod_spec(per_token, tiles_per_seq):
    if per_token:
        return pl.BlockSpec((TOKEN_TILE, D_MODEL), lambda i: (i, 0))
    return pl.BlockSpec((None, 1, D_MODEL), lambda i: (i // tiles_per_seq, 0, 0))


def _inproj(x, shift, scale, g, w_main, w_glr, out_dtype, per_token, tiles_per_seq):
    n = x.shape[0]
    mod = _mod_spec(per_token, tiles_per_seq)
    const = lambda i: (0, 0)
    return pl.pallas_call(
        _inproj_kernel,
        grid=(n // TOKEN_TILE,),
        in_specs=[pl.BlockSpec((TOKEN_TILE, D_MODEL), lambda i: (i, 0)), mod, mod,
                  pl.BlockSpec((1, D_MODEL), const),
                  pl.BlockSpec((D_MODEL, D_MAIN), const, pipeline_mode=pl.Buffered(1)),
                  pl.BlockSpec((D_MODEL, LANES), const, pipeline_mode=pl.Buffered(1))],
        out_specs=[pl.BlockSpec((TOKEN_TILE, D_MAIN), lambda i: (i, 0)),
                   pl.BlockSpec((TOKEN_TILE, LANES), lambda i: (i, 0))],
        out_shape=[jax.ShapeDtypeStruct((n, D_MAIN), out_dtype),
                   jax.ShapeDtypeStruct((n, LANES), F32)],
        compiler_params=_params("parallel"),
        name="inproj",
    )(x, shift, scale, g, w_main, w_glr)


def _gla_log_gate(glr, wgk, bgk):
    return _log_sigmoid(_dot(glr, wgk, HIGHEST) + bgk) * (1.0 / GATE_TAU)


def _gla_out(o, r, g):
    r = r.astype(F32)
    return _rms(o, g) * (r * jax.nn.sigmoid(r))


def _gla_pair_levels(n):
    t = np.arange(n)[:, None]
    s = np.arange(n)[None, :]
    x = t ^ s
    top = np.where(x > 0, 1 << np.floor(np.log2(np.maximum(x, 1))).astype(np.int64), 0)
    return np.where(s > t, -1, top).astype(np.int32)


def _gla_block_ref(b, h):
    n, w = b.shape
    if 2 * h == n:
        return jnp.broadcast_to(b[h - 1:h, :], (n, w))
    if h >= SUBLANES // 2:
        picked = b.reshape(n // (2 * h), 2 * h, w)[:, h - 1:h, :]
        return jnp.broadcast_to(picked, (n // (2 * h), 2 * h, w)).reshape(n, w)
    r = lax.broadcasted_iota(jnp.int32, (n, 1), 0) % (2 * h)
    out = b
    for d in range(1, h + 1):
        out = jnp.where(r == h - 1 + d, pltpu.roll(b, d, 0), out)
    for d in range(1, h):
        out = jnp.where(r == h - 1 - d, pltpu.roll(b, n - d, 0), out)
    return out


def _gla_prompt_kernel(q_ref, k_ref, v_ref, r_ref, glr_ref, wgk_ref, bgk_ref, g_ref, lev_ref,
                       o_ref, s_ref, st_ref):
    c = pl.program_id(1)

    @pl.when(c == 0)
    def _():
        st_ref[...] = jnp.zeros_like(st_ref)

    n = GLA_BLOCK
    lev = lev_ref[...]
    causal = lax.broadcasted_iota(jnp.int32, (n, n), 1) <= lax.broadcasted_iota(jnp.int32, (n, n), 0)
    lg = _gla_log_gate(glr_ref[...], wgk_ref[...], bgk_ref[...])
    b = _dot_sum(jnp.where(causal, 1.0, 0.0).astype(BF16), lg)
    b_last = b[n - 1:n, :]
    q = q_ref[...].astype(F32) * (GLA_DK ** -0.5)
    k = k_ref[...].astype(F32)
    qe = (q * jnp.exp(b)).astype(BF16)
    kd = (k * jnp.exp(b_last - b)).astype(BF16)
    decay = jnp.exp(b_last)
    levels = [0] + [1 << p for p in range(n.bit_length() - 1)]
    q_lv, k_lv = [q.astype(BF16)], [k.astype(BF16)]
    for h in levels[1:]:
        e = jnp.exp(-jnp.abs(b - _gla_block_ref(b, h)))
        q_lv.append((q * e).astype(BF16))
        k_lv.append((k * e).astype(BF16))
    for h in range(GLA_HEADS):
        dk = slice(h * GLA_DK, (h + 1) * GLA_DK)
        dv = slice(h * GLA_DV, (h + 1) * GLA_DV)
        attn = jnp.zeros((n, n), F32)
        for level, ql, kl in zip(levels, q_lv, k_lv):
            attn = jnp.where(lev == level, _dot_nt(ql[:, dk], kl[:, dk]), attn)
        v = v_ref[:, dv]
        st = st_ref[h]
        o = _dot_nt(qe[:, dk], st.astype(BF16)) + _dot(attn.astype(BF16), v)
        st_ref[h] = st * decay[:, dk] + _dot_tn(v, kd[:, dk])
        o_ref[:, dv] = _gla_out(o, r_ref[:, dv], g_ref[...]).astype(o_ref.dtype)

    @pl.when(c == pl.num_programs(1) - 1)
    def _():
        for h in range(GLA_HEADS):
            s_ref[h] = st_ref[h].T


def _gla_prompt(yp, glr, wgk, bgk, gnorm, batch, seq):
    nb = seq // GLA_BLOCK
    hk, hv = GLA_HEADS * GLA_DK, GLA_HEADS * GLA_DV
    tok = lambda b, c: b * nb + c
    const = lambda b, c: (0, 0)
    return pl.pallas_call(
        _gla_prompt_kernel,
        grid=(batch, nb),
        in_specs=[pl.BlockSpec((GLA_BLOCK, hk), lambda b, c: (tok(b, c), COL_QG // hk)),
                  pl.BlockSpec((GLA_BLOCK, hk), lambda b, c: (tok(b, c), COL_KG // hk)),
                  pl.BlockSpec((GLA_BLOCK, hv), lambda b, c: (tok(b, c), COL_VG // hv)),
                  pl.BlockSpec((GLA_BLOCK, hv), lambda b, c: (tok(b, c), COL_RG // hv)),
                  pl.BlockSpec((GLA_BLOCK, LANES), lambda b, c: (tok(b, c), 0)),
                  pl.BlockSpec((LANES, hk), const),
                  pl.BlockSpec((1, hk), const),
                  pl.BlockSpec((1, GLA_DV), const),
                  pl.BlockSpec((GLA_BLOCK, GLA_BLOCK), const)],
        out_specs=[pl.BlockSpec((GLA_BLOCK, hv), lambda b, c: (tok(b, c), 0)),
                   pl.BlockSpec((None, GLA_HEADS, GLA_DK, GLA_DV), lambda b, c: (b, 0, 0, 0))],
        out_shape=[jax.ShapeDtypeStruct((batch * seq, hv), BF16),
                   jax.ShapeDtypeStruct((batch, GLA_HEADS, GLA_DK, GLA_DV), F32)],
        scratch_shapes=[pltpu.VMEM((GLA_HEADS, GLA_DV, GLA_DK), F32)],
        compiler_params=_params("parallel", "arbitrary"),
        name="gla_prompt",
    )(yp, yp, yp, yp, glr, wgk, bgk, gnorm, jnp.asarray(_gla_pair_levels(GLA_BLOCK)))


def _gla_sample_kernel(q_ref, k_ref, v_ref, r_ref, glr_ref, wgk_ref, bgk_ref, g_ref, s0_ref, lev_ref,
                       o_ref, s_ref, *, n_real):
    R = SAMPLE_ROWS
    rows = q_ref.shape[0]
    row = lax.broadcasted_iota(jnp.int32, (rows, rows), 0)
    col = lax.broadcasted_iota(jnp.int32, (rows, rows), 1)
    same = (row // R) == (col // R)
    causal = same & (col <= row)
    real = lax.broadcasted_iota(jnp.int32, (rows, 1), 0) % R < n_real
    lg = jnp.where(real, _gla_log_gate(glr_ref[...], wgk_ref[...], bgk_ref[...]), 0.0)
    b = _dot_sum(jnp.where(causal, 1.0, 0.0).astype(BF16), lg)
    b_last = _dot_sum(jnp.where(same, 1.0, 0.0).astype(BF16), lg)
    q = q_ref[...] * (GLA_DK ** -0.5)
    k = jnp.where(real, k_ref[...], 0.0)
    v = v_ref[...]
    qe = q * jnp.exp(b)
    kd = k * jnp.exp(b_last - b)
    lev = lev_ref[...]
    levels = [0] + [1 << p for p in range(R.bit_length() - 1)]
    q_lv, k_lv = [q], [k]
    for h in levels[1:]:
        e = jnp.exp(-jnp.abs(b - _gla_block_ref(b, h)))
        q_lv.append(q * e)
        k_lv.append(k * e)
    for h in range(GLA_HEADS):
        dk = slice(h * GLA_DK, (h + 1) * GLA_DK)
        dv = slice(h * GLA_DV, (h + 1) * GLA_DV)
        attn = jnp.zeros((rows, rows), F32)
        for level, ql, kl in zip(levels, q_lv, k_lv):
            attn = jnp.where(lev == level, _dot_nt(ql[:, dk], kl[:, dk]), attn)
        o_intra = _dot(attn, v[:, dv])
        decay_t = jnp.exp(b_last[:, dk]).T
        outs = []
        for s in range(rows // R):
            sl = slice(s * R, (s + 1) * R)
            s0 = s0_ref[s, h]
            outs.append(_dot(qe[sl, dk], s0) + o_intra[sl, :])
            s_ref[s, h] = s0 * decay_t[:, s * R:s * R + 1] + _dot_tn(kd[sl, dk], v[sl, dv])
        o_ref[:, dv] = _gla_out(jnp.concatenate(outs, axis=0), r_ref[:, dv], g_ref[...])


def _gla_sample(ys, glr, wgk, bgk, gnorm, state, n_seq, n_real):
    R = SAMPLE_SEQS * SAMPLE_ROWS
    hk, hv = GLA_HEADS * GLA_DK, GLA_HEADS * GLA_DV
    st_spec = pl.BlockSpec((SAMPLE_SEQS, GLA_HEADS, GLA_DK, GLA_DV), lambda b: (b, 0, 0, 0))
    lev = _gla_pair_levels(R)
    lev = np.where(lev >= SAMPLE_ROWS, -1, lev)
    return pl.pallas_call(
        functools.partial(_gla_sample_kernel, n_real=n_real),
        grid=(n_seq // SAMPLE_SEQS,),
        in_specs=[pl.BlockSpec((R, hk), lambda b: (b, COL_QG // hk)),
                  pl.BlockSpec((R, hk), lambda b: (b, COL_KG // hk)),
                  pl.BlockSpec((R, hv), lambda b: (b, COL_VG // hv)),
                  pl.BlockSpec((R, hv), lambda b: (b, COL_RG // hv)),
                  pl.BlockSpec((R, LANES), lambda b: (b, 0)),
                  pl.BlockSpec((LANES, hk), lambda b: (0, 0)),
                  pl.BlockSpec((1, hk), lambda b: (0, 0)),
                  pl.BlockSpec((1, GLA_DV), lambda b: (0, 0)),
                  st_spec,
                  pl.BlockSpec((R, R), lambda b: (0, 0))],
        out_specs=[pl.BlockSpec((R, hv), lambda b: (b, 0)), st_spec],
        out_shape=[jax.ShapeDtypeStruct((n_seq * SAMPLE_ROWS, hv), F32),
                   jax.ShapeDtypeStruct((n_seq, GLA_HEADS, GLA_DK, GLA_DV), F32)],
        compiler_params=_params("parallel"),
        name="gla_sample",
    )(ys, ys, ys, ys, glr, wgk, bgk, gnorm, state, jnp.asarray(lev))


def _lane_lower(shape):
    return lax.broadcasted_iota(jnp.int32, shape, len(shape) - 1) % LANES < HEAD_DIM


def _head_norm_rope(x, g, cos, sin, seg):
    sq = x * x
    hi = sq.astype(BF16)
    lo = (sq - hi.astype(F32)).astype(BF16)
    ss = _dot(hi, seg) + _dot(lo, seg)
    y = x * lax.rsqrt(ss * (1.0 / HEAD_DIM) + NORM_EPS) * g
    half = HEAD_DIM // 2
    lane = lax.broadcasted_iota(jnp.int32, y.shape, 1)
    rot = jnp.where(lane % HEAD_DIM < half, pltpu.roll(y, LANES - half, 1), pltpu.roll(y, half, 1))
    return y * cos + rot * sin


def _both_halves(blk, half):
    sw = pltpu.roll(blk, HEAD_DIM, 1)
    lower = _lane_lower(blk.shape)
    return jnp.where(lower, blk, sw) if half == 0 else jnp.where(lower, sw, blk)


def _stack_heads(q_blocks):
    parts = []
    for qb in q_blocks:
        lower = _lane_lower(qb.shape)
        zero = jnp.zeros_like(qb)
        parts += [jnp.where(lower, qb, zero), jnp.where(lower, zero, qb)]
    return jnp.concatenate(parts, axis=0)


def _swa_prompt_kernel(sink_ref, q_ref, k_ref, v_ref, cos_ref, sin_ref, gq_ref, gk_ref, seg_ref,
                       o_ref, ko_ref, vo_ref, kprev_ref, vprev_ref):
    n = pl.program_id(1)
    W = WINDOW

    @pl.when(n == 0)
    def _():
        kprev_ref[...] = jnp.zeros_like(kprev_ref)
        vprev_ref[...] = jnp.zeros_like(vprev_ref)

    cos, sin, seg = cos_ref[...], sin_ref[...], seg_ref[...]
    k_cur = jnp.concatenate(
        [_head_norm_rope(k_ref[:, j * LANES:(j + 1) * LANES].astype(F32), gk_ref[...], cos, sin, seg)
         for j in range(SWA_KV_HEADS * HEAD_DIM // LANES)], axis=1)
    v_cur = v_ref[...].astype(F32)
    ko_ref[...] = k_cur
    vo_ref[...] = v_cur
    k2 = jnp.concatenate([kprev_ref[...], k_cur], axis=0)
    v2 = jnp.concatenate([vprev_ref[...], v_cur], axis=0)
    kprev_ref[...] = k_cur
    vprev_ref[...] = v_cur

    qi = lax.broadcasted_iota(jnp.int32, (W, 2 * W), 0)
    ki = lax.broadcasted_iota(jnp.int32, (W, 2 * W), 1)
    dq = qi + W - ki
    mask = (dq >= 0) & (dq < W) & ((ki >= W) | (n > 0))
    for kh in range(SWA_KV_HEADS):
        blk = slice((kh // 2) * LANES, (kh // 2 + 1) * LANES)
        kb = _both_halves(k2[:, blk], kh % 2).astype(BF16)
        vb = _both_halves(v2[:, blk], kh % 2).astype(BF16)
        qblocks = []
        for j in range(2):
            c0 = (2 * kh + j) * LANES
            qn = _head_norm_rope(q_ref[:, c0:c0 + LANES].astype(F32), gq_ref[...], cos, sin, seg)
            qblocks.append(qn * (HEAD_DIM ** -0.5))
        qs = _stack_heads(qblocks).astype(BF16)
        s = _dot_nt(qs, kb)
        outs = []
        for g in range(SWA_GROUP):
            sg = jnp.where(mask, s[g * W:(g + 1) * W, :], -jnp.inf)
            sink = sink_ref[kh * SWA_GROUP + g]
            m = jnp.maximum(jnp.max(sg, axis=-1, keepdims=True), sink)
            p = jnp.exp(sg - m)
            denom = jnp.sum(p, axis=-1, keepdims=True) + jnp.exp(sink - m)
            outs.append(_dot(p.astype(BF16), vb) / denom)
        lower = _lane_lower((W, LANES))
        for j in range(2):
            c0 = (2 * kh + j) * LANES
            o_ref[:, c0:c0 + LANES] = jnp.where(lower, outs[2 * j], outs[2 * j + 1]).astype(o_ref.dtype)


def _swa_prompt(yp, sinks, cos, sin, gq, gk, seg, batch, seq):
    nb = seq // WINDOW
    kvw = SWA_KV_HEADS * HEAD_DIM
    tok = lambda b, n: b * nb + n
    const = lambda b, n: (0, 0)
    cache_spec = pl.BlockSpec((None, WINDOW, kvw), lambda b, n: (b, 0, 0))
    return pl.pallas_call(
        _swa_prompt_kernel,
        grid=(batch, nb),
        in_specs=[pl.BlockSpec(memory_space=pltpu.SMEM),
                  pl.BlockSpec((WINDOW, D_MODEL), lambda b, n: (tok(b, n), COL_QS // D_MODEL)),
                  pl.BlockSpec((WINDOW, kvw), lambda b, n: (tok(b, n), COL_KS // kvw)),
                  pl.BlockSpec((WINDOW, kvw), lambda b, n: (tok(b, n), COL_VS // kvw)),
                  pl.BlockSpec((WINDOW, LANES), lambda b, n: (n, 0)),
                  pl.BlockSpec((WINDOW, LANES), lambda b, n: (n, 0)),
                  pl.BlockSpec((1, LANES), const), pl.BlockSpec((1, LANES), const),
                  pl.BlockSpec((LANES, LANES), const)],
        out_specs=[pl.BlockSpec((WINDOW, D_MODEL), lambda b, n: (tok(b, n), 0)), cache_spec, cache_spec],
        out_shape=[jax.ShapeDtypeStruct((batch * seq, D_MODEL), BF16),
                   jax.ShapeDtypeStruct((batch, WINDOW, kvw), F32),
                   jax.ShapeDtypeStruct((batch, WINDOW, kvw), F32)],
        scratch_shapes=[pltpu.VMEM((WINDOW, kvw), F32), pltpu.VMEM((WINDOW, kvw), F32)],
        compiler_params=_params("parallel", "arbitrary"),
        name="swa_prompt",
    )(sinks, yp, yp, yp, cos, sin, gq, gk, seg)


def _shift_cache(cache, new, n_real):
    R = SAMPLE_ROWS
    rolled = pltpu.roll(cache, WINDOW - n_real, 0)
    tail_new = pltpu.roll(new, R - n_real, 0)
    row = lax.broadcasted_iota(jnp.int32, (R, cache.shape[1]), 0)
    tail = jnp.where(row < R - n_real, rolled[WINDOW - R:, :], tail_new)
    return jnp.concatenate([rolled[:WINDOW - R, :], tail], axis=0)


def _swa_sample_kernel(q_ref, k_ref, v_ref, kc_ref, vc_ref, cos_ref, sin_ref, gq_ref, gk_ref,
                       seg_ref, sink_ref, o_ref, ko_ref, vo_ref, *, n_real):
    R, W = SAMPLE_ROWS, WINDOW
    kvw = SWA_KV_HEADS * HEAD_DIM
    cos, sin, seg = cos_ref[...], sin_ref[...], seg_ref[...]
    k_new = jnp.concatenate(
        [_head_norm_rope(k_ref[:, j * LANES:(j + 1) * LANES], gk_ref[...], cos, sin, seg)
         for j in range(kvw // LANES)], axis=1)
    v_new = v_ref[...]
    q_pairs = [_head_norm_rope(q_ref[:, j * LANES:(j + 1) * LANES], gq_ref[...], cos, sin, seg)
               * (HEAD_DIM ** -0.5) for j in range(SWA_HEADS // 2)]

    hr = SWA_HEADS * R
    t_c = lax.broadcasted_iota(jnp.int32, (hr, W), 0) % R
    mask_c = lax.broadcasted_iota(jnp.int32, (hr, W), 1) > t_c
    t_n = lax.broadcasted_iota(jnp.int32, (hr, R), 0) % R
    mask_n = lax.broadcasted_iota(jnp.int32, (hr, R), 1) <= t_n
    sink = sink_ref[...]
    lower = _lane_lower((R, LANES))
    zeros = jnp.zeros((R, LANES), F32)
    for s in range(q_ref.shape[0] // R):
        sl = slice(s * R, (s + 1) * R)
        kc, vc = kc_ref[s], vc_ref[s]
        kn, vn = k_new[sl, :], v_new[sl, :]
        ko_ref[s] = _shift_cache(kc, kn, n_real)
        vo_ref[s] = _shift_cache(vc, vn, n_real)
        q_rows = []
        for h in range(SWA_HEADS):
            kh = h // SWA_GROUP
            x = q_pairs[h // 2][sl, :]
            if h % 2 != kh % 2:
                x = pltpu.roll(x, HEAD_DIM, 1)
            x = jnp.where(lower, x, zeros) if kh % 2 == 0 else jnp.where(lower, zeros, x)
            q_rows.append(jnp.concatenate([x, zeros] if kh // 2 == 0 else [zeros, x], axis=1))
        qbd = jnp.concatenate(q_rows, axis=0)
        sc = jnp.where(mask_c, _dot_nt(qbd, kc), -jnp.inf)
        sn = jnp.where(mask_n, _dot_nt(qbd, kn), -jnp.inf)
        m = jnp.maximum(jnp.maximum(jnp.max(sc, axis=-1, keepdims=True),
                                    jnp.max(sn, axis=-1, keepdims=True)), sink)
        pc, pn = jnp.exp(sc - m), jnp.exp(sn - m)
        denom = (jnp.sum(pc, axis=-1, keepdims=True) + jnp.sum(pn, axis=-1, keepdims=True)
                 + jnp.exp(sink - m))
        o = (_dot(pc, vc) + _dot(pn, vn)) / denom
        for j in range(SWA_HEADS // 2):
            halves = []
            for h in (2 * j, 2 * j + 1):
                kh = h // SWA_GROUP
                y = o[h * R:(h + 1) * R, (kh // 2) * LANES:(kh // 2 + 1) * LANES]
                halves.append(pltpu.roll(y, HEAD_DIM, 1) if h % 2 != kh % 2 else y)
            o_ref[sl, j * LANES:(j + 1) * LANES] = jnp.where(lower, halves[0], halves[1])


def _swa_sample(ys, sink_rows, kcache, vcache, cos, sin, gq, gk, seg, n_seq, n_real):
    R = SAMPLE_SEQS * SAMPLE_ROWS
    kvw = SWA_KV_HEADS * HEAD_DIM
    const = lambda b: (0, 0)
    cache_spec = pl.BlockSpec((SAMPLE_SEQS, WINDOW, kvw), lambda b: (b, 0, 0))
    return pl.pallas_call(
        functools.partial(_swa_sample_kernel, n_real=n_real),
        grid=(n_seq // SAMPLE_SEQS,),
        in_specs=[pl.BlockSpec((R, D_MODEL), lambda b: (b, COL_QS // D_MODEL)),
                  pl.BlockSpec((R, kvw), lambda b: (b, COL_KS // kvw)),
                  pl.BlockSpec((R, kvw), lambda b: (b, COL_VS // kvw)),
                  cache_spec, cache_spec,
                  pl.BlockSpec((R, LANES), const), pl.BlockSpec((R, LANES), const),
                  pl.BlockSpec((1, LANES), const), pl.BlockSpec((1, LANES), const),
                  pl.BlockSpec((LANES, LANES), const),
                  pl.BlockSpec((SWA_HEADS * SAMPLE_ROWS, 1), const)],
        out_specs=[pl.BlockSpec((R, D_MODEL), lambda b: (b, 0)), cache_spec, cache_spec],
        out_shape=[jax.ShapeDtypeStruct((n_seq * SAMPLE_ROWS, D_MODEL), F32),
                   jax.ShapeDtypeStruct((n_seq, WINDOW, kvw), F32),
                   jax.ShapeDtypeStruct((n_seq, WINDOW, kvw), F32)],
        compiler_params=_params("parallel"),
        name="swa_sample",
    )(ys, ys, ys, kcache, vcache, cos, sin, gq, gk, seg, sink_rows)


def _post_kernel(ag_ref, as_ref, og_ref, os_ref, x_ref, gate_ref, shift_ref, scale_ref, g2_ref,
                 wo_ref, wr_ref, br_ref, x1_ref, h2_ref, slot_ref, gatek_ref, cnt_ref, *, n_valid):
    merged =(jax.nn.sigmoid(ag_ref[...].astype(F32)) * og_ref[...].astype(F32)
              + jax.nn.sigmoid(as_ref[...].astype(F32)) * os_ref[...].astype(F32))
    y = _dot(merged.astype(BF16), wo_ref[...])
    x1 = x_ref[...] + gate_ref[...] * y
    x1_ref[...] = x1
    h2 = _rms(x1, g2_ref[...]) * (1.0 + scale_ref[...]) + shift_ref[...]
    h2_ref[...] = h2.astype(BF16)

    logits = _dot(h2, wr_ref[...], HIGHEST) + br_ref[...]
    lane_i = lax.broadcasted_iota(jnp.int32, logits.shape, 1)
    lane = lane_i.astype(F32)
    work = logits
    vals, hots = [], []
    for _ in range(TOP_K):
        m = jnp.max(work, axis=-1, keepdims=True)
        idx = jnp.min(jnp.where(work == m, lane, float(LANES)), axis=-1, keepdims=True)
        hot = lane == idx
        vals.append(m)
        hots.append(hot)
        work = jnp.where(hot, -jnp.inf, work)
    exps = [jnp.exp(v - vals[0]) for v in vals]
    denom = exps[0] + exps[1] + exps[2] + exps[3]

    tm = logits.shape[0]
    valid = lax.broadcasted_iota(jnp.int32, (tm, 1), 0) % SAMPLE_ROWS < n_valid
    sel = jnp.zeros_like(logits)
    for hot in hots:
        sel = jnp.where(hot, 1.0, sel)
    sel = jnp.where(valid, sel, 0.0)
    earlier = (lax.broadcasted_iota(jnp.int32, (tm, tm), 1)
               < lax.broadcasted_iota(jnp.int32, (tm, tm), 0))
    rank = _dot(jnp.where(earlier, 1.0, 0.0).astype(BF16), sel.astype(BF16))
    cnt = jnp.sum(sel, axis=0, keepdims=True)
    cnt_pad = jnp.floor((cnt + (ROW_UNIT - 1.0)) * (1.0 / ROW_UNIT)) * ROW_UNIT
    below = (lax.broadcasted_iota(jnp.int32, (LANES, LANES), 0)
             < lax.broadcasted_iota(jnp.int32, (LANES, LANES), 1))
    seg_start = _dot(jnp.broadcast_to(cnt_pad, (8, LANES)), jnp.where(below, 1.0, 0.0), HIGHEST)[0:1]
    pos = seg_start + rank
    slots = jnp.full_like(logits, -1.0)
    gates = jnp.zeros_like(logits)
    for k in range(TOP_K):
        s_k = jnp.sum(jnp.where(hots[k], pos, 0.0), axis=-1, keepdims=True)
        slots = jnp.where(lane_i == k, s_k, slots)
        gates = jnp.where(lane_i == k, exps[k] / denom, gates)
    slot_ref[...] = jnp.where(valid, slots, -1.0)
    gatek_ref[...] = gates
    cnt_ref[...] = cnt


def _post(y_all, o_gla, o_swa, x, gate, shift, scale, g2, wo, wr, br, per_token, tiles_per_seq, n_valid):
    n = x.shape[0]
    mod = _mod_spec(per_token, tiles_per_seq)
    row = lambda i: (i, 0)
    const = lambda i: (0, 0)
    wide = pl.BlockSpec((TOKEN_TILE, D_MODEL), row)
    narrow = pl.BlockSpec((TOKEN_TILE, LANES), row)
    return pl.pallas_call(
        functools.partial(_post_kernel, n_valid=n_valid),
        grid=(n // TOKEN_TILE,),
        in_specs=[pl.BlockSpec((TOKEN_TILE, D_MODEL), lambda i: (i, COL_AG // D_MODEL)),
                  pl.BlockSpec((TOKEN_TILE, D_MODEL), lambda i: (i, COL_AS // D_MODEL)),
                  wide, wide, wide, mod, mod, mod,
                  pl.BlockSpec((1, D_MODEL), const),
                  pl.BlockSpec((D_MODEL, D_MODEL), const),
                  pl.BlockSpec((D_MODEL, LANES), const),
                  pl.BlockSpec((1, LANES), const)],
        out_specs=[wide, wide, narrow, narrow, pl.BlockSpec((None, 1, LANES), lambda i: (i, 0, 0))],
        out_shape=[jax.ShapeDtypeStruct((n, D_MODEL), F32),
                   jax.ShapeDtypeStruct((n, D_MODEL), BF16),
                   jax.ShapeDtypeStruct((n, LANES), F32),
                   jax.ShapeDtypeStruct((n, LANES), F32),
                   jax.ShapeDtypeStruct((n // TOKEN_TILE, 1, LANES), F32)],
        compiler_params=_params("parallel"),
        name="post",
    )(y_all, y_all, o_gla, o_swa, x, gate, shift, scale, g2, wo, wr, br)


def _slot_matrix(slot_cols, weights, chunk):
    tm = slot_cols[0].shape[0]
    j = lax.broadcasted_iota(jnp.int32, (tm, tm), 1) + chunk * tm
    out = jnp.zeros((tm, tm), F32)
    for s, w in zip(slot_cols, weights):
        out = jnp.where(s == j, w, out)
    return out.astype(BF16)


def _dispatch_kernel(nu_ref, dst_ref, ntail_ref, tail_ref, h_ref, slot_ref, xg_ref,
                     sorted_ref, zero_ref, sem, tail_sem):
    t = pl.program_id(0)
    last = pl.num_programs(0) - 1
    buf = t % 2
    tm = h_ref.shape[0]
    slots = slot_ref[...].astype(jnp.int32)
    slot_cols = [slots[:, k:k + 1] for k in range(TOP_K)]
    h = h_ref[...]
    for c in range(LOCAL_ROWS // tm):
        onehot = _slot_matrix(slot_cols, [1.0] * TOP_K, c)
        sorted_ref[buf, c * tm:(c + 1) * tm, :] = _dot_tn(onehot, h).astype(BF16)

    def unit_copy(tile, b, i):
        src = pl.multiple_of(i * ROW_UNIT, ROW_UNIT)
        dst = pl.multiple_of(dst_ref[tile, i] * ROW_UNIT, ROW_UNIT)
        return pltpu.make_async_copy(sorted_ref.at[b, pl.ds(src, ROW_UNIT), :],
                                     xg_ref.at[pl.ds(dst, ROW_UNIT), :], sem.at[b])

    def start_all(tile, b):
        lax.fori_loop(0, nu_ref[tile], lambda i, c: (unit_copy(tile, b, i).start(), c)[1], 0)

    def wait_all(tile, b):
        lax.fori_loop(0, nu_ref[tile], lambda i, c: (unit_copy(tile, b, i).wait(), c)[1], 0)

    @pl.when(t > 0)
    def _():
        wait_all(t - 1, 1 - buf)

    start_all(t, buf)

    def tail_copy(i):
        dst = pl.multiple_of(tail_ref[i] * ROW_UNIT, ROW_UNIT)
        return pltpu.make_async_copy(zero_ref, xg_ref.at[pl.ds(dst, ROW_UNIT), :], tail_sem)

    @pl.when(t == last)
    def _():
        zero_ref[...] = jnp.zeros_like(zero_ref)
        n_tail = ntail_ref[0]
        lax.fori_loop(0, n_tail, lambda i, c: (tail_copy(i).start(), c)[1], 0)
        lax.fori_loop(0, n_tail, lambda i, c: (tail_copy(i).wait(), c)[1], 0)
        wait_all(t, buf)


def _dispatch(h2, slots, n_units, unit_dst, n_tail, tail_dst, rows_max):
    n = h2.shape[0]
    return pl.pallas_call(
        _dispatch_kernel,
        grid_spec=pltpu.PrefetchScalarGridSpec(
            num_scalar_prefetch=4,
            grid=(n // TOKEN_TILE,),
            in_specs=[pl.BlockSpec((TOKEN_TILE, D_MODEL), lambda t, *_: (t, 0)),
                      pl.BlockSpec((TOKEN_TILE, LANES), lambda t, *_: (t, 0))],
            out_specs=pl.BlockSpec(memory_space=pl.ANY),
            scratch_shapes=[pltpu.VMEM((2, LOCAL_ROWS, D_MODEL), BF16),
                            pltpu.VMEM((ROW_UNIT, D_MODEL), BF16),
                            pltpu.SemaphoreType.DMA((2,)), pltpu.SemaphoreType.DMA]),
        out_shape=jax.ShapeDtypeStruct((rows_max, D_MODEL), BF16),
        compiler_params=_params("arbitrary"),
        name="dispatch",
    )(n_units, unit_dst, n_tail, tail_dst, h2, slots)


def _expert_kernel(tg_ref, ge_ref, ng_ref, nused_ref, rows_ref, x_ref, bgu_ref, bd_ref, wgu_hbm, wd_hbm,
                   y_ref, wgu_f32, wd_f32, wgu_bf, wd_bf, sem):
    i = pl.program_id(0)

    def fetch(g, b):
        e = ge_ref[g]
        return (pltpu.make_async_copy(wgu_hbm.at[e], wgu_f32.at[b], sem.at[0, b]),
                pltpu.make_async_copy(wd_hbm.at[e], wd_f32.at[b], sem.at[1, b]))

    @pl.when(i == 0)
    def _():
        for cp in fetch(0, 0):
            cp.start()

    @pl.when(i < nused_ref[0])
    def _():
        g = tg_ref[i]
        b = g % 2

        @pl.when((i == 0) | (g != tg_ref[jnp.maximum(i - 1, 0)]))
        def _():
            @pl.when(g + 1 < ng_ref[0])
            def _():
                for cp in fetch(g + 1, 1 - b):
                    cp.start()

            for cp in fetch(g, b):
                cp.wait()
            wgu_bf[...] = wgu_f32[b].astype(BF16)
            wd_bf[...] = wd_f32[b].astype(BF16)

        for part in range(EXPERT_TILE // EXPERT_PART):
            @pl.when(rows_ref[i] > part * EXPERT_PART)
            def _():
                sl = slice(part * EXPERT_PART, (part + 1) * EXPERT_PART)
                gu = _dot(x_ref[sl, :], wgu_bf[...]) + bgu_ref[...]
                gate = jnp.minimum(gu[:, :D_FF], SWIGLU_LIMIT)
                up = jnp.clip(gu[:, D_FF:], -SWIGLU_LIMIT, SWIGLU_LIMIT)
                act = (up + 1.0) * gate * jax.nn.sigmoid(SWIGLU_ALPHA * gate)
                y_ref[sl, :] = (_dot(act.astype(BF16), wd_bf[...]) + bd_ref[...]).astype(y_ref.dtype)


def _experts(xg, tile_group, group_expert, n_groups, n_used, tile_rows, wgu, bgu, wd, bd):
    rows_max = xg.shape[0]
    used = lambda i, nu: jnp.maximum(jnp.minimum(i, nu[0] - 1), 0)
    row = lambda i, tg, ge, ng, nu, tr: (used(i, nu), 0)
    exp = lambda i, tg, ge, ng, nu, tr: (ge[tg[used(i, nu)]], 0, 0)
    return pl.pallas_call(
        _expert_kernel,
        grid_spec=pltpu.PrefetchScalarGridSpec(
            num_scalar_prefetch=5,
            grid=(rows_max // EXPERT_TILE,),
            in_specs=[pl.BlockSpec((EXPERT_TILE, D_MODEL), row),
                      pl.BlockSpec((None, 1, 2 * D_FF), exp),
                      pl.BlockSpec((None, 1, D_MODEL), exp),
                      pl.BlockSpec(memory_space=pl.ANY),
                      pl.BlockSpec(memory_space=pl.ANY)],
            out_specs=pl.BlockSpec((EXPERT_TILE, D_MODEL), row),
            scratch_shapes=[pltpu.VMEM((2, D_MODEL, 2 * D_FF), F32), pltpu.VMEM((2, D_FF, D_MODEL), F32),
                            pltpu.VMEM((D_MODEL, 2 * D_FF), BF16), pltpu.VMEM((D_FF, D_MODEL), BF16),
                            pltpu.SemaphoreType.DMA((2, 2))]),
        out_shape=jax.ShapeDtypeStruct((rows_max, D_MODEL), BF16),
        compiler_params=_params("arbitrary"),
        name="experts",
    )(tile_group, group_expert, n_groups, n_used, tile_rows, xg, bgu, bd, wgu, wd)


def _combine_kernel(nu_ref, src_ref, slot_ref, gatek_ref, x_ref, gmlp_ref, y_ref, o_ref, ys_ref, sem,
                    *, tile_offset):
    j = pl.program_id(0)
    t = j + tile_offset
    buf = j % 2
    tm = x_ref.shape[0]

    def unit_copy(tile, b, i):
        src = pl.multiple_of(src_ref[tile, i] * ROW_UNIT, ROW_UNIT)
        dst = pl.multiple_of(i * ROW_UNIT, ROW_UNIT)
        return pltpu.make_async_copy(y_ref.at[pl.ds(src, ROW_UNIT), :],
                                     ys_ref.at[b, pl.ds(dst, ROW_UNIT), :], sem.at[b])

    def fetch(tile, b):
        n_units = nu_ref[tile]
        lax.fori_loop(0, n_units, lambda i, c: (unit_copy(tile, b, i).start(), c)[1], 0)

        def zero_unit(i, c):
            ys_ref[b, pl.ds(pl.multiple_of(i * ROW_UNIT, ROW_UNIT), ROW_UNIT), :] = jnp.zeros(
                (ROW_UNIT, D_MODEL), ys_ref.dtype)
            return c

        lax.fori_loop(n_units, LOCAL_ROWS // ROW_UNIT, zero_unit, 0)

    @pl.when(j == 0)
    def _():
        fetch(t, buf)

    @pl.when(j + 1 < pl.num_programs(0))
    def _():
        fetch(t + 1, 1 - buf)

    lax.fori_loop(0, nu_ref[t], lambda i, c: (unit_copy(t, buf, i).wait(), c)[1], 0)

    slots = slot_ref[...].astype(jnp.int32)
    gates = gatek_ref[...]
    slot_cols = [slots[:, k:k + 1] for k in range(TOP_K)]
    gate_cols = [gates[:, k:k + 1] for k in range(TOP_K)]
    acc = jnp.zeros((tm, D_MODEL), F32)
    for c in range(LOCAL_ROWS // tm):
        acc = acc + _dot(_slot_matrix(slot_cols, gate_cols, c), ys_ref[buf, c * tm:(c + 1) * tm, :])
    o_ref[...] = x_ref[...] + gmlp_ref[...] * acc


def _combine(y, slots, gates, x1, gmlp, n_units, unit_src, tile_offset, per_token, tiles_per_seq):
    n = x1.shape[0]
    if per_token:
        mod = pl.BlockSpec((TOKEN_TILE, D_MODEL), lambda i, *_: (i, 0))
    else:
        mod = pl.BlockSpec((None, 1, D_MODEL), lambda i, *_: (i // tiles_per_seq, 0, 0))
    wide = pl.BlockSpec((TOKEN_TILE, D_MODEL), lambda i, *_: (i, 0))
    narrow = pl.BlockSpec((TOKEN_TILE, LANES), lambda i, *_: (i + tile_offset, 0))
    return pl.pallas_call(
        functools.partial(_combine_kernel, tile_offset=tile_offset),
        grid_spec=pltpu.PrefetchScalarGridSpec(
            num_scalar_prefetch=2,
            grid=(n // TOKEN_TILE,),
            in_specs=[narrow, narrow, wide, mod, pl.BlockSpec(memory_space=pl.ANY)],
            out_specs=wide,
            scratch_shapes=[pltpu.VMEM((2, LOCAL_ROWS, D_MODEL), BF16), pltpu.SemaphoreType.DMA((2,))]),
        out_shape=jax.ShapeDtypeStruct((n, D_MODEL), F32),
        compiler_params=_params("arbitrary"),
        name="combine",
    )(n_units, unit_src, slots, gates, x1, gmlp, y)


def _route_tables(cnt, rows_max):
    units = (cnt + ROW_UNIT - 1) // ROW_UNIT
    group_units = jnp.sum(units, axis=0)
    upt = EXPERT_TILE // ROW_UNIT
    group_pad = (group_units + upt - 1) // upt * upt
    group_end = jnp.cumsum(group_pad)
    group_start = group_end - group_pad
    seg_start = group_start[None, :] + jnp.cumsum(units, axis=0) - units
    local_end = jnp.cumsum(units, axis=1)
    local_start = local_end - units
    n_units = local_end[:, -1].astype(jnp.int32)

    def pick(lo, hi, pos, value):
        return jnp.sum(jnp.where((pos >= lo) & (pos < hi), value, 0), axis=-1).astype(jnp.int32)

    i = jnp.arange(LOCAL_ROWS // ROW_UNIT, dtype=jnp.int32)[None, :, None]
    unit_hbm = pick(local_start[:, None, :], local_end[:, None, :], i,
                    seg_start[:, None, :] + i - local_start[:, None, :])

    n_tail_e = group_pad - group_units
    j = jnp.arange(N_EXPERTS * upt, dtype=jnp.int32)[:, None]
    tail_end = jnp.cumsum(n_tail_e)
    tail_start = tail_end - n_tail_e
    tail_hbm = pick(tail_start[None, :], tail_end[None, :], j,
                    (group_start + group_units)[None, :] + j - tail_start[None, :])
    n_tail = tail_end[-1:].astype(jnp.int32)

    r = jnp.arange(rows_max // EXPERT_TILE, dtype=jnp.int32)[:, None] * upt
    n_used = (group_end[-1:] // upt).astype(jnp.int32)
    nonempty = group_units > 0
    group_of_expert = jnp.cumsum(nonempty) - 1
    tile_rows = pick(group_start[None, :], group_end[None, :], r,
                     jnp.clip(((group_start + group_units)[None, :] - r) * ROW_UNIT, 0, EXPERT_TILE))
    tile_group = pick(group_start[None, :], group_end[None, :], r, group_of_expert[None, :])
    g = jnp.arange(N_EXPERTS, dtype=jnp.int32)
    group_expert = jnp.sum(jnp.where(nonempty[None, :] & (group_of_expert[None, :] == g[:, None]),
                                     g[None, :], 0), axis=-1).astype(jnp.int32)
    n_groups = jnp.sum(nonempty)[None].astype(jnp.int32)
    return (n_units, unit_hbm, n_tail, tail_hbm,
            tile_group, group_expert, n_groups, n_used, tile_rows)


def _rope_tables(pos):
    half = HEAD_DIM // 2
    lane = np.arange(LANES)
    inv = jnp.asarray(ROPE_THETA, F32) ** (-jnp.asarray(lane % half, F32) / half)
    sign = jnp.asarray(np.where(lane % HEAD_DIM < half, -1.0, 1.0), F32)
    ang = pos.astype(F32)[:, None] * inv[None, :]
    return jnp.cos(ang), jnp.sin(ang) * sign[None, :]


def kernel(x_prompt, x_sample, c_prompt, c_sample, state_gla, cache_swa_k, cache_swa_v, w_ada, b_ada,
           norm1_g, norm2_g, w_in, w_gk2, b_gk, gla_norm_g, q_norm_g, k_norm_g, attn_sinks, w_o,
           w_router, b_router, w_gate_up, b_gate_up, w_down, b_down):
    batch, seq, d = x_prompt.shape
    n_seq, n_real, _ = x_sample.shape
    depth = w_in.shape[0]
    assert depth == 1 and d == D_MODEL and n_real <= SAMPLE_ROWS
    assert seq % TOKEN_TILE == 0 and (n_seq * SAMPLE_ROWS) % TOKEN_TILE == 0 and n_seq % SAMPLE_SEQS == 0
    R = SAMPLE_ROWS
    kvw = SWA_KV_HEADS * HEAD_DIM
    tiles_per_seq = seq // TOKEN_TILE

    qg, kg, vg, rg, glr_w, qs, ks, vs, ag, as_ = jnp.split(
        w_in[0], [512, 1024, 2048, 3072, 3088, 4112, 4368, 4624, 5648], axis=1)
    w_main = jnp.concatenate([vg, rg, qs, ag, as_, qg, kg, ks, vs], axis=1).astype(BF16)
    w_glr = jnp.pad(glr_w, ((0, 0), (0, LANES - GATE_RANK))).astype(BF16)
    wgk = jnp.pad(w_gk2[0], ((0, LANES - GATE_RANK), (0, 0)))
    bgk = b_gk[0].reshape(1, -1)
    gnorm = gla_norm_g[0].reshape(1, -1)
    gq = jnp.tile(q_norm_g[0], LANES // HEAD_DIM).reshape(1, LANES)
    gk = jnp.tile(k_norm_g[0], LANES // HEAD_DIM).reshape(1, LANES)
    seg = jnp.asarray(np.kron(np.eye(LANES // HEAD_DIM), np.ones((HEAD_DIM, HEAD_DIM))), BF16)
    sinks = attn_sinks[0]
    wo = w_o[0].astype(BF16)
    wr = jnp.pad(w_router[0], ((0, 0), (0, LANES - N_EXPERTS)))
    br = jnp.pad(b_router[0], (0, LANES - N_EXPERTS), constant_values=-1e30).reshape(1, LANES)
    bgu = b_gate_up[0].reshape(N_EXPERTS, 1, 2 * D_FF)
    bd = b_down[0].reshape(N_EXPERTS, 1, D_MODEL)
    g1 = norm1_g[0].reshape(1, -1)
    g2 = norm2_g[0].reshape(1, -1)

    n_c = batch + n_seq
    c_all = jnp.pad(jnp.concatenate([c_prompt, c_sample], axis=0), ((0, -n_c % 8), (0, 0)))
    m_all = _ada(c_all, w_ada[0], b_ada[0])
    mp = [m_all[:batch, i * d:(i + 1) * d].reshape(batch, 1, d) for i in range(6)]
    ms = [jnp.repeat(m_all[batch:n_c, i * d:(i + 1) * d], R, axis=0) for i in range(6)]

    xp = x_prompt.reshape(batch * seq, d)
    xs = jnp.pad(x_sample, ((0, 0), (0, R - n_real), (0, 0))).reshape(n_seq * R, d)
    cos_p, sin_p = _rope_tables(jnp.arange(seq))
    cos_s, sin_s = _rope_tables(PAST_LEN + jnp.tile(jnp.arange(R), SAMPLE_SEQS))
    sink_rows = jnp.repeat(sinks, R).reshape(SWA_HEADS * R, 1)

    yp, glr_p = _inproj(xp, mp[0], mp[1], g1, w_main, w_glr, BF16, False, tiles_per_seq)
    ys, glr_s = _inproj(xs, ms[0], ms[1], g1, w_main, w_glr, F32, True, 1)
    og_p, st_p = _gla_prompt(yp, glr_p, wgk, bgk, gnorm, batch, seq)
    og_s, st_s = _gla_sample(ys, glr_s, wgk, bgk, gnorm, state_gla[0], n_seq, n_real)
    os_p, kc_p, vc_p = _swa_prompt(yp, sinks, cos_p, sin_p, gq, gk, seg, batch, seq)
    os_s, kc_s, vc_s = _swa_sample(ys, sink_rows, cache_swa_k[0].reshape(n_seq, WINDOW, kvw),
                                   cache_swa_v[0].reshape(n_seq, WINDOW, kvw),
                                   cos_s, sin_s, gq, gk, seg, n_seq, n_real)
    x1_p, h2_p, sl_p, gt_p, cnt_p = _post(yp, og_p, os_p, xp, mp[2], mp[3], mp[4], g2, wo, wr, br,
                                          False, tiles_per_seq, R)
    x1_s, h2_s, sl_s, gt_s, cnt_s = _post(ys, og_s, os_s, xs, ms[2], ms[3], ms[4], g2, wo, wr, br,
                                          True, 1, n_real)

    h2 = jnp.concatenate([h2_p, h2_s], axis=0)
    slots = jnp.concatenate([sl_p, sl_s], axis=0)
    gates = jnp.concatenate([gt_p, gt_s], axis=0)
    cnt = jnp.concatenate([cnt_p, cnt_s], axis=0)[:, 0, :N_EXPERTS].astype(jnp.int32)
    n_tiles = cnt.shape[0]
    rows_bound = (TOP_K * (batch * seq + n_seq * n_real) + n_tiles * N_EXPERTS * (ROW_UNIT - 1)
                  + N_EXPERTS * (EXPERT_TILE - 1))
    rows_max = -(-rows_bound // EXPERT_TILE) * EXPERT_TILE
    (n_units, unit_hbm, n_tail, tail_hbm,
     tile_group, group_expert, n_groups, n_used, tile_rows) = _route_tables(cnt, rows_max)
    xg = _dispatch(h2, slots, n_units, unit_hbm, n_tail, tail_hbm, rows_max)
    yg = _experts(xg, tile_group, group_expert, n_groups, n_used, tile_rows,
                  w_gate_up[0], bgu, w_down[0], bd)
    p_tiles = batch * seq // TOKEN_TILE
    out_p = _combine(yg, slots, gates, x1_p, mp[5], n_units, unit_hbm, 0, False, tiles_per_seq)
    out_s = _combine(yg, slots, gates, x1_s, ms[5], n_units, unit_hbm, p_tiles, True, 1)

    cache_shape = (WINDOW, SWA_KV_HEADS, HEAD_DIM)
    return (out_p.reshape(batch, seq, d),
            out_s.reshape(n_seq, R, d)[:, :n_real],
            st_p[None],
            kc_p.reshape(1, batch, *cache_shape),
            vc_p.reshape(1, batch, *cache_shape),
            st_s[None],
            kc_s.reshape(1, n_seq, *cache_shape),
            vc_s.reshape(1, n_seq, *cache_shape))
```

```python
import functools

import numpy as np
import jax
import jax.numpy as jnp
from jax import lax
from jax.experimental import pallas as pl
from jax.experimental.pallas import tpu as pltpu

F32 = jnp.float32
BF16 = jnp.bfloat16
HIGHEST = lax.Precision.HIGHEST

D_MODEL = 1024
PAST_LEN = 16384
GLA_HEADS = 4
GLA_DK = 128
GLA_DV = 256
GATE_RANK = 16
GATE_TAU = 16.0
SWA_HEADS = 16
SWA_KV_HEADS = 4
HEAD_DIM = 64
SWA_GROUP = SWA_HEADS // SWA_KV_HEADS
WINDOW = 128
ROPE_THETA = 10000.0
N_EXPERTS = 32
TOP_K = 4
D_FF = 1024
SWIGLU_ALPHA = 1.702
SWIGLU_LIMIT = 7.0
NORM_EPS = 1e-6

LANES = 128
SUBLANES = 8
SAMPLE_ROWS = 8
SAMPLE_SEQS = 16
TOKEN_TILE = 512
GLA_CHUNK = 64
GLA_BLOCK = 256
VMEM_LIMIT = 56 * 1024 * 1024
ROW_UNIT = 16
EXPERT_TILE = 1024
EXPERT_PART = 256
LOCAL_ROWS = -(-(TOP_K * TOKEN_TILE + N_EXPERTS * (ROW_UNIT - 1)) // TOKEN_TILE) * TOKEN_TILE

COL_VG, COL_RG, COL_QS, COL_AG, COL_AS = 0, 1024, 2048, 3072, 4096
COL_QG, COL_KG, COL_KS, COL_VS = 5120, 5632, 6144, 6400
D_MAIN = 6656
PROJ_CHUNK = 512


def _dot(a, b, precision=None):
    return jnp.dot(a, b, preferred_element_type=F32, precision=precision)


def _dot_nt(a, b, precision=None):
    return lax.dot_general(a, b, (((1,), (1,)), ((), ())), preferred_element_type=F32, precision=precision)


def _dot_tn(a, b, precision=None):
    return lax.dot_general(a, b, (((0,), (0,)), ((), ())), preferred_element_type=F32, precision=precision)


def _dot_sum(sel, x):
    hi = x.astype(BF16)
    r1 = x - hi.astype(F32)
    mid = r1.astype(BF16)
    lo = (r1 - mid.astype(F32)).astype(BF16)
    return _dot(sel, hi) + _dot(sel, mid) + _dot(sel, lo)


def _params(*sem):
    return pltpu.CompilerParams(dimension_semantics=sem, vmem_limit_bytes=VMEM_LIMIT)


def _rms(x, g):
    return x * lax.rsqrt(jnp.mean(x * x, axis=-1, keepdims=True) + NORM_EPS) * g


def _log_sigmoid(x):
    return jnp.minimum(x, 0.0) - jnp.log(1.0 + jnp.exp(-jnp.abs(x)))


def _ada_kernel(c_ref, w_ref, b_ref, o_ref):
    c = c_ref[...]
    s = c * jax.nn.sigmoid(c)
    o_ref[...] = _dot(s, w_ref[...], HIGHEST) + b_ref[...]


def _ada(c_all, w_ada, b_ada):
    rows = c_all.shape[0]
    tn = 768
    return pl.pallas_call(
        _ada_kernel,
        grid=(6 * D_MODEL // tn,),
        in_specs=[pl.BlockSpec((rows, D_MODEL), lambda j: (0, 0)),
                  pl.BlockSpec((D_MODEL, tn), lambda j: (0, j)),
                  pl.BlockSpec((1, tn), lambda j: (0, j))],
        out_specs=pl.BlockSpec((rows, tn), lambda j: (0, j)),
        out_shape=jax.ShapeDtypeStruct((rows, 6 * D_MODEL), F32),
        compiler_params=_params("parallel"),
        name="ada",
    )(c_all, w_ada, b_ada.reshape(1, -1))


def _inproj_kernel(x_ref, shift_ref, scale_ref, g_ref, w_ref, wg_ref, o_ref, og_ref):
    h = _rms(x_ref[...], g_ref[...]) * (1.0 + scale_ref[...]) + shift_ref[...]
    hb = h.astype(BF16)
    for j in range(D_MAIN // PROJ_CHUNK):
        sl = slice(j * PROJ_CHUNK, (j + 1) * PROJ_CHUNK)
        o_ref[:, sl] = _dot(hb, w_ref[:, sl]).astype(o_ref.dtype)
    og_ref[...] = _dot(hb, wg_ref[...])


def _mod_spec(per_token, tiles_per_seq):
    if per_token:
        return pl.BlockSpec((TOKEN_TILE, D_MODEL), lambda i: (i, 0))
    return pl.BlockSpec((None, 1, D_MODEL), lambda i: (i // tiles_per_seq, 0, 0))


def _inproj(x, shift, scale, g, w_main, w_glr, out_dtype, per_token, tiles_per_seq):
    n = x.shape[0]
    mod = _mod_spec(per_token, tiles_per_seq)
    const = lambda i: (0, 0)
    return pl.pallas_call(
        _inproj_kernel,
        grid=(n // TOKEN_TILE,),
        in_specs=[pl.BlockSpec((TOKEN_TILE, D_MODEL), lambda i: (i, 0)), mod, mod,
                  pl.BlockSpec((1, D_MODEL), const),
                  pl.BlockSpec((D_MODEL, D_MAIN), const, pipeline_mode=pl.Buffered(1)),
                  pl.BlockSpec((D_MODEL, LANES), const, pipeline_mode=pl.Buffered(1))],
        out_specs=[pl.BlockSpec((TOKEN_TILE, D_MAIN), lambda i: (i, 0)),
                   pl.BlockSpec((TOKEN_TILE, LANES), lambda i: (i, 0))],
        out_shape=[jax.ShapeDtypeStruct((n, D_MAIN), out_dtype),
                   jax.ShapeDtypeStruct((n, LANES), F32)],
        compiler_params=_params("parallel"),
        name="inproj",
    )(x, shift, scale, g, w_main, w_glr)


def _gla_log_gate(glr, wgk, bgk):
    return _log_sigmoid(_dot(glr, wgk, HIGHEST) + bgk) * (1.0 / GATE_TAU)


def _gla_out(o, r, g):
    r = r.astype(F32)
    return _rms(o, g) * (r * jax.nn.sigmoid(r))


def _gla_pair_levels(n):
    t = np.arange(n)[:, None]
    s = np.arange(n)[None, :]
    x = t ^ s
    top = np.where(x > 0, 1 << np.floor(np.log2(np.maximum(x, 1))).astype(np.int64), 0)
    return np.where(s > t, -1, top).astype(np.int32)


def _gla_block_ref(b, h):
    n, w = b.shape
    if 2 * h == n:
        return jnp.broadcast_to(b[h - 1:h, :], (n, w))
    if h >= SUBLANES // 2:
        picked = b.reshape(n // (2 * h), 2 * h, w)[:, h - 1:h, :]
        return jnp.broadcast_to(picked, (n // (2 * h), 2 * h, w)).reshape(n, w)
    r = lax.broadcasted_iota(jnp.int32, (n, 1), 0) % (2 * h)
    out = b
    for d in range(1, h + 1):
        out = jnp.where(r == h - 1 + d, pltpu.roll(b, d, 0), out)
    for d in range(1, h):
        out = jnp.where(r == h - 1 - d, pltpu.roll(b, n - d, 0), out)
    return out


def _gla_prompt_kernel(q_ref, k_ref, v_ref, r_ref, glr_ref, wgk_ref, bgk_ref, g_ref, lev_ref,
                       o_ref, s_ref, st_ref):
    c = pl.program_id(1)

    @pl.when(c == 0)
    def _():
        st_ref[...] = jnp.zeros_like(st_ref)

    n = GLA_BLOCK
    lev = lev_ref[...]
    causal = lax.broadcasted_iota(jnp.int32, (n, n), 1) <= lax.broadcasted_iota(jnp.int32, (n, n), 0)
    lg = _gla_log_gate(glr_ref[...], wgk_ref[...], bgk_ref[...])
    b = _dot_sum(jnp.where(causal, 1.0, 0.0).astype(BF16), lg)
    b_last = b[n - 1:n, :]
    q = q_ref[...].astype(F32) * (GLA_DK ** -0.5)
    k = k_ref[...].astype(F32)
    qe = (q * jnp.exp(b)).astype(BF16)
    kd = (k * jnp.exp(b_last - b)).astype(BF16)
    decay = jnp.exp(b_last)
    levels = [0] + [1 << p for p in range(n.bit_length() - 1)]
    q_lv, k_lv = [q.astype(BF16)], [k.astype(BF16)]
    for h in levels[1:]:
        e = jnp.exp(-jnp.abs(b - _gla_block_ref(b, h)))
        q_lv.append((q * e).astype(BF16))
        k_lv.append((k * e).astype(BF16))
    for h in range(GLA_HEADS):
        dk = slice(h * GLA_DK, (h + 1) * GLA_DK)
        dv = slice(h * GLA_DV, (h + 1) * GLA_DV)
        attn = jnp.zeros((n, n), F32)
        for level, ql, kl in zip(levels, q_lv, k_lv):
            attn = jnp.where(lev == level, _dot_nt(ql[:, dk], kl[:, dk]), attn)
        v = v_ref[:, dv]
        st = st_ref[h]
        o = _dot_nt(qe[:, dk], st.astype(BF16)) + _dot(attn.astype(BF16), v)
        st_ref[h] = st * decay[:, dk] + _dot_tn(v, kd[:, dk])
        o_ref[:, dv] = _gla_out(o, r_ref[:, dv], g_ref[...]).astype(o_ref.dtype)

    @pl.when(c == pl.num_programs(1) - 1)
    def _():
        for h in range(GLA_HEADS):
            s_ref[h] = st_ref[h].T


def _gla_prompt(yp, glr, wgk, bgk, gnorm, batch, seq):
    nb = seq // GLA_BLOCK
    hk, hv = GLA_HEADS * GLA_DK, GLA_HEADS * GLA_DV
    tok = lambda b, c: b * nb + c
    const = lambda b, c: (0, 0)
    return pl.pallas_call(
        _gla_prompt_kernel,
        grid=(batch, nb),
        in_specs=[pl.BlockSpec((GLA_BLOCK, hk), lambda b, c: (tok(b, c), COL_QG // hk)),
                  pl.BlockSpec((GLA_BLOCK, hk), lambda b, c: (tok(b, c), COL_KG // hk)),
                  pl.BlockSpec((GLA_BLOCK, hv), lambda b, c: (tok(b, c), COL_VG // hv)),
                  pl.BlockSpec((GLA_BLOCK, hv), lambda b, c: (tok(b, c), COL_RG // hv)),
                  pl.BlockSpec((GLA_BLOCK, LANES), lambda b, c: (tok(b, c), 0)),
                  pl.BlockSpec((LANES, hk), const),
                  pl.BlockSpec((1, hk), const),
                  pl.BlockSpec((1, GLA_DV), const),
                  pl.BlockSpec((GLA_BLOCK, GLA_BLOCK), const)],
        out_specs=[pl.BlockSpec((GLA_BLOCK, hv), lambda b, c: (tok(b, c), 0)),
                   pl.BlockSpec((None, GLA_HEADS, GLA_DK, GLA_DV), lambda b, c: (b, 0, 0, 0))],
        out_shape=[jax.ShapeDtypeStruct((batch * seq, hv), BF16),
                   jax.ShapeDtypeStruct((batch, GLA_HEADS, GLA_DK, GLA_DV), F32)],
        scratch_shapes=[pltpu.VMEM((GLA_HEADS, GLA_DV, GLA_DK), F32)],
        compiler_params=_params("parallel", "arbitrary"),
        name="gla_prompt",
    )(yp, yp, yp, yp, glr, wgk, bgk, gnorm, jnp.asarray(_gla_pair_levels(GLA_BLOCK)))


def _gla_sample_kernel(q_ref, k_ref, v_ref, r_ref, glr_ref, wgk_ref, bgk_ref, g_ref, s0_ref, lev_ref,
                       o_ref, s_ref, *, n_real):
    R = SAMPLE_ROWS
    rows = q_ref.shape[0]
    row = lax.broadcasted_iota(jnp.int32, (rows, rows), 0)
    col = lax.broadcasted_iota(jnp.int32, (rows, rows), 1)
    same = (row // R) == (col // R)
    causal = same & (col <= row)
    real = lax.broadcasted_iota(jnp.int32, (rows, 1), 0) % R < n_real
    lg = jnp.where(real, _gla_log_gate(glr_ref[...], wgk_ref[...], bgk_ref[...]), 0.0)
    b = _dot_sum(jnp.where(causal, 1.0, 0.0).astype(BF16), lg)
    b_last = _dot_sum(jnp.where(same, 1.0, 0.0).astype(BF16), lg)
    q = q_ref[...] * (GLA_DK ** -0.5)
    k = jnp.where(real, k_ref[...], 0.0)
    v = v_ref[...]
    qe = q * jnp.exp(b)
    kd = k * jnp.exp(b_last - b)
    lev = lev_ref[...]
    levels = [0] + [1 << p for p in range(R.bit_length() - 1)]
    q_lv, k_lv = [q], [k]
    for h in levels[1:]:
        e = jnp.exp(-jnp.abs(b - _gla_block_ref(b, h)))
        q_lv.append(q * e)
        k_lv.append(k * e)
    for h in range(GLA_HEADS):
        dk = slice(h * GLA_DK, (h + 1) * GLA_DK)
        dv = slice(h * GLA_DV, (h + 1) * GLA_DV)
        attn = jnp.zeros((rows, rows), F32)
        for level, ql, kl in zip(levels, q_lv, k_lv):
            attn = jnp.where(lev == level, _dot_nt(ql[:, dk], kl[:, dk]), attn)
        o_intra = _dot(attn, v[:, dv])
        decay_t = jnp.exp(b_last[:, dk]).T
        outs = []
        for s in range(rows // R):
            sl = slice(s * R, (s + 1) * R)
            s0 = s0_ref[s, h]
            outs.append(_dot(qe[sl, dk], s0) + o_intra[sl, :])
            s_ref[s, h] = s0 * decay_t[:, s * R:s * R + 1] + _dot_tn(kd[sl, dk], v[sl, dv])
        o_ref[:, dv] = _gla_out(jnp.concatenate(outs, axis=0), r_ref[:, dv], g_ref[...])


def _gla_sample(ys, glr, wgk, bgk, gnorm, state, n_seq, n_real):
    R = SAMPLE_SEQS * SAMPLE_ROWS
    hk, hv = GLA_HEADS * GLA_DK, GLA_HEADS * GLA_DV
    st_spec = pl.BlockSpec((SAMPLE_SEQS, GLA_HEADS, GLA_DK, GLA_DV), lambda b: (b, 0, 0, 0))
    lev = _gla_pair_levels(R)
    lev = np.where(lev >= SAMPLE_ROWS, -1, lev)
    return pl.pallas_call(
        functools.partial(_gla_sample_kernel, n_real=n_real),
        grid=(n_seq // SAMPLE_SEQS,),
        in_specs=[pl.BlockSpec((R, hk), lambda b: (b, COL_QG // hk)),
                  pl.BlockSpec((R, hk), lambda b: (b, COL_KG // hk)),
                  pl.BlockSpec((R, hv), lambda b: (b, COL_VG // hv)),
                  pl.BlockSpec((R, hv), lambda b: (b, COL_RG // hv)),
                  pl.BlockSpec((R, LANES), lambda b: (b, 0)),
                  pl.BlockSpec((LANES, hk), lambda b: (0, 0)),
                  pl.BlockSpec((1, hk), lambda b: (0, 0)),
                  pl.BlockSpec((1, GLA_DV), lambda b: (0, 0)),
                  st_spec,
                  pl.BlockSpec((R, R), lambda b: (0, 0))],
        out_specs=[pl.BlockSpec((R, hv), lambda b: (b, 0)), st_spec],
        out_shape=[jax.ShapeDtypeStruct((n_seq * SAMPLE_ROWS, hv), F32),
                   jax.ShapeDtypeStruct((n_seq, GLA_HEADS, GLA_DK, GLA_DV), F32)],
        compiler_params=_params("parallel"),
        name="gla_sample",
    )(ys, ys, ys, ys, glr, wgk, bgk, gnorm, state, jnp.asarray(lev))


def _lane_lower(shape):
    return lax.broadcasted_iota(jnp.int32, shape, len(shape) - 1) % LANES < HEAD_DIM


def _head_norm_rope(x, g, cos, sin, seg):
    sq = x * x
    hi = sq.astype(BF16)
    lo = (sq - hi.astype(F32)).astype(BF16)
    ss = _dot(hi, seg) + _dot(lo, seg)
    y = x * lax.rsqrt(ss * (1.0 / HEAD_DIM) + NORM_EPS) * g
    half = HEAD_DIM // 2
    lane = lax.broadcasted_iota(jnp.int32, y.shape, 1)
    rot = jnp.where(lane % HEAD_DIM < half, pltpu.roll(y, LANES - half, 1), pltpu.roll(y, half, 1))
    return y * cos + rot * sin


def _both_halves(blk, half):
    sw = pltpu.roll(blk, HEAD_DIM, 1)
    lower = _lane_lower(blk.shape)
    return jnp.where(lower, blk, sw) if half == 0 else jnp.where(lower, sw, blk)


def _stack_heads(q_blocks):
    parts = []
    for qb in q_blocks:
        lower = _lane_lower(qb.shape)
        zero = jnp.zeros_like(qb)
        parts += [jnp.where(lower, qb, zero), jnp.where(lower, zero, qb)]
    return jnp.concatenate(parts, axis=0)


def _swa_prompt_kernel(sink_ref, q_ref, k_ref, v_ref, cos_ref, sin_ref, gq_ref, gk_ref, seg_ref,
                       o_ref, ko_ref, vo_ref, kprev_ref, vprev_ref):
    n = pl.program_id(1)
    W = WINDOW

    @pl.when(n == 0)
    def _():
        kprev_ref[...] = jnp.zeros_like(kprev_ref)
        vprev_ref[...] = jnp.zeros_like(vprev_ref)

    cos, sin, seg = cos_ref[...], sin_ref[...], seg_ref[...]
    k_cur = jnp.concatenate(
        [_head_norm_rope(k_ref[:, j * LANES:(j + 1) * LANES].astype(F32), gk_ref[...], cos, sin, seg)
         for j in range(SWA_KV_HEADS * HEAD_DIM // LANES)], axis=1)
    v_cur = v_ref[...].astype(F32)
    ko_ref[...] = k_cur
    vo_ref[...] = v_cur
    k_prev, v_prev = kprev_ref[...], vprev_ref[...]
    kprev_ref[...] = k_cur
    vprev_ref[...] = v_cur

    qi = lax.broadcasted_iota(jnp.int32, (W, W), 0)
    ki = lax.broadcasted_iota(jnp.int32, (W, W), 1)
    from_cur = ki <= qi
    prev_fill = jnp.where(n > 0, 0.0, -jnp.inf)
    for kh in range(SWA_KV_HEADS):
        blk = slice((kh // 2) * LANES, (kh // 2 + 1) * LANES)
        kb_prev = _both_halves(k_prev[:, blk], kh % 2).astype(BF16)
        kb_cur = _both_halves(k_cur[:, blk], kh % 2).astype(BF16)
        vb_prev = _both_halves(v_prev[:, blk], kh % 2).astype(BF16)
        vb_cur = _both_halves(v_cur[:, blk], kh % 2).astype(BF16)
        qblocks = []
        for j in range(2):
            c0 = (2 * kh + j) * LANES
            qn = _head_norm_rope(q_ref[:, c0:c0 + LANES].astype(F32), gq_ref[...], cos, sin, seg)
            qblocks.append(qn * (HEAD_DIM ** -0.5))
        qs = _stack_heads(qblocks).astype(BF16)
        s_prev = _dot_nt(qs, kb_prev)
        s_cur = _dot_nt(qs, kb_cur)
        outs = []
        for g in range(SWA_GROUP):
            rows = slice(g * W, (g + 1) * W)
            sg = jnp.where(from_cur, s_cur[rows, :], s_prev[rows, :] + prev_fill)
            sink = sink_ref[kh * SWA_GROUP + g]
            m = jnp.maximum(jnp.max(sg, axis=-1, keepdims=True), sink)
            p = jnp.exp(sg - m)
            denom = jnp.sum(p, axis=-1, keepdims=True) + jnp.exp(sink - m)
            p_cur = jnp.where(from_cur, p, 0.0).astype(BF16)
            p_prev = jnp.where(from_cur, 0.0, p).astype(BF16)
            outs.append((_dot(p_prev, vb_prev) + _dot(p_cur, vb_cur)) / denom)
        lower = _lane_lower((W, LANES))
        for j in range(2):
            c0 = (2 * kh + j) * LANES
            o_ref[:, c0:c0 + LANES] = jnp.where(lower, outs[2 * j], outs[2 * j + 1]).astype(o_ref.dtype)


def _swa_prompt(yp, sinks, cos, sin, gq, gk, seg, batch, seq):
    nb = seq // WINDOW
    kvw = SWA_KV_HEADS * HEAD_DIM
    tok = lambda b, n: b * nb + n
    const = lambda b, n: (0, 0)
    cache_spec = pl.BlockSpec((None, WINDOW, kvw), lambda b, n: (b, 0, 0))
    return pl.pallas_call(
        _swa_prompt_kernel,
        grid=(batch, nb),
        in_specs=[pl.BlockSpec(memory_space=pltpu.SMEM),
                  pl.BlockSpec((WINDOW, D_MODEL), lambda b, n: (tok(b, n), COL_QS // D_MODEL)),
                  pl.BlockSpec((WINDOW, kvw), lambda b, n: (tok(b, n), COL_KS // kvw)),
                  pl.BlockSpec((WINDOW, kvw), lambda b, n: (tok(b, n), COL_VS // kvw)),
                  pl.BlockSpec((WINDOW, LANES), lambda b, n: (n, 0)),
                  pl.BlockSpec((WINDOW, LANES), lambda b, n: (n, 0)),
                  pl.BlockSpec((1, LANES), const), pl.BlockSpec((1, LANES), const),
                  pl.BlockSpec((LANES, LANES), const)],
        out_specs=[pl.BlockSpec((WINDOW, D_MODEL), lambda b, n: (tok(b, n), 0)), cache_spec, cache_spec],
        out_shape=[jax.ShapeDtypeStruct((batch * seq, D_MODEL), BF16),
                   jax.ShapeDtypeStruct((batch, WINDOW, kvw), F32),
                   jax.ShapeDtypeStruct((batch, WINDOW, kvw), F32)],
        scratch_shapes=[pltpu.VMEM((WINDOW, kvw), F32), pltpu.VMEM((WINDOW, kvw), F32)],
        compiler_params=_params("parallel", "arbitrary"),
        name="swa_prompt",
    )(sinks, yp, yp, yp, cos, sin, gq, gk, seg)


def _shift_cache(cache, new, n_real):
    R = SAMPLE_ROWS
    rolled = pltpu.roll(cache, WINDOW - n_real, 0)
    tail_new = pltpu.roll(new, R - n_real, 0)
    row = lax.broadcasted_iota(jnp.int32, (R, cache.shape[1]), 0)
    tail = jnp.where(row < R - n_real, rolled[WINDOW - R:, :], tail_new)
    return jnp.concatenate([rolled[:WINDOW - R, :], tail], axis=0)


def _swa_sample_kernel(q_ref, k_ref, v_ref, kc_ref, vc_ref, cos_ref, sin_ref, gq_ref, gk_ref,
                       seg_ref, sink_ref, o_ref, ko_ref, vo_ref, *, n_real):
    R, W = SAMPLE_ROWS, WINDOW
    kvw = SWA_KV_HEADS * HEAD_DIM
    cos, sin, seg = cos_ref[...], sin_ref[...], seg_ref[...]
    k_new = jnp.concatenate(
        [_head_norm_rope(k_ref[:, j * LANES:(j + 1) * LANES], gk_ref[...], cos, sin, seg)
         for j in range(kvw // LANES)], axis=1)
    v_new = v_ref[...]
    q_pairs = [_head_norm_rope(q_ref[:, j * LANES:(j + 1) * LANES], gq_ref[...], cos, sin, seg)
               * (HEAD_DIM ** -0.5) for j in range(SWA_HEADS // 2)]

    hr = SWA_HEADS * R
    t_c = lax.broadcasted_iota(jnp.int32, (hr, W), 0) % R
    mask_c = lax.broadcasted_iota(jnp.int32, (hr, W), 1) > t_c
    t_n = lax.broadcasted_iota(jnp.int32, (hr, R), 0) % R
    mask_n = lax.broadcasted_iota(jnp.int32, (hr, R), 1) <= t_n
    sink = sink_ref[...]
    lower = _lane_lower((R, LANES))
    zeros = jnp.zeros((R, LANES), F32)
    for s in range(q_ref.shape[0] // R):
        sl = slice(s * R, (s + 1) * R)
        kc, vc = kc_ref[s], vc_ref[s]
        kn, vn = k_new[sl, :], v_new[sl, :]
        ko_ref[s] = _shift_cache(kc, kn, n_real)
        vo_ref[s] = _shift_cache(vc, vn, n_real)
        q_rows = []
        for h in range(SWA_HEADS):
            kh = h // SWA_GROUP
            x = q_pairs[h // 2][sl, :]
            if h % 2 != kh % 2:
                x = pltpu.roll(x, HEAD_DIM, 1)
            x = jnp.where(lower, x, zeros) if kh % 2 == 0 else jnp.where(lower, zeros, x)
            q_rows.append(jnp.concatenate([x, zeros] if kh // 2 == 0 else [zeros, x], axis=1))
        qbd = jnp.concatenate(q_rows, axis=0)
        sc = jnp.where(mask_c, _dot_nt(qbd, kc), -jnp.inf)
        sn = jnp.where(mask_n, _dot_nt(qbd, kn), -jnp.inf)
        m = jnp.maximum(jnp.maximum(jnp.max(sc, axis=-1, keepdims=True),
                                    jnp.max(sn, axis=-1, keepdims=True)), sink)
        pc, pn = jnp.exp(sc - m), jnp.exp(sn - m)
        denom = (jnp.sum(pc, axis=-1, keepdims=True) + jnp.sum(pn, axis=-1, keepdims=True)
                 + jnp.exp(sink - m))
        o = (_dot(pc, vc) + _dot(pn, vn)) / denom
        for j in range(SWA_HEADS // 2):
            halves = []
            for h in (2 * j, 2 * j + 1):
                kh = h // SWA_GROUP
                y = o[h * R:(h + 1) * R, (kh // 2) * LANES:(kh // 2 + 1) * LANES]
                halves.append(pltpu.roll(y, HEAD_DIM, 1) if h % 2 != kh % 2 else y)
            o_ref[sl, j * LANES:(j + 1) * LANES] = jnp.where(lower, halves[0], halves[1])


def _swa_sample(ys, sink_rows, kcache, vcache, cos, sin, gq, gk, seg, n_seq, n_real):
    R = SAMPLE_SEQS * SAMPLE_ROWS
    kvw = SWA_KV_HEADS * HEAD_DIM
    const = lambda b: (0, 0)
    cache_spec = pl.BlockSpec((SAMPLE_SEQS, WINDOW, kvw), lambda b: (b, 0, 0))
    return pl.pallas_call(
        functools.partial(_swa_sample_kernel, n_real=n_real),
        grid=(n_seq // SAMPLE_SEQS,),
        in_specs=[pl.BlockSpec((R, D_MODEL), lambda b: (b, COL_QS // D_MODEL)),
                  pl.BlockSpec((R, kvw), lambda b: (b, COL_KS // kvw)),
                  pl.BlockSpec((R, kvw), lambda b: (b, COL_VS // kvw)),
                  cache_spec, cache_spec,
                  pl.BlockSpec((R, LANES), const), pl.BlockSpec((R, LANES), const),
                  pl.BlockSpec((1, LANES), const), pl.BlockSpec((1, LANES), const),
                  pl.BlockSpec((LANES, LANES), const),
                  pl.BlockSpec((SWA_HEADS * SAMPLE_ROWS, 1), const)],
        out_specs=[pl.BlockSpec((R, D_MODEL), lambda b: (b, 0)), cache_spec, cache_spec],
        out_shape=[jax.ShapeDtypeStruct((n_seq * SAMPLE_ROWS, D_MODEL), F32),
                   jax.ShapeDtypeStruct((n_seq, WINDOW, kvw), F32),
                   jax.ShapeDtypeStruct((n_seq, WINDOW, kvw), F32)],
        compiler_params=_params("parallel"),
        name="swa_sample",
    )(ys, ys, ys, kcache, vcache, cos, sin, gq, gk, seg, sink_rows)


def _post_kernel(ag_ref, as_ref, og_ref, os_ref, x_ref, gate_ref, shift_ref, scale_ref, g2_ref,
                 wo_ref, wr_ref, br_ref, x1_ref, h2_ref, slot_ref, gatek_ref, cnt_ref, *, n_valid):
    merged =(jax.nn.sigmoid(ag_ref[...].astype(F32)) * og_ref[...].astype(F32)
              + jax.nn.sigmoid(as_ref[...].astype(F32)) * os_ref[...].astype(F32))
    y = _dot(merged.astype(BF16), wo_ref[...])
    x1 = x_ref[...] + gate_ref[...] * y
    x1_ref[...] = x1
    h2 = _rms(x1, g2_ref[...]) * (1.0 + scale_ref[...]) + shift_ref[...]
    h2_hi = h2.astype(BF16)
    h2_ref[...] = h2_hi

    h2_lo = (h2 - h2_hi.astype(F32)).astype(BF16)
    w_hi, w_lo = wr_ref[0], wr_ref[1]
    logits = _dot(h2_hi, w_hi) + (_dot(h2_lo, w_hi) + _dot(h2_hi, w_lo)) + br_ref[...]
    lane_i = lax.broadcasted_iota(jnp.int32, logits.shape, 1)
    lane = lane_i.astype(F32)
    work = logits
    vals, hots = [], []
    for _ in range(TOP_K):
        m = jnp.max(work, axis=-1, keepdims=True)
        idx = jnp.min(jnp.where(work == m, lane, float(LANES)), axis=-1, keepdims=True)
        hot = lane == idx
        vals.append(m)
        hots.append(hot)
        work = jnp.where(hot, -jnp.inf, work)
    exps = [jnp.exp(v - vals[0]) for v in vals]
    denom = exps[0] + exps[1] + exps[2] + exps[3]

    tm = logits.shape[0]
    valid = lax.broadcasted_iota(jnp.int32, (tm, 1), 0) % SAMPLE_ROWS < n_valid
    sel = jnp.zeros_like(logits)
    for hot in hots:
        sel = jnp.where(hot, 1.0, sel)
    sel = jnp.where(valid, sel, 0.0)
    earlier = (lax.broadcasted_iota(jnp.int32, (tm, tm), 1)
               < lax.broadcasted_iota(jnp.int32, (tm, tm), 0))
    rank = _dot(jnp.where(earlier, 1.0, 0.0).astype(BF16), sel.astype(BF16))
    cnt = jnp.sum(sel, axis=0, keepdims=True)
    cnt_pad = jnp.floor((cnt + (ROW_UNIT - 1.0)) * (1.0 / ROW_UNIT)) * ROW_UNIT
    below = (lax.broadcasted_iota(jnp.int32, (LANES, LANES), 0)
             < lax.broadcasted_iota(jnp.int32, (LANES, LANES), 1))
    seg_start = _dot(jnp.broadcast_to(cnt_pad, (8, LANES)), jnp.where(below, 1.0, 0.0), HIGHEST)[0:1]
    pos = seg_start + rank
    slots = jnp.full_like(logits, -1.0)
    gates = jnp.zeros_like(logits)
    for k in range(TOP_K):
        s_k = jnp.sum(jnp.where(hots[k], pos, 0.0), axis=-1, keepdims=True)
        slots = jnp.where(lane_i == k, s_k, slots)
        gates = jnp.where(lane_i == k, exps[k] / denom, gates)
    slot_ref[...] = jnp.where(valid, slots, -1.0)
    gatek_ref[...] = gates
    cnt_ref[...] = cnt


def _post(y_all, o_gla, o_swa, x, gate, shift, scale, g2, wo, wr, br, per_token, tiles_per_seq, n_valid):
    n = x.shape[0]
    mod = _mod_spec(per_token, tiles_per_seq)
    row = lambda i: (i, 0)
    const = lambda i: (0, 0)
    wide = pl.BlockSpec((TOKEN_TILE, D_MODEL), row)
    narrow = pl.BlockSpec((TOKEN_TILE, LANES), row)
    return pl.pallas_call(
        functools.partial(_post_kernel, n_valid=n_valid),
        grid=(n // TOKEN_TILE,),
        in_specs=[pl.BlockSpec((TOKEN_TILE, D_MODEL), lambda i: (i, COL_AG // D_MODEL)),
                  pl.BlockSpec((TOKEN_TILE, D_MODEL), lambda i: (i, COL_AS // D_MODEL)),
                  wide, wide, wide, mod, mod, mod,
                  pl.BlockSpec((1, D_MODEL), const),
                  pl.BlockSpec((D_MODEL, D_MODEL), const),
                  pl.BlockSpec((2, D_MODEL, LANES), lambda i: (0, 0, 0)),
                  pl.BlockSpec((1, LANES), const)],
        out_specs=[wide, wide, narrow, narrow, pl.BlockSpec((None, 1, LANES), lambda i: (i, 0, 0))],
        out_shape=[jax.ShapeDtypeStruct((n, D_MODEL), F32),
                   jax.ShapeDtypeStruct((n, D_MODEL), BF16),
                   jax.ShapeDtypeStruct((n, LANES), F32),
                   jax.ShapeDtypeStruct((n, LANES), F32),
                   jax.ShapeDtypeStruct((n // TOKEN_TILE, 1, LANES), F32)],
        compiler_params=_params("parallel"),
        name="post",
    )(y_all, y_all, o_gla, o_swa, x, gate, shift, scale, g2, wo, wr, br)


def _slot_matrix(slot_cols, weights, chunk):
    tm = slot_cols[0].shape[0]
    j = lax.broadcasted_iota(jnp.int32, (tm, tm), 1) + chunk * tm
    out = jnp.zeros((tm, tm), F32)
    for s, w in zip(slot_cols, weights):
        out = jnp.where(s == j, w, out)
    return out.astype(BF16)


def _dispatch_kernel(nu_ref, dst_ref, ntail_ref, tail_ref, hp_ref, slp_ref, hs_ref, sls_ref, xg_ref,
                     sorted_ref, zero_ref, sem, tail_sem, *, prompt_tiles):
    t = pl.program_id(0)
    last = pl.num_programs(0) - 1
    buf = t % 2

    def sort_tile(h_ref, slot_ref):
        tm = h_ref.shape[0]
        slots = slot_ref[...].astype(jnp.int32)
        slot_cols = [slots[:, k:k + 1] for k in range(TOP_K)]
        h = h_ref[...]
        for c in range(LOCAL_ROWS // tm):
            onehot = _slot_matrix(slot_cols, [1.0] * TOP_K, c)
            sorted_ref[buf, c * tm:(c + 1) * tm, :] = _dot_tn(onehot, h).astype(BF16)

    @pl.when(t < prompt_tiles)
    def _():
        sort_tile(hp_ref, slp_ref)

    @pl.when(t >= prompt_tiles)
    def _():
        sort_tile(hs_ref, sls_ref)

    def unit_copy(tile, b, i):
        src = pl.multiple_of(i * ROW_UNIT, ROW_UNIT)
        dst = pl.multiple_of(dst_ref[tile, i] * ROW_UNIT, ROW_UNIT)
        return pltpu.make_async_copy(sorted_ref.at[b, pl.ds(src, ROW_UNIT), :],
                                     xg_ref.at[pl.ds(dst, ROW_UNIT), :], sem.at[b])

    def start_all(tile, b):
        lax.fori_loop(0, nu_ref[tile], lambda i, c: (unit_copy(tile, b, i).start(), c)[1], 0)

    def wait_all(tile, b):
        rows = nu_ref[tile] * ROW_UNIT

        @pl.when(rows > 0)
        def _():
            pltpu.make_async_copy(sorted_ref.at[b, pl.ds(0, rows), :],
                                  xg_ref.at[pl.ds(0, rows), :], sem.at[b]).wait()

    @pl.when(t > 0)
    def _():
        wait_all(t - 1, 1 - buf)

    start_all(t, buf)

    def tail_copy(i):
        dst = pl.multiple_of(tail_ref[i] * ROW_UNIT, ROW_UNIT)
        return pltpu.make_async_copy(zero_ref, xg_ref.at[pl.ds(dst, ROW_UNIT), :], tail_sem)

    @pl.when(t == last)
    def _():
        zero_ref[...] = jnp.zeros_like(zero_ref)
        n_tail = ntail_ref[0]
        lax.fori_loop(0, n_tail, lambda i, c: (tail_copy(i).start(), c)[1], 0)
        lax.fori_loop(0, n_tail, lambda i, c: (tail_copy(i).wait(), c)[1], 0)
        wait_all(t, buf)


def _dispatch(h2_p, slots_p, h2_s, slots_s, n_units, unit_dst, n_tail, tail_dst, rows_max):
    p_tiles = h2_p.shape[0] // TOKEN_TILE
    s_tiles = h2_s.shape[0] // TOKEN_TILE
    p_row = lambda t, *_: (jnp.minimum(t, p_tiles - 1), 0)
    s_row = lambda t, *_: (jnp.maximum(t - p_tiles, 0), 0)
    return pl.pallas_call(
        functools.partial(_dispatch_kernel, prompt_tiles=p_tiles),
        grid_spec=pltpu.PrefetchScalarGridSpec(
            num_scalar_prefetch=4,
            grid=(p_tiles + s_tiles,),
            in_specs=[pl.BlockSpec((TOKEN_TILE, D_MODEL), p_row),
                      pl.BlockSpec((TOKEN_TILE, LANES), p_row),
                      pl.BlockSpec((TOKEN_TILE, D_MODEL), s_row),
                      pl.BlockSpec((TOKEN_TILE, LANES), s_row)],
            out_specs=pl.BlockSpec(memory_space=pl.ANY),
            scratch_shapes=[pltpu.VMEM((2, LOCAL_ROWS, D_MODEL), BF16),
                            pltpu.VMEM((ROW_UNIT, D_MODEL), BF16),
                            pltpu.SemaphoreType.DMA((2,)), pltpu.SemaphoreType.DMA]),
        out_shape=jax.ShapeDtypeStruct((rows_max, D_MODEL), BF16),
        compiler_params=_params("arbitrary"),
        name="dispatch",
    )(n_units, unit_dst, n_tail, tail_dst, h2_p, slots_p, h2_s, slots_s)


def _expert_kernel(tg_ref, ge_ref, ng_ref, nused_ref, rows_ref, x_ref, bgu_ref, bd_ref, wgu_hbm, wd_hbm,
                   y_ref, wgu_f32, wd_f32, wgu_bf, wd_bf, sem):
    i = pl.program_id(0)

    def fetch(g, b):
        e = ge_ref[g]
        return (pltpu.make_async_copy(wgu_hbm.at[e], wgu_f32.at[b], sem.at[0, b]),
                pltpu.make_async_copy(wd_hbm.at[e], wd_f32.at[b], sem.at[1, b]))

    @pl.when(i == 0)
    def _():
        for cp in fetch(0, 0):
            cp.start()

    @pl.when(i < nused_ref[0])
    def _():
        g = tg_ref[i]
        b = g % 2

        @pl.when((i == 0) | (g != tg_ref[jnp.maximum(i - 1, 0)]))
        def _():
            @pl.when(g + 1 < ng_ref[0])
            def _():
                for cp in fetch(g + 1, 1 - b):
                    cp.start()

            for cp in fetch(g, b):
                cp.wait()
            wgu_bf[...] = wgu_f32[b].astype(BF16)
            wd_bf[...] = wd_f32[b].astype(BF16)

        for part in range(EXPERT_TILE // EXPERT_PART):
            @pl.when(rows_ref[i] > part * EXPERT_PART)
            def _():
                sl = slice(part * EXPERT_PART, (part + 1) * EXPERT_PART)
                gu = _dot(x_ref[sl, :], wgu_bf[...]) + bgu_ref[...]
                gate = jnp.minimum(gu[:, :D_FF], SWIGLU_LIMIT)
                up = jnp.clip(gu[:, D_FF:], -SWIGLU_LIMIT, SWIGLU_LIMIT)
                act = (up + 1.0) * gate * jax.nn.sigmoid(SWIGLU_ALPHA * gate)
                y_ref[sl, :] = (_dot(act.astype(BF16), wd_bf[...]) + bd_ref[...]).astype(y_ref.dtype)


def _experts(xg, tile_group, group_expert, n_groups, n_used, tile_rows, wgu, bgu, wd, bd):
    rows_max = xg.shape[0]
    used = lambda i, nu: jnp.maximum(jnp.minimum(i, nu[0] - 1), 0)
    row = lambda i, tg, ge, ng, nu, tr: (used(i, nu), 0)
    exp = lambda i, tg, ge, ng, nu, tr: (ge[tg[used(i, nu)]], 0, 0)
    return pl.pallas_call(
        _expert_kernel,
        grid_spec=pltpu.PrefetchScalarGridSpec(
            num_scalar_prefetch=5,
            grid=(rows_max // EXPERT_TILE,),
            in_specs=[pl.BlockSpec((EXPERT_TILE, D_MODEL), row),
                      pl.BlockSpec((None, 1, 2 * D_FF), exp),
                      pl.BlockSpec((None, 1, D_MODEL), exp),
                      pl.BlockSpec(memory_space=pl.ANY),
                      pl.BlockSpec(memory_space=pl.ANY)],
            out_specs=pl.BlockSpec((EXPERT_TILE, D_MODEL), row),
            scratch_shapes=[pltpu.VMEM((2, D_MODEL, 2 * D_FF), F32), pltpu.VMEM((2, D_FF, D_MODEL), F32),
                            pltpu.VMEM((D_MODEL, 2 * D_FF), BF16), pltpu.VMEM((D_FF, D_MODEL), BF16),
                            pltpu.SemaphoreType.DMA((2, 2))]),
        out_shape=jax.ShapeDtypeStruct((rows_max, D_MODEL), BF16),
        compiler_params=_params("arbitrary"),
        name="experts",
    )(tile_group, group_expert, n_groups, n_used, tile_rows, xg, bgu, bd, wgu, wd)


def _combine_kernel(nu_ref, src_ref, slot_ref, gatek_ref, x_ref, gmlp_ref, y_ref, o_ref, ys_ref, sem,
                    *, tile_offset):
    j = pl.program_id(0)
    t = j + tile_offset
    buf = j % 2
    tm = x_ref.shape[0]

    def unit_copy(tile, b, i):
        src = pl.multiple_of(src_ref[tile, i] * ROW_UNIT, ROW_UNIT)
        dst = pl.multiple_of(i * ROW_UNIT, ROW_UNIT)
        return pltpu.make_async_copy(y_ref.at[pl.ds(src, ROW_UNIT), :],
                                     ys_ref.at[b, pl.ds(dst, ROW_UNIT), :], sem.at[b])

    def fetch(tile, b):
        n_units = nu_ref[tile]
        lax.fori_loop(0, n_units, lambda i, c: (unit_copy(tile, b, i).start(), c)[1], 0)

        def zero_unit(i, c):
            ys_ref[b, pl.ds(pl.multiple_of(i * ROW_UNIT, ROW_UNIT), ROW_UNIT), :] = jnp.zeros(
                (ROW_UNIT, D_MODEL), ys_ref.dtype)
            return c

        lax.fori_loop(n_units, LOCAL_ROWS // ROW_UNIT, zero_unit, 0)

    @pl.when(j == 0)
    def _():
        fetch(t, buf)

    @pl.when(j + 1 < pl.num_programs(0))
    def _():
        fetch(t + 1, 1 - buf)

    rows = nu_ref[t] * ROW_UNIT

    @pl.when(rows > 0)
    def _():
        pltpu.make_async_copy(y_ref.at[pl.ds(0, rows), :], ys_ref.at[buf, pl.ds(0, rows), :],
                              sem.at[buf]).wait()

    slots = slot_ref[...].astype(jnp.int32)
    gates = gatek_ref[...]
    slot_cols = [slots[:, k:k + 1] for k in range(TOP_K)]
    gate_cols = [gates[:, k:k + 1] for k in range(TOP_K)]
    acc = jnp.zeros((tm, D_MODEL), F32)
    for c in range(LOCAL_ROWS // tm):
        acc = acc + _dot(_slot_matrix(slot_cols, gate_cols, c), ys_ref[buf, c * tm:(c + 1) * tm, :])
    o_ref[...] = x_ref[...] + gmlp_ref[...] * acc


def _combine(y, slots, gates, x1, gmlp, n_units, unit_src, tile_offset, per_token, tiles_per_seq):
    n = x1.shape[0]
    if per_token:
        mod = pl.BlockSpec((TOKEN_TILE, D_MODEL), lambda i, *_: (i, 0))
    else:
        mod = pl.BlockSpec((None, 1, D_MODEL), lambda i, *_: (i // tiles_per_seq, 0, 0))
    wide = pl.BlockSpec((TOKEN_TILE, D_MODEL), lambda i, *_: (i, 0))
    narrow = pl.BlockSpec((TOKEN_TILE, LANES), lambda i, *_: (i, 0))
    return pl.pallas_call(
        functools.partial(_combine_kernel, tile_offset=tile_offset),
        grid_spec=pltpu.PrefetchScalarGridSpec(
            num_scalar_prefetch=2,
            grid=(n // TOKEN_TILE,),
            in_specs=[narrow, narrow, wide, mod, pl.BlockSpec(memory_space=pl.ANY)],
            out_specs=wide,
            scratch_shapes=[pltpu.VMEM((2, LOCAL_ROWS, D_MODEL), BF16), pltpu.SemaphoreType.DMA((2,))]),
        out_shape=jax.ShapeDtypeStruct((n, D_MODEL), F32),
        compiler_params=_params("arbitrary"),
        name="combine",
    )(n_units, unit_src, slots, gates, x1, gmlp, y)


def _route_tables(cnt, rows_max):
    units = (cnt + ROW_UNIT - 1) // ROW_UNIT
    group_units = jnp.sum(units, axis=0)
    upt = EXPERT_TILE // ROW_UNIT
    group_pad = (group_units + upt - 1) // upt * upt
    group_end = jnp.cumsum(group_pad)
    group_start = group_end - group_pad
    seg_start = group_start[None, :] + jnp.cumsum(units, axis=0) - units
    local_end = jnp.cumsum(units, axis=1)
    local_start = local_end - units
    n_units = local_end[:, -1].astype(jnp.int32)

    def pick(lo, hi, pos, value):
        return jnp.sum(jnp.where((pos >= lo) & (pos < hi), value, 0), axis=-1).astype(jnp.int32)

    i = jnp.arange(LOCAL_ROWS // ROW_UNIT, dtype=jnp.int32)[None, :, None]
    unit_hbm = pick(local_start[:, None, :], local_end[:, None, :], i,
                    seg_start[:, None, :] + i - local_start[:, None, :])

    n_tail_e = group_pad - group_units
    j = jnp.arange(N_EXPERTS * upt, dtype=jnp.int32)[:, None]
    tail_end = jnp.cumsum(n_tail_e)
    tail_start = tail_end - n_tail_e
    tail_hbm = pick(tail_start[None, :], tail_end[None, :], j,
                    (group_start + group_units)[None, :] + j - tail_start[None, :])
    n_tail = tail_end[-1:].astype(jnp.int32)

    r = jnp.arange(rows_max // EXPERT_TILE, dtype=jnp.int32)[:, None] * upt
    n_used = (group_end[-1:] // upt).astype(jnp.int32)
    nonempty = group_units > 0
    group_of_expert = jnp.cumsum(nonempty) - 1
    tile_rows = pick(group_start[None, :], group_end[None, :], r,
                     jnp.clip(((group_start + group_units)[None, :] - r) * ROW_UNIT, 0, EXPERT_TILE))
    tile_group = pick(group_start[None, :], group_end[None, :], r, group_of_expert[None, :])
    g = jnp.arange(N_EXPERTS, dtype=jnp.int32)
    group_expert = jnp.sum(jnp.where(nonempty[None, :] & (group_of_expert[None, :] == g[:, None]),
                                     g[None, :], 0), axis=-1).astype(jnp.int32)
    n_groups = jnp.sum(nonempty)[None].astype(jnp.int32)
    return (n_units, unit_hbm, n_tail, tail_hbm,
            tile_group, group_expert, n_groups, n_used, tile_rows)


def _rope_tables(pos):
    half = HEAD_DIM // 2
    lane = np.arange(LANES)
    inv = jnp.asarray(ROPE_THETA, F32) ** (-jnp.asarray(lane % half, F32) / half)
    sign = jnp.asarray(np.where(lane % HEAD_DIM < half, -1.0, 1.0), F32)
    ang = pos.astype(F32)[:, None] * inv[None, :]
    return jnp.cos(ang), jnp.sin(ang) * sign[None, :]


def kernel(x_prompt, x_sample, c_prompt, c_sample, state_gla, cache_swa_k, cache_swa_v, w_ada, b_ada,
           norm1_g, norm2_g, w_in, w_gk2, b_gk, gla_norm_g, q_norm_g, k_norm_g, attn_sinks, w_o,
           w_router, b_router, w_gate_up, b_gate_up, w_down, b_down):
    batch, seq, d = x_prompt.shape
    n_seq, n_real, _ = x_sample.shape
    depth = w_in.shape[0]
    assert depth == 1 and d == D_MODEL and n_real <= SAMPLE_ROWS
    assert seq % TOKEN_TILE == 0 and (n_seq * SAMPLE_ROWS) % TOKEN_TILE == 0 and n_seq % SAMPLE_SEQS == 0
    R = SAMPLE_ROWS
    kvw = SWA_KV_HEADS * HEAD_DIM
    tiles_per_seq = seq // TOKEN_TILE

    qg, kg, vg, rg, glr_w, qs, ks, vs, ag, as_ = jnp.split(
        w_in[0], [512, 1024, 2048, 3072, 3088, 4112, 4368, 4624, 5648], axis=1)
    w_main = jnp.concatenate([vg, rg, qs, ag, as_, qg, kg, ks, vs], axis=1).astype(BF16)
    w_glr = jnp.pad(glr_w, ((0, 0), (0, LANES - GATE_RANK))).astype(BF16)
    wgk = jnp.pad(w_gk2[0], ((0, LANES - GATE_RANK), (0, 0)))
    bgk = b_gk[0].reshape(1, -1)
    gnorm = gla_norm_g[0].reshape(1, -1)
    gq = jnp.tile(q_norm_g[0], LANES // HEAD_DIM).reshape(1, LANES)
    gk = jnp.tile(k_norm_g[0], LANES // HEAD_DIM).reshape(1, LANES)
    seg = jnp.asarray(np.kron(np.eye(LANES // HEAD_DIM), np.ones((HEAD_DIM, HEAD_DIM))), BF16)
    sinks = attn_sinks[0]
    wo = w_o[0].astype(BF16)
    wr = jnp.pad(w_router[0], ((0, 0), (0, LANES - N_EXPERTS)))
    wr_hi = wr.astype(BF16)
    wr = jnp.stack([wr_hi, (wr - wr_hi.astype(F32)).astype(BF16)])
    br = jnp.pad(b_router[0], (0, LANES - N_EXPERTS), constant_values=-1e30).reshape(1, LANES)
    bgu = b_gate_up[0].reshape(N_EXPERTS, 1, 2 * D_FF)
    bd = b_down[0].reshape(N_EXPERTS, 1, D_MODEL)
    g1 = norm1_g[0].reshape(1, -1)
    g2 = norm2_g[0].reshape(1, -1)

    n_c = batch + n_seq
    c_all = jnp.pad(jnp.concatenate([c_prompt, c_sample], axis=0), ((0, -n_c % 8), (0, 0)))
    m_all = _ada(c_all, w_ada[0], b_ada[0])
    mp = [m_all[:batch, i * d:(i + 1) * d].reshape(batch, 1, d) for i in range(6)]
    ms = [jnp.repeat(m_all[batch:n_c, i * d:(i + 1) * d], R, axis=0) for i in range(6)]

    xp = x_prompt.reshape(batch * seq, d)
    xs = jnp.pad(x_sample, ((0, 0), (0, R - n_real), (0, 0))).reshape(n_seq * R, d)
    cos_p, sin_p = _rope_tables(jnp.arange(seq))
    cos_s, sin_s = _rope_tables(PAST_LEN + jnp.tile(jnp.arange(R), SAMPLE_SEQS))
    sink_rows = jnp.repeat(sinks, R).reshape(SWA_HEADS * R, 1)

    yp, glr_p = _inproj(xp, mp[0], mp[1], g1, w_main, w_glr, BF16, False, tiles_per_seq)
    ys, glr_s = _inproj(xs, ms[0], ms[1], g1, w_main, w_glr, F32, True, 1)
    og_p, st_p = _gla_prompt(yp, glr_p, wgk, bgk, gnorm, batch, seq)
    og_s, st_s = _gla_sample(ys, glr_s, wgk, bgk, gnorm, state_gla[0], n_seq, n_real)
    os_p, kc_p, vc_p = _swa_prompt(yp, sinks, cos_p, sin_p, gq, gk, seg, batch, seq)
    os_s, kc_s, vc_s = _swa_sample(ys, sink_rows, cache_swa_k[0].reshape(n_seq, WINDOW, kvw),
                                   cache_swa_v[0].reshape(n_seq, WINDOW, kvw),
                                   cos_s, sin_s, gq, gk, seg, n_seq, n_real)
    x1_p, h2_p, sl_p, gt_p, cnt_p = _post(yp, og_p, os_p, xp, mp[2], mp[3], mp[4], g2, wo, wr, br,
                                          False, tiles_per_seq, R)
    x1_s, h2_s, sl_s, gt_s, cnt_s = _post(ys, og_s, os_s, xs, ms[2], ms[3], ms[4], g2, wo, wr, br,
                                          True, 1, n_real)

    cnt = jnp.concatenate([cnt_p, cnt_s], axis=0)[:, 0, :N_EXPERTS].astype(jnp.int32)
    n_tiles = cnt.shape[0]
    rows_bound = (TOP_K * (batch * seq + n_seq * n_real) + n_tiles * N_EXPERTS * (ROW_UNIT - 1)
                  + N_EXPERTS * (EXPERT_TILE - 1))
    rows_max = -(-rows_bound // EXPERT_TILE) * EXPERT_TILE
    (n_units, unit_hbm, n_tail, tail_hbm,
     tile_group, group_expert, n_groups, n_used, tile_rows) = _route_tables(cnt, rows_max)
    xg = _dispatch(h2_p, sl_p, h2_s, sl_s, n_units, unit_hbm, n_tail, tail_hbm, rows_max)
    yg = _experts(xg, tile_group, group_expert, n_groups, n_used, tile_rows,
                  w_gate_up[0], bgu, w_down[0], bd)
    p_tiles = batch * seq // TOKEN_TILE
    out_p = _combine(yg, sl_p, gt_p, x1_p, mp[5], n_units, unit_hbm, 0, False, tiles_per_seq)
    out_s = _combine(yg, sl_s, gt_s, x1_s, ms[5], n_units, unit_hbm, p_tiles, True, 1)

    cache_shape = (WINDOW, SWA_KV_HEADS, HEAD_DIM)
    return (out_p.reshape(batch, seq, d),
            out_s.reshape(n_seq, R, d)[:, :n_real],
            st_p[None],
            kc_p.reshape(1, batch, *cache_shape),
            vc_p.reshape(1, batch, *cache_shape),
            st_s[None],
            kc_s.reshape(1, n_seq, *cache_shape),
            vc_s.reshape(1, n_seq, *cache_shape))
```

```python
import functools

import numpy as np
import jax
import jax.numpy as jnp
from jax import lax
from jax.experimental import pallas as pl
from jax.experimental.pallas import tpu as pltpu

F32 = jnp.float32
BF16 = jnp.bfloat16
HIGHEST = lax.Precision.HIGHEST

D_MODEL = 1024
PAST_LEN = 16384
GLA_HEADS = 4
GLA_DK = 128
GLA_DV = 256
GATE_RANK = 16
GATE_TAU = 16.0
SWA_HEADS = 16
SWA_KV_HEADS = 4
HEAD_DIM = 64
SWA_GROUP = SWA_HEADS // SWA_KV_HEADS
WINDOW = 128
ROPE_THETA = 10000.0
N_EXPERTS = 32
TOP_K = 4
D_FF = 1024
SWIGLU_ALPHA = 1.702
SWIGLU_LIMIT = 7.0
NORM_EPS = 1e-6

LANES = 128
SUBLANES = 8
SAMPLE_ROWS = 8
SAMPLE_SEQS = 16
TOKEN_TILE = 512
SWA_STEP_BLOCKS = 2
GLA_BLOCK = 256
VMEM_LIMIT = 56 * 1024 * 1024
ROW_UNIT = 16
EXPERT_TILE = 1024
EXPERT_PART = 256
LOCAL_ROWS = -(-(TOP_K * TOKEN_TILE + N_EXPERTS * (ROW_UNIT - 1)) // TOKEN_TILE) * TOKEN_TILE

COL_VG, COL_RG, COL_QS, COL_AG, COL_AS = 0, 1024, 2048, 3072, 4096
COL_QG, COL_KG, COL_KS, COL_VS = 5120, 5632, 6144, 6400
D_MAIN = 6656
PROJ_CHUNK = 512


def _dot(a, b, precision=None):
    return jnp.dot(a, b, preferred_element_type=F32, precision=precision)


def _dot_nt(a, b, precision=None):
    return lax.dot_general(a, b, (((1,), (1,)), ((), ())), preferred_element_type=F32, precision=precision)


def _dot_tn(a, b, precision=None):
    return lax.dot_general(a, b, (((0,), (0,)), ((), ())), preferred_element_type=F32, precision=precision)


def _dot_sum(sel, x):
    hi = x.astype(BF16)
    r1 = x - hi.astype(F32)
    mid = r1.astype(BF16)
    lo = (r1 - mid.astype(F32)).astype(BF16)
    return _dot(sel, hi) + _dot(sel, mid) + _dot(sel, lo)


def _params(*sem):
    return pltpu.CompilerParams(dimension_semantics=sem, vmem_limit_bytes=VMEM_LIMIT)


def _rms(x, g):
    return x * lax.rsqrt(jnp.mean(x * x, axis=-1, keepdims=True) + NORM_EPS) * g


def _log_sigmoid(x):
    return jnp.minimum(x, 0.0) - jnp.log(1.0 + jnp.exp(-jnp.abs(x)))


def _ada_kernel(c_ref, w_ref, b_ref, o_ref):
    c = c_ref[...]
    s = c * jax.nn.sigmoid(c)
    o_ref[...] = _dot(s, w_ref[...], HIGHEST) + b_ref[...]


def _ada(c_all, w_ada, b_ada):
    rows = c_all.shape[0]
    tn = 768
    return pl.pallas_call(
        _ada_kernel,
        grid=(6 * D_MODEL // tn,),
        in_specs=[pl.BlockSpec((rows, D_MODEL), lambda j: (0, 0)),
                  pl.BlockSpec((D_MODEL, tn), lambda j: (0, j)),
                  pl.BlockSpec((1, tn), lambda j: (0, j))],
        out_specs=pl.BlockSpec((rows, tn), lambda j: (0, j)),
        out_shape=jax.ShapeDtypeStruct((rows, 6 * D_MODEL), F32),
        compiler_params=_params("parallel"),
        name="ada",
    )(c_all, w_ada, b_ada.reshape(1, -1))


def _inproj_kernel(x_ref, shift_ref, scale_ref, g_ref, w_ref, wg_ref, o_ref, og_ref):
    rows = x_ref.shape[0]
    h = _rms(x_ref[...], g_ref[...]) * (1.0 + _mod_rows(scale_ref, rows)) + _mod_rows(shift_ref, rows)
    hb = h.astype(BF16)
    for j in range(D_MAIN // PROJ_CHUNK):
        sl = slice(j * PROJ_CHUNK, (j + 1) * PROJ_CHUNK)
        o_ref[:, sl] = _dot(hb, w_ref[:, sl]).astype(o_ref.dtype)
    og_ref[...] = _dot(hb, wg_ref[...])


def _mod_spec(per_token, tiles_per_seq):
    if per_token:
        return pl.BlockSpec((TOKEN_TILE // SAMPLE_ROWS, D_MODEL), lambda i, *_: (i, 0))
    return pl.BlockSpec((None, 1, D_MODEL), lambda i, *_: (i // tiles_per_seq, 0, 0))


def _mod_rows(ref, rows):
    m = ref[...]
    if m.shape[0] == 1:
        return m
    return jnp.broadcast_to(m[:, None, :], (m.shape[0], rows // m.shape[0], m.shape[1])).reshape(
        rows, m.shape[1])


def _inproj(x, shift, scale, g, w_main, w_glr, out_dtype, per_token, tiles_per_seq):
    n = x.shape[0]
    mod = _mod_spec(per_token, tiles_per_seq)
    const = lambda i: (0, 0)
    return pl.pallas_call(
        _inproj_kernel,
        grid=(n // TOKEN_TILE,),
        in_specs=[pl.BlockSpec((TOKEN_TILE, D_MODEL), lambda i: (i, 0)), mod, mod,
                  pl.BlockSpec((1, D_MODEL), const),
                  pl.BlockSpec((D_MODEL, D_MAIN), const, pipeline_mode=pl.Buffered(1)),
                  pl.BlockSpec((D_MODEL, LANES), const, pipeline_mode=pl.Buffered(1))],
        out_specs=[pl.BlockSpec((TOKEN_TILE, D_MAIN), lambda i: (i, 0)),
                   pl.BlockSpec((TOKEN_TILE, LANES), lambda i: (i, 0))],
        out_shape=[jax.ShapeDtypeStruct((n, D_MAIN), out_dtype),
                   jax.ShapeDtypeStruct((n, LANES), F32)],
        compiler_params=_params("parallel"),
        name="inproj",
    )(x, shift, scale, g, w_main, w_glr)


def _gla_log_gate(glr, wgk, bgk):
    return _log_sigmoid(_dot(glr, wgk, HIGHEST) + bgk) * (1.0 / GATE_TAU)


def _gla_out(o, r, g):
    r = r.astype(F32)
    return _rms(o, g) * (r * jax.nn.sigmoid(r))


def _gla_pair_levels(n):
    t = np.arange(n)[:, None]
    s = np.arange(n)[None, :]
    x = t ^ s
    top = np.where(x > 0, 1 << np.floor(np.log2(np.maximum(x, 1))).astype(np.int64), 0)
    return np.where(s > t, -1, top).astype(np.int32)


def _gla_block_ref(b, h):
    n, w = b.shape
    if 2 * h == n:
        return jnp.broadcast_to(b[h - 1:h, :], (n, w))
    if h >= SUBLANES // 2:
        picked = b.reshape(n // (2 * h), 2 * h, w)[:, h - 1:h, :]
        return jnp.broadcast_to(picked, (n // (2 * h), 2 * h, w)).reshape(n, w)
    r = lax.broadcasted_iota(jnp.int32, (n, 1), 0) % (2 * h)
    out = b
    for d in range(1, h + 1):
        out = jnp.where(r == h - 1 + d, pltpu.roll(b, d, 0), out)
    for d in range(1, h):
        out = jnp.where(r == h - 1 - d, pltpu.roll(b, n - d, 0), out)
    return out


def _gla_prompt_kernel(q_ref, k_ref, v_ref, r_ref, glr_ref, wgk_ref, bgk_ref, g_ref, lev_ref,
                       o_ref, s_ref, st_ref):
    c = pl.program_id(1)

    @pl.when(c == 0)
    def _():
        st_ref[...] = jnp.zeros_like(st_ref)

    n = GLA_BLOCK
    lev = lev_ref[...]
    causal = lax.broadcasted_iota(jnp.int32, (n, n), 1) <= lax.broadcasted_iota(jnp.int32, (n, n), 0)
    lg = _gla_log_gate(glr_ref[...], wgk_ref[...], bgk_ref[...])
    b = _dot_sum(jnp.where(causal, 1.0, 0.0).astype(BF16), lg)
    b_last = b[n - 1:n, :]
    q = q_ref[...].astype(F32) * (GLA_DK ** -0.5)
    k = k_ref[...].astype(F32)
    qe = (q * jnp.exp(b)).astype(BF16)
    kd = (k * jnp.exp(b_last - b)).astype(BF16)
    decay = jnp.exp(b_last)
    levels = [0] + [1 << p for p in range(n.bit_length() - 1)]
    q_lv, k_lv = [q.astype(BF16)], [k.astype(BF16)]
    for h in levels[1:]:
        e = jnp.exp(-jnp.abs(b - _gla_block_ref(b, h)))
        q_lv.append((q * e).astype(BF16))
        k_lv.append((k * e).astype(BF16))
    for h in range(GLA_HEADS):
        dk = slice(h * GLA_DK, (h + 1) * GLA_DK)
        dv = slice(h * GLA_DV, (h + 1) * GLA_DV)
        attn = jnp.zeros((n, n), F32)
        for level, ql, kl in zip(levels, q_lv, k_lv):
            attn = jnp.where(lev == level, _dot_nt(ql[:, dk], kl[:, dk]), attn)
        v = v_ref[:, dv]
        st = st_ref[h]
        o = _dot_nt(qe[:, dk], st.astype(BF16)) + _dot(attn.astype(BF16), v)
        st_ref[h] = st * decay[:, dk] + _dot_tn(v, kd[:, dk])
        o_ref[:, dv] = _gla_out(o, r_ref[:, dv], g_ref[...]).astype(o_ref.dtype)

    @pl.when(c == pl.num_programs(1) - 1)
    def _():
        for h in range(GLA_HEADS):
            s_ref[h] = st_ref[h].T


def _gla_prompt(yp, glr, wgk, bgk, gnorm, batch, seq):
    nb = seq // GLA_BLOCK
    hk, hv = GLA_HEADS * GLA_DK, GLA_HEADS * GLA_DV
    tok = lambda b, c: b * nb + c
    const = lambda b, c: (0, 0)
    return pl.pallas_call(
        _gla_prompt_kernel,
        grid=(batch, nb),
        in_specs=[pl.BlockSpec((GLA_BLOCK, hk), lambda b, c: (tok(b, c), COL_QG // hk)),
                  pl.BlockSpec((GLA_BLOCK, hk), lambda b, c: (tok(b, c), COL_KG // hk)),
                  pl.BlockSpec((GLA_BLOCK, hv), lambda b, c: (tok(b, c), COL_VG // hv)),
                  pl.BlockSpec((GLA_BLOCK, hv), lambda b, c: (tok(b, c), COL_RG // hv)),
                  pl.BlockSpec((GLA_BLOCK, LANES), lambda b, c: (tok(b, c), 0)),
                  pl.BlockSpec((LANES, hk), const),
                  pl.BlockSpec((1, hk), const),
                  pl.BlockSpec((1, GLA_DV), const),
                  pl.BlockSpec((GLA_BLOCK, GLA_BLOCK), const)],
        out_specs=[pl.BlockSpec((GLA_BLOCK, hv), lambda b, c: (tok(b, c), 0)),
                   pl.BlockSpec((None, GLA_HEADS, GLA_DK, GLA_DV), lambda b, c: (b, 0, 0, 0))],
        out_shape=[jax.ShapeDtypeStruct((batch * seq, hv), BF16),
                   jax.ShapeDtypeStruct((batch, GLA_HEADS, GLA_DK, GLA_DV), F32)],
        scratch_shapes=[pltpu.VMEM((GLA_HEADS, GLA_DV, GLA_DK), F32)],
        compiler_params=_params("parallel", "arbitrary"),
        name="gla_prompt",
    )(yp, yp, yp, yp, glr, wgk, bgk, gnorm, jnp.asarray(_gla_pair_levels(GLA_BLOCK)))


def _gla_sample_kernel(q_ref, k_ref, v_ref, r_ref, glr_ref, wgk_ref, bgk_ref, g_ref, s0_ref, lev_ref,
                       o_ref, s_ref, *, n_real):
    R = SAMPLE_ROWS
    rows = q_ref.shape[0]
    row = lax.broadcasted_iota(jnp.int32, (rows, rows), 0)
    col = lax.broadcasted_iota(jnp.int32, (rows, rows), 1)
    same = (row // R) == (col // R)
    causal = same & (col <= row)
    real = lax.broadcasted_iota(jnp.int32, (rows, 1), 0) % R < n_real
    lg = jnp.where(real, _gla_log_gate(glr_ref[...], wgk_ref[...], bgk_ref[...]), 0.0)
    b = _dot_sum(jnp.where(causal, 1.0, 0.0).astype(BF16), lg)
    b_last = _dot_sum(jnp.where(same, 1.0, 0.0).astype(BF16), lg)
    q = q_ref[...] * (GLA_DK ** -0.5)
    k = jnp.where(real, k_ref[...], 0.0)
    v = v_ref[...]
    qe = q * jnp.exp(b)
    kd = k * jnp.exp(b_last - b)
    lev = lev_ref[...]
    levels = [0] + [1 << p for p in range(R.bit_length() - 1)]
    q_lv, k_lv = [q.astype(BF16)], [k.astype(BF16)]
    for h in levels[1:]:
        e = jnp.exp(-jnp.abs(b - _gla_block_ref(b, h)))
        q_lv.append((q * e).astype(BF16))
        k_lv.append((k * e).astype(BF16))
    v_bf = v.astype(BF16)
    for h in range(GLA_HEADS):
        dk = slice(h * GLA_DK, (h + 1) * GLA_DK)
        dv = slice(h * GLA_DV, (h + 1) * GLA_DV)
        attn = jnp.zeros((rows, rows), F32)
        for level, ql, kl in zip(levels, q_lv, k_lv):
            attn = jnp.where(lev == level, _dot_nt(ql[:, dk], kl[:, dk]), attn)
        o_intra = _dot(attn.astype(BF16), v_bf[:, dv])
        decay_t = jnp.exp(b_last[:, dk]).T
        outs = []
        for s in range(rows // R):
            sl = slice(s * R, (s + 1) * R)
            s0 = s0_ref[s, h]
            outs.append(_dot(qe[sl, dk], s0) + o_intra[sl, :])
            s_ref[s, h] = s0 * decay_t[:, s * R:s * R + 1] + _dot_tn(kd[sl, dk], v[sl, dv])
        o_ref[:, dv] = _gla_out(jnp.concatenate(outs, axis=0), r_ref[:, dv], g_ref[...])


def _gla_sample(ys, glr, wgk, bgk, gnorm, state, n_seq, n_real):
    R = SAMPLE_SEQS * SAMPLE_ROWS
    hk, hv = GLA_HEADS * GLA_DK, GLA_HEADS * GLA_DV
    st_spec = pl.BlockSpec((SAMPLE_SEQS, GLA_HEADS, GLA_DK, GLA_DV), lambda b: (b, 0, 0, 0))
    lev = _gla_pair_levels(R)
    lev = np.where(lev >= SAMPLE_ROWS, -1, lev)
    return pl.pallas_call(
        functools.partial(_gla_sample_kernel, n_real=n_real),
        grid=(n_seq // SAMPLE_SEQS,),
        in_specs=[pl.BlockSpec((R, hk), lambda b: (b, COL_QG // hk)),
                  pl.BlockSpec((R, hk), lambda b: (b, COL_KG // hk)),
                  pl.BlockSpec((R, hv), lambda b: (b, COL_VG // hv)),
                  pl.BlockSpec((R, hv), lambda b: (b, COL_RG // hv)),
                  pl.BlockSpec((R, LANES), lambda b: (b, 0)),
                  pl.BlockSpec((LANES, hk), lambda b: (0, 0)),
                  pl.BlockSpec((1, hk), lambda b: (0, 0)),
                  pl.BlockSpec((1, GLA_DV), lambda b: (0, 0)),
                  st_spec,
                  pl.BlockSpec((R, R), lambda b: (0, 0))],
        out_specs=[pl.BlockSpec((R, hv), lambda b: (b, 0)), st_spec],
        out_shape=[jax.ShapeDtypeStruct((n_seq * SAMPLE_ROWS, hv), F32),
                   jax.ShapeDtypeStruct((n_seq, GLA_HEADS, GLA_DK, GLA_DV), F32)],
        compiler_params=_params("parallel"),
        name="gla_sample",
    )(ys, ys, ys, ys, glr, wgk, bgk, gnorm, state, jnp.asarray(lev))


def _lane_lower(shape):
    return lax.broadcasted_iota(jnp.int32, shape, len(shape) - 1) % LANES < HEAD_DIM


def _head_norm_rope(x, g, cos, sin, seg):
    sq = x * x
    hi = sq.astype(BF16)
    lo = (sq - hi.astype(F32)).astype(BF16)
    ss = _dot(hi, seg) + _dot(lo, seg)
    y = x * lax.rsqrt(ss * (1.0 / HEAD_DIM) + NORM_EPS) * g
    half = HEAD_DIM // 2
    lane = lax.broadcasted_iota(jnp.int32, y.shape, 1)
    rot = jnp.where(lane % HEAD_DIM < half, pltpu.roll(y, LANES - half, 1), pltpu.roll(y, half, 1))
    return y * cos + rot * sin


def _both_halves(blk, half):
    sw = pltpu.roll(blk, HEAD_DIM, 1)
    lower = _lane_lower(blk.shape)
    return jnp.where(lower, blk, sw) if half == 0 else jnp.where(lower, sw, blk)


def _stack_heads(q_blocks):
    parts = []
    for qb in q_blocks:
        lower = _lane_lower(qb.shape)
        zero = jnp.zeros_like(qb)
        parts += [jnp.where(lower, qb, zero), jnp.where(lower, zero, qb)]
    return jnp.concatenate(parts, axis=0)


def _swa_prompt_kernel(sink_ref, q_ref, k_ref, v_ref, cos_ref, sin_ref, gq_ref, gk_ref, seg_ref,
                       o_ref, ko_ref, vo_ref, kprev_ref, vprev_ref):
    n = pl.program_id(1)
    W = WINDOW

    @pl.when(n == 0)
    def _():
        kprev_ref[...] = jnp.zeros_like(kprev_ref)
        vprev_ref[...] = jnp.zeros_like(vprev_ref)

    seg = seg_ref[...]
    k_prev, v_prev = kprev_ref[...], vprev_ref[...]
    for sub in range(SWA_STEP_BLOCKS):
        tok = slice(sub * W, (sub + 1) * W)
        cos, sin = cos_ref[tok, :], sin_ref[tok, :]
        k_cur = jnp.concatenate(
            [_head_norm_rope(k_ref[tok, j * LANES:(j + 1) * LANES].astype(F32), gk_ref[...], cos, sin, seg)
             for j in range(SWA_KV_HEADS * HEAD_DIM // LANES)], axis=1)
        v_cur = v_ref[tok, :].astype(F32)
        prev_fill = jnp.where(n > 0, 0.0, -jnp.inf) if sub == 0 else 0.0
        _swa_block(sink_ref, q_ref, o_ref, tok, k_prev, v_prev, k_cur, v_cur, prev_fill,
                   cos, sin, seg, gq_ref[...])
        k_prev, v_prev = k_cur, v_cur
    ko_ref[...] = k_prev
    vo_ref[...] = v_prev
    kprev_ref[...] = k_prev
    vprev_ref[...] = v_prev


def _swa_block(sink_ref, q_ref, o_ref, tok, k_prev, v_prev, k_cur, v_cur, prev_fill, cos, sin, seg, gq):
    W = WINDOW
    qi = lax.broadcasted_iota(jnp.int32, (W, W), 0)
    ki = lax.broadcasted_iota(jnp.int32, (W, W), 1)
    from_cur = ki <= qi
    for kh in range(SWA_KV_HEADS):
        blk = slice((kh // 2) * LANES, (kh // 2 + 1) * LANES)
        kb_prev = _both_halves(k_prev[:, blk], kh % 2).astype(BF16)
        kb_cur = _both_halves(k_cur[:, blk], kh % 2).astype(BF16)
        vb_prev = _both_halves(v_prev[:, blk], kh % 2).astype(BF16)
        vb_cur = _both_halves(v_cur[:, blk], kh % 2).astype(BF16)
        qblocks = []
        for j in range(2):
            c0 = (2 * kh + j) * LANES
            qn = _head_norm_rope(q_ref[tok, c0:c0 + LANES].astype(F32), gq, cos, sin, seg)
            qblocks.append(qn * (HEAD_DIM ** -0.5))
        qs = _stack_heads(qblocks).astype(BF16)
        s_prev = _dot_nt(qs, kb_prev)
        s_cur = _dot_nt(qs, kb_cur)
        outs = []
        for g in range(SWA_GROUP):
            rows = slice(g * W, (g + 1) * W)
            sg = jnp.where(from_cur, s_cur[rows, :], s_prev[rows, :] + prev_fill)
            sink = sink_ref[kh * SWA_GROUP + g]
            m = jnp.maximum(jnp.max(sg, axis=-1, keepdims=True), sink)
            p = jnp.exp(sg - m)
            denom = jnp.sum(p, axis=-1, keepdims=True) + jnp.exp(sink - m)
            p_cur = jnp.where(from_cur, p, 0.0).astype(BF16)
            p_prev = jnp.where(from_cur, 0.0, p).astype(BF16)
            outs.append((_dot(p_prev, vb_prev) + _dot(p_cur, vb_cur)) / denom)
        lower = _lane_lower((W, LANES))
        for j in range(2):
            c0 = (2 * kh + j) * LANES
            o_ref[tok, c0:c0 + LANES] = jnp.where(lower, outs[2 * j], outs[2 * j + 1]).astype(o_ref.dtype)


def _swa_prompt(yp, sinks, cos, sin, gq, gk, seg, batch, seq):
    rows = SWA_STEP_BLOCKS * WINDOW
    nb = seq // rows
    kvw = SWA_KV_HEADS * HEAD_DIM
    tok = lambda b, n: b * nb + n
    const = lambda b, n: (0, 0)
    cache_spec = pl.BlockSpec((None, WINDOW, kvw), lambda b, n: (b, 0, 0))
    return pl.pallas_call(
        _swa_prompt_kernel,
        grid=(batch, nb),
        in_specs=[pl.BlockSpec(memory_space=pltpu.SMEM),
                  pl.BlockSpec((rows, D_MODEL), lambda b, n: (tok(b, n), COL_QS // D_MODEL)),
                  pl.BlockSpec((rows, kvw), lambda b, n: (tok(b, n), COL_KS // kvw)),
                  pl.BlockSpec((rows, kvw), lambda b, n: (tok(b, n), COL_VS // kvw)),
                  pl.BlockSpec((rows, LANES), lambda b, n: (n, 0)),
                  pl.BlockSpec((rows, LANES), lambda b, n: (n, 0)),
                  pl.BlockSpec((1, LANES), const), pl.BlockSpec((1, LANES), const),
                  pl.BlockSpec((LANES, LANES), const)],
        out_specs=[pl.BlockSpec((rows, D_MODEL), lambda b, n: (tok(b, n), 0)), cache_spec, cache_spec],
        out_shape=[jax.ShapeDtypeStruct((batch * seq, D_MODEL), BF16),
                   jax.ShapeDtypeStruct((batch, WINDOW, kvw), F32),
                   jax.ShapeDtypeStruct((batch, WINDOW, kvw), F32)],
        scratch_shapes=[pltpu.VMEM((WINDOW, kvw), F32), pltpu.VMEM((WINDOW, kvw), F32)],
        compiler_params=_params("parallel", "arbitrary"),
        name="swa_prompt",
    )(sinks, yp, yp, yp, cos, sin, gq, gk, seg)


def _shift_cache(cache, new, n_real):
    R = SAMPLE_ROWS
    rolled = pltpu.roll(cache, WINDOW - n_real, 0)
    tail_new = pltpu.roll(new, R - n_real, 0)
    row = lax.broadcasted_iota(jnp.int32, (R, cache.shape[1]), 0)
    tail = jnp.where(row < R - n_real, rolled[WINDOW - R:, :], tail_new)
    return jnp.concatenate([rolled[:WINDOW - R, :], tail], axis=0)


def _swa_sample_kernel(q_ref, k_ref, v_ref, kc_ref, vc_ref, cos_ref, sin_ref, gq_ref, gk_ref,
                       seg_ref, sink_ref, o_ref, ko_ref, vo_ref, *, n_real):
    R, W = SAMPLE_ROWS, WINDOW
    kvw = SWA_KV_HEADS * HEAD_DIM
    cos, sin, seg = cos_ref[...], sin_ref[...], seg_ref[...]
    k_new = jnp.concatenate(
        [_head_norm_rope(k_ref[:, j * LANES:(j + 1) * LANES], gk_ref[...], cos, sin, seg)
         for j in range(kvw // LANES)], axis=1)
    v_new = v_ref[...]
    q_pairs = [_head_norm_rope(q_ref[:, j * LANES:(j + 1) * LANES], gq_ref[...], cos, sin, seg)
               * (HEAD_DIM ** -0.5) for j in range(SWA_HEADS // 2)]

    hr = SWA_HEADS * R
    t_c = lax.broadcasted_iota(jnp.int32, (hr, W), 0) % R
    mask_c = lax.broadcasted_iota(jnp.int32, (hr, W), 1) > t_c
    t_n = lax.broadcasted_iota(jnp.int32, (hr, R), 0) % R
    mask_n = lax.broadcasted_iota(jnp.int32, (hr, R), 1) <= t_n
    sink = sink_ref[...]
    lower = _lane_lower((R, LANES))
    zeros = jnp.zeros((R, LANES), F32)
    for s in range(q_ref.shape[0] // R):
        sl = slice(s * R, (s + 1) * R)
        kc, vc = kc_ref[s], vc_ref[s]
        kn, vn = k_new[sl, :], v_new[sl, :]
        ko_ref[s] = _shift_cache(kc, kn, n_real)
        vo_ref[s] = _shift_cache(vc, vn, n_real)
        q_rows = []
        for h in range(SWA_HEADS):
            kh = h // SWA_GROUP
            x = q_pairs[h // 2][sl, :]
            if h % 2 != kh % 2:
                x = pltpu.roll(x, HEAD_DIM, 1)
            x = jnp.where(lower, x, zeros) if kh % 2 == 0 else jnp.where(lower, zeros, x)
            q_rows.append(jnp.concatenate([x, zeros] if kh // 2 == 0 else [zeros, x], axis=1))
        qbd = jnp.concatenate(q_rows, axis=0)
        sc = jnp.where(mask_c, _dot_nt(qbd.astype(BF16), kc.astype(BF16)), -jnp.inf)
        sn = jnp.where(mask_n, _dot_nt(qbd, kn), -jnp.inf)
        m = jnp.maximum(jnp.maximum(jnp.max(sc, axis=-1, keepdims=True),
                                    jnp.max(sn, axis=-1, keepdims=True)), sink)
        pc, pn = jnp.exp(sc - m), jnp.exp(sn - m)
        denom = (jnp.sum(pc, axis=-1, keepdims=True) + jnp.sum(pn, axis=-1, keepdims=True)
                 + jnp.exp(sink - m))
        o = (_dot(pc.astype(BF16), vc.astype(BF16)) + _dot(pn, vn)) / denom
        for j in range(SWA_HEADS // 2):
            halves = []
            for h in (2 * j, 2 * j + 1):
                kh = h // SWA_GROUP
                y = o[h * R:(h + 1) * R, (kh // 2) * LANES:(kh // 2 + 1) * LANES]
                halves.append(pltpu.roll(y, HEAD_DIM, 1) if h % 2 != kh % 2 else y)
            o_ref[sl, j * LANES:(j + 1) * LANES] = jnp.where(lower, halves[0], halves[1])


def _swa_sample(ys, sink_rows, kcache, vcache, cos, sin, gq, gk, seg, n_seq, n_real):
    R = SAMPLE_SEQS * SAMPLE_ROWS
    kvw = SWA_KV_HEADS * HEAD_DIM
    const = lambda b: (0, 0)
    cache_spec = pl.BlockSpec((SAMPLE_SEQS, WINDOW, kvw), lambda b: (b, 0, 0))
    return pl.pallas_call(
        functools.partial(_swa_sample_kernel, n_real=n_real),
        grid=(n_seq // SAMPLE_SEQS,),
        in_specs=[pl.BlockSpec((R, D_MODEL), lambda b: (b, COL_QS // D_MODEL)),
                  pl.BlockSpec((R, kvw), lambda b: (b, COL_KS // kvw)),
                  pl.BlockSpec((R, kvw), lambda b: (b, COL_VS // kvw)),
                  cache_spec, cache_spec,
                  pl.BlockSpec((R, LANES), const), pl.BlockSpec((R, LANES), const),
                  pl.BlockSpec((1, LANES), const), pl.BlockSpec((1, LANES), const),
                  pl.BlockSpec((LANES, LANES), const),
                  pl.BlockSpec((SWA_HEADS * SAMPLE_ROWS, 1), const)],
        out_specs=[pl.BlockSpec((R, D_MODEL), lambda b: (b, 0)), cache_spec, cache_spec],
        out_shape=[jax.ShapeDtypeStruct((n_seq * SAMPLE_ROWS, D_MODEL), F32),
                   jax.ShapeDtypeStruct((n_seq, WINDOW, kvw), F32),
                   jax.ShapeDtypeStruct((n_seq, WINDOW, kvw), F32)],
        compiler_params=_params("parallel"),
        name="swa_sample",
    )(ys, ys, ys, kcache, vcache, cos, sin, gq, gk, seg, sink_rows)


def _post_kernel(ag_ref, as_ref, og_ref, os_ref, x_ref, gate_ref, shift_ref, scale_ref, g2_ref,
                 wo_ref, wr_ref, br_ref, x1_ref, h2_ref, slot_ref, gatek_ref, cnt_ref, *, n_valid):
    merged =(jax.nn.sigmoid(ag_ref[...].astype(F32)) * og_ref[...].astype(F32)
              + jax.nn.sigmoid(as_ref[...].astype(F32)) * os_ref[...].astype(F32))
    y = _dot(merged.astype(BF16), wo_ref[...])
    rows = x_ref.shape[0]
    x1 = x_ref[...] + _mod_rows(gate_ref, rows) * y
    x1_ref[...] = x1
    h2 = _rms(x1, g2_ref[...]) * (1.0 + _mod_rows(scale_ref, rows)) + _mod_rows(shift_ref, rows)
    h2_hi = h2.astype(BF16)
    h2_ref[...] = h2_hi

    h2_lo = (h2 - h2_hi.astype(F32)).astype(BF16)
    w_hi, w_lo = wr_ref[0], wr_ref[1]
    logits = _dot(h2_hi, w_hi) + (_dot(h2_lo, w_hi) + _dot(h2_hi, w_lo)) + br_ref[...]
    lane_i = lax.broadcasted_iota(jnp.int32, logits.shape, 1)
    lane = lane_i.astype(F32)
    work = logits
    vals, hots = [], []
    for _ in range(TOP_K):
        m = jnp.max(work, axis=-1, keepdims=True)
        idx = jnp.min(jnp.where(work == m, lane, float(LANES)), axis=-1, keepdims=True)
        hot = lane == idx
        vals.append(m)
        hots.append(hot)
        work = jnp.where(hot, -jnp.inf, work)
    exps = [jnp.exp(v - vals[0]) for v in vals]
    denom = exps[0] + exps[1] + exps[2] + exps[3]

    tm = logits.shape[0]
    valid = lax.broadcasted_iota(jnp.int32, (tm, 1), 0) % SAMPLE_ROWS < n_valid
    sel = jnp.zeros_like(logits)
    for hot in hots:
        sel = jnp.where(hot, 1.0, sel)
    sel = jnp.where(valid, sel, 0.0)
    earlier = (lax.broadcasted_iota(jnp.int32, (tm, tm), 1)
               < lax.broadcasted_iota(jnp.int32, (tm, tm), 0))
    rank = _dot(jnp.where(earlier, 1.0, 0.0).astype(BF16), sel.astype(BF16))
    cnt = jnp.sum(sel, axis=0, keepdims=True)
    cnt_pad = jnp.floor((cnt + (ROW_UNIT - 1.0)) * (1.0 / ROW_UNIT)) * ROW_UNIT
    below = (lax.broadcasted_iota(jnp.int32, (LANES, LANES), 0)
             < lax.broadcasted_iota(jnp.int32, (LANES, LANES), 1))
    seg_start = _dot(jnp.broadcast_to(cnt_pad, (8, LANES)), jnp.where(below, 1.0, 0.0), HIGHEST)[0:1]
    pos = seg_start + rank
    slots = jnp.full_like(logits, -1.0)
    gates = jnp.zeros_like(logits)
    for k in range(TOP_K):
        s_k = jnp.sum(jnp.where(hots[k], pos, 0.0), axis=-1, keepdims=True)
        slots = jnp.where(lane_i == k, s_k, slots)
        gates = jnp.where(lane_i == k, exps[k] / denom, gates)
    slot_ref[...] = jnp.where(valid, slots, -1.0)
    gatek_ref[...] = gates
    cnt_ref[...] = cnt


def _post(y_all, o_gla, o_swa, x, gate, shift, scale, g2, wo, wr, br, per_token, tiles_per_seq, n_valid):
    n = x.shape[0]
    mod = _mod_spec(per_token, tiles_per_seq)
    row = lambda i: (i, 0)
    const = lambda i: (0, 0)
    wide = pl.BlockSpec((TOKEN_TILE, D_MODEL), row)
    narrow = pl.BlockSpec((TOKEN_TILE, LANES), row)
    return pl.pallas_call(
        functools.partial(_post_kernel, n_valid=n_valid),
        grid=(n // TOKEN_TILE,),
        in_specs=[pl.BlockSpec((TOKEN_TILE, D_MODEL), lambda i: (i, COL_AG // D_MODEL)),
                  pl.BlockSpec((TOKEN_TILE, D_MODEL), lambda i: (i, COL_AS // D_MODEL)),
                  wide, wide, wide, mod, mod, mod,
                  pl.BlockSpec((1, D_MODEL), const),
                  pl.BlockSpec((D_MODEL, D_MODEL), const),
                  pl.BlockSpec((2, D_MODEL, LANES), lambda i: (0, 0, 0)),
                  pl.BlockSpec((1, LANES), const)],
        out_specs=[wide, wide, narrow, narrow, pl.BlockSpec((None, 1, LANES), lambda i: (i, 0, 0))],
        out_shape=[jax.ShapeDtypeStruct((n, D_MODEL), F32),
                   jax.ShapeDtypeStruct((n, D_MODEL), BF16),
                   jax.ShapeDtypeStruct((n, LANES), F32),
                   jax.ShapeDtypeStruct((n, LANES), F32),
                   jax.ShapeDtypeStruct((n // TOKEN_TILE, 1, LANES), F32)],
        compiler_params=_params("parallel"),
        name="post",
    )(y_all, y_all, o_gla, o_swa, x, gate, shift, scale, g2, wo, wr, br)


def _slot_matrix(slot_cols, weights, chunk):
    tm = slot_cols[0].shape[0]
    j = lax.broadcasted_iota(jnp.int32, (tm, tm), 1) + chunk * tm
    out = jnp.zeros((tm, tm), F32)
    for s, w in zip(slot_cols, weights):
        out = jnp.where(s == j, w, out)
    return out.astype(BF16)


def _dispatch_kernel(nu_ref, dst_ref, ntail_ref, tail_ref, hp_ref, slp_ref, hs_ref, sls_ref, xg_ref,
                     sorted_ref, zero_ref, sem, tail_sem, *, prompt_tiles):
    t = pl.program_id(0)
    last = pl.num_programs(0) - 1
    buf = t % 2

    def sort_tile(h_ref, slot_ref):
        tm = h_ref.shape[0]
        slots = slot_ref[...].astype(jnp.int32)
        slot_cols = [slots[:, k:k + 1] for k in range(TOP_K)]
        h = h_ref[...]
        for c in range(LOCAL_ROWS // tm):
            onehot = _slot_matrix(slot_cols, [1.0] * TOP_K, c)
            upc = tm // ROW_UNIT
            sorted_ref[buf, c * upc:(c + 1) * upc] = _dot_tn(onehot, h).astype(BF16).reshape(
                upc, ROW_UNIT, D_MODEL)

    @pl.when(t < prompt_tiles)
    def _():
        sort_tile(hp_ref, slp_ref)

    @pl.when(t >= prompt_tiles)
    def _():
        sort_tile(hs_ref, sls_ref)

    def unit_copy(tile, b, i):
        return pltpu.make_async_copy(sorted_ref.at[b, i], xg_ref.at[dst_ref[tile, i]], sem.at[b])

    def start_all(tile, b):
        lax.fori_loop(0, nu_ref[tile], lambda i, c: (unit_copy(tile, b, i).start(), c)[1], 0)

    def wait_all(tile, b):
        n_units = nu_ref[tile]

        @pl.when(n_units > 0)
        def _():
            pltpu.make_async_copy(sorted_ref.at[b, pl.ds(0, n_units)],
                                  xg_ref.at[pl.ds(0, n_units)], sem.at[b]).wait()

    @pl.when(t > 0)
    def _():
        wait_all(t - 1, 1 - buf)

    start_all(t, buf)

    def tail_copy(i):
        return pltpu.make_async_copy(zero_ref, xg_ref.at[tail_ref[i]], tail_sem)

    @pl.when(t == last)
    def _():
        zero_ref[...] = jnp.zeros_like(zero_ref)
        n_tail = ntail_ref[0]
        lax.fori_loop(0, n_tail, lambda i, c: (tail_copy(i).start(), c)[1], 0)
        lax.fori_loop(0, n_tail, lambda i, c: (tail_copy(i).wait(), c)[1], 0)
        wait_all(t, buf)


def _dispatch(h2_p, slots_p, h2_s, slots_s, n_units, unit_dst, n_tail, tail_dst, rows_max):
    p_tiles = h2_p.shape[0] // TOKEN_TILE
    s_tiles = h2_s.shape[0] // TOKEN_TILE
    p_row = lambda t, *_: (jnp.minimum(t, p_tiles - 1), 0)
    s_row = lambda t, *_: (jnp.maximum(t - p_tiles, 0), 0)
    return pl.pallas_call(
        functools.partial(_dispatch_kernel, prompt_tiles=p_tiles),
        grid_spec=pltpu.PrefetchScalarGridSpec(
            num_scalar_prefetch=4,
            grid=(p_tiles + s_tiles,),
            in_specs=[pl.BlockSpec((TOKEN_TILE, D_MODEL), p_row),
                      pl.BlockSpec((TOKEN_TILE, LANES), p_row),
                      pl.BlockSpec((TOKEN_TILE, D_MODEL), s_row),
                      pl.BlockSpec((TOKEN_TILE, LANES), s_row)],
            out_specs=pl.BlockSpec(memory_space=pl.ANY),
            scratch_shapes=[pltpu.VMEM((2, LOCAL_ROWS // ROW_UNIT, ROW_UNIT, D_MODEL), BF16),
                            pltpu.VMEM((ROW_UNIT, D_MODEL), BF16),
                            pltpu.SemaphoreType.DMA((2,)), pltpu.SemaphoreType.DMA]),
        out_shape=jax.ShapeDtypeStruct((rows_max // ROW_UNIT, ROW_UNIT, D_MODEL), BF16),
        compiler_params=_params("arbitrary"),
        name="dispatch",
    )(n_units, unit_dst, n_tail, tail_dst, h2_p, slots_p, h2_s, slots_s).reshape(rows_max, D_MODEL)


def _expert_kernel(tg_ref, ge_ref, ng_ref, nused_ref, rows_ref, x_ref, bgu_ref, bd_ref, wgu_hbm, wd_hbm,
                   y_ref, wgu_f32, wd_f32, wgu_bf, wd_bf, sem):
    i = pl.program_id(0)

    def fetch(g, b):
        e = ge_ref[g]
        return (pltpu.make_async_copy(wgu_hbm.at[e], wgu_f32.at[b], sem.at[0, b]),
                pltpu.make_async_copy(wd_hbm.at[e], wd_f32.at[b], sem.at[1, b]))

    @pl.when(i == 0)
    def _():
        for cp in fetch(0, 0):
            cp.start()

    @pl.when(i < nused_ref[0])
    def _():
        g = tg_ref[i]
        b = g % 2

        @pl.when((i == 0) | (g != tg_ref[jnp.maximum(i - 1, 0)]))
        def _():
            @pl.when(g + 1 < ng_ref[0])
            def _():
                for cp in fetch(g + 1, 1 - b):
                    cp.start()

            for cp in fetch(g, b):
                cp.wait()
            wgu_bf[...] = wgu_f32[b].astype(BF16)
            wd_bf[...] = wd_f32[b].astype(BF16)

        for part in range(EXPERT_TILE // EXPERT_PART):
            @pl.when(rows_ref[i] > part * EXPERT_PART)
            def _():
                sl = slice(part * EXPERT_PART, (part + 1) * EXPERT_PART)
                gu = _dot(x_ref[sl, :], wgu_bf[...]) + bgu_ref[...]
                gate = jnp.minimum(gu[:, :D_FF], SWIGLU_LIMIT)
                up = jnp.clip(gu[:, D_FF:], -SWIGLU_LIMIT, SWIGLU_LIMIT)
                act = (up + 1.0) * gate * jax.nn.sigmoid(SWIGLU_ALPHA * gate)
                y_ref[sl, :] = (_dot(act.astype(BF16), wd_bf[...]) + bd_ref[...]).astype(y_ref.dtype)


def _experts(xg, tile_group, group_expert, n_groups, n_used, tile_rows, wgu, bgu, wd, bd):
    rows_max = xg.shape[0]
    used = lambda i, nu: jnp.maximum(jnp.minimum(i, nu[0] - 1), 0)
    row = lambda i, tg, ge, ng, nu, tr: (used(i, nu), 0)
    exp = lambda i, tg, ge, ng, nu, tr: (ge[tg[used(i, nu)]], 0, 0)
    return pl.pallas_call(
        _expert_kernel,
        grid_spec=pltpu.PrefetchScalarGridSpec(
            num_scalar_prefetch=5,
            grid=(rows_max // EXPERT_TILE,),
            in_specs=[pl.BlockSpec((EXPERT_TILE, D_MODEL), row),
                      pl.BlockSpec((None, 1, 2 * D_FF), exp),
                      pl.BlockSpec((None, 1, D_MODEL), exp),
                      pl.BlockSpec(memory_space=pl.ANY),
                      pl.BlockSpec(memory_space=pl.ANY)],
            out_specs=pl.BlockSpec((EXPERT_TILE, D_MODEL), row),
            scratch_shapes=[pltpu.VMEM((2, D_MODEL, 2 * D_FF), F32), pltpu.VMEM((2, D_FF, D_MODEL), F32),
                            pltpu.VMEM((D_MODEL, 2 * D_FF), BF16), pltpu.VMEM((D_FF, D_MODEL), BF16),
                            pltpu.SemaphoreType.DMA((2, 2))]),
        out_shape=jax.ShapeDtypeStruct((rows_max, D_MODEL), BF16),
        compiler_params=_params("arbitrary"),
        name="experts",
    )(tile_group, group_expert, n_groups, n_used, tile_rows, xg, bgu, bd, wgu, wd)


def _combine_kernel(nu_ref, src_ref, slot_ref, gatek_ref, x_ref, gmlp_ref, y_ref, o_ref, ys_ref, sem,
                    *, tile_offset):
    j = pl.program_id(0)
    t = j + tile_offset
    buf = j % 2
    tm = x_ref.shape[0]

    def unit_copy(tile, b, i):
        return pltpu.make_async_copy(y_ref.at[src_ref[tile, i]], ys_ref.at[b, i], sem.at[b])

    def fetch(tile, b):
        n_units = nu_ref[tile]
        lax.fori_loop(0, n_units, lambda i, c: (unit_copy(tile, b, i).start(), c)[1], 0)

        def zero_unit(i, c):
            ys_ref[b, i] = jnp.zeros((ROW_UNIT, D_MODEL), ys_ref.dtype)
            return c

        lax.fori_loop(n_units, LOCAL_ROWS // ROW_UNIT, zero_unit, 0)

    @pl.when(j == 0)
    def _():
        fetch(t, buf)

    @pl.when(j + 1 < pl.num_programs(0))
    def _():
        fetch(t + 1, 1 - buf)

    n_units = nu_ref[t]

    @pl.when(n_units > 0)
    def _():
        pltpu.make_async_copy(y_ref.at[pl.ds(0, n_units)], ys_ref.at[buf, pl.ds(0, n_units)],
                              sem.at[buf]).wait()

    slots = slot_ref[...].astype(jnp.int32)
    gates = gatek_ref[...]
    slot_cols = [slots[:, k:k + 1] for k in range(TOP_K)]
    gate_cols = [gates[:, k:k + 1] for k in range(TOP_K)]
    acc = jnp.zeros((tm, D_MODEL), F32)
    for c in range(LOCAL_ROWS // tm):
        upc = tm // ROW_UNIT
        rows_c = ys_ref[buf, c * upc:(c + 1) * upc].reshape(tm, D_MODEL)
        acc = acc + _dot(_slot_matrix(slot_cols, gate_cols, c), rows_c)
    o_ref[...] = x_ref[...] + _mod_rows(gmlp_ref, tm) * acc


def _combine(y, slots, gates, x1, gmlp, n_units, unit_src, tile_offset, per_token, tiles_per_seq):
    n = x1.shape[0]
    mod = _mod_spec(per_token, tiles_per_seq)
    wide = pl.BlockSpec((TOKEN_TILE, D_MODEL), lambda i, *_: (i, 0))
    narrow = pl.BlockSpec((TOKEN_TILE, LANES), lambda i, *_: (i, 0))
    return pl.pallas_call(
        functools.partial(_combine_kernel, tile_offset=tile_offset),
        grid_spec=pltpu.PrefetchScalarGridSpec(
            num_scalar_prefetch=2,
            grid=(n // TOKEN_TILE,),
            in_specs=[narrow, narrow, wide, mod, pl.BlockSpec(memory_space=pl.ANY)],
            out_specs=wide,
            scratch_shapes=[pltpu.VMEM((2, LOCAL_ROWS // ROW_UNIT, ROW_UNIT, D_MODEL), BF16),
                            pltpu.SemaphoreType.DMA((2,))]),
        out_shape=jax.ShapeDtypeStruct((n, D_MODEL), F32),
        compiler_params=_params("arbitrary"),
        name="combine",
    )(n_units, unit_src, slots, gates, x1, gmlp, y.reshape(-1, ROW_UNIT, D_MODEL))


def _route_tables(cnt, rows_max):
    units = (cnt + ROW_UNIT - 1) // ROW_UNIT
    group_units = jnp.sum(units, axis=0)
    upt = EXPERT_TILE // ROW_UNIT
    group_pad = (group_units + upt - 1) // upt * upt
    group_end = jnp.cumsum(group_pad)
    group_start = group_end - group_pad
    seg_start = group_start[None, :] + jnp.cumsum(units, axis=0) - units
    local_end = jnp.cumsum(units, axis=1)
    local_start = local_end - units
    n_units = local_end[:, -1].astype(jnp.int32)

    def pick(lo, hi, pos, value):
        return jnp.sum(jnp.where((pos >= lo) & (pos < hi), value, 0), axis=-1).astype(jnp.int32)

    i = jnp.arange(LOCAL_ROWS // ROW_UNIT, dtype=jnp.int32)[None, :, None]
    unit_hbm = pick(local_start[:, None, :], local_end[:, None, :], i,
                    seg_start[:, None, :] + i - local_start[:, None, :])

    n_tail_e = group_pad - group_units
    j = jnp.arange(N_EXPERTS * upt, dtype=jnp.int32)[:, None]
    tail_end = jnp.cumsum(n_tail_e)
    tail_start = tail_end - n_tail_e
    tail_hbm = pick(tail_start[None, :], tail_end[None, :], j,
                    (group_start + group_units)[None, :] + j - tail_start[None, :])
    n_tail = tail_end[-1:].astype(jnp.int32)

    r = jnp.arange(rows_max // EXPERT_TILE, dtype=jnp.int32)[:, None] * upt
    n_used = (group_end[-1:] // upt).astype(jnp.int32)
    nonempty = group_units > 0
    group_of_expert = jnp.cumsum(nonempty) - 1
    tile_rows = pick(group_start[None, :], group_end[None, :], r,
                     jnp.clip(((group_start + group_units)[None, :] - r) * ROW_UNIT, 0, EXPERT_TILE))
    tile_group = pick(group_start[None, :], group_end[None, :], r, group_of_expert[None, :])
    g = jnp.arange(N_EXPERTS, dtype=jnp.int32)
    group_expert = jnp.sum(jnp.where(nonempty[None, :] & (group_of_expert[None, :] == g[:, None]),
                                     g[None, :], 0), axis=-1).astype(jnp.int32)
    n_groups = jnp.sum(nonempty)[None].astype(jnp.int32)
    return (n_units, unit_hbm, n_tail, tail_hbm,
            tile_group, group_expert, n_groups, n_used, tile_rows)


def _rope_tables(pos):
    half = HEAD_DIM // 2
    lane = np.arange(LANES)
    inv = jnp.asarray(ROPE_THETA, F32) ** (-jnp.asarray(lane % half, F32) / half)
    sign = jnp.asarray(np.where(lane % HEAD_DIM < half, -1.0, 1.0), F32)
    ang = pos.astype(F32)[:, None] * inv[None, :]
    return jnp.cos(ang), jnp.sin(ang) * sign[None, :]


def kernel(x_prompt, x_sample, c_prompt, c_sample, state_gla, cache_swa_k, cache_swa_v, w_ada, b_ada,
           norm1_g, norm2_g, w_in, w_gk2, b_gk, gla_norm_g, q_norm_g, k_norm_g, attn_sinks, w_o,
           w_router, b_router, w_gate_up, b_gate_up, w_down, b_down):
    batch, seq, d = x_prompt.shape
    n_seq, n_real, _ = x_sample.shape
    depth = w_in.shape[0]
    assert depth == 1 and d == D_MODEL and n_real <= SAMPLE_ROWS
    assert seq % TOKEN_TILE == 0 and (n_seq * SAMPLE_ROWS) % TOKEN_TILE == 0 and n_seq % SAMPLE_SEQS == 0
    R = SAMPLE_ROWS
    kvw = SWA_KV_HEADS * HEAD_DIM
    tiles_per_seq = seq // TOKEN_TILE

    qg, kg, vg, rg, glr_w, qs, ks, vs, ag, as_ = jnp.split(
        w_in[0], [512, 1024, 2048, 3072, 3088, 4112, 4368, 4624, 5648], axis=1)
    w_main = jnp.concatenate([vg, rg, qs, ag, as_, qg, kg, ks, vs], axis=1).astype(BF16)
    w_glr = jnp.pad(glr_w, ((0, 0), (0, LANES - GATE_RANK))).astype(BF16)
    wgk = jnp.pad(w_gk2[0], ((0, LANES - GATE_RANK), (0, 0)))
    bgk = b_gk[0].reshape(1, -1)
    gnorm = gla_norm_g[0].reshape(1, -1)
    gq = jnp.tile(q_norm_g[0], LANES // HEAD_DIM).reshape(1, LANES)
    gk = jnp.tile(k_norm_g[0], LANES // HEAD_DIM).reshape(1, LANES)
    seg = jnp.asarray(np.kron(np.eye(LANES // HEAD_DIM), np.ones((HEAD_DIM, HEAD_DIM))), BF16)
    sinks = attn_sinks[0]
    wo = w_o[0].astype(BF16)
    wr = jnp.pad(w_router[0], ((0, 0), (0, LANES - N_EXPERTS)))
    wr_hi = wr.astype(BF16)
    wr = jnp.stack([wr_hi, (wr - wr_hi.astype(F32)).astype(BF16)])
    br = jnp.pad(b_router[0], (0, LANES - N_EXPERTS), constant_values=-1e30).reshape(1, LANES)
    bgu = b_gate_up[0].reshape(N_EXPERTS, 1, 2 * D_FF)
    bd = b_down[0].reshape(N_EXPERTS, 1, D_MODEL)
    g1 = norm1_g[0].reshape(1, -1)
    g2 = norm2_g[0].reshape(1, -1)

    n_c = batch + n_seq
    c_all = jnp.pad(jnp.concatenate([c_prompt, c_sample], axis=0), ((0, -n_c % 8), (0, 0)))
    m_all = _ada(c_all, w_ada[0], b_ada[0])
    mp = [m_all[:batch, i * d:(i + 1) * d].reshape(batch, 1, d) for i in range(6)]
    ms = [m_all[batch:n_c, i * d:(i + 1) * d] for i in range(6)]

    xp = x_prompt.reshape(batch * seq, d)
    xs = jnp.pad(x_sample, ((0, 0), (0, R - n_real), (0, 0))).reshape(n_seq * R, d)
    cos_p, sin_p = _rope_tables(jnp.arange(seq))
    cos_s, sin_s = _rope_tables(PAST_LEN + jnp.tile(jnp.arange(R), SAMPLE_SEQS))
    sink_rows = jnp.repeat(sinks, R).reshape(SWA_HEADS * R, 1)

    yp, glr_p = _inproj(xp, mp[0], mp[1], g1, w_main, w_glr, BF16, False, tiles_per_seq)
    ys, glr_s = _inproj(xs, ms[0], ms[1], g1, w_main, w_glr, F32, True, 1)
    og_p, st_p = _gla_prompt(yp, glr_p, wgk, bgk, gnorm, batch, seq)
    og_s, st_s = _gla_sample(ys, glr_s, wgk, bgk, gnorm, state_gla[0], n_seq, n_real)
    os_p, kc_p, vc_p = _swa_prompt(yp, sinks, cos_p, sin_p, gq, gk, seg, batch, seq)
    os_s, kc_s, vc_s = _swa_sample(ys, sink_rows, cache_swa_k[0].reshape(n_seq, WINDOW, kvw),
                                   cache_swa_v[0].reshape(n_seq, WINDOW, kvw),
                                   cos_s, sin_s, gq, gk, seg, n_seq, n_real)
    x1_p, h2_p, sl_p, gt_p, cnt_p = _post(yp, og_p, os_p, xp, mp[2], mp[3], mp[4], g2, wo, wr, br,
                                          False, tiles_per_seq, R)
    x1_s, h2_s, sl_s, gt_s, cnt_s = _post(ys, og_s, os_s, xs, ms[2], ms[3], ms[4], g2, wo, wr, br,
                                          True, 1, n_real)

    cnt = jnp.concatenate([cnt_p, cnt_s], axis=0)[:, 0, :N_EXPERTS].astype(jnp.int32)
    n_tiles = cnt.shape[0]
    rows_bound = (TOP_K * (batch * seq + n_seq * n_real) + n_tiles * N_EXPERTS * (ROW_UNIT - 1)
                  + N_EXPERTS * (EXPERT_TILE - 1))
    rows_max = -(-rows_bound // EXPERT_TILE) * EXPERT_TILE
    (n_units, unit_hbm, n_tail, tail_hbm,
     tile_group, group_expert, n_groups, n_used, tile_rows) = _route_tables(cnt, rows_max)
    xg = _dispatch(h2_p, sl_p, h2_s, sl_s, n_units, unit_hbm, n_tail, tail_hbm, rows_max)
    yg = _experts(xg, tile_group, group_expert, n_groups, n_used, tile_rows,
                  w_gate_up[0], bgu, w_down[0], bd)
    p_tiles = batch * seq // TOKEN_TILE
    out_p = _combine(yg, sl_p, gt_p, x1_p, mp[5], n_units, unit_hbm, 0, False, tiles_per_seq)
    out_s = _combine(yg, sl_s, gt_s, x1_s, ms[5], n_units, unit_hbm, p_tiles, True, 1)

    cache_shape = (WINDOW, SWA_KV_HEADS, HEAD_DIM)
    return (out_p.reshape(batch, seq, d),
            out_s.reshape(n_seq, R, d)[:, :n_real],
            st_p[None],
            kc_p.reshape(1, batch, *cache_shape),
            vc_p.reshape(1, batch, *cache_shape),
            st_s[None],
            kc_s.reshape(1, n_seq, *cache_shape),
            vc_s.reshape(1, n_seq, *cache_shape))
```

```python
import functools

import numpy as np
import jax
import jax.numpy as jnp
from jax import lax
from jax.experimental import pallas as pl
from jax.experimental.pallas import tpu as pltpu

F32 = jnp.float32
BF16 = jnp.bfloat16
HIGHEST = lax.Precision.HIGHEST

D_MODEL = 1024
PAST_LEN = 16384
GLA_HEADS = 4
GLA_DK = 128
GLA_DV = 256
GATE_RANK = 16
GATE_TAU = 16.0
SWA_HEADS = 16
SWA_KV_HEADS = 4
HEAD_DIM = 64
SWA_GROUP = SWA_HEADS // SWA_KV_HEADS
WINDOW = 128
ROPE_THETA = 10000.0
N_EXPERTS = 32
TOP_K = 4
D_FF = 1024
SWIGLU_ALPHA = 1.702
SWIGLU_LIMIT = 7.0
NORM_EPS = 1e-6

LANES = 128
SUBLANES = 8
SAMPLE_ROWS = 8
SAMPLE_SEQS = 16
TOKEN_TILE = 512
SWA_STEP_BLOCKS = 2
GLA_BLOCK = 256
VMEM_LIMIT = 56 * 1024 * 1024
ROW_UNIT = 16
EXPERT_TILE = 1024
EXPERT_PART = 256
LOCAL_ROWS = -(-(TOP_K * TOKEN_TILE + N_EXPERTS * (ROW_UNIT - 1)) // TOKEN_TILE) * TOKEN_TILE

COL_VG, COL_RG, COL_QS, COL_AG, COL_AS = 0, 1024, 2048, 3072, 4096
COL_QG, COL_KG, COL_KS, COL_VS = 5120, 5632, 6144, 6400
D_MAIN = 6656
PROJ_CHUNK = 512


def _dot(a, b, precision=None):
    return jnp.dot(a, b, preferred_element_type=F32, precision=precision)


def _dot_nt(a, b, precision=None):
    return lax.dot_general(a, b, (((1,), (1,)), ((), ())), preferred_element_type=F32, precision=precision)


def _dot_tn(a, b, precision=None):
    return lax.dot_general(a, b, (((0,), (0,)), ((), ())), preferred_element_type=F32, precision=precision)


def _dot_sum(sel, x):
    hi = x.astype(BF16)
    r1 = x - hi.astype(F32)
    mid = r1.astype(BF16)
    lo = (r1 - mid.astype(F32)).astype(BF16)
    return _dot(sel, hi) + _dot(sel, mid) + _dot(sel, lo)


def _params(*sem):
    return pltpu.CompilerParams(dimension_semantics=sem, vmem_limit_bytes=VMEM_LIMIT)


def _rms(x, g):
    return x * lax.rsqrt(jnp.mean(x * x, axis=-1, keepdims=True) + NORM_EPS) * g


def _log_sigmoid(x):
    return jnp.minimum(x, 0.0) - jnp.log(1.0 + jnp.exp(-jnp.abs(x)))


def _ada_kernel(c_ref, w_ref, b_ref, o_ref):
    c = c_ref[...]
    s = c * jax.nn.sigmoid(c)
    o_ref[...] = _dot(s, w_ref[...], HIGHEST) + b_ref[...]


def _ada(c_all, w_ada, b_ada):
    rows = c_all.shape[0]
    tn = 768
    return pl.pallas_call(
        _ada_kernel,
        grid=(6 * D_MODEL // tn,),
        in_specs=[pl.BlockSpec((rows, D_MODEL), lambda j: (0, 0)),
                  pl.BlockSpec((D_MODEL, tn), lambda j: (0, j)),
                  pl.BlockSpec((1, tn), lambda j: (0, j))],
        out_specs=pl.BlockSpec((rows, tn), lambda j: (0, j)),
        out_shape=jax.ShapeDtypeStruct((rows, 6 * D_MODEL), F32),
        compiler_params=_params("parallel"),
        name="ada",
    )(c_all, w_ada, b_ada.reshape(1, -1))


def _inproj_kernel(x_ref, shift_ref, scale_ref, g_ref, w_ref, wg_ref, cos_ref, sin_ref, gq_ref, gk_ref,
                   seg_ref, o_ref, og_ref):
    rows = x_ref.shape[0]
    h = _rms(x_ref[...], g_ref[...]) * (1.0 + _mod_rows(scale_ref, rows)) + _mod_rows(shift_ref, rows)
    hb = h.astype(BF16)
    cos, sin, seg = cos_ref[...], sin_ref[...], seg_ref[...]
    def has_followup(j):
        c0, c1 = j * PROJ_CHUNK, (j + 1) * PROJ_CHUNK
        return c0 < COL_AS + D_MODEL and c1 > COL_QS or c0 < COL_KS + SWA_KV_HEADS * HEAD_DIM and c1 > COL_KS

    chunks = range(D_MAIN // PROJ_CHUNK)
    for j in sorted(chunks, key=lambda j: not has_followup(j)):
        sl = slice(j * PROJ_CHUNK, (j + 1) * PROJ_CHUNK)
        r = _dot(hb, w_ref[:, sl])
        blocks = []
        for m in range(PROJ_CHUNK // LANES):
            c0 = j * PROJ_CHUNK + m * LANES
            blk = r[:, m * LANES:(m + 1) * LANES]
            if COL_QS <= c0 < COL_QS + SWA_HEADS * HEAD_DIM:
                blk = _head_norm_rope(blk, gq_ref[...], cos, sin, seg) * (HEAD_DIM ** -0.5)
            elif COL_KS <= c0 < COL_KS + SWA_KV_HEADS * HEAD_DIM:
                blk = _head_norm_rope(blk, gk_ref[...], cos, sin, seg)
            elif COL_AG <= c0 < COL_AS + D_MODEL:
                blk = jax.nn.sigmoid(blk)
            blocks.append(blk)
        o_ref[:, sl] = jnp.concatenate(blocks, axis=1).astype(o_ref.dtype)
    og_ref[...] = _dot(hb, wg_ref[...])


def _mod_spec(per_token, tiles_per_seq):
    if per_token:
        return pl.BlockSpec((TOKEN_TILE // SAMPLE_ROWS, D_MODEL), lambda i, *_: (i, 0))
    return pl.BlockSpec((None, 1, D_MODEL), lambda i, *_: (i // tiles_per_seq, 0, 0))


def _mod_rows(ref, rows):
    m = ref[...]
    if m.shape[0] == 1:
        return m
    return jnp.broadcast_to(m[:, None, :], (m.shape[0], rows // m.shape[0], m.shape[1])).reshape(
        rows, m.shape[1])


def _inproj(x, shift, scale, g, w_main, w_glr, cos, sin, gq, gk, seg, out_dtype, per_token, tiles_per_seq):
    n = x.shape[0]
    mod = _mod_spec(per_token, tiles_per_seq)
    const = lambda i: (0, 0)
    rope = pl.BlockSpec((TOKEN_TILE, LANES), (lambda i: (i, 0)) if per_token else
                        (lambda i: (i % tiles_per_seq, 0)))
    return pl.pallas_call(
        _inproj_kernel,
        grid=(n // TOKEN_TILE,),
        in_specs=[pl.BlockSpec((TOKEN_TILE, D_MODEL), lambda i: (i, 0)), mod, mod,
                  pl.BlockSpec((1, D_MODEL), const),
                  pl.BlockSpec((D_MODEL, D_MAIN), const, pipeline_mode=pl.Buffered(1)),
                  pl.BlockSpec((D_MODEL, LANES), const, pipeline_mode=pl.Buffered(1)),
                  rope, rope,
                  pl.BlockSpec((1, LANES), const), pl.BlockSpec((1, LANES), const),
                  pl.BlockSpec((LANES, LANES), const)],
        out_specs=[pl.BlockSpec((TOKEN_TILE, D_MAIN), lambda i: (i, 0)),
                   pl.BlockSpec((TOKEN_TILE, LANES), lambda i: (i, 0))],
        out_shape=[jax.ShapeDtypeStruct((n, D_MAIN), out_dtype),
                   jax.ShapeDtypeStruct((n, LANES), F32)],
        compiler_params=_params("parallel"),
        name="inproj",
    )(x, shift, scale, g, w_main, w_glr, cos, sin, gq, gk, seg)


def _gla_log_gate(glr, wgk, bgk):
    return _log_sigmoid(_dot(glr, wgk, HIGHEST) + bgk) * (1.0 / GATE_TAU)


def _gla_out(o, r, g):
    r = r.astype(F32)
    return _rms(o, g) * (r * jax.nn.sigmoid(r))


def _gla_pair_levels(n):
    t = np.arange(n)[:, None]
    s = np.arange(n)[None, :]
    x = t ^ s
    top = np.where(x > 0, 1 << np.floor(np.log2(np.maximum(x, 1))).astype(np.int64), 0)
    return np.where(s > t, -1, top).astype(np.int32)


def _gla_block_ref(b, h):
    n, w = b.shape
    if 2 * h == n:
        return jnp.broadcast_to(b[h - 1:h, :], (n, w))
    if h >= SUBLANES // 2:
        picked = b.reshape(n // (2 * h), 2 * h, w)[:, h - 1:h, :]
        return jnp.broadcast_to(picked, (n // (2 * h), 2 * h, w)).reshape(n, w)
    r = lax.broadcasted_iota(jnp.int32, (n, 1), 0) % (2 * h)
    out = b
    for d in range(1, h + 1):
        out = jnp.where(r == h - 1 + d, pltpu.roll(b, d, 0), out)
    for d in range(1, h):
        out = jnp.where(r == h - 1 - d, pltpu.roll(b, n - d, 0), out)
    return out


def _gla_prompt_kernel(q_ref, k_ref, v_ref, r_ref, glr_ref, wgk_ref, bgk_ref, g_ref, lev_ref,
                       o_ref, s_ref, st_ref):
    c = pl.program_id(1)

    @pl.when(c == 0)
    def _():
        st_ref[...] = jnp.zeros_like(st_ref)

    n = GLA_BLOCK
    lev = lev_ref[...]
    causal = lax.broadcasted_iota(jnp.int32, (n, n), 1) <= lax.broadcasted_iota(jnp.int32, (n, n), 0)
    lg = _gla_log_gate(glr_ref[...], wgk_ref[...], bgk_ref[...])
    b = _dot_sum(jnp.where(causal, 1.0, 0.0).astype(BF16), lg)
    b_last = b[n - 1:n, :]
    q = q_ref[...].astype(F32) * (GLA_DK ** -0.5)
    k = k_ref[...].astype(F32)
    qe = (q * jnp.exp(b)).astype(BF16)
    kd = (k * jnp.exp(b_last - b)).astype(BF16)
    decay = jnp.exp(b_last)
    levels = [0] + [1 << p for p in range(n.bit_length() - 1)]
    q_lv, k_lv = [q.astype(BF16)], [k.astype(BF16)]
    for h in levels[1:]:
        e = jnp.exp(-jnp.abs(b - _gla_block_ref(b, h)))
        q_lv.append((q * e).astype(BF16))
        k_lv.append((k * e).astype(BF16))
    for h in range(GLA_HEADS):
        dk = slice(h * GLA_DK, (h + 1) * GLA_DK)
        dv = slice(h * GLA_DV, (h + 1) * GLA_DV)
        attn = jnp.zeros((n, n), F32)
        for level, ql, kl in zip(levels, q_lv, k_lv):
            attn = jnp.where(lev == level, _dot_nt(ql[:, dk], kl[:, dk]), attn)
        v = v_ref[:, dv]
        st = st_ref[h]
        o = _dot_nt(qe[:, dk], st.astype(BF16)) + _dot(attn.astype(BF16), v)
        st_ref[h] = st * decay[:, dk] + _dot_tn(v, kd[:, dk])
        o_ref[:, dv] = _gla_out(o, r_ref[:, dv], g_ref[...]).astype(o_ref.dtype)

    @pl.when(c == pl.num_programs(1) - 1)
    def _():
        for h in range(GLA_HEADS):
            s_ref[h] = st_ref[h].T


def _gla_prompt(yp, glr, wgk, bgk, gnorm, batch, seq):
    nb = seq // GLA_BLOCK
    hk, hv = GLA_HEADS * GLA_DK, GLA_HEADS * GLA_DV
    tok = lambda b, c: b * nb + c
    const = lambda b, c: (0, 0)
    return pl.pallas_call(
        _gla_prompt_kernel,
        grid=(batch, nb),
        in_specs=[pl.BlockSpec((GLA_BLOCK, hk), lambda b, c: (tok(b, c), COL_QG // hk)),
                  pl.BlockSpec((GLA_BLOCK, hk), lambda b, c: (tok(b, c), COL_KG // hk)),
                  pl.BlockSpec((GLA_BLOCK, hv), lambda b, c: (tok(b, c), COL_VG // hv)),
                  pl.BlockSpec((GLA_BLOCK, hv), lambda b, c: (tok(b, c), COL_RG // hv)),
                  pl.BlockSpec((GLA_BLOCK, LANES), lambda b, c: (tok(b, c), 0)),
                  pl.BlockSpec((LANES, hk), const),
                  pl.BlockSpec((1, hk), const),
                  pl.BlockSpec((1, GLA_DV), const),
                  pl.BlockSpec((GLA_BLOCK, GLA_BLOCK), const)],
        out_specs=[pl.BlockSpec((GLA_BLOCK, hv), lambda b, c: (tok(b, c), 0)),
                   pl.BlockSpec((None, GLA_HEADS, GLA_DK, GLA_DV), lambda b, c: (b, 0, 0, 0))],
        out_shape=[jax.ShapeDtypeStruct((batch * seq, hv), BF16),
                   jax.ShapeDtypeStruct((batch, GLA_HEADS, GLA_DK, GLA_DV), F32)],
        scratch_shapes=[pltpu.VMEM((GLA_HEADS, GLA_DV, GLA_DK), F32)],
        compiler_params=_params("parallel", "arbitrary"),
        name="gla_prompt",
    )(yp, yp, yp, yp, glr, wgk, bgk, gnorm, jnp.asarray(_gla_pair_levels(GLA_BLOCK)))


def _gla_sample_kernel(q_ref, k_ref, v_ref, r_ref, glr_ref, wgk_ref, bgk_ref, g_ref, s0_ref, lev_ref,
                       o_ref, s_ref, *, n_real):
    R = SAMPLE_ROWS
    rows = q_ref.shape[0]
    row = lax.broadcasted_iota(jnp.int32, (rows, rows), 0)
    col = lax.broadcasted_iota(jnp.int32, (rows, rows), 1)
    same = (row // R) == (col // R)
    causal = same & (col <= row)
    real = lax.broadcasted_iota(jnp.int32, (rows, 1), 0) % R < n_real
    lg = jnp.where(real, _gla_log_gate(glr_ref[...], wgk_ref[...], bgk_ref[...]), 0.0)
    b = _dot_sum(jnp.where(causal, 1.0, 0.0).astype(BF16), lg)
    b_last = _dot_sum(jnp.where(same, 1.0, 0.0).astype(BF16), lg)
    q = q_ref[...] * (GLA_DK ** -0.5)
    k = jnp.where(real, k_ref[...], 0.0)
    v = v_ref[...]
    qe = q * jnp.exp(b)
    kd = k * jnp.exp(b_last - b)
    lev = lev_ref[...]
    levels = [0] + [1 << p for p in range(R.bit_length() - 1)]
    q_lv, k_lv = [q.astype(BF16)], [k.astype(BF16)]
    for h in levels[1:]:
        e = jnp.exp(-jnp.abs(b - _gla_block_ref(b, h)))
        q_lv.append((q * e).astype(BF16))
        k_lv.append((k * e).astype(BF16))
    v_bf = v.astype(BF16)
    for h in range(GLA_HEADS):
        dk = slice(h * GLA_DK, (h + 1) * GLA_DK)
        dv = slice(h * GLA_DV, (h + 1) * GLA_DV)
        attn = jnp.zeros((rows, rows), F32)
        for level, ql, kl in zip(levels, q_lv, k_lv):
            attn = jnp.where(lev == level, _dot_nt(ql[:, dk], kl[:, dk]), attn)
        o_intra = _dot(attn.astype(BF16), v_bf[:, dv])
        decay_t = jnp.exp(b_last[:, dk]).T
        outs = []
        for s in range(rows // R):
            sl = slice(s * R, (s + 1) * R)
            s0 = s0_ref[s, h]
            outs.append(_dot(qe[sl, dk], s0) + o_intra[sl, :])
            s_ref[s, h] = s0 * decay_t[:, s * R:s * R + 1] + _dot_tn(kd[sl, dk], v[sl, dv])
        o_ref[:, dv] = _gla_out(jnp.concatenate(outs, axis=0), r_ref[:, dv], g_ref[...])


def _gla_sample(ys, glr, wgk, bgk, gnorm, state, n_seq, n_real):
    R = SAMPLE_SEQS * SAMPLE_ROWS
    hk, hv = GLA_HEADS * GLA_DK, GLA_HEADS * GLA_DV
    st_spec = pl.BlockSpec((SAMPLE_SEQS, GLA_HEADS, GLA_DK, GLA_DV), lambda b: (b, 0, 0, 0))
    lev = _gla_pair_levels(R)
    lev = np.where(lev >= SAMPLE_ROWS, -1, lev)
    return pl.pallas_call(
        functools.partial(_gla_sample_kernel, n_real=n_real),
        grid=(n_seq // SAMPLE_SEQS,),
        in_specs=[pl.BlockSpec((R, hk), lambda b: (b, COL_QG // hk)),
                  pl.BlockSpec((R, hk), lambda b: (b, COL_KG // hk)),
                  pl.BlockSpec((R, hv), lambda b: (b, COL_VG // hv)),
                  pl.BlockSpec((R, hv), lambda b: (b, COL_RG // hv)),
                  pl.BlockSpec((R, LANES), lambda b: (b, 0)),
                  pl.BlockSpec((LANES, hk), lambda b: (0, 0)),
                  pl.BlockSpec((1, hk), lambda b: (0, 0)),
                  pl.BlockSpec((1, GLA_DV), lambda b: (0, 0)),
                  st_spec,
                  pl.BlockSpec((R, R), lambda b: (0, 0))],
        out_specs=[pl.BlockSpec((R, hv), lambda b: (b, 0)), st_spec],
        out_shape=[jax.ShapeDtypeStruct((n_seq * SAMPLE_ROWS, hv), F32),
                   jax.ShapeDtypeStruct((n_seq, GLA_HEADS, GLA_DK, GLA_DV), F32)],
        compiler_params=_params("parallel"),
        name="gla_sample",
    )(ys, ys, ys, ys, glr, wgk, bgk, gnorm, state, jnp.asarray(lev))


def _lane_lower(shape):
    return lax.broadcasted_iota(jnp.int32, shape, len(shape) - 1) % LANES < HEAD_DIM


def _head_norm_rope(x, g, cos, sin, seg):
    ss = _dot((x * x).astype(BF16), seg)
    y = x * lax.rsqrt(ss * (1.0 / HEAD_DIM) + NORM_EPS) * g
    half = HEAD_DIM // 2
    lane = lax.broadcasted_iota(jnp.int32, y.shape, 1)
    rot = jnp.where(lane % HEAD_DIM < half, pltpu.roll(y, LANES - half, 1), pltpu.roll(y, half, 1))
    return y * cos + rot * sin


def _both_halves(blk, half):
    sw = pltpu.roll(blk, HEAD_DIM, 1)
    lower = _lane_lower(blk.shape)
    return jnp.where(lower, blk, sw) if half == 0 else jnp.where(lower, sw, blk)


def _stack_heads(q_blocks):
    parts = []
    for qb in q_blocks:
        lower = _lane_lower(qb.shape)
        zero = jnp.zeros_like(qb)
        parts += [jnp.where(lower, qb, zero), jnp.where(lower, zero, qb)]
    return jnp.concatenate(parts, axis=0)


def _swa_prompt_kernel(sink_ref, q_ref, k_ref, v_ref, o_ref, ko_ref, vo_ref, kprev_ref, vprev_ref):
    n = pl.program_id(1)
    W = WINDOW

    @pl.when(n == 0)
    def _():
        kprev_ref[...] = jnp.zeros_like(kprev_ref)
        vprev_ref[...] = jnp.zeros_like(vprev_ref)

    k_prev, v_prev = kprev_ref[...], vprev_ref[...]
    for sub in range(SWA_STEP_BLOCKS):
        tok = slice(sub * W, (sub + 1) * W)
        k_cur = k_ref[tok, :].astype(F32)
        v_cur = v_ref[tok, :].astype(F32)
        prev_fill = jnp.where(n > 0, 0.0, -jnp.inf) if sub == 0 else 0.0
        _swa_block(sink_ref, q_ref, o_ref, tok, k_prev, v_prev, k_cur, v_cur, prev_fill)
        k_prev, v_prev = k_cur, v_cur
    ko_ref[...] = k_prev
    vo_ref[...] = v_prev
    kprev_ref[...] = k_prev
    vprev_ref[...] = v_prev


def _swa_block(sink_ref, q_ref, o_ref, tok, k_prev, v_prev, k_cur, v_cur, prev_fill):
    W = WINDOW
    qi = lax.broadcasted_iota(jnp.int32, (W, W), 0)
    ki = lax.broadcasted_iota(jnp.int32, (W, W), 1)
    from_cur = ki <= qi
    for kh in range(SWA_KV_HEADS):
        blk = slice((kh // 2) * LANES, (kh // 2 + 1) * LANES)
        kb_prev = _both_halves(k_prev[:, blk], kh % 2).astype(BF16)
        kb_cur = _both_halves(k_cur[:, blk], kh % 2).astype(BF16)
        vb_prev = _both_halves(v_prev[:, blk], kh % 2).astype(BF16)
        vb_cur = _both_halves(v_cur[:, blk], kh % 2).astype(BF16)
        qblocks = [q_ref[tok, (2 * kh + j) * LANES:(2 * kh + j + 1) * LANES] for j in range(2)]
        qs = _stack_heads(qblocks)
        s_prev = _dot_nt(qs, kb_prev)
        s_cur = _dot_nt(qs, kb_cur)
        outs = []
        for g in range(SWA_GROUP):
            rows = slice(g * W, (g + 1) * W)
            sg = jnp.where(from_cur, s_cur[rows, :], s_prev[rows, :] + prev_fill)
            sink = sink_ref[kh * SWA_GROUP + g]
            m = jnp.maximum(jnp.max(sg, axis=-1, keepdims=True), sink)
            p = jnp.exp(sg - m)
            denom = jnp.sum(p, axis=-1, keepdims=True) + jnp.exp(sink - m)
            p_cur = jnp.where(from_cur, p, 0.0).astype(BF16)
            p_prev = jnp.where(from_cur, 0.0, p).astype(BF16)
            outs.append((_dot(p_prev, vb_prev) + _dot(p_cur, vb_cur)) / denom)
        lower = _lane_lower((W, LANES))
        for j in range(2):
            c0 = (2 * kh + j) * LANES
            o_ref[tok, c0:c0 + LANES] = jnp.where(lower, outs[2 * j], outs[2 * j + 1]).astype(o_ref.dtype)


def _swa_prompt(yp, sinks, batch, seq):
    rows = SWA_STEP_BLOCKS * WINDOW
    nb = seq // rows
    kvw = SWA_KV_HEADS * HEAD_DIM
    tok = lambda b, n: b * nb + n
    cache_spec = pl.BlockSpec((None, WINDOW, kvw), lambda b, n: (b, 0, 0))
    return pl.pallas_call(
        _swa_prompt_kernel,
        grid=(batch, nb),
        in_specs=[pl.BlockSpec(memory_space=pltpu.SMEM),
                  pl.BlockSpec((rows, D_MODEL), lambda b, n: (tok(b, n), COL_QS // D_MODEL)),
                  pl.BlockSpec((rows, kvw), lambda b, n: (tok(b, n), COL_KS // kvw)),
                  pl.BlockSpec((rows, kvw), lambda b, n: (tok(b, n), COL_VS // kvw))],
        out_specs=[pl.BlockSpec((rows, D_MODEL), lambda b, n: (tok(b, n), 0)), cache_spec, cache_spec],
        out_shape=[jax.ShapeDtypeStruct((batch * seq, D_MODEL), BF16),
                   jax.ShapeDtypeStruct((batch, WINDOW, kvw), F32),
                   jax.ShapeDtypeStruct((batch, WINDOW, kvw), F32)],
        scratch_shapes=[pltpu.VMEM((WINDOW, kvw), F32), pltpu.VMEM((WINDOW, kvw), F32)],
        compiler_params=_params("parallel", "arbitrary"),
        name="swa_prompt",
    )(sinks, yp, yp, yp)


def _shift_cache(cache, new, n_real):
    R = SAMPLE_ROWS
    rolled = pltpu.roll(cache, WINDOW - n_real, 0)
    tail_new = pltpu.roll(new, R - n_real, 0)
    row = lax.broadcasted_iota(jnp.int32, (R, cache.shape[1]), 0)
    tail = jnp.where(row < R - n_real, rolled[WINDOW - R:, :], tail_new)
    return jnp.concatenate([rolled[:WINDOW - R, :], tail], axis=0)


def _swa_sample_kernel(q_ref, k_ref, v_ref, kc_ref, vc_ref, sink_ref, o_ref, ko_ref, vo_ref, *, n_real):
    R, W = SAMPLE_ROWS, WINDOW
    k_new = k_ref[...]
    v_new = v_ref[...]
    q_pairs = [q_ref[:, j * LANES:(j + 1) * LANES] for j in range(SWA_HEADS // 2)]

    hr = SWA_HEADS * R
    t_c = lax.broadcasted_iota(jnp.int32, (hr, W), 0) % R
    mask_c = lax.broadcasted_iota(jnp.int32, (hr, W), 1) > t_c
    t_n = lax.broadcasted_iota(jnp.int32, (hr, R), 0) % R
    mask_n = lax.broadcasted_iota(jnp.int32, (hr, R), 1) <= t_n
    sink = sink_ref[...]
    lower = _lane_lower((R, LANES))
    zeros = jnp.zeros((R, LANES), F32)
    for s in range(q_ref.shape[0] // R):
        sl = slice(s * R, (s + 1) * R)
        kc, vc = kc_ref[s], vc_ref[s]
        kn, vn = k_new[sl, :], v_new[sl, :]
        ko_ref[s] = _shift_cache(kc, kn, n_real)
        vo_ref[s] = _shift_cache(vc, vn, n_real)
        q_rows = []
        for h in range(SWA_HEADS):
            kh = h // SWA_GROUP
            x = q_pairs[h // 2][sl, :]
            if h % 2 != kh % 2:
                x = pltpu.roll(x, HEAD_DIM, 1)
            x = jnp.where(lower, x, zeros) if kh % 2 == 0 else jnp.where(lower, zeros, x)
            q_rows.append(jnp.concatenate([x, zeros] if kh // 2 == 0 else [zeros, x], axis=1))
        qbd = jnp.concatenate(q_rows, axis=0)
        sc = jnp.where(mask_c, _dot_nt(qbd.astype(BF16), kc.astype(BF16)), -jnp.inf)
        sn = jnp.where(mask_n, _dot_nt(qbd, kn), -jnp.inf)
        m = jnp.maximum(jnp.maximum(jnp.max(sc, axis=-1, keepdims=True),
                                    jnp.max(sn, axis=-1, keepdims=True)), sink)
        pc, pn = jnp.exp(sc - m), jnp.exp(sn - m)
        denom = (jnp.sum(pc, axis=-1, keepdims=True) + jnp.sum(pn, axis=-1, keepdims=True)
                 + jnp.exp(sink - m))
        o = (_dot(pc.astype(BF16), vc.astype(BF16)) + _dot(pn, vn)) / denom
        for j in range(SWA_HEADS // 2):
            halves = []
            for h in (2 * j, 2 * j + 1):
                kh = h // SWA_GROUP
                y = o[h * R:(h + 1) * R, (kh // 2) * LANES:(kh // 2 + 1) * LANES]
                halves.append(pltpu.roll(y, HEAD_DIM, 1) if h % 2 != kh % 2 else y)
            o_ref[sl, j * LANES:(j + 1) * LANES] = jnp.where(lower, halves[0], halves[1])


def _swa_sample(ys, sink_rows, kcache, vcache, n_seq, n_real):
    R = SAMPLE_SEQS * SAMPLE_ROWS
    kvw = SWA_KV_HEADS * HEAD_DIM
    const = lambda b: (0, 0)
    cache_spec = pl.BlockSpec((SAMPLE_SEQS, WINDOW, kvw), lambda b: (b, 0, 0))
    return pl.pallas_call(
        functools.partial(_swa_sample_kernel, n_real=n_real),
        grid=(n_seq // SAMPLE_SEQS,),
        in_specs=[pl.BlockSpec((R, D_MODEL), lambda b: (b, COL_QS // D_MODEL)),
                  pl.BlockSpec((R, kvw), lambda b: (b, COL_KS // kvw)),
                  pl.BlockSpec((R, kvw), lambda b: (b, COL_VS // kvw)),
                  cache_spec, cache_spec,
                  pl.BlockSpec((SWA_HEADS * SAMPLE_ROWS, 1), const)],
        out_specs=[pl.BlockSpec((R, D_MODEL), lambda b: (b, 0)), cache_spec, cache_spec],
        out_shape=[jax.ShapeDtypeStruct((n_seq * SAMPLE_ROWS, D_MODEL), F32),
                   jax.ShapeDtypeStruct((n_seq, WINDOW, kvw), F32),
                   jax.ShapeDtypeStruct((n_seq, WINDOW, kvw), F32)],
        compiler_params=_params("parallel"),
        name="swa_sample",
    )(ys, ys, ys, kcache, vcache, sink_rows)


def _post_kernel(ag_ref, as_ref, og_ref, os_ref, x_ref, gate_ref, shift_ref, scale_ref, g2_ref,
                 wo_ref, wr_ref, br_ref, x1_ref, h2_ref, slot_ref, gatek_ref, cnt_ref, *, n_valid):
    merged = (ag_ref[...].astype(F32) * og_ref[...].astype(F32)
              + as_ref[...].astype(F32) * os_ref[...].astype(F32))
    y = _dot(merged.astype(BF16), wo_ref[...])
    rows = x_ref.shape[0]
    x1 = x_ref[...] + _mod_rows(gate_ref, rows) * y
    x1_ref[...] = x1
    h2 = _rms(x1, g2_ref[...]) * (1.0 + _mod_rows(scale_ref, rows)) + _mod_rows(shift_ref, rows)
    h2_hi = h2.astype(BF16)
    h2_ref[...] = h2_hi

    h2_lo = (h2 - h2_hi.astype(F32)).astype(BF16)
    w_hi, w_lo = wr_ref[0], wr_ref[1]
    logits = _dot(h2_hi, w_hi) + (_dot(h2_lo, w_hi) + _dot(h2_hi, w_lo)) + br_ref[...]
    lane_i = lax.broadcasted_iota(jnp.int32, logits.shape, 1)
    lane = lane_i.astype(F32)
    work = logits
    vals, hots = [], []
    for _ in range(TOP_K):
        m = jnp.max(work, axis=-1, keepdims=True)
        idx = jnp.min(jnp.where(work == m, lane, float(LANES)), axis=-1, keepdims=True)
        hot = lane == idx
        vals.append(m)
        hots.append(hot)
        work = jnp.where(hot, -jnp.inf, work)
    exps = [jnp.exp(v - vals[0]) for v in vals]
    denom = exps[0] + exps[1] + exps[2] + exps[3]

    tm = logits.shape[0]
    valid = lax.broadcasted_iota(jnp.int32, (tm, 1), 0) % SAMPLE_ROWS < n_valid
    sel = jnp.zeros_like(logits)
    for hot in hots:
        sel = jnp.where(hot, 1.0, sel)
    sel = jnp.where(valid, sel, 0.0)
    earlier = (lax.broadcasted_iota(jnp.int32, (tm, tm), 1)
               < lax.broadcasted_iota(jnp.int32, (tm, tm), 0))
    rank = _dot(jnp.where(earlier, 1.0, 0.0).astype(BF16), sel.astype(BF16))
    cnt = jnp.sum(sel, axis=0, keepdims=True)
    cnt_pad = jnp.floor((cnt + (ROW_UNIT - 1.0)) * (1.0 / ROW_UNIT)) * ROW_UNIT
    below = (lax.broadcasted_iota(jnp.int32, (LANES, LANES), 0)
             < lax.broadcasted_iota(jnp.int32, (LANES, LANES), 1))
    seg_start = _dot(jnp.broadcast_to(cnt_pad, (8, LANES)), jnp.where(below, 1.0, 0.0), HIGHEST)[0:1]
    pos = seg_start + rank
    slots = jnp.full_like(logits, -1.0)
    gates = jnp.zeros_like(logits)
    for k in range(TOP_K):
        s_k = jnp.sum(jnp.where(hots[k], pos, 0.0), axis=-1, keepdims=True)
        slots = jnp.where(lane_i == k, s_k, slots)
        gates = jnp.where(lane_i == k, exps[k] / denom, gates)
    slot_ref[...] = jnp.where(valid, slots, -1.0)
    gatek_ref[...] = gates
    cnt_ref[...] = cnt


def _post(y_all, o_gla, o_swa, x, gate, shift, scale, g2, wo, wr, br, per_token, tiles_per_seq, n_valid):
    n = x.shape[0]
    mod = _mod_spec(per_token, tiles_per_seq)
    row = lambda i: (i, 0)
    const = lambda i: (0, 0)
    wide = pl.BlockSpec((TOKEN_TILE, D_MODEL), row)
    narrow = pl.BlockSpec((TOKEN_TILE, LANES), row)
    return pl.pallas_call(
        functools.partial(_post_kernel, n_valid=n_valid),
        grid=(n // TOKEN_TILE,),
        in_specs=[pl.BlockSpec((TOKEN_TILE, D_MODEL), lambda i: (i, COL_AG // D_MODEL)),
                  pl.BlockSpec((TOKEN_TILE, D_MODEL), lambda i: (i, COL_AS // D_MODEL)),
                  wide, wide, wide, mod, mod, mod,
                  pl.BlockSpec((1, D_MODEL), const),
                  pl.BlockSpec((D_MODEL, D_MODEL), const),
                  pl.BlockSpec((2, D_MODEL, LANES), lambda i: (0, 0, 0)),
                  pl.BlockSpec((1, LANES), const)],
        out_specs=[wide, wide, narrow, narrow, pl.BlockSpec((None, 1, LANES), lambda i: (i, 0, 0))],
        out_shape=[jax.ShapeDtypeStruct((n, D_MODEL), F32),
                   jax.ShapeDtypeStruct((n, D_MODEL), BF16),
                   jax.ShapeDtypeStruct((n, LANES), F32),
                   jax.ShapeDtypeStruct((n, LANES), F32),
                   jax.ShapeDtypeStruct((n // TOKEN_TILE, 1, LANES), F32)],
        compiler_params=_params("parallel"),
        name="post",
    )(y_all, y_all, o_gla, o_swa, x, gate, shift, scale, g2, wo, wr, br)


def _slot_matrix(slot_cols, weights, chunk):
    tm = slot_cols[0].shape[0]
    j = lax.broadcasted_iota(jnp.int32, (tm, tm), 1) + chunk * tm
    out = jnp.zeros((tm, tm), F32)
    for s, w in zip(slot_cols, weights):
        out = jnp.where(s == j, w, out)
    return out.astype(BF16)


def _dispatch_kernel(nu_ref, seg_ref, ntail_ref, tail_ref, hp_ref, slp_ref, hs_ref, sls_ref, xg_ref,
                     sorted_ref, zero_ref, sem, tail_sem, *, prompt_tiles):
    t = pl.program_id(0)
    last = pl.num_programs(0) - 1
    buf = t % 2

    def sort_tile(h_ref, slot_ref):
        tm = h_ref.shape[0]
        slots = slot_ref[...].astype(jnp.int32)
        slot_cols = [slots[:, k:k + 1] for k in range(TOP_K)]
        h = h_ref[...]
        for c in range(LOCAL_ROWS // tm):
            onehot = _slot_matrix(slot_cols, [1.0] * TOP_K, c)
            upc = tm // ROW_UNIT
            sorted_ref[buf, c * upc:(c + 1) * upc] = _dot_tn(onehot, h).astype(BF16).reshape(
                upc, ROW_UNIT, D_MODEL)

    @pl.when(t < prompt_tiles)
    def _():
        sort_tile(hp_ref, slp_ref)

    @pl.when(t >= prompt_tiles)
    def _():
        sort_tile(hs_ref, sls_ref)

    def start_all(tile, b):
        def body(e, c):
            n = seg_ref[tile, 0, e]

            @pl.when(n > 0)
            def _():
                pltpu.make_async_copy(sorted_ref.at[b, pl.ds(seg_ref[tile, 1, e], n)],
                                      xg_ref.at[pl.ds(seg_ref[tile, 2, e], n)], sem.at[b]).start()
            return c

        lax.fori_loop(0, N_EXPERTS, body, 0)

    def wait_all(tile, b):
        n_units = nu_ref[tile]

        @pl.when(n_units > 0)
        def _():
            pltpu.make_async_copy(sorted_ref.at[b, pl.ds(0, n_units)],
                                  xg_ref.at[pl.ds(0, n_units)], sem.at[b]).wait()

    @pl.when(t > 0)
    def _():
        wait_all(t - 1, 1 - buf)

    start_all(t, buf)

    def tail_copy(i):
        return pltpu.make_async_copy(zero_ref, xg_ref.at[tail_ref[i]], tail_sem)

    @pl.when(t == last)
    def _():
        zero_ref[...] = jnp.zeros_like(zero_ref)
        n_tail = ntail_ref[0]
        lax.fori_loop(0, n_tail, lambda i, c: (tail_copy(i).start(), c)[1], 0)
        lax.fori_loop(0, n_tail, lambda i, c: (tail_copy(i).wait(), c)[1], 0)
        wait_all(t, buf)


def _dispatch(h2_p, slots_p, h2_s, slots_s, n_units, seg, n_tail, tail_dst, rows_max):
    p_tiles = h2_p.shape[0] // TOKEN_TILE
    s_tiles = h2_s.shape[0] // TOKEN_TILE
    p_row = lambda t, *_: (jnp.minimum(t, p_tiles - 1), 0)
    s_row = lambda t, *_: (jnp.maximum(t - p_tiles, 0), 0)
    return pl.pallas_call(
        functools.partial(_dispatch_kernel, prompt_tiles=p_tiles),
        grid_spec=pltpu.PrefetchScalarGridSpec(
            num_scalar_prefetch=4,
            grid=(p_tiles + s_tiles,),
            in_specs=[pl.BlockSpec((TOKEN_TILE, D_MODEL), p_row),
                      pl.BlockSpec((TOKEN_TILE, LANES), p_row),
                      pl.BlockSpec((TOKEN_TILE, D_MODEL), s_row),
                      pl.BlockSpec((TOKEN_TILE, LANES), s_row)],
            out_specs=pl.BlockSpec(memory_space=pl.ANY),
            scratch_shapes=[pltpu.VMEM((2, LOCAL_ROWS // ROW_UNIT, ROW_UNIT, D_MODEL), BF16),
                            pltpu.VMEM((ROW_UNIT, D_MODEL), BF16),
                            pltpu.SemaphoreType.DMA((2,)), pltpu.SemaphoreType.DMA]),
        out_shape=jax.ShapeDtypeStruct((rows_max // ROW_UNIT, ROW_UNIT, D_MODEL), BF16),
        compiler_params=_params("arbitrary"),
        name="dispatch",
    )(n_units, seg, n_tail, tail_dst, h2_p, slots_p, h2_s, slots_s).reshape(rows_max, D_MODEL)


def _expert_kernel(tg_ref, ge_ref, ng_ref, nused_ref, rows_ref, x_ref, bgu_ref, bd_ref, wgu_hbm, wd_hbm,
                   y_ref, wgu_f32, wd_f32, wgu_bf, wd_bf, sem):
    i = pl.program_id(0)

    def fetch(g, b):
        e = ge_ref[g]
        return (pltpu.make_async_copy(wgu_hbm.at[e], wgu_f32.at[b], sem.at[0, b]),
                pltpu.make_async_copy(wd_hbm.at[e], wd_f32.at[b], sem.at[1, b]))

    @pl.when(i == 0)
    def _():
        for cp in fetch(0, 0):
            cp.start()

    @pl.when(i < nused_ref[0])
    def _():
        g = tg_ref[i]
        b = g % 2

        @pl.when((i == 0) | (g != tg_ref[jnp.maximum(i - 1, 0)]))
        def _():
            @pl.when(g + 1 < ng_ref[0])
            def _():
                for cp in fetch(g + 1, 1 - b):
                    cp.start()

            for cp in fetch(g, b):
                cp.wait()
            wgu_bf[...] = wgu_f32[b].astype(BF16)
            wd_bf[...] = wd_f32[b].astype(BF16)

        for part in range(EXPERT_TILE // EXPERT_PART):
            @pl.when(rows_ref[i] > part * EXPERT_PART)
            def _():
                sl = slice(part * EXPERT_PART, (part + 1) * EXPERT_PART)
                gu = _dot(x_ref[sl, :], wgu_bf[...]) + bgu_ref[...]
                gate = jnp.minimum(gu[:, :D_FF], SWIGLU_LIMIT)
                up = jnp.clip(gu[:, D_FF:], -SWIGLU_LIMIT, SWIGLU_LIMIT)
                act = (up + 1.0) * gate * jax.nn.sigmoid(SWIGLU_ALPHA * gate)
                y_ref[sl, :] = (_dot(act.astype(BF16), wd_bf[...]) + bd_ref[...]).astype(y_ref.dtype)


def _experts(xg, tile_group, group_expert, n_groups, n_used, tile_rows, wgu, bgu, wd, bd):
    rows_max = xg.shape[0]
    used = lambda i, nu: jnp.maximum(jnp.minimum(i, nu[0] - 1), 0)
    row = lambda i, tg, ge, ng, nu, tr: (used(i, nu), 0)
    exp = lambda i, tg, ge, ng, nu, tr: (ge[tg[used(i, nu)]], 0, 0)
    return pl.pallas_call(
        _expert_kernel,
        grid_spec=pltpu.PrefetchScalarGridSpec(
            num_scalar_prefetch=5,
            grid=(rows_max // EXPERT_TILE,),
            in_specs=[pl.BlockSpec((EXPERT_TILE, D_MODEL), row),
                      pl.BlockSpec((None, 1, 2 * D_FF), exp),
                      pl.BlockSpec((None, 1, D_MODEL), exp),
                      pl.BlockSpec(memory_space=pl.ANY),
                      pl.BlockSpec(memory_space=pl.ANY)],
            out_specs=pl.BlockSpec((EXPERT_TILE, D_MODEL), row),
            scratch_shapes=[pltpu.VMEM((2, D_MODEL, 2 * D_FF), F32), pltpu.VMEM((2, D_FF, D_MODEL), F32),
                            pltpu.VMEM((D_MODEL, 2 * D_FF), BF16), pltpu.VMEM((D_FF, D_MODEL), BF16),
                            pltpu.SemaphoreType.DMA((2, 2))]),
        out_shape=jax.ShapeDtypeStruct((rows_max, D_MODEL), BF16),
        compiler_params=_params("arbitrary"),
        name="experts",
    )(tile_group, group_expert, n_groups, n_used, tile_rows, xg, bgu, bd, wgu, wd)


def _combine_kernel(nu_ref, seg_ref, slot_ref, gatek_ref, x_ref, gmlp_ref, y_ref, o_ref, ys_ref, sem,
                    *, tile_offset):
    j = pl.program_id(0)
    t = j + tile_offset
    buf = j % 2
    tm = x_ref.shape[0]

    def fetch(tile, b):
        def body(e, c):
            n = seg_ref[tile, 0, e]

            @pl.when(n > 0)
            def _():
                pltpu.make_async_copy(y_ref.at[pl.ds(seg_ref[tile, 2, e], n)],
                                      ys_ref.at[b, pl.ds(seg_ref[tile, 1, e], n)], sem.at[b]).start()
            return c

        lax.fori_loop(0, N_EXPERTS, body, 0)
        n_units = nu_ref[tile]

        def zero_unit(i, c):
            ys_ref[b, i] = jnp.zeros((ROW_UNIT, D_MODEL), ys_ref.dtype)
            return c

        lax.fori_loop(n_units, LOCAL_ROWS // ROW_UNIT, zero_unit, 0)

    @pl.when(j == 0)
    def _():
        fetch(t, buf)

    @pl.when(j + 1 < pl.num_programs(0))
    def _():
        fetch(t + 1, 1 - buf)

    n_units = nu_ref[t]

    @pl.when(n_units > 0)
    def _():
        pltpu.make_async_copy(y_ref.at[pl.ds(0, n_units)], ys_ref.at[buf, pl.ds(0, n_units)],
                              sem.at[buf]).wait()

    slots = slot_ref[...].astype(jnp.int32)
    gates = gatek_ref[...]
    slot_cols = [slots[:, k:k + 1] for k in range(TOP_K)]
    gate_cols = [gates[:, k:k + 1] for k in range(TOP_K)]
    acc = jnp.zeros((tm, D_MODEL), F32)
    for c in range(LOCAL_ROWS // tm):
        upc = tm // ROW_UNIT
        rows_c = ys_ref[buf, c * upc:(c + 1) * upc].reshape(tm, D_MODEL)
        acc = acc + _dot(_slot_matrix(slot_cols, gate_cols, c), rows_c)
    o_ref[...] = x_ref[...] + _mod_rows(gmlp_ref, tm) * acc


def _combine(y, slots, gates, x1, gmlp, n_units, seg, tile_offset, per_token, tiles_per_seq):
    n = x1.shape[0]
    mod = _mod_spec(per_token, tiles_per_seq)
    wide = pl.BlockSpec((TOKEN_TILE, D_MODEL), lambda i, *_: (i, 0))
    narrow = pl.BlockSpec((TOKEN_TILE, LANES), lambda i, *_: (i, 0))
    return pl.pallas_call(
        functools.partial(_combine_kernel, tile_offset=tile_offset),
        grid_spec=pltpu.PrefetchScalarGridSpec(
            num_scalar_prefetch=2,
            grid=(n // TOKEN_TILE,),
            in_specs=[narrow, narrow, wide, mod, pl.BlockSpec(memory_space=pl.ANY)],
            out_specs=wide,
            scratch_shapes=[pltpu.VMEM((2, LOCAL_ROWS // ROW_UNIT, ROW_UNIT, D_MODEL), BF16),
                            pltpu.SemaphoreType.DMA((2,))]),
        out_shape=jax.ShapeDtypeStruct((n, D_MODEL), F32),
        compiler_params=_params("arbitrary"),
        name="combine",
    )(n_units, seg, slots, gates, x1, gmlp, y.reshape(-1, ROW_UNIT, D_MODEL))


def _route_tables(cnt, rows_max):
    units = (cnt + ROW_UNIT - 1) // ROW_UNIT
    group_units = jnp.sum(units, axis=0)
    upt = EXPERT_TILE // ROW_UNIT
    group_pad = (group_units + upt - 1) // upt * upt
    group_end = jnp.cumsum(group_pad)
    group_start = group_end - group_pad
    seg_start = group_start[None, :] + jnp.cumsum(units, axis=0) - units
    local_end = jnp.cumsum(units, axis=1)
    local_start = local_end - units
    n_units = local_end[:, -1].astype(jnp.int32)

    def pick(lo, hi, pos, value):
        return jnp.sum(jnp.where((pos >= lo) & (pos < hi), value, 0), axis=-1).astype(jnp.int32)

    seg = jnp.stack([units, local_start, seg_start], axis=1).astype(jnp.int32)

    n_tail_e = group_pad - group_units
    j = jnp.arange(N_EXPERTS * upt, dtype=jnp.int32)[:, None]
    tail_end = jnp.cumsum(n_tail_e)
    tail_start = tail_end - n_tail_e
    tail_hbm = pick(tail_start[None, :], tail_end[None, :], j,
                    (group_start + group_units)[None, :] + j - tail_start[None, :])
    n_tail = tail_end[-1:].astype(jnp.int32)

    r = jnp.arange(rows_max // EXPERT_TILE, dtype=jnp.int32)[:, None] * upt
    n_used = (group_end[-1:] // upt).astype(jnp.int32)
    nonempty = group_units > 0
    group_of_expert = jnp.cumsum(nonempty) - 1
    tile_rows = pick(group_start[None, :], group_end[None, :], r,
                     jnp.clip(((group_start + group_units)[None, :] - r) * ROW_UNIT, 0, EXPERT_TILE))
    tile_group = pick(group_start[None, :], group_end[None, :], r, group_of_expert[None, :])
    g = jnp.arange(N_EXPERTS, dtype=jnp.int32)
    group_expert = jnp.sum(jnp.where(nonempty[None, :] & (group_of_expert[None, :] == g[:, None]),
                                     g[None, :], 0), axis=-1).astype(jnp.int32)
    n_groups = jnp.sum(nonempty)[None].astype(jnp.int32)
    return (n_units, seg, n_tail, tail_hbm,
            tile_group, group_expert, n_groups, n_used, tile_rows)


def _rope_tables(pos):
    half = HEAD_DIM // 2
    lane = np.arange(LANES)
    inv = jnp.asarray(ROPE_THETA, F32) ** (-jnp.asarray(lane % half, F32) / half)
    sign = jnp.asarray(np.where(lane % HEAD_DIM < half, -1.0, 1.0), F32)
    ang = pos.astype(F32)[:, None] * inv[None, :]
    return jnp.cos(ang), jnp.sin(ang) * sign[None, :]


def kernel(x_prompt, x_sample, c_prompt, c_sample, state_gla, cache_swa_k, cache_swa_v, w_ada, b_ada,
           norm1_g, norm2_g, w_in, w_gk2, b_gk, gla_norm_g, q_norm_g, k_norm_g, attn_sinks, w_o,
           w_router, b_router, w_gate_up, b_gate_up, w_down, b_down):
    batch, seq, d = x_prompt.shape
    n_seq, n_real, _ = x_sample.shape
    depth = w_in.shape[0]
    assert depth == 1 and d == D_MODEL and n_real <= SAMPLE_ROWS
    assert seq % TOKEN_TILE == 0 and (n_seq * SAMPLE_ROWS) % TOKEN_TILE == 0 and n_seq % SAMPLE_SEQS == 0
    R = SAMPLE_ROWS
    kvw = SWA_KV_HEADS * HEAD_DIM
    tiles_per_seq = seq // TOKEN_TILE

    qg, kg, vg, rg, glr_w, qs, ks, vs, ag, as_ = jnp.split(
        w_in[0], [512, 1024, 2048, 3072, 3088, 4112, 4368, 4624, 5648], axis=1)
    w_main = jnp.concatenate([vg, rg, qs, ag, as_, qg, kg, ks, vs], axis=1).astype(BF16)
    w_glr = jnp.pad(glr_w, ((0, 0), (0, LANES - GATE_RANK))).astype(BF16)
    wgk = jnp.pad(w_gk2[0], ((0, LANES - GATE_RANK), (0, 0)))
    bgk = b_gk[0].reshape(1, -1)
    gnorm = gla_norm_g[0].reshape(1, -1)
    gq = jnp.tile(q_norm_g[0], LANES // HEAD_DIM).reshape(1, LANES)
    gk = jnp.tile(k_norm_g[0], LANES // HEAD_DIM).reshape(1, LANES)
    seg = jnp.asarray(np.kron(np.eye(LANES // HEAD_DIM), np.ones((HEAD_DIM, HEAD_DIM))), BF16)
    sinks = attn_sinks[0]
    wo = w_o[0].astype(BF16)
    wr = jnp.pad(w_router[0], ((0, 0), (0, LANES - N_EXPERTS)))
    wr_hi = wr.astype(BF16)
    wr = jnp.stack([wr_hi, (wr - wr_hi.astype(F32)).astype(BF16)])
    br = jnp.pad(b_router[0], (0, LANES - N_EXPERTS), constant_values=-1e30).reshape(1, LANES)
    bgu = b_gate_up[0].reshape(N_EXPERTS, 1, 2 * D_FF)
    bd = b_down[0].reshape(N_EXPERTS, 1, D_MODEL)
    g1 = norm1_g[0].reshape(1, -1)
    g2 = norm2_g[0].reshape(1, -1)

    n_c = batch + n_seq
    c_all = jnp.pad(jnp.concatenate([c_prompt, c_sample], axis=0), ((0, -n_c % 8), (0, 0)))
    m_all = _ada(c_all, w_ada[0], b_ada[0])
    mp = [m_all[:batch, i * d:(i + 1) * d].reshape(batch, 1, d) for i in range(6)]
    ms = [m_all[batch:n_c, i * d:(i + 1) * d] for i in range(6)]

    xp = x_prompt.reshape(batch * seq, d)
    xs = jnp.pad(x_sample, ((0, 0), (0, R - n_real), (0, 0))).reshape(n_seq * R, d)
    cos_p, sin_p = _rope_tables(jnp.arange(seq))
    cos_s, sin_s = _rope_tables(PAST_LEN + jnp.tile(jnp.arange(R), n_seq))
    sink_rows = jnp.repeat(sinks, R).reshape(SWA_HEADS * R, 1)

    yp, glr_p = _inproj(xp, mp[0], mp[1], g1, w_main, w_glr, cos_p, sin_p, gq, gk, seg,
                        BF16, False, tiles_per_seq)
    ys, glr_s = _inproj(xs, ms[0], ms[1], g1, w_main, w_glr, cos_s, sin_s, gq, gk, seg, F32, True, 1)
    og_p, st_p = _gla_prompt(yp, glr_p, wgk, bgk, gnorm, batch, seq)
    og_s, st_s = _gla_sample(ys, glr_s, wgk, bgk, gnorm, state_gla[0], n_seq, n_real)
    os_p, kc_p, vc_p = _swa_prompt(yp, sinks, batch, seq)
    os_s, kc_s, vc_s = _swa_sample(ys, sink_rows, cache_swa_k[0].reshape(n_seq, WINDOW, kvw),
                                   cache_swa_v[0].reshape(n_seq, WINDOW, kvw), n_seq, n_real)
    x1_p, h2_p, sl_p, gt_p, cnt_p = _post(yp, og_p, os_p, xp, mp[2], mp[3], mp[4], g2, wo, wr, br,
                                          False, tiles_per_seq, R)
    x1_s, h2_s, sl_s, gt_s, cnt_s = _post(ys, og_s, os_s, xs, ms[2], ms[3], ms[4], g2, wo, wr, br,
                                          True, 1, n_real)

    cnt = jnp.concatenate([cnt_p, cnt_s], axis=0)[:, 0, :N_EXPERTS].astype(jnp.int32)
    n_tiles = cnt.shape[0]
    rows_bound = (TOP_K * (batch * seq + n_seq * n_real) + n_tiles * N_EXPERTS * (ROW_UNIT - 1)
                  + N_EXPERTS * (EXPERT_TILE - 1))
    rows_max = -(-rows_bound // EXPERT_TILE) * EXPERT_TILE
    (n_units, seg, n_tail, tail_hbm,
     tile_group, group_expert, n_groups, n_used, tile_rows) = _route_tables(cnt, rows_max)
    xg = _dispatch(h2_p, sl_p, h2_s, sl_s, n_units, seg, n_tail, tail_hbm, rows_max)
    yg = _experts(xg, tile_group, group_expert, n_groups, n_used, tile_rows,
                  w_gate_up[0], bgu, w_down[0], bd)
    p_tiles = batch * seq // TOKEN_TILE
    out_p = _combine(yg, sl_p, gt_p, x1_p, mp[5], n_units, seg, 0, False, tiles_per_seq)
    out_s = _combine(yg, sl_s, gt_s, x1_s, ms[5], n_units, seg, p_tiles, True, 1)

    cache_shape = (WINDOW, SWA_KV_HEADS, HEAD_DIM)
    return (out_p.reshape(batch, seq, d),
            out_s.reshape(n_seq, R, d)[:, :n_real],
            st_p[None],
            kc_p.reshape(1, batch, *cache_shape),
            vc_p.reshape(1, batch, *cache_shape),
            st_s[None],
            kc_s.reshape(1, n_seq, *cache_shape),
            vc_s.reshape(1, n_seq, *cache_shape))
```

```python
import functools

import numpy as np
import jax
import jax.numpy as jnp
from jax import lax
from jax.experimental import pallas as pl
from jax.experimental.pallas import tpu as pltpu

F32 = jnp.float32
BF16 = jnp.bfloat16
HIGHEST = lax.Precision.HIGHEST

D_MODEL = 1024
PAST_LEN = 16384
GLA_HEADS = 4
GLA_DK = 128
GLA_DV = 256
GATE_RANK = 16
GATE_TAU = 16.0
SWA_HEADS = 16
SWA_KV_HEADS = 4
HEAD_DIM = 64
SWA_GROUP = SWA_HEADS // SWA_KV_HEADS
WINDOW = 128
ROPE_THETA = 10000.0
N_EXPERTS = 32
TOP_K = 4
D_FF = 1024
SWIGLU_ALPHA = 1.702
SWIGLU_LIMIT = 7.0
NORM_EPS = 1e-6

LANES = 128
SUBLANES = 8
SAMPLE_ROWS = 8
SAMPLE_SEQS = 16
TOKEN_TILE = 512
SWA_STEP_BLOCKS = 2
GLA_BLOCK = 256
GLA_MASK_BLOCK = 128
VMEM_LIMIT = 56 * 1024 * 1024
ROW_UNIT = 16
EXPERT_TILE = 1024
EXPERT_PART = 256
LOCAL_ROWS = -(-(TOP_K * TOKEN_TILE + N_EXPERTS * (ROW_UNIT - 1)) // TOKEN_TILE) * TOKEN_TILE

COL_VG, COL_RG, COL_QS, COL_AG, COL_AS = 0, 1024, 2048, 3072, 4096
COL_QG, COL_KG, COL_KS, COL_VS = 5120, 5632, 6144, 6400
D_MAIN = 6656
PROJ_CHUNK = 512
PROJ_AHEAD = 2


def _dot(a, b, precision=None):
    return jnp.dot(a, b, preferred_element_type=F32, precision=precision)


def _dot_nt(a, b, precision=None):
    return lax.dot_general(a, b, (((1,), (1,)), ((), ())), preferred_element_type=F32, precision=precision)


def _dot_tn(a, b, precision=None):
    return lax.dot_general(a, b, (((0,), (0,)), ((), ())), preferred_element_type=F32, precision=precision)


def _dot_sum(sel, x):
    hi = x.astype(BF16)
    r1 = x - hi.astype(F32)
    mid = r1.astype(BF16)
    lo = (r1 - mid.astype(F32)).astype(BF16)
    return _dot(sel, hi) + _dot(sel, mid) + _dot(sel, lo)


def _params(*sem):
    return pltpu.CompilerParams(dimension_semantics=sem, vmem_limit_bytes=VMEM_LIMIT)


def _rms(x, g):
    return x * lax.rsqrt(jnp.mean(x * x, axis=-1, keepdims=True) + NORM_EPS) * g


def _log_sigmoid(x):
    return jnp.minimum(x, 0.0) - jnp.log(1.0 + jnp.exp(-jnp.abs(x)))


def _ada_kernel(c_ref, w_ref, b_ref, o_ref):
    c = c_ref[...]
    s = c * jax.nn.sigmoid(c)
    s_hi = s.astype(BF16)
    s_lo = (s - s_hi.astype(F32)).astype(BF16)
    w = w_ref[...]
    w_hi = w.astype(BF16)
    w_lo = (w - w_hi.astype(F32)).astype(BF16)
    o_ref[...] = _dot(s_hi, w_hi) + (_dot(s_lo, w_hi) + _dot(s_hi, w_lo)) + b_ref[...]


def _ada(c_all, w_ada, b_ada):
    rows = c_all.shape[0]
    tn = 768
    return pl.pallas_call(
        _ada_kernel,
        grid=(6 * D_MODEL // tn,),
        in_specs=[pl.BlockSpec((rows, D_MODEL), lambda j: (0, 0)),
                  pl.BlockSpec((D_MODEL, tn), lambda j: (0, j)),
                  pl.BlockSpec((1, tn), lambda j: (0, j))],
        out_specs=pl.BlockSpec((rows, tn), lambda j: (0, j)),
        out_shape=jax.ShapeDtypeStruct((rows, 6 * D_MODEL), F32),
        compiler_params=_params("parallel"),
        name="ada",
    )(c_all, w_ada, b_ada.reshape(1, -1))


def _inproj_kernel(x_ref, shift_ref, scale_ref, g_ref, w_ref, wg_ref, cos_ref, sin_ref, gq_ref, gk_ref,
                   seg_ref, o_ref, og_ref):
    rows = x_ref.shape[0]
    h = _rms(x_ref[...], g_ref[...]) * (1.0 + _mod_rows(scale_ref, rows)) + _mod_rows(shift_ref, rows)
    hb = h.astype(BF16)
    cos, sin, seg = cos_ref[...], sin_ref[...], seg_ref[...]
    def has_followup(j):
        c0, c1 = j * PROJ_CHUNK, (j + 1) * PROJ_CHUNK
        return c0 < COL_AS + D_MODEL and c1 > COL_QS or c0 < COL_KS + SWA_KV_HEADS * HEAD_DIM and c1 > COL_KS

    chunks = range(D_MAIN // PROJ_CHUNK)
    busy = [j for j in chunks if has_followup(j)]
    plain = [j for j in chunks if not has_followup(j)]
    order = [j for pair in zip(busy, plain) for j in pair] + busy[len(plain):] + plain[len(busy):]
    project = lambda j: _dot(hb, w_ref[:, j * PROJ_CHUNK:(j + 1) * PROJ_CHUNK])
    ahead = [project(j) for j in order[:PROJ_AHEAD]]
    for idx, j in enumerate(order):
        sl = slice(j * PROJ_CHUNK, (j + 1) * PROJ_CHUNK)
        r = ahead.pop(0)
        if idx + PROJ_AHEAD < len(order):
            ahead.append(project(order[idx + PROJ_AHEAD]))
        blocks = []
        for m in range(PROJ_CHUNK // LANES):
            c0 = j * PROJ_CHUNK + m * LANES
            blk = r[:, m * LANES:(m + 1) * LANES]
            if COL_QS <= c0 < COL_QS + SWA_HEADS * HEAD_DIM:
                blk = _head_norm_rope(blk, gq_ref[...], cos, sin, seg) * (HEAD_DIM ** -0.5)
            elif COL_KS <= c0 < COL_KS + SWA_KV_HEADS * HEAD_DIM:
                blk = _head_norm_rope(blk, gk_ref[...], cos, sin, seg)
            elif COL_AG <= c0 < COL_AS + D_MODEL:
                blk = jax.nn.sigmoid(blk)
            blocks.append(blk)
        o_ref[:, sl] = jnp.concatenate(blocks, axis=1).astype(o_ref.dtype)
    og_ref[...] = _dot(hb, wg_ref[...])


def _mod_spec(per_token, tiles_per_seq):
    if per_token:
        return pl.BlockSpec((TOKEN_TILE // SAMPLE_ROWS, D_MODEL), lambda i, *_: (i, 0))
    return pl.BlockSpec((None, 1, D_MODEL), lambda i, *_: (i // tiles_per_seq, 0, 0))


def _mod_rows(ref, rows):
    m = ref[...]
    if m.shape[0] == 1:
        return m
    return jnp.broadcast_to(m[:, None, :], (m.shape[0], rows // m.shape[0], m.shape[1])).reshape(
        rows, m.shape[1])


def _inproj(x, shift, scale, g, w_main, w_glr, cos, sin, gq, gk, seg, out_dtype, per_token, tiles_per_seq):
    n = x.shape[0]
    mod = _mod_spec(per_token, tiles_per_seq)
    const = lambda i: (0, 0)
    rope = pl.BlockSpec((TOKEN_TILE, LANES), (lambda i: (i, 0)) if per_token else
                        (lambda i: (i % tiles_per_seq, 0)))
    return pl.pallas_call(
        _inproj_kernel,
        grid=(n // TOKEN_TILE,),
        in_specs=[pl.BlockSpec((TOKEN_TILE, D_MODEL), lambda i: (i, 0)), mod, mod,
                  pl.BlockSpec((1, D_MODEL), const),
                  pl.BlockSpec((D_MODEL, D_MAIN), const, pipeline_mode=pl.Buffered(1)),
                  pl.BlockSpec((D_MODEL, LANES), const, pipeline_mode=pl.Buffered(1)),
                  rope, rope,
                  pl.BlockSpec((1, LANES), const), pl.BlockSpec((1, LANES), const),
                  pl.BlockSpec((LANES, LANES), const)],
        out_specs=[pl.BlockSpec((TOKEN_TILE, D_MAIN), lambda i: (i, 0)),
                   pl.BlockSpec((TOKEN_TILE, LANES), lambda i: (i, 0))],
        out_shape=[jax.ShapeDtypeStruct((n, D_MAIN), out_dtype),
                   jax.ShapeDtypeStruct((n, LANES), F32)],
        compiler_params=_params("parallel"),
        name="inproj",
    )(x, shift, scale, g, w_main, w_glr, cos, sin, gq, gk, seg)


def _gla_log_gate(glr, wgk_ref, bgk):
    g_hi = glr.astype(BF16)
    g_lo = (glr - g_hi.astype(F32)).astype(BF16)
    w_hi, w_lo = wgk_ref[0], wgk_ref[1]
    x = _dot(g_hi, w_hi) + (_dot(g_lo, w_hi) + _dot(g_hi, w_lo))
    return _log_sigmoid(x + bgk) * (1.0 / GATE_TAU)


def _gla_out(o, r, g):
    r = r.astype(F32)
    return _rms(o, g) * (r * jax.nn.sigmoid(r))


def _gla_pair_levels(n):
    t = np.arange(n)[:, None]
    s = np.arange(n)[None, :]
    x = t ^ s
    top = np.where(x > 0, 1 << np.floor(np.log2(np.maximum(x, 1))).astype(np.int64), 0)
    return np.where(s > t, -1, top).astype(np.int32)


def _gla_block_ref(b, h):
    n, w = b.shape
    if 2 * h == n:
        return jnp.broadcast_to(b[h - 1:h, :], (n, w))
    if h >= SUBLANES // 2:
        picked = b.reshape(n // (2 * h), 2 * h, w)[:, h - 1:h, :]
        return jnp.broadcast_to(picked, (n // (2 * h), 2 * h, w)).reshape(n, w)
    r = lax.broadcasted_iota(jnp.int32, (n, 1), 0) % (2 * h)
    out = b
    for d in range(1, h + 1):
        out = jnp.where(r == h - 1 + d, pltpu.roll(b, d, 0), out)
    for d in range(1, h):
        out = jnp.where(r == h - 1 - d, pltpu.roll(b, n - d, 0), out)
    return out


def _gla_prompt_kernel(q_ref, k_ref, v_ref, r_ref, glr_ref, wgk_ref, bgk_ref, g_ref, lev_ref,
                       o_ref, s_ref, st_ref):
    c = pl.program_id(1)

    @pl.when(c == 0)
    def _():
        st_ref[...] = jnp.zeros_like(st_ref)

    n = GLA_BLOCK
    lev = lev_ref[...]
    causal = lax.broadcasted_iota(jnp.int32, (n, n), 1) <= lax.broadcasted_iota(jnp.int32, (n, n), 0)
    lg = _gla_log_gate(glr_ref[...], wgk_ref, bgk_ref[...])
    b = _dot_sum(jnp.where(causal, 1.0, 0.0).astype(BF16), lg)
    b_last = b[n - 1:n, :]
    q = q_ref[...].astype(F32) * (GLA_DK ** -0.5)
    k = k_ref[...].astype(F32)
    qe = (q * jnp.exp(b)).astype(BF16)
    kd = (k * jnp.exp(b_last - b)).astype(BF16)
    decay = jnp.exp(b_last)
    levels = [0] + [1 << p for p in range(n.bit_length() - 1)]
    q_lv, k_lv = [q.astype(BF16)], [k.astype(BF16)]
    for h in levels[1:]:
        e = jnp.exp(-jnp.abs(b - _gla_block_ref(b, h)))
        q_lv.append((q * e).astype(BF16))
        k_lv.append((k * e).astype(BF16))
    scaled = {level: (ql, kl) for level, ql, kl in zip(levels, q_lv, k_lv)}

    def intra(lo, hi, dk, v):
        size = hi - lo
        if size == GLA_MASK_BLOCK:
            attn = jnp.zeros((size, size), F32)
            for level in levels:
                if level < size:
                    ql, kl = scaled[level]
                    attn = jnp.where(lev == level, _dot_nt(ql[lo:hi, dk], kl[lo:hi, dk]), attn)
            return _dot(attn.astype(BF16), v[lo:hi, :])
        mid = lo + size // 2
        ql, kl = scaled[size // 2]
        cross = _dot_nt(ql[mid:hi, dk], kl[lo:mid, dk]).astype(BF16)
        return jnp.concatenate([intra(lo, mid, dk, v),
                                _dot(cross, v[lo:mid, :]) + intra(mid, hi, dk, v)], axis=0)

    for h in range(GLA_HEADS):
        dk = slice(h * GLA_DK, (h + 1) * GLA_DK)
        dv = slice(h * GLA_DV, (h + 1) * GLA_DV)
        v = v_ref[:, dv]
        st = st_ref[h]
        o = _dot_nt(qe[:, dk], st.astype(BF16)) + intra(0, n, dk, v)
        st_ref[h] = st * decay[:, dk] + _dot_tn(v, kd[:, dk])
        o_ref[:, dv] = _gla_out(o, r_ref[:, dv], g_ref[...]).astype(o_ref.dtype)

    @pl.when(c == pl.num_programs(1) - 1)
    def _():
        for h in range(GLA_HEADS):
            s_ref[h] = st_ref[h].T


def _gla_prompt(yp, glr, wgk, bgk, gnorm, batch, seq):
    nb = seq // GLA_BLOCK
    hk, hv = GLA_HEADS * GLA_DK, GLA_HEADS * GLA_DV
    tok = lambda b, c: b * nb + c
    const = lambda b, c: (0, 0)
    return pl.pallas_call(
        _gla_prompt_kernel,
        grid=(batch, nb),
        in_specs=[pl.BlockSpec((GLA_BLOCK, hk), lambda b, c: (tok(b, c), COL_QG // hk)),
                  pl.BlockSpec((GLA_BLOCK, hk), lambda b, c: (tok(b, c), COL_KG // hk)),
                  pl.BlockSpec((GLA_BLOCK, hv), lambda b, c: (tok(b, c), COL_VG // hv)),
                  pl.BlockSpec((GLA_BLOCK, hv), lambda b, c: (tok(b, c), COL_RG // hv)),
                  pl.BlockSpec((GLA_BLOCK, LANES), lambda b, c: (tok(b, c), 0)),
                  pl.BlockSpec((2, LANES, hk), lambda b, c: (0, 0, 0)),
                  pl.BlockSpec((1, hk), const),
                  pl.BlockSpec((1, GLA_DV), const),
                  pl.BlockSpec((GLA_MASK_BLOCK, GLA_MASK_BLOCK), const)],
        out_specs=[pl.BlockSpec((GLA_BLOCK, hv), lambda b, c: (tok(b, c), 0)),
                   pl.BlockSpec((None, GLA_HEADS, GLA_DK, GLA_DV), lambda b, c: (b, 0, 0, 0))],
        out_shape=[jax.ShapeDtypeStruct((batch * seq, hv), BF16),
                   jax.ShapeDtypeStruct((batch, GLA_HEADS, GLA_DK, GLA_DV), F32)],
        scratch_shapes=[pltpu.VMEM((GLA_HEADS, GLA_DV, GLA_DK), F32)],
        compiler_params=_params("parallel", "arbitrary"),
        name="gla_prompt",
    )(yp, yp, yp, yp, glr, wgk, bgk, gnorm, jnp.asarray(_gla_pair_levels(GLA_MASK_BLOCK)))


def _gla_sample_kernel(q_ref, k_ref, v_ref, r_ref, glr_ref, wgk_ref, bgk_ref, g_ref, s0_ref, lev_ref,
                       o_ref, s_ref, *, n_real):
    R = SAMPLE_ROWS
    rows = q_ref.shape[0]
    row = lax.broadcasted_iota(jnp.int32, (rows, rows), 0)
    col = lax.broadcasted_iota(jnp.int32, (rows, rows), 1)
    same = (row // R) == (col // R)
    causal = same & (col <= row)
    real = lax.broadcasted_iota(jnp.int32, (rows, 1), 0) % R < n_real
    lg = jnp.where(real, _gla_log_gate(glr_ref[...], wgk_ref, bgk_ref[...]), 0.0)
    b = _dot_sum(jnp.where(causal, 1.0, 0.0).astype(BF16), lg)
    b_last = _dot_sum(jnp.where(same, 1.0, 0.0).astype(BF16), lg)
    q = q_ref[...] * (GLA_DK ** -0.5)
    k = jnp.where(real, k_ref[...], 0.0)
    v = v_ref[...]
    qe = q * jnp.exp(b)
    kd = k * jnp.exp(b_last - b)
    lev = lev_ref[...]
    levels = [0] + [1 << p for p in range(R.bit_length() - 1)]
    q_lv, k_lv = [q.astype(BF16)], [k.astype(BF16)]
    for h in levels[1:]:
        e = jnp.exp(-jnp.abs(b - _gla_block_ref(b, h)))
        q_lv.append((q * e).astype(BF16))
        k_lv.append((k * e).astype(BF16))
    v_bf = v.astype(BF16)
    for h in range(GLA_HEADS):
        dk = slice(h * GLA_DK, (h + 1) * GLA_DK)
        dv = slice(h * GLA_DV, (h + 1) * GLA_DV)
        attn = jnp.zeros((rows, rows), F32)
        for level, ql, kl in zip(levels, q_lv, k_lv):
            attn = jnp.where(lev == level, _dot_nt(ql[:, dk], kl[:, dk]), attn)
        o_intra = _dot(attn.astype(BF16), v_bf[:, dv])
        decay_t = jnp.exp(b_last[:, dk]).T
        outs = []
        for s in range(rows // R):
            sl = slice(s * R, (s + 1) * R)
            s0 = s0_ref[s, h]
            outs.append(_dot(qe[sl, dk], s0) + o_intra[sl, :])
            s_ref[s, h] = s0 * decay_t[:, s * R:s * R + 1] + _dot_tn(kd[sl, dk], v[sl, dv])
        o_ref[:, dv] = _gla_out(jnp.concatenate(outs, axis=0), r_ref[:, dv], g_ref[...])


def _gla_sample(ys, glr, wgk, bgk, gnorm, state, n_seq, n_real):
    R = SAMPLE_SEQS * SAMPLE_ROWS
    hk, hv = GLA_HEADS * GLA_DK, GLA_HEADS * GLA_DV
    st_spec = pl.BlockSpec((SAMPLE_SEQS, GLA_HEADS, GLA_DK, GLA_DV), lambda b: (b, 0, 0, 0))
    lev = _gla_pair_levels(R)
    lev = np.where(lev >= SAMPLE_ROWS, -1, lev)
    return pl.pallas_call(
        functools.partial(_gla_sample_kernel, n_real=n_real),
        grid=(n_seq // SAMPLE_SEQS,),
        in_specs=[pl.BlockSpec((R, hk), lambda b: (b, COL_QG // hk)),
                  pl.BlockSpec((R, hk), lambda b: (b, COL_KG // hk)),
                  pl.BlockSpec((R, hv), lambda b: (b, COL_VG // hv)),
                  pl.BlockSpec((R, hv), lambda b: (b, COL_RG // hv)),
                  pl.BlockSpec((R, LANES), lambda b: (b, 0)),
                  pl.BlockSpec((2, LANES, hk), lambda b: (0, 0, 0)),
                  pl.BlockSpec((1, hk), lambda b: (0, 0)),
                  pl.BlockSpec((1, GLA_DV), lambda b: (0, 0)),
                  st_spec,
                  pl.BlockSpec((R, R), lambda b: (0, 0))],
        out_specs=[pl.BlockSpec((R, hv), lambda b: (b, 0)), st_spec],
        out_shape=[jax.ShapeDtypeStruct((n_seq * SAMPLE_ROWS, hv), F32),
                   jax.ShapeDtypeStruct((n_seq, GLA_HEADS, GLA_DK, GLA_DV), F32)],
        compiler_params=_params("parallel"),
        name="gla_sample",
    )(ys, ys, ys, ys, glr, wgk, bgk, gnorm, state, jnp.asarray(lev))


def _lane_lower(shape):
    return lax.broadcasted_iota(jnp.int32, shape, len(shape) - 1) % LANES < HEAD_DIM


def _head_norm_rope(x, g, cos, sin, seg):
    ss = _dot((x * x).astype(BF16), seg)
    y = x * lax.rsqrt(ss * (1.0 / HEAD_DIM) + NORM_EPS) * g
    half = HEAD_DIM // 2
    lane = lax.broadcasted_iota(jnp.int32, y.shape, 1)
    rot = jnp.where(lane % HEAD_DIM < half, pltpu.roll(y, LANES - half, 1), pltpu.roll(y, half, 1))
    return y * cos + rot * sin


def _both_halves(blk, half):
    sw = pltpu.roll(blk, HEAD_DIM, 1)
    lower = _lane_lower(blk.shape)
    return jnp.where(lower, blk, sw) if half == 0 else jnp.where(lower, sw, blk)


def _stack_heads(q_blocks):
    parts = []
    for qb in q_blocks:
        lower = _lane_lower(qb.shape)
        zero = jnp.zeros_like(qb)
        parts += [jnp.where(lower, qb, zero), jnp.where(lower, zero, qb)]
    return jnp.concatenate(parts, axis=0)


def _swa_prompt_kernel(sink_ref, q_ref, k_ref, v_ref, o_ref, ko_ref, vo_ref, kprev_ref, vprev_ref):
    n = pl.program_id(1)
    W = WINDOW

    @pl.when(n == 0)
    def _():
        kprev_ref[...] = jnp.zeros_like(kprev_ref)
        vprev_ref[...] = jnp.zeros_like(vprev_ref)

    k_prev, v_prev = kprev_ref[...], vprev_ref[...]
    for sub in range(SWA_STEP_BLOCKS):
        tok = slice(sub * W, (sub + 1) * W)
        k_cur = k_ref[tok, :].astype(F32)
        v_cur = v_ref[tok, :].astype(F32)
        prev_fill = jnp.where(n > 0, 0.0, -jnp.inf) if sub == 0 else 0.0
        _swa_block(sink_ref, q_ref, o_ref, tok, k_prev, v_prev, k_cur, v_cur, prev_fill)
        k_prev, v_prev = k_cur, v_cur
    ko_ref[...] = k_prev
    vo_ref[...] = v_prev
    kprev_ref[...] = k_prev
    vprev_ref[...] = v_prev


def _swa_block(sink_ref, q_ref, o_ref, tok, k_prev, v_prev, k_cur, v_cur, prev_fill):
    W = WINDOW
    qi = lax.broadcasted_iota(jnp.int32, (W, W), 0)
    ki = lax.broadcasted_iota(jnp.int32, (W, W), 1)
    from_cur = ki <= qi
    for kh in range(SWA_KV_HEADS):
        blk = slice((kh // 2) * LANES, (kh // 2 + 1) * LANES)
        kb_prev = _both_halves(k_prev[:, blk], kh % 2).astype(BF16)
        kb_cur = _both_halves(k_cur[:, blk], kh % 2).astype(BF16)
        vb_prev = _both_halves(v_prev[:, blk], kh % 2).astype(BF16)
        vb_cur = _both_halves(v_cur[:, blk], kh % 2).astype(BF16)
        qblocks = [q_ref[tok, (2 * kh + j) * LANES:(2 * kh + j + 1) * LANES] for j in range(2)]
        qs = _stack_heads(qblocks)
        s_prev = _dot_nt(qs, kb_prev)
        s_cur = _dot_nt(qs, kb_cur)
        outs = []
        for g in range(SWA_GROUP):
            rows = slice(g * W, (g + 1) * W)
            sg = jnp.where(from_cur, s_cur[rows, :], s_prev[rows, :] + prev_fill)
            sink = sink_ref[kh * SWA_GROUP + g]
            m = jnp.maximum(jnp.max(sg, axis=-1, keepdims=True), sink)
            p = jnp.exp(sg - m)
            denom = jnp.sum(p, axis=-1, keepdims=True) + jnp.exp(sink - m)
            p_cur = jnp.where(from_cur, p, 0.0).astype(BF16)
            p_prev = jnp.where(from_cur, 0.0, p).astype(BF16)
            outs.append((_dot(p_prev, vb_prev) + _dot(p_cur, vb_cur)) / denom)
        lower = _lane_lower((W, LANES))
        for j in range(2):
            c0 = (2 * kh + j) * LANES
            o_ref[tok, c0:c0 + LANES] = jnp.where(lower, outs[2 * j], outs[2 * j + 1]).astype(o_ref.dtype)


def _swa_prompt(yp, sinks, batch, seq):
    rows = SWA_STEP_BLOCKS * WINDOW
    nb = seq // rows
    kvw = SWA_KV_HEADS * HEAD_DIM
    tok = lambda b, n: b * nb + n
    cache_spec = pl.BlockSpec((None, WINDOW, kvw), lambda b, n: (b, 0, 0))
    return pl.pallas_call(
        _swa_prompt_kernel,
        grid=(batch, nb),
        in_specs=[pl.BlockSpec(memory_space=pltpu.SMEM),
                  pl.BlockSpec((rows, D_MODEL), lambda b, n: (tok(b, n), COL_QS // D_MODEL)),
                  pl.BlockSpec((rows, kvw), lambda b, n: (tok(b, n), COL_KS // kvw)),
                  pl.BlockSpec((rows, kvw), lambda b, n: (tok(b, n), COL_VS // kvw))],
        out_specs=[pl.BlockSpec((rows, D_MODEL), lambda b, n: (tok(b, n), 0)), cache_spec, cache_spec],
        out_shape=[jax.ShapeDtypeStruct((batch * seq, D_MODEL), BF16),
                   jax.ShapeDtypeStruct((batch, WINDOW, kvw), F32),
                   jax.ShapeDtypeStruct((batch, WINDOW, kvw), F32)],
        scratch_shapes=[pltpu.VMEM((WINDOW, kvw), F32), pltpu.VMEM((WINDOW, kvw), F32)],
        compiler_params=_params("parallel", "arbitrary"),
        name="swa_prompt",
    )(sinks, yp, yp, yp)


def _shift_cache(cache, new, n_real):
    R = SAMPLE_ROWS
    rolled = pltpu.roll(cache, WINDOW - n_real, 0)
    tail_new = pltpu.roll(new, R - n_real, 0)
    row = lax.broadcasted_iota(jnp.int32, (R, cache.shape[1]), 0)
    tail = jnp.where(row < R - n_real, rolled[WINDOW - R:, :], tail_new)
    return jnp.concatenate([rolled[:WINDOW - R, :], tail], axis=0)


def _swa_sample_kernel(q_ref, k_ref, v_ref, kc_ref, vc_ref, sink_ref, o_ref, ko_ref, vo_ref, *, n_real):
    R, W = SAMPLE_ROWS, WINDOW
    k_new = k_ref[...]
    v_new = v_ref[...]
    q_pairs = [q_ref[:, j * LANES:(j + 1) * LANES] for j in range(SWA_HEADS // 2)]

    hr = SWA_HEADS * R
    t_c = lax.broadcasted_iota(jnp.int32, (hr, W), 0) % R
    mask_c = lax.broadcasted_iota(jnp.int32, (hr, W), 1) > t_c
    t_n = lax.broadcasted_iota(jnp.int32, (hr, R), 0) % R
    mask_n = lax.broadcasted_iota(jnp.int32, (hr, R), 1) <= t_n
    sink = sink_ref[...]
    lower = _lane_lower((R, LANES))
    zeros = jnp.zeros((R, LANES), F32)
    for s in range(q_ref.shape[0] // R):
        sl = slice(s * R, (s + 1) * R)
        kc, vc = kc_ref[s], vc_ref[s]
        kn, vn = k_new[sl, :], v_new[sl, :]
        ko_ref[s] = _shift_cache(kc, kn, n_real)
        vo_ref[s] = _shift_cache(vc, vn, n_real)
        q_rows = []
        for h in range(SWA_HEADS):
            kh = h // SWA_GROUP
            x = q_pairs[h // 2][sl, :]
            if h % 2 != kh % 2:
                x = pltpu.roll(x, HEAD_DIM, 1)
            x = jnp.where(lower, x, zeros) if kh % 2 == 0 else jnp.where(lower, zeros, x)
            q_rows.append(jnp.concatenate([x, zeros] if kh // 2 == 0 else [zeros, x], axis=1))
        qbd = jnp.concatenate(q_rows, axis=0)
        sc = jnp.where(mask_c, _dot_nt(qbd.astype(BF16), kc.astype(BF16)), -jnp.inf)
        sn = jnp.where(mask_n, _dot_nt(qbd, kn), -jnp.inf)
        m = jnp.maximum(jnp.maximum(jnp.max(sc, axis=-1, keepdims=True),
                                    jnp.max(sn, axis=-1, keepdims=True)), sink)
        pc, pn = jnp.exp(sc - m), jnp.exp(sn - m)
        denom = (jnp.sum(pc, axis=-1, keepdims=True) + jnp.sum(pn, axis=-1, keepdims=True)
                 + jnp.exp(sink - m))
        o = (_dot(pc.astype(BF16), vc.astype(BF16)) + _dot(pn, vn)) / denom
        for j in range(SWA_HEADS // 2):
            halves = []
            for h in (2 * j, 2 * j + 1):
                kh = h // SWA_GROUP
                y = o[h * R:(h + 1) * R, (kh // 2) * LANES:(kh // 2 + 1) * LANES]
                halves.append(pltpu.roll(y, HEAD_DIM, 1) if h % 2 != kh % 2 else y)
            o_ref[sl, j * LANES:(j + 1) * LANES] = jnp.where(lower, halves[0], halves[1])


def _swa_sample(ys, sink_rows, kcache, vcache, n_seq, n_real):
    R = SAMPLE_SEQS * SAMPLE_ROWS
    kvw = SWA_KV_HEADS * HEAD_DIM
    const = lambda b: (0, 0)
    cache_spec = pl.BlockSpec((SAMPLE_SEQS, WINDOW, kvw), lambda b: (b, 0, 0))
    return pl.pallas_call(
        functools.partial(_swa_sample_kernel, n_real=n_real),
        grid=(n_seq // SAMPLE_SEQS,),
        in_specs=[pl.BlockSpec((R, D_MODEL), lambda b: (b, COL_QS // D_MODEL)),
                  pl.BlockSpec((R, kvw), lambda b: (b, COL_KS // kvw)),
                  pl.BlockSpec((R, kvw), lambda b: (b, COL_VS // kvw)),
                  cache_spec, cache_spec,
                  pl.BlockSpec((SWA_HEADS * SAMPLE_ROWS, 1), const)],
        out_specs=[pl.BlockSpec((R, D_MODEL), lambda b: (b, 0)), cache_spec, cache_spec],
        out_shape=[jax.ShapeDtypeStruct((n_seq * SAMPLE_ROWS, D_MODEL), F32),
                   jax.ShapeDtypeStruct((n_seq, WINDOW, kvw), F32),
                   jax.ShapeDtypeStruct((n_seq, WINDOW, kvw), F32)],
        compiler_params=_params("parallel"),
        name="swa_sample",
    )(ys, ys, ys, kcache, vcache, sink_rows)


def _post_kernel(ag_ref, as_ref, og_ref, os_ref, x_ref, gate_ref, shift_ref, scale_ref, g2_ref,
                 wo_ref, wr_ref, br_ref, x1_ref, h2_ref, slot_ref, gatek_ref, cnt_ref, *, n_valid):
    merged = (ag_ref[...].astype(F32) * og_ref[...].astype(F32)
              + as_ref[...].astype(F32) * os_ref[...].astype(F32))
    y = _dot(merged.astype(BF16), wo_ref[...])
    rows = x_ref.shape[0]
    x1 = x_ref[...] + _mod_rows(gate_ref, rows) * y
    x1_ref[...] = x1
    h2 = _rms(x1, g2_ref[...]) * (1.0 + _mod_rows(scale_ref, rows)) + _mod_rows(shift_ref, rows)
    h2_hi = h2.astype(BF16)
    h2_ref[...] = h2_hi

    h2_lo = (h2 - h2_hi.astype(F32)).astype(BF16)
    w_hi, w_lo = wr_ref[0], wr_ref[1]
    logits = _dot(h2_hi, w_hi) + (_dot(h2_lo, w_hi) + _dot(h2_hi, w_lo)) + br_ref[...]
    lane_i = lax.broadcasted_iota(jnp.int32, logits.shape, 1)
    lane = lane_i.astype(F32)
    work = logits
    vals, hots = [], []
    for _ in range(TOP_K):
        m = jnp.max(work, axis=-1, keepdims=True)
        idx = jnp.min(jnp.where(work == m, lane, float(LANES)), axis=-1, keepdims=True)
        hot = lane == idx
        vals.append(m)
        hots.append(hot)
        work = jnp.where(hot, -jnp.inf, work)
    exps = [jnp.exp(v - vals[0]) for v in vals]
    denom = exps[0] + exps[1] + exps[2] + exps[3]

    tm = logits.shape[0]
    valid = lax.broadcasted_iota(jnp.int32, (tm, 1), 0) % SAMPLE_ROWS < n_valid
    sel = jnp.zeros_like(logits)
    for hot in hots:
        sel = jnp.where(hot, 1.0, sel)
    sel = jnp.where(valid, sel, 0.0)
    earlier = (lax.broadcasted_iota(jnp.int32, (tm, tm), 1)
               < lax.broadcasted_iota(jnp.int32, (tm, tm), 0))
    rank = _dot(jnp.where(earlier, 1.0, 0.0).astype(BF16), sel.astype(BF16))
    cnt = jnp.sum(sel, axis=0, keepdims=True)
    cnt_pad = jnp.floor((cnt + (ROW_UNIT - 1.0)) * (1.0 / ROW_UNIT)) * ROW_UNIT
    below = (lax.broadcasted_iota(jnp.int32, (LANES, LANES), 0)
             < lax.broadcasted_iota(jnp.int32, (LANES, LANES), 1))
    seg_start = _dot(jnp.broadcast_to(cnt_pad, (8, LANES)), jnp.where(below, 1.0, 0.0), HIGHEST)[0:1]
    pos = seg_start + rank
    slots = jnp.full_like(logits, -1.0)
    gates = jnp.zeros_like(logits)
    for k in range(TOP_K):
        s_k = jnp.sum(jnp.where(hots[k], pos, 0.0), axis=-1, keepdims=True)
        slots = jnp.where(lane_i == k, s_k, slots)
        gates = jnp.where(lane_i == k, exps[k] / denom, gates)
    slot_ref[...] = jnp.where(valid, slots, -1.0)
    gatek_ref[...] = gates
    cnt_ref[...] = cnt


def _post(y_all, o_gla, o_swa, x, gate, shift, scale, g2, wo, wr, br, per_token, tiles_per_seq, n_valid):
    n = x.shape[0]
    mod = _mod_spec(per_token, tiles_per_seq)
    row = lambda i: (i, 0)
    const = lambda i: (0, 0)
    wide = pl.BlockSpec((TOKEN_TILE, D_MODEL), row)
    narrow = pl.BlockSpec((TOKEN_TILE, LANES), row)
    return pl.pallas_call(
        functools.partial(_post_kernel, n_valid=n_valid),
        grid=(n // TOKEN_TILE,),
        in_specs=[pl.BlockSpec((TOKEN_TILE, D_MODEL), lambda i: (i, COL_AG // D_MODEL)),
                  pl.BlockSpec((TOKEN_TILE, D_MODEL), lambda i: (i, COL_AS // D_MODEL)),
                  wide, wide, wide, mod, mod, mod,
                  pl.BlockSpec((1, D_MODEL), const),
                  pl.BlockSpec((D_MODEL, D_MODEL), const),
                  pl.BlockSpec((2, D_MODEL, LANES), lambda i: (0, 0, 0)),
                  pl.BlockSpec((1, LANES), const)],
        out_specs=[wide, wide, narrow, narrow, pl.BlockSpec((None, 1, LANES), lambda i: (i, 0, 0))],
        out_shape=[jax.ShapeDtypeStruct((n, D_MODEL), F32),
                   jax.ShapeDtypeStruct((n, D_MODEL), BF16),
                   jax.ShapeDtypeStruct((n, LANES), F32),
                   jax.ShapeDtypeStruct((n, LANES), F32),
                   jax.ShapeDtypeStruct((n // TOKEN_TILE, 1, LANES), F32)],
        compiler_params=_params("parallel"),
        name="post",
    )(y_all, y_all, o_gla, o_swa, x, gate, shift, scale, g2, wo, wr, br)


def _slot_matrix(slot_cols, weights, chunk):
    tm = slot_cols[0].shape[0]
    j = lax.broadcasted_iota(jnp.int32, (tm, tm), 1) + chunk * tm
    out = jnp.zeros((tm, tm), F32)
    for s, w in zip(slot_cols, weights):
        out = jnp.where(s == j, w, out)
    return out.astype(BF16)


def _dispatch_kernel(nu_ref, seg_ref, ntail_ref, tail_ref, hp_ref, slp_ref, hs_ref, sls_ref, xg_ref,
                     sorted_ref, zero_ref, sem, tail_sem, *, prompt_tiles):
    t = pl.program_id(0)
    last = pl.num_programs(0) - 1
    buf = t % 2

    def sort_tile(h_ref, slot_ref):
        tm = h_ref.shape[0]
        slots = slot_ref[...].astype(jnp.int32)
        slot_cols = [slots[:, k:k + 1] for k in range(TOP_K)]
        h = h_ref[...]
        for c in range(LOCAL_ROWS // tm):
            onehot = _slot_matrix(slot_cols, [1.0] * TOP_K, c)
            upc = tm // ROW_UNIT
            sorted_ref[buf, c * upc:(c + 1) * upc] = _dot_tn(onehot, h).astype(BF16).reshape(
                upc, ROW_UNIT, D_MODEL)

    @pl.when(t < prompt_tiles)
    def _():
        sort_tile(hp_ref, slp_ref)

    @pl.when(t >= prompt_tiles)
    def _():
        sort_tile(hs_ref, sls_ref)

    def start_all(tile, b):
        def body(e, c):
            n = seg_ref[tile, 0, e]

            @pl.when(n > 0)
            def _():
                pltpu.make_async_copy(sorted_ref.at[b, pl.ds(seg_ref[tile, 1, e], n)],
                                      xg_ref.at[pl.ds(seg_ref[tile, 2, e], n)], sem.at[b]).start()
            return c

        lax.fori_loop(0, N_EXPERTS, body, 0)

    def wait_all(tile, b):
        n_units = nu_ref[tile]

        @pl.when(n_units > 0)
        def _():
            pltpu.make_async_copy(sorted_ref.at[b, pl.ds(0, n_units)],
                                  xg_ref.at[pl.ds(0, n_units)], sem.at[b]).wait()

    @pl.when(t > 0)
    def _():
        wait_all(t - 1, 1 - buf)

    start_all(t, buf)

    def tail_copy(i):
        return pltpu.make_async_copy(zero_ref, xg_ref.at[tail_ref[i]], tail_sem)

    @pl.when(t == last)
    def _():
        zero_ref[...] = jnp.zeros_like(zero_ref)
        n_tail = ntail_ref[0]
        lax.fori_loop(0, n_tail, lambda i, c: (tail_copy(i).start(), c)[1], 0)
        lax.fori_loop(0, n_tail, lambda i, c: (tail_copy(i).wait(), c)[1], 0)
        wait_all(t, buf)


def _dispatch(h2_p, slots_p, h2_s, slots_s, n_units, seg, n_tail, tail_dst, rows_max):
    p_tiles = h2_p.shape[0] // TOKEN_TILE
    s_tiles = h2_s.shape[0] // TOKEN_TILE
    p_row = lambda t, *_: (jnp.minimum(t, p_tiles - 1), 0)
    s_row = lambda t, *_: (jnp.maximum(t - p_tiles, 0), 0)
    return pl.pallas_call(
        functools.partial(_dispatch_kernel, prompt_tiles=p_tiles),
        grid_spec=pltpu.PrefetchScalarGridSpec(
            num_scalar_prefetch=4,
            grid=(p_tiles + s_tiles,),
            in_specs=[pl.BlockSpec((TOKEN_TILE, D_MODEL), p_row),
                      pl.BlockSpec((TOKEN_TILE, LANES), p_row),
                      pl.BlockSpec((TOKEN_TILE, D_MODEL), s_row),
                      pl.BlockSpec((TOKEN_TILE, LANES), s_row)],
            out_specs=pl.BlockSpec(memory_space=pl.ANY),
            scratch_shapes=[pltpu.VMEM((2, LOCAL_ROWS // ROW_UNIT, ROW_UNIT, D_MODEL), BF16),
                            pltpu.VMEM((ROW_UNIT, D_MODEL), BF16),
                            pltpu.SemaphoreType.DMA((2,)), pltpu.SemaphoreType.DMA]),
        out_shape=jax.ShapeDtypeStruct((rows_max // ROW_UNIT, ROW_UNIT, D_MODEL), BF16),
        compiler_params=_params("arbitrary"),
        name="dispatch",
    )(n_units, seg, n_tail, tail_dst, h2_p, slots_p, h2_s, slots_s).reshape(rows_max, D_MODEL)


def _expert_kernel(tg_ref, ge_ref, ng_ref, nused_ref, rows_ref, x_ref, bgu_ref, bd_ref, wgu_hbm, wd_hbm,
                   y_ref, wgu_f32, wd_f32, wgu_bf, wd_bf, sem):
    i = pl.program_id(0)

    def fetch(g, b):
        e = ge_ref[g]
        return (pltpu.make_async_copy(wgu_hbm.at[e], wgu_f32.at[b], sem.at[0, b]),
                pltpu.make_async_copy(wd_hbm.at[e], wd_f32.at[b], sem.at[1, b]))

    @pl.when(i == 0)
    def _():
        for cp in fetch(0, 0):
            cp.start()

    @pl.when(i < nused_ref[0])
    def _():
        g = tg_ref[i]
        b = g % 2

        @pl.when((i == 0) | (g != tg_ref[jnp.maximum(i - 1, 0)]))
        def _():
            @pl.when(g + 1 < ng_ref[0])
            def _():
                for cp in fetch(g + 1, 1 - b):
                    cp.start()

            for cp in fetch(g, b):
                cp.wait()
            wgu_bf[...] = wgu_f32[b].astype(BF16)
            wd_bf[...] = wd_f32[b].astype(BF16)

        for part in range(EXPERT_TILE // EXPERT_PART):
            @pl.when(rows_ref[i] > part * EXPERT_PART)
            def _():
                sl = slice(part * EXPERT_PART, (part + 1) * EXPERT_PART)
                gu = _dot(x_ref[sl, :], wgu_bf[...]) + bgu_ref[...]
                gate = jnp.minimum(gu[:, :D_FF], SWIGLU_LIMIT)
                up = jnp.clip(gu[:, D_FF:], -SWIGLU_LIMIT, SWIGLU_LIMIT)
                act = (up + 1.0) * gate * jax.nn.sigmoid(SWIGLU_ALPHA * gate)
                y_ref[sl, :] = (_dot(act.astype(BF16), wd_bf[...]) + bd_ref[...]).astype(y_ref.dtype)


def _experts(xg, tile_group, group_expert, n_groups, n_used, tile_rows, wgu, bgu, wd, bd):
    rows_max = xg.shape[0]
    used = lambda i, nu: jnp.maximum(jnp.minimum(i, nu[0] - 1), 0)
    row = lambda i, tg, ge, ng, nu, tr: (used(i, nu), 0)
    exp = lambda i, tg, ge, ng, nu, tr: (ge[tg[used(i, nu)]], 0, 0)
    return pl.pallas_call(
        _expert_kernel,
        grid_spec=pltpu.PrefetchScalarGridSpec(
            num_scalar_prefetch=5,
            grid=(rows_max // EXPERT_TILE,),
            in_specs=[pl.BlockSpec((EXPERT_TILE, D_MODEL), row),
                      pl.BlockSpec((None, 1, 2 * D_FF), exp),
                      pl.BlockSpec((None, 1, D_MODEL), exp),
                      pl.BlockSpec(memory_space=pl.ANY),
                      pl.BlockSpec(memory_space=pl.ANY)],
            out_specs=pl.BlockSpec((EXPERT_TILE, D_MODEL), row),
            scratch_shapes=[pltpu.VMEM((2, D_MODEL, 2 * D_FF), F32), pltpu.VMEM((2, D_FF, D_MODEL), F32),
                            pltpu.VMEM((D_MODEL, 2 * D_FF), BF16), pltpu.VMEM((D_FF, D_MODEL), BF16),
                            pltpu.SemaphoreType.DMA((2, 2))]),
        out_shape=jax.ShapeDtypeStruct((rows_max, D_MODEL), BF16),
        compiler_params=_params("arbitrary"),
        name="experts",
    )(tile_group, group_expert, n_groups, n_used, tile_rows, xg, bgu, bd, wgu, wd)


def _combine_kernel(nu_ref, seg_ref, slot_ref, gatek_ref, x_ref, gmlp_ref, y_ref, o_ref, ys_ref, sem,
                    *, tile_offset):
    j = pl.program_id(0)
    t = j + tile_offset
    buf = j % 2
    tm = x_ref.shape[0]

    def fetch(tile, b):
        def body(e, c):
            n = seg_ref[tile, 0, e]

            @pl.when(n > 0)
            def _():
                pltpu.make_async_copy(y_ref.at[pl.ds(seg_ref[tile, 2, e], n)],
                                      ys_ref.at[b, pl.ds(seg_ref[tile, 1, e], n)], sem.at[b]).start()
            return c

        lax.fori_loop(0, N_EXPERTS, body, 0)
        n_units = nu_ref[tile]

        def zero_unit(i, c):
            ys_ref[b, i] = jnp.zeros((ROW_UNIT, D_MODEL), ys_ref.dtype)
            return c

        lax.fori_loop(n_units, LOCAL_ROWS // ROW_UNIT, zero_unit, 0)

    @pl.when(j == 0)
    def _():
        fetch(t, buf)

    @pl.when(j + 1 < pl.num_programs(0))
    def _():
        fetch(t + 1, 1 - buf)

    n_units = nu_ref[t]

    @pl.when(n_units > 0)
    def _():
        pltpu.make_async_copy(y_ref.at[pl.ds(0, n_units)], ys_ref.at[buf, pl.ds(0, n_units)],
                              sem.at[buf]).wait()

    slots = slot_ref[...].astype(jnp.int32)
    gates = gatek_ref[...]
    slot_cols = [slots[:, k:k + 1] for k in range(TOP_K)]
    gate_cols = [gates[:, k:k + 1] for k in range(TOP_K)]
    acc = jnp.zeros((tm, D_MODEL), F32)
    for c in range(LOCAL_ROWS // tm):
        upc = tm // ROW_UNIT
        rows_c = ys_ref[buf, c * upc:(c + 1) * upc].reshape(tm, D_MODEL)
        acc = acc + _dot(_slot_matrix(slot_cols, gate_cols, c), rows_c)
    o_ref[...] = x_ref[...] + _mod_rows(gmlp_ref, tm) * acc


def _combine(y, slots, gates, x1, gmlp, n_units, seg, tile_offset, per_token, tiles_per_seq):
    n = x1.shape[0]
    mod = _mod_spec(per_token, tiles_per_seq)
    wide = pl.BlockSpec((TOKEN_TILE, D_MODEL), lambda i, *_: (i, 0))
    narrow = pl.BlockSpec((TOKEN_TILE, LANES), lambda i, *_: (i, 0))
    return pl.pallas_call(
        functools.partial(_combine_kernel, tile_offset=tile_offset),
        grid_spec=pltpu.PrefetchScalarGridSpec(
            num_scalar_prefetch=2,
            grid=(n // TOKEN_TILE,),
            in_specs=[narrow, narrow, wide, mod, pl.BlockSpec(memory_space=pl.ANY)],
            out_specs=wide,
            scratch_shapes=[pltpu.VMEM((2, LOCAL_ROWS // ROW_UNIT, ROW_UNIT, D_MODEL), BF16),
                            pltpu.SemaphoreType.DMA((2,))]),
        out_shape=jax.ShapeDtypeStruct((n, D_MODEL), F32),
        compiler_params=_params("arbitrary"),
        name="combine",
    )(n_units, seg, slots, gates, x1, gmlp, y.reshape(-1, ROW_UNIT, D_MODEL))


def _route_tables(cnt, rows_max):
    units = (cnt + ROW_UNIT - 1) // ROW_UNIT
    group_units = jnp.sum(units, axis=0)
    upt = EXPERT_TILE // ROW_UNIT
    group_pad = (group_units + upt - 1) // upt * upt
    group_end = jnp.cumsum(group_pad)
    group_start = group_end - group_pad
    seg_start = group_start[None, :] + jnp.cumsum(units, axis=0) - units
    local_end = jnp.cumsum(units, axis=1)
    local_start = local_end - units
    n_units = local_end[:, -1].astype(jnp.int32)

    def pick(lo, hi, pos, value):
        return jnp.sum(jnp.where((pos >= lo) & (pos < hi), value, 0), axis=-1).astype(jnp.int32)

    seg = jnp.stack([units, local_start, seg_start], axis=1).astype(jnp.int32)

    n_tail_e = group_pad - group_units
    j = jnp.arange(N_EXPERTS * upt, dtype=jnp.int32)[:, None]
    tail_end = jnp.cumsum(n_tail_e)
    tail_start = tail_end - n_tail_e
    tail_hbm = pick(tail_start[None, :], tail_end[None, :], j,
                    (group_start + group_units)[None, :] + j - tail_start[None, :])
    n_tail = tail_end[-1:].astype(jnp.int32)

    r = jnp.arange(rows_max // EXPERT_TILE, dtype=jnp.int32)[:, None] * upt
    n_used = (group_end[-1:] // upt).astype(jnp.int32)
    nonempty = group_units > 0
    group_of_expert = jnp.cumsum(nonempty) - 1
    tile_rows = pick(group_start[None, :], group_end[None, :], r,
                     jnp.clip(((group_start + group_units)[None, :] - r) * ROW_UNIT, 0, EXPERT_TILE))
    tile_group = pick(group_start[None, :], group_end[None, :], r, group_of_expert[None, :])
    g = jnp.arange(N_EXPERTS, dtype=jnp.int32)
    group_expert = jnp.sum(jnp.where(nonempty[None, :] & (group_of_expert[None, :] == g[:, None]),
                                     g[None, :], 0), axis=-1).astype(jnp.int32)
    n_groups = jnp.sum(nonempty)[None].astype(jnp.int32)
    return (n_units, seg, n_tail, tail_hbm,
            tile_group, group_expert, n_groups, n_used, tile_rows)


def _rope_tables(pos):
    half = HEAD_DIM // 2
    lane = np.arange(LANES)
    inv = jnp.asarray(ROPE_THETA, F32) ** (-jnp.asarray(lane % half, F32) / half)
    sign = jnp.asarray(np.where(lane % HEAD_DIM < half, -1.0, 1.0), F32)
    ang = pos.astype(F32)[:, None] * inv[None, :]
    return jnp.cos(ang), jnp.sin(ang) * sign[None, :]


def kernel(x_prompt, x_sample, c_prompt, c_sample, state_gla, cache_swa_k, cache_swa_v, w_ada, b_ada,
           norm1_g, norm2_g, w_in, w_gk2, b_gk, gla_norm_g, q_norm_g, k_norm_g, attn_sinks, w_o,
           w_router, b_router, w_gate_up, b_gate_up, w_down, b_down):
    batch, seq, d = x_prompt.shape
    n_seq, n_real, _ = x_sample.shape
    depth = w_in.shape[0]
    assert depth == 1 and d == D_MODEL and n_real <= SAMPLE_ROWS
    assert seq % TOKEN_TILE == 0 and (n_seq * SAMPLE_ROWS) % TOKEN_TILE == 0 and n_seq % SAMPLE_SEQS == 0
    R = SAMPLE_ROWS
    kvw = SWA_KV_HEADS * HEAD_DIM
    tiles_per_seq = seq // TOKEN_TILE

    qg, kg, vg, rg, glr_w, qs, ks, vs, ag, as_ = jnp.split(
        w_in[0], [512, 1024, 2048, 3072, 3088, 4112, 4368, 4624, 5648], axis=1)
    w_main = jnp.concatenate([vg, rg, qs, ag, as_, qg, kg, ks, vs], axis=1).astype(BF16)
    w_glr = jnp.pad(glr_w, ((0, 0), (0, LANES - GATE_RANK))).astype(BF16)
    wgk = jnp.pad(w_gk2[0], ((0, LANES - GATE_RANK), (0, 0)))
    wgk_hi = wgk.astype(BF16)
    wgk = jnp.stack([wgk_hi, (wgk - wgk_hi.astype(F32)).astype(BF16)])
    bgk = b_gk[0].reshape(1, -1)
    gnorm = gla_norm_g[0].reshape(1, -1)
    gq = jnp.tile(q_norm_g[0], LANES // HEAD_DIM).reshape(1, LANES)
    gk = jnp.tile(k_norm_g[0], LANES // HEAD_DIM).reshape(1, LANES)
    seg = jnp.asarray(np.kron(np.eye(LANES // HEAD_DIM), np.ones((HEAD_DIM, HEAD_DIM))), BF16)
    sinks = attn_sinks[0]
    wo = w_o[0].astype(BF16)
    wr = jnp.pad(w_router[0], ((0, 0), (0, LANES - N_EXPERTS)))
    wr_hi = wr.astype(BF16)
    wr = jnp.stack([wr_hi, (wr - wr_hi.astype(F32)).astype(BF16)])
    br = jnp.pad(b_router[0], (0, LANES - N_EXPERTS), constant_values=-1e30).reshape(1, LANES)
    bgu = b_gate_up[0].reshape(N_EXPERTS, 1, 2 * D_FF)
    bd = b_down[0].reshape(N_EXPERTS, 1, D_MODEL)
    g1 = norm1_g[0].reshape(1, -1)
    g2 = norm2_g[0].reshape(1, -1)

    n_c = batch + n_seq
    c_all = jnp.pad(jnp.concatenate([c_prompt, c_sample], axis=0), ((0, -n_c % 8), (0, 0)))
    m_all = _ada(c_all, w_ada[0], b_ada[0])
    mp = [m_all[:batch, i * d:(i + 1) * d].reshape(batch, 1, d) for i in range(6)]
    ms = [m_all[batch:n_c, i * d:(i + 1) * d] for i in range(6)]

    xp = x_prompt.reshape(batch * seq, d)
    xs = jnp.pad(x_sample, ((0, 0), (0, R - n_real), (0, 0))).reshape(n_seq * R, d)
    cos_p, sin_p = _rope_tables(jnp.arange(seq))
    cos_s, sin_s = _rope_tables(PAST_LEN + jnp.tile(jnp.arange(R), n_seq))
    sink_rows = jnp.repeat(sinks, R).reshape(SWA_HEADS * R, 1)

    yp, glr_p = _inproj(xp, mp[0], mp[1], g1, w_main, w_glr, cos_p, sin_p, gq, gk, seg,
                        BF16, False, tiles_per_seq)
    ys, glr_s = _inproj(xs, ms[0], ms[1], g1, w_main, w_glr, cos_s, sin_s, gq, gk, seg, F32, True, 1)
    og_p, st_p = _gla_prompt(yp, glr_p, wgk, bgk, gnorm, batch, seq)
    og_s, st_s = _gla_sample(ys, glr_s, wgk, bgk, gnorm, state_gla[0], n_seq, n_real)
    os_p, kc_p, vc_p = _swa_prompt(yp, sinks, batch, seq)
    os_s, kc_s, vc_s = _swa_sample(ys, sink_rows, cache_swa_k[0].reshape(n_seq, WINDOW, kvw),
                                   cache_swa_v[0].reshape(n_seq, WINDOW, kvw), n_seq, n_real)
    x1_p, h2_p, sl_p, gt_p, cnt_p = _post(yp, og_p, os_p, xp, mp[2], mp[3], mp[4], g2, wo, wr, br,
                                          False, tiles_per_seq, R)
    x1_s, h2_s, sl_s, gt_s, cnt_s = _post(ys, og_s, os_s, xs, ms[2], ms[3], ms[4], g2, wo, wr, br,
                                          True, 1, n_real)

    cnt = jnp.concatenate([cnt_p, cnt_s], axis=0)[:, 0, :N_EXPERTS].astype(jnp.int32)
    n_tiles = cnt.shape[0]
    rows_bound = (TOP_K * (batch * seq + n_seq * n_real) + n_tiles * N_EXPERTS * (ROW_UNIT - 1)
                  + N_EXPERTS * (EXPERT_TILE - 1))
    rows_max = -(-rows_bound // EXPERT_TILE) * EXPERT_TILE
    (n_units, seg, n_tail, tail_hbm,
     tile_group, group_expert, n_groups, n_used, tile_rows) = _route_tables(cnt, rows_max)
    xg = _dispatch(h2_p, sl_p, h2_s, sl_s, n_units, seg, n_tail, tail_hbm, rows_max)
    yg = _experts(xg, tile_group, group_expert, n_groups, n_used, tile_rows,
                  w_gate_up[0], bgu, w_down[0], bd)
    p_tiles = batch * seq // TOKEN_TILE
    out_p = _combine(yg, sl_p, gt_p, x1_p, mp[5], n_units, seg, 0, False, tiles_per_seq)
    out_s = _combine(yg, sl_s, gt_s, x1_s, ms[5], n_units, seg, p_tiles, True, 1)

    cache_shape = (WINDOW, SWA_KV_HEADS, HEAD_DIM)
    return (out_p.reshape(batch, seq, d),
            out_s.reshape(n_seq, R, d)[:, :n_real],
            st_p[None],
            kc_p.reshape(1, batch, *cache_shape),
            vc_p.reshape(1, batch, *cache_shape),
            st_s[None],
            kc_s.reshape(1, n_seq, *cache_shape),
            vc_s.reshape(1, n_seq, *cache_shape))
```

```python
import functools

import numpy as np
import jax
import jax.numpy as jnp
from jax import lax
from jax.experimental import pallas as pl
from jax.experimental.pallas import tpu as pltpu

F32 = jnp.float32
BF16 = jnp.bfloat16
HIGHEST = lax.Precision.HIGHEST

D_MODEL = 1024
PAST_LEN = 16384
GLA_HEADS = 4
GLA_DK = 128
GLA_DV = 256
GATE_RANK = 16
GATE_TAU = 16.0
SWA_HEADS = 16
SWA_KV_HEADS = 4
HEAD_DIM = 64
SWA_GROUP = SWA_HEADS // SWA_KV_HEADS
WINDOW = 128
ROPE_THETA = 10000.0
N_EXPERTS = 32
TOP_K = 4
D_FF = 1024
SWIGLU_ALPHA = 1.702
SWIGLU_LIMIT = 7.0
NORM_EPS = 1e-6

LANES = 128
SUBLANES = 8
SAMPLE_ROWS = 8
SAMPLE_SEQS = 16
TOKEN_TILE = 512
SWA_STEP_BLOCKS = 4
GLA_BLOCK = 256
GLA_MASK_BLOCK = 128
VMEM_LIMIT = 56 * 1024 * 1024
ROW_UNIT = 16
SLOT_CHUNK = 512
EXPERT_TILE = 1024
EXPERT_PART = 256
LOCAL_ROWS = -(-(TOP_K * TOKEN_TILE + N_EXPERTS * (ROW_UNIT - 1)) // TOKEN_TILE) * TOKEN_TILE

COL_QG, COL_KG, COL_VG, COL_RG = 0, 512, 1024, 2048
COL_QS, COL_KS, COL_VS, COL_AG, COL_AS = 3072, 4096, 4352, 4608, 5632
D_MAIN = 6656
PROJ_CHUNK = 512
PROJ_AHEAD = 2


def _dot(a, b, precision=None):
    return jnp.dot(a, b, preferred_element_type=F32, precision=precision)


def _dot_nt(a, b, precision=None):
    return lax.dot_general(a, b, (((1,), (1,)), ((), ())), preferred_element_type=F32, precision=precision)


def _dot_tn(a, b, precision=None):
    return lax.dot_general(a, b, (((0,), (0,)), ((), ())), preferred_element_type=F32, precision=precision)


def _dot_sum(sel, x):
    hi = x.astype(BF16)
    r1 = x - hi.astype(F32)
    mid = r1.astype(BF16)
    lo = (r1 - mid.astype(F32)).astype(BF16)
    return _dot(sel, hi) + _dot(sel, mid) + _dot(sel, lo)


def _params(*sem):
    return pltpu.CompilerParams(dimension_semantics=sem, vmem_limit_bytes=VMEM_LIMIT)


def _rms(x, g):
    return x * lax.rsqrt(jnp.mean(x * x, axis=-1, keepdims=True) + NORM_EPS) * g


def _log_sigmoid(x):
    return jnp.minimum(x, 0.0) - jnp.log(1.0 + jnp.exp(-jnp.abs(x)))


def _ada_kernel(c_ref, w_ref, b_ref, o_ref):
    c = c_ref[...]
    s = c * jax.nn.sigmoid(c)
    s_hi = s.astype(BF16)
    s_lo = (s - s_hi.astype(F32)).astype(BF16)
    w = w_ref[...]
    w_hi = w.astype(BF16)
    w_lo = (w - w_hi.astype(F32)).astype(BF16)
    o_ref[...] = _dot(s_hi, w_hi) + (_dot(s_lo, w_hi) + _dot(s_hi, w_lo)) + b_ref[...]


def _ada(c_all, w_ada, b_ada):
    rows = c_all.shape[0]
    tn = 768
    return pl.pallas_call(
        _ada_kernel,
        grid=(6 * D_MODEL // tn,),
        in_specs=[pl.BlockSpec((rows, D_MODEL), lambda j: (0, 0)),
                  pl.BlockSpec((D_MODEL, tn), lambda j: (0, j)),
                  pl.BlockSpec((1, tn), lambda j: (0, j))],
        out_specs=pl.BlockSpec((rows, tn), lambda j: (0, j)),
        out_shape=jax.ShapeDtypeStruct((rows, 6 * D_MODEL), F32),
        compiler_params=_params("parallel"),
        name="ada",
    )(c_all, w_ada, b_ada.reshape(1, -1))


def _inproj_kernel(x_ref, shift_ref, scale_ref, g_ref, w_ref, wg_ref, cos_ref, sin_ref, gq_ref, gk_ref,
                   seg_ref, o_ref, og_ref):
    rows = x_ref.shape[0]
    h = _rms(x_ref[...], g_ref[...]) * (1.0 + _mod_rows(scale_ref, rows)) + _mod_rows(shift_ref, rows)
    hb = h.astype(BF16)
    cos, sin, seg = cos_ref[...], sin_ref[...], seg_ref[...]
    def followup(c0):
        if COL_QS <= c0 < COL_QS + SWA_HEADS * HEAD_DIM:
            return "query"
        if COL_KS <= c0 < COL_KS + SWA_KV_HEADS * HEAD_DIM:
            return "key"
        if COL_AG <= c0 < COL_AG + D_MODEL or COL_AS <= c0 < COL_AS + D_MODEL:
            return "gate"
        return None

    def has_followup(j):
        return any(followup(c0) for c0 in range(j * PROJ_CHUNK, (j + 1) * PROJ_CHUNK, LANES))

    chunks = range(D_MAIN // PROJ_CHUNK)
    busy = [j for j in chunks if has_followup(j)]
    plain = [j for j in chunks if not has_followup(j)]
    order = [j for pair in zip(busy, plain) for j in pair] + busy[len(plain):] + plain[len(busy):]
    project = lambda j: _dot(hb, w_ref[:, j * PROJ_CHUNK:(j + 1) * PROJ_CHUNK])
    ahead = [project(j) for j in order[:PROJ_AHEAD]]
    for idx, j in enumerate(order):
        sl = slice(j * PROJ_CHUNK, (j + 1) * PROJ_CHUNK)
        r = ahead.pop(0)
        if idx + PROJ_AHEAD < len(order):
            ahead.append(project(order[idx + PROJ_AHEAD]))
        blocks = []
        for m in range(PROJ_CHUNK // LANES):
            c0 = j * PROJ_CHUNK + m * LANES
            blk = r[:, m * LANES:(m + 1) * LANES]
            kind = followup(c0)
            if kind == "query":
                blk = _head_norm_rope(blk, gq_ref[...], cos, sin, seg) * (HEAD_DIM ** -0.5)
            elif kind == "key":
                blk = _head_norm_rope(blk, gk_ref[...], cos, sin, seg)
            elif kind == "gate":
                blk = jax.nn.sigmoid(blk)
            blocks.append(blk)
        o_ref[:, sl] = jnp.concatenate(blocks, axis=1).astype(o_ref.dtype)
    og_ref[...] = _dot(hb, wg_ref[...])


def _mod_spec(per_token, tiles_per_seq):
    if per_token:
        return pl.BlockSpec((TOKEN_TILE // SAMPLE_ROWS, D_MODEL), lambda i, *_: (i, 0))
    return pl.BlockSpec((None, 1, D_MODEL), lambda i, *_: (i // tiles_per_seq, 0, 0))


def _mod_rows(ref, rows):
    m = ref[...]
    if m.shape[0] == 1:
        return m
    return jnp.broadcast_to(m[:, None, :], (m.shape[0], rows // m.shape[0], m.shape[1])).reshape(
        rows, m.shape[1])


def _inproj(x, shift, scale, g, w_main, w_glr, cos, sin, gq, gk, seg, out_dtype, per_token, tiles_per_seq):
    n = x.shape[0]
    mod = _mod_spec(per_token, tiles_per_seq)
    const = lambda i: (0, 0)
    rope = pl.BlockSpec((TOKEN_TILE, LANES), (lambda i: (i, 0)) if per_token else
                        (lambda i: (i % tiles_per_seq, 0)))
    return pl.pallas_call(
        _inproj_kernel,
        grid=(n // TOKEN_TILE,),
        in_specs=[pl.BlockSpec((TOKEN_TILE, D_MODEL), lambda i: (i, 0)), mod, mod,
                  pl.BlockSpec((1, D_MODEL), const),
                  pl.BlockSpec((D_MODEL, D_MAIN), const, pipeline_mode=pl.Buffered(1)),
                  pl.BlockSpec((D_MODEL, LANES), const, pipeline_mode=pl.Buffered(1)),
                  rope, rope,
                  pl.BlockSpec((1, LANES), const), pl.BlockSpec((1, LANES), const),
                  pl.BlockSpec((LANES, LANES), const)],
        out_specs=[pl.BlockSpec((TOKEN_TILE, D_MAIN), lambda i: (i, 0)),
                   pl.BlockSpec((TOKEN_TILE, LANES), lambda i: (i, 0))],
        out_shape=[jax.ShapeDtypeStruct((n, D_MAIN), out_dtype),
                   jax.ShapeDtypeStruct((n, LANES), F32)],
        compiler_params=_params("parallel"),
        name="inproj",
    )(x, shift, scale, g, w_main, w_glr, cos, sin, gq, gk, seg)


def _gla_log_gate(glr, wgk_ref, bgk):
    g_hi = glr.astype(BF16)
    g_lo = (glr - g_hi.astype(F32)).astype(BF16)
    w_hi, w_lo = wgk_ref[0], wgk_ref[1]
    x = _dot(g_hi, w_hi) + (_dot(g_lo, w_hi) + _dot(g_hi, w_lo))
    return _log_sigmoid(x + bgk) * (1.0 / GATE_TAU)


def _gla_out(o, r, g):
    r = r.astype(F32)
    return _rms(o, g) * (r * jax.nn.sigmoid(r))


def _gla_pair_levels(n):
    t = np.arange(n)[:, None]
    s = np.arange(n)[None, :]
    x = t ^ s
    top = np.where(x > 0, 1 << np.floor(np.log2(np.maximum(x, 1))).astype(np.int64), 0)
    return np.where(s > t, -1, top).astype(np.int32)


def _gla_block_ref(b, h):
    n, w = b.shape
    if 2 * h == n:
        return jnp.broadcast_to(b[h - 1:h, :], (n, w))
    if h >= SUBLANES // 2:
        picked = b.reshape(n // (2 * h), 2 * h, w)[:, h - 1:h, :]
        return jnp.broadcast_to(picked, (n // (2 * h), 2 * h, w)).reshape(n, w)
    r = lax.broadcasted_iota(jnp.int32, (n, 1), 0) % (2 * h)
    out = b
    for d in range(1, h + 1):
        out = jnp.where(r == h - 1 + d, pltpu.roll(b, d, 0), out)
    for d in range(1, h):
        out = jnp.where(r == h - 1 - d, pltpu.roll(b, n - d, 0), out)
    return out


def _gla_prompt_kernel(q_ref, k_ref, v_ref, r_ref, glr_ref, wgk_ref, bgk_ref, g_ref, lev_ref,
                       o_ref, s_ref, st_ref):
    c = pl.program_id(1)

    @pl.when(c == 0)
    def _():
        st_ref[...] = jnp.zeros_like(st_ref)

    n = GLA_BLOCK
    lev = lev_ref[...]
    causal = lax.broadcasted_iota(jnp.int32, (n, n), 1) <= lax.broadcasted_iota(jnp.int32, (n, n), 0)
    lg = _gla_log_gate(glr_ref[...], wgk_ref, bgk_ref[...])
    b = _dot_sum(jnp.where(causal, 1.0, 0.0).astype(BF16), lg)
    b_last = b[n - 1:n, :]
    q = q_ref[...].astype(F32) * (GLA_DK ** -0.5)
    k = k_ref[...].astype(F32)
    qe = (q * jnp.exp(b)).astype(BF16)
    kd = (k * jnp.exp(b_last - b)).astype(BF16)
    decay = jnp.exp(b_last)
    levels = [0] + [1 << p for p in range(n.bit_length() - 1)]
    q_lv, k_lv = [q.astype(BF16)], [k.astype(BF16)]
    for h in levels[1:]:
        e = jnp.exp(-jnp.abs(b - _gla_block_ref(b, h)))
        q_lv.append((q * e).astype(BF16))
        k_lv.append((k * e).astype(BF16))
    scaled = {level: (ql, kl) for level, ql, kl in zip(levels, q_lv, k_lv)}

    def intra(lo, hi, dk, v):
        size = hi - lo
        if size == GLA_MASK_BLOCK:
            attn = jnp.zeros((size, size), F32)
            for level in levels:
                if level < size:
                    ql, kl = scaled[level]
                    attn = jnp.where(lev == level, _dot_nt(ql[lo:hi, dk], kl[lo:hi, dk]), attn)
            return _dot(attn.astype(BF16), v[lo:hi, :])
        mid = lo + size // 2
        ql, kl = scaled[size // 2]
        cross = _dot_nt(ql[mid:hi, dk], kl[lo:mid, dk]).astype(BF16)
        return jnp.concatenate([intra(lo, mid, dk, v),
                                _dot(cross, v[lo:mid, :]) + intra(mid, hi, dk, v)], axis=0)

    for h in range(GLA_HEADS):
        dk = slice(h * GLA_DK, (h + 1) * GLA_DK)
        dv = slice(h * GLA_DV, (h + 1) * GLA_DV)
        v = v_ref[:, dv]
        st = st_ref[h]
        o = _dot_nt(qe[:, dk], st.astype(BF16)) + intra(0, n, dk, v)
        st_ref[h] = st * decay[:, dk] + _dot_tn(v, kd[:, dk])
        o_ref[:, dv] = _gla_out(o, r_ref[:, dv], g_ref[...]).astype(o_ref.dtype)

    @pl.when(c == pl.num_programs(1) - 1)
    def _():
        for h in range(GLA_HEADS):
            s_ref[h] = st_ref[h].T


def _gla_prompt(yp, glr, wgk, bgk, gnorm, batch, seq):
    nb = seq // GLA_BLOCK
    hk, hv = GLA_HEADS * GLA_DK, GLA_HEADS * GLA_DV
    tok = lambda b, c: b * nb + c
    const = lambda b, c: (0, 0)
    return pl.pallas_call(
        _gla_prompt_kernel,
        grid=(batch, nb),
        in_specs=[pl.BlockSpec((GLA_BLOCK, hk), lambda b, c: (tok(b, c), COL_QG // hk)),
                  pl.BlockSpec((GLA_BLOCK, hk), lambda b, c: (tok(b, c), COL_KG // hk)),
                  pl.BlockSpec((GLA_BLOCK, hv), lambda b, c: (tok(b, c), COL_VG // hv)),
                  pl.BlockSpec((GLA_BLOCK, hv), lambda b, c: (tok(b, c), COL_RG // hv)),
                  pl.BlockSpec((GLA_BLOCK, LANES), lambda b, c: (tok(b, c), 0)),
                  pl.BlockSpec((2, LANES, hk), lambda b, c: (0, 0, 0)),
                  pl.BlockSpec((1, hk), const),
                  pl.BlockSpec((1, GLA_DV), const),
                  pl.BlockSpec((GLA_MASK_BLOCK, GLA_MASK_BLOCK), const)],
        out_specs=[pl.BlockSpec((GLA_BLOCK, hv), lambda b, c: (tok(b, c), 0)),
                   pl.BlockSpec((None, GLA_HEADS, GLA_DK, GLA_DV), lambda b, c: (b, 0, 0, 0))],
        out_shape=[jax.ShapeDtypeStruct((batch * seq, hv), BF16),
                   jax.ShapeDtypeStruct((batch, GLA_HEADS, GLA_DK, GLA_DV), F32)],
        scratch_shapes=[pltpu.VMEM((GLA_HEADS, GLA_DV, GLA_DK), F32)],
        compiler_params=_params("parallel", "arbitrary"),
        name="gla_prompt",
    )(yp, yp, yp, yp, glr, wgk, bgk, gnorm, jnp.asarray(_gla_pair_levels(GLA_MASK_BLOCK)))


def _gla_sample_kernel(q_ref, k_ref, v_ref, r_ref, glr_ref, wgk_ref, bgk_ref, g_ref, s0_ref, lev_ref,
                       o_ref, s_ref, *, n_real):
    R = SAMPLE_ROWS
    rows = q_ref.shape[0]
    row = lax.broadcasted_iota(jnp.int32, (rows, rows), 0)
    col = lax.broadcasted_iota(jnp.int32, (rows, rows), 1)
    same = (row // R) == (col // R)
    causal = same & (col <= row)
    real = lax.broadcasted_iota(jnp.int32, (rows, 1), 0) % R < n_real
    lg = jnp.where(real, _gla_log_gate(glr_ref[...], wgk_ref, bgk_ref[...]), 0.0)
    b = _dot_sum(jnp.where(causal, 1.0, 0.0).astype(BF16), lg)
    b_last = _dot_sum(jnp.where(same, 1.0, 0.0).astype(BF16), lg)
    q = q_ref[...] * (GLA_DK ** -0.5)
    k = jnp.where(real, k_ref[...], 0.0)
    v = v_ref[...]
    qe = q * jnp.exp(b)
    kd = k * jnp.exp(b_last - b)
    lev = lev_ref[...]
    levels = [0] + [1 << p for p in range(R.bit_length() - 1)]
    q_lv, k_lv = [q.astype(BF16)], [k.astype(BF16)]
    for h in levels[1:]:
        e = jnp.exp(-jnp.abs(b - _gla_block_ref(b, h)))
        q_lv.append((q * e).astype(BF16))
        k_lv.append((k * e).astype(BF16))
    v_bf = v.astype(BF16)
    for h in range(GLA_HEADS):
        dk = slice(h * GLA_DK, (h + 1) * GLA_DK)
        dv = slice(h * GLA_DV, (h + 1) * GLA_DV)
        attn = jnp.zeros((rows, rows), F32)
        for level, ql, kl in zip(levels, q_lv, k_lv):
            attn = jnp.where(lev == level, _dot_nt(ql[:, dk], kl[:, dk]), attn)
        o_intra = _dot(attn.astype(BF16), v_bf[:, dv])
        decay_t = jnp.exp(b_last[:, dk]).T
        outs = []
        for s in range(rows // R):
            sl = slice(s * R, (s + 1) * R)
            s0 = s0_ref[s, h]
            outs.append(_dot(qe[sl, dk], s0) + o_intra[sl, :])
            s_ref[s, h] = s0 * decay_t[:, s * R:s * R + 1] + _dot_tn(kd[sl, dk], v[sl, dv])
        o_ref[:, dv] = _gla_out(jnp.concatenate(outs, axis=0), r_ref[:, dv], g_ref[...])


def _gla_sample(ys, glr, wgk, bgk, gnorm, state, n_seq, n_real):
    R = SAMPLE_SEQS * SAMPLE_ROWS
    hk, hv = GLA_HEADS * GLA_DK, GLA_HEADS * GLA_DV
    st_spec = pl.BlockSpec((SAMPLE_SEQS, GLA_HEADS, GLA_DK, GLA_DV), lambda b: (b, 0, 0, 0))
    lev = _gla_pair_levels(R)
    lev = np.where(lev >= SAMPLE_ROWS, -1, lev)
    return pl.pallas_call(
        functools.partial(_gla_sample_kernel, n_real=n_real),
        grid=(n_seq // SAMPLE_SEQS,),
        in_specs=[pl.BlockSpec((R, hk), lambda b: (b, COL_QG // hk)),
                  pl.BlockSpec((R, hk), lambda b: (b, COL_KG // hk)),
                  pl.BlockSpec((R, hv), lambda b: (b, COL_VG // hv)),
                  pl.BlockSpec((R, hv), lambda b: (b, COL_RG // hv)),
                  pl.BlockSpec((R, LANES), lambda b: (b, 0)),
                  pl.BlockSpec((2, LANES, hk), lambda b: (0, 0, 0)),
                  pl.BlockSpec((1, hk), lambda b: (0, 0)),
                  pl.BlockSpec((1, GLA_DV), lambda b: (0, 0)),
                  st_spec,
                  pl.BlockSpec((R, R), lambda b: (0, 0))],
        out_specs=[pl.BlockSpec((R, hv), lambda b: (b, 0)), st_spec],
        out_shape=[jax.ShapeDtypeStruct((n_seq * SAMPLE_ROWS, hv), F32),
                   jax.ShapeDtypeStruct((n_seq, GLA_HEADS, GLA_DK, GLA_DV), F32)],
        compiler_params=_params("parallel"),
        name="gla_sample",
    )(ys, ys, ys, ys, glr, wgk, bgk, gnorm, state, jnp.asarray(lev))


def _lane_lower(shape):
    return lax.broadcasted_iota(jnp.int32, shape, len(shape) - 1) % LANES < HEAD_DIM


def _head_norm_rope(x, g, cos, sin, seg):
    ss = _dot((x * x).astype(BF16), seg)
    y = x * lax.rsqrt(ss * (1.0 / HEAD_DIM) + NORM_EPS) * g
    half = HEAD_DIM // 2
    lane = lax.broadcasted_iota(jnp.int32, y.shape, 1)
    rot = jnp.where(lane % HEAD_DIM < half, pltpu.roll(y, LANES - half, 1), pltpu.roll(y, half, 1))
    return y * cos + rot * sin


def _both_halves(blk, half):
    sw = pltpu.roll(blk, HEAD_DIM, 1)
    lower = _lane_lower(blk.shape)
    return jnp.where(lower, blk, sw) if half == 0 else jnp.where(lower, sw, blk)


def _stack_heads(q_blocks):
    parts = []
    for qb in q_blocks:
        lower = _lane_lower(qb.shape)
        zero = jnp.zeros_like(qb)
        parts += [jnp.where(lower, qb, zero), jnp.where(lower, zero, qb)]
    return jnp.concatenate(parts, axis=0)


def _swa_prompt_kernel(sink_ref, q_ref, k_ref, v_ref, o_ref, ko_ref, vo_ref, kprev_ref, vprev_ref):
    n = pl.program_id(1)
    W = WINDOW

    @pl.when(n == 0)
    def _():
        kprev_ref[...] = jnp.zeros_like(kprev_ref)
        vprev_ref[...] = jnp.zeros_like(vprev_ref)

    def per_head(x):
        return [_both_halves(x[:, (kh // 2) * LANES:(kh // 2 + 1) * LANES], kh % 2).astype(BF16)
                for kh in range(SWA_KV_HEADS)]

    kb_prev = [kprev_ref[kh] for kh in range(SWA_KV_HEADS)]
    vb_prev = [vprev_ref[kh] for kh in range(SWA_KV_HEADS)]
    for sub in range(SWA_STEP_BLOCKS):
        tok = slice(sub * W, (sub + 1) * W)
        k_cur = k_ref[tok, :].astype(F32)
        v_cur = v_ref[tok, :].astype(F32)
        kb_cur, vb_cur = per_head(k_cur), per_head(v_cur)
        prev_fill = jnp.where(n > 0, 0.0, -jnp.inf) if sub == 0 else 0.0
        _swa_block(sink_ref, q_ref, o_ref, tok, kb_prev, vb_prev, kb_cur, vb_cur, prev_fill)
        kb_prev, vb_prev = kb_cur, vb_cur
    ko_ref[...] = k_cur
    vo_ref[...] = v_cur
    for kh in range(SWA_KV_HEADS):
        kprev_ref[kh] = kb_prev[kh]
        vprev_ref[kh] = vb_prev[kh]


def _swa_block(sink_ref, q_ref, o_ref, tok, kb_prev_all, vb_prev_all, kb_cur_all, vb_cur_all, prev_fill):
    W = WINDOW
    qi = lax.broadcasted_iota(jnp.int32, (W, W), 0)
    ki = lax.broadcasted_iota(jnp.int32, (W, W), 1)
    from_cur = ki <= qi
    for kh in range(SWA_KV_HEADS):
        kb_prev, kb_cur = kb_prev_all[kh], kb_cur_all[kh]
        vb_prev, vb_cur = vb_prev_all[kh], vb_cur_all[kh]
        qblocks = [q_ref[tok, (2 * kh + j) * LANES:(2 * kh + j + 1) * LANES] for j in range(2)]
        qs = _stack_heads(qblocks)
        s_prev = _dot_nt(qs, kb_prev)
        s_cur = _dot_nt(qs, kb_cur)
        outs = []
        for g in range(SWA_GROUP):
            rows = slice(g * W, (g + 1) * W)
            sg = jnp.where(from_cur, s_cur[rows, :], s_prev[rows, :] + prev_fill)
            sink = sink_ref[kh * SWA_GROUP + g]
            m = jnp.maximum(jnp.max(sg, axis=-1, keepdims=True), sink)
            p = jnp.exp(sg - m)
            denom = jnp.sum(p, axis=-1, keepdims=True) + jnp.exp(sink - m)
            p_cur = jnp.where(from_cur, p, 0.0).astype(BF16)
            p_prev = jnp.where(from_cur, 0.0, p).astype(BF16)
            outs.append((_dot(p_prev, vb_prev) + _dot(p_cur, vb_cur)) / denom)
        lower = _lane_lower((W, LANES))
        for j in range(2):
            c0 = (2 * kh + j) * LANES
            o_ref[tok, c0:c0 + LANES] = jnp.where(lower, outs[2 * j], outs[2 * j + 1]).astype(o_ref.dtype)


def _swa_prompt(yp, sinks, batch, seq):
    rows = SWA_STEP_BLOCKS * WINDOW
    nb = seq // rows
    kvw = SWA_KV_HEADS * HEAD_DIM
    tok = lambda b, n: b * nb + n
    cache_spec = pl.BlockSpec((None, WINDOW, kvw), lambda b, n: (b, 0, 0))
    return pl.pallas_call(
        _swa_prompt_kernel,
        grid=(batch, nb),
        in_specs=[pl.BlockSpec(memory_space=pltpu.SMEM),
                  pl.BlockSpec((rows, D_MODEL), lambda b, n: (tok(b, n), COL_QS // D_MODEL)),
                  pl.BlockSpec((rows, kvw), lambda b, n: (tok(b, n), COL_KS // kvw)),
                  pl.BlockSpec((rows, kvw), lambda b, n: (tok(b, n), COL_VS // kvw))],
        out_specs=[pl.BlockSpec((rows, D_MODEL), lambda b, n: (tok(b, n), 0)), cache_spec, cache_spec],
        out_shape=[jax.ShapeDtypeStruct((batch * seq, D_MODEL), BF16),
                   jax.ShapeDtypeStruct((batch, WINDOW, kvw), F32),
                   jax.ShapeDtypeStruct((batch, WINDOW, kvw), F32)],
        scratch_shapes=[pltpu.VMEM((SWA_KV_HEADS, WINDOW, LANES), BF16),
                        pltpu.VMEM((SWA_KV_HEADS, WINDOW, LANES), BF16)],
        compiler_params=_params("parallel", "arbitrary"),
        name="swa_prompt",
    )(sinks, yp, yp, yp)


def _shift_cache(cache, new, n_real):
    R = SAMPLE_ROWS
    rolled = pltpu.roll(cache, WINDOW - n_real, 0)
    tail_new = pltpu.roll(new, R - n_real, 0)
    row = lax.broadcasted_iota(jnp.int32, (R, cache.shape[1]), 0)
    tail = jnp.where(row < R - n_real, rolled[WINDOW - R:, :], tail_new)
    return jnp.concatenate([rolled[:WINDOW - R, :], tail], axis=0)


def _swa_sample_kernel(q_ref, k_ref, v_ref, kc_ref, vc_ref, sink_ref, o_ref, ko_ref, vo_ref, *, n_real):
    R, W = SAMPLE_ROWS, WINDOW
    k_new = k_ref[...]
    v_new = v_ref[...]
    q_pairs = [q_ref[:, j * LANES:(j + 1) * LANES] for j in range(SWA_HEADS // 2)]

    hr = SWA_HEADS * R
    t_c = lax.broadcasted_iota(jnp.int32, (hr, W), 0) % R
    mask_c = lax.broadcasted_iota(jnp.int32, (hr, W), 1) > t_c
    t_n = lax.broadcasted_iota(jnp.int32, (hr, R), 0) % R
    mask_n = lax.broadcasted_iota(jnp.int32, (hr, R), 1) <= t_n
    sink = sink_ref[...]
    lower = _lane_lower((R, LANES))
    zeros = jnp.zeros((R, LANES), F32)
    for s in range(q_ref.shape[0] // R):
        sl = slice(s * R, (s + 1) * R)
        kc, vc = kc_ref[s], vc_ref[s]
        kn, vn = k_new[sl, :], v_new[sl, :]
        ko_ref[s] = _shift_cache(kc, kn, n_real)
        vo_ref[s] = _shift_cache(vc, vn, n_real)
        q_rows = []
        for h in range(SWA_HEADS):
            kh = h // SWA_GROUP
            x = q_pairs[h // 2][sl, :]
            if h % 2 != kh % 2:
                x = pltpu.roll(x, HEAD_DIM, 1)
            x = jnp.where(lower, x, zeros) if kh % 2 == 0 else jnp.where(lower, zeros, x)
            q_rows.append(jnp.concatenate([x, zeros] if kh // 2 == 0 else [zeros, x], axis=1))
        qbd = jnp.concatenate(q_rows, axis=0)
        sc = jnp.where(mask_c, _dot_nt(qbd.astype(BF16), kc.astype(BF16)), -jnp.inf)
        sn = jnp.where(mask_n, _dot_nt(qbd, kn), -jnp.inf)
        m = jnp.maximum(jnp.maximum(jnp.max(sc, axis=-1, keepdims=True),
                                    jnp.max(sn, axis=-1, keepdims=True)), sink)
        pc, pn = jnp.exp(sc - m), jnp.exp(sn - m)
        denom = (jnp.sum(pc, axis=-1, keepdims=True) + jnp.sum(pn, axis=-1, keepdims=True)
                 + jnp.exp(sink - m))
        o = (_dot(pc.astype(BF16), vc.astype(BF16)) + _dot(pn, vn)) / denom
        for j in range(SWA_HEADS // 2):
            halves = []
            for h in (2 * j, 2 * j + 1):
                kh = h // SWA_GROUP
                y = o[h * R:(h + 1) * R, (kh // 2) * LANES:(kh // 2 + 1) * LANES]
                halves.append(pltpu.roll(y, HEAD_DIM, 1) if h % 2 != kh % 2 else y)
            o_ref[sl, j * LANES:(j + 1) * LANES] = jnp.where(lower, halves[0], halves[1])


def _swa_sample(ys, sink_rows, kcache, vcache, n_seq, n_real):
    R = SAMPLE_SEQS * SAMPLE_ROWS
    kvw = SWA_KV_HEADS * HEAD_DIM
    const = lambda b: (0, 0)
    cache_spec = pl.BlockSpec((SAMPLE_SEQS, WINDOW, kvw), lambda b: (b, 0, 0))
    return pl.pallas_call(
        functools.partial(_swa_sample_kernel, n_real=n_real),
        grid=(n_seq // SAMPLE_SEQS,),
        in_specs=[pl.BlockSpec((R, D_MODEL), lambda b: (b, COL_QS // D_MODEL)),
                  pl.BlockSpec((R, kvw), lambda b: (b, COL_KS // kvw)),
                  pl.BlockSpec((R, kvw), lambda b: (b, COL_VS // kvw)),
                  cache_spec, cache_spec,
                  pl.BlockSpec((SWA_HEADS * SAMPLE_ROWS, 1), const)],
        out_specs=[pl.BlockSpec((R, D_MODEL), lambda b: (b, 0)), cache_spec, cache_spec],
        out_shape=[jax.ShapeDtypeStruct((n_seq * SAMPLE_ROWS, D_MODEL), F32),
                   jax.ShapeDtypeStruct((n_seq, WINDOW, kvw), F32),
                   jax.ShapeDtypeStruct((n_seq, WINDOW, kvw), F32)],
        compiler_params=_params("parallel"),
        name="swa_sample",
    )(ys, ys, ys, kcache, vcache, sink_rows)


def _post_kernel(ag0_ref, ag1_ref, as0_ref, as1_ref, og_ref, os_ref, x_ref, gate_ref, shift_ref, scale_ref,
                 g2_ref, wo_ref, wr_ref, br_ref, x1_ref, h2_ref, slot_ref, gatek_ref, cnt_ref, *, n_valid):
    half = D_MODEL // 2
    merged = jnp.concatenate(
        [a_g[...].astype(F32) * og_ref[:, sl].astype(F32) + a_s[...].astype(F32) * os_ref[:, sl].astype(F32)
         for a_g, a_s, sl in ((ag0_ref, as0_ref, slice(0, half)), (ag1_ref, as1_ref, slice(half, D_MODEL)))],
        axis=1)
    y = _dot(merged.astype(BF16), wo_ref[...])
    rows = x_ref.shape[0]
    x1 = x_ref[...] + _mod_rows(gate_ref, rows) * y
    x1_ref[...] = x1
    h2 = _rms(x1, g2_ref[...]) * (1.0 + _mod_rows(scale_ref, rows)) + _mod_rows(shift_ref, rows)
    h2_hi = h2.astype(BF16)
    h2_ref[...] = h2_hi

    h2_lo = (h2 - h2_hi.astype(F32)).astype(BF16)
    w_hi, w_lo = wr_ref[0], wr_ref[1]
    logits = _dot(h2_hi, w_hi) + (_dot(h2_lo, w_hi) + _dot(h2_hi, w_lo)) + br_ref[...]
    lane_i = lax.broadcasted_iota(jnp.int32, logits.shape, 1)
    lane = lane_i.astype(F32)
    work = logits
    vals, hots = [], []
    for _ in range(TOP_K):
        m = jnp.max(work, axis=-1, keepdims=True)
        idx = jnp.min(jnp.where(work == m, lane, float(LANES)), axis=-1, keepdims=True)
        hot = lane == idx
        vals.append(m)
        hots.append(hot)
        work = jnp.where(hot, -jnp.inf, work)
    exps = [jnp.exp(v - vals[0]) for v in vals]
    denom = exps[0] + exps[1] + exps[2] + exps[3]

    tm = logits.shape[0]
    valid = lax.broadcasted_iota(jnp.int32, (tm, 1), 0) % SAMPLE_ROWS < n_valid
    sel = jnp.zeros_like(logits)
    for hot in hots:
        sel = jnp.where(hot, 1.0, sel)
    sel = jnp.where(valid, sel, 0.0)
    earlier = (lax.broadcasted_iota(jnp.int32, (tm, tm), 1)
               < lax.broadcasted_iota(jnp.int32, (tm, tm), 0))
    rank = _dot(jnp.where(earlier, 1.0, 0.0).astype(BF16), sel.astype(BF16))
    cnt = jnp.sum(sel, axis=0, keepdims=True)
    cnt_pad = jnp.floor((cnt + (ROW_UNIT - 1.0)) * (1.0 / ROW_UNIT)) * ROW_UNIT
    below = (lax.broadcasted_iota(jnp.int32, (LANES, LANES), 0)
             < lax.broadcasted_iota(jnp.int32, (LANES, LANES), 1))
    seg_start = _dot(jnp.broadcast_to(cnt_pad, (8, LANES)), jnp.where(below, 1.0, 0.0), HIGHEST)[0:1]
    pos = seg_start + rank
    slots = jnp.full_like(logits, -1.0)
    gates = jnp.zeros_like(logits)
    for k in range(TOP_K):
        s_k = jnp.sum(jnp.where(hots[k], pos, 0.0), axis=-1, keepdims=True)
        slots = jnp.where(lane_i == k, s_k, slots)
        gates = jnp.where(lane_i == k, exps[k] / denom, gates)
    slot_ref[...] = jnp.where(valid, slots, -1.0)
    gatek_ref[...] = gates
    cnt_ref[...] = cnt


def _post(y_all, o_gla, o_swa, x, gate, shift, scale, g2, wo, wr, br, per_token, tiles_per_seq, n_valid):
    n = x.shape[0]
    mod = _mod_spec(per_token, tiles_per_seq)
    row = lambda i: (i, 0)
    const = lambda i: (0, 0)
    wide = pl.BlockSpec((TOKEN_TILE, D_MODEL), row)
    narrow = pl.BlockSpec((TOKEN_TILE, LANES), row)
    half = D_MODEL // 2
    return pl.pallas_call(
        functools.partial(_post_kernel, n_valid=n_valid),
        grid=(n // TOKEN_TILE,),
        in_specs=[pl.BlockSpec((TOKEN_TILE, half), lambda i: (i, COL_AG // half)),
                  pl.BlockSpec((TOKEN_TILE, half), lambda i: (i, COL_AG // half + 1)),
                  pl.BlockSpec((TOKEN_TILE, half), lambda i: (i, COL_AS // half)),
                  pl.BlockSpec((TOKEN_TILE, half), lambda i: (i, COL_AS // half + 1)),
                  wide, wide, wide, mod, mod, mod,
                  pl.BlockSpec((1, D_MODEL), const),
                  pl.BlockSpec((D_MODEL, D_MODEL), const),
                  pl.BlockSpec((2, D_MODEL, LANES), lambda i: (0, 0, 0)),
                  pl.BlockSpec((1, LANES), const)],
        out_specs=[wide, wide, narrow, narrow, pl.BlockSpec((None, 1, LANES), lambda i: (i, 0, 0))],
        out_shape=[jax.ShapeDtypeStruct((n, D_MODEL), F32),
                   jax.ShapeDtypeStruct((n, D_MODEL), BF16),
                   jax.ShapeDtypeStruct((n, LANES), F32),
                   jax.ShapeDtypeStruct((n, LANES), F32),
                   jax.ShapeDtypeStruct((n // TOKEN_TILE, 1, LANES), F32)],
        compiler_params=_params("parallel"),
        name="post",
    )(y_all, y_all, y_all, y_all, o_gla, o_swa, x, gate, shift, scale, g2, wo, wr, br)


def _slot_matrix(slot_cols, weights, chunk):
    tm = slot_cols[0].shape[0]
    j = lax.broadcasted_iota(jnp.int32, (tm, SLOT_CHUNK), 1) + chunk * SLOT_CHUNK
    out = jnp.zeros((tm, SLOT_CHUNK), F32)
    for s, w in zip(slot_cols, weights):
        out = jnp.where(s == j, w, out)
    return out.astype(BF16)


def _dispatch_kernel(nu_ref, seg_ref, ntail_ref, tail_ref, hp_ref, slp_ref, hs_ref, sls_ref, xg_ref,
                     sorted_ref, zero_ref, sem, tail_sem, *, prompt_tiles):
    t = pl.program_id(0)
    last = pl.num_programs(0) - 1
    buf = t % 2

    def sort_tile(h_ref, slot_ref):
        slots = slot_ref[...].astype(jnp.int32)
        slot_cols = [slots[:, k:k + 1] for k in range(TOP_K)]
        h = h_ref[...]
        upc = SLOT_CHUNK // ROW_UNIT
        for c in range(LOCAL_ROWS // SLOT_CHUNK):
            onehot = _slot_matrix(slot_cols, [1.0] * TOP_K, c)
            sorted_ref[buf, c * upc:(c + 1) * upc] = _dot_tn(onehot, h).astype(BF16).reshape(
                upc, ROW_UNIT, D_MODEL)

    @pl.when(t < prompt_tiles)
    def _():
        sort_tile(hp_ref, slp_ref)

    @pl.when(t >= prompt_tiles)
    def _():
        sort_tile(hs_ref, sls_ref)

    def start_all(tile, b):
        def body(e, c):
            n = seg_ref[tile, 0, e]

            @pl.when(n > 0)
            def _():
                pltpu.make_async_copy(sorted_ref.at[b, pl.ds(seg_ref[tile, 1, e], n)],
                                      xg_ref.at[pl.ds(seg_ref[tile, 2, e], n)], sem.at[b]).start()
            return c

        lax.fori_loop(0, N_EXPERTS, body, 0)

    def wait_all(tile, b):
        n_units = nu_ref[tile]

        @pl.when(n_units > 0)
        def _():
            pltpu.make_async_copy(sorted_ref.at[b, pl.ds(0, n_units)],
                                  xg_ref.at[pl.ds(0, n_units)], sem.at[b]).wait()

    @pl.when(t > 0)
    def _():
        wait_all(t - 1, 1 - buf)

    start_all(t, buf)

    def tail_copy(i):
        return pltpu.make_async_copy(zero_ref, xg_ref.at[tail_ref[i]], tail_sem)

    @pl.when(t == last)
    def _():
        zero_ref[...] = jnp.zeros_like(zero_ref)
        n_tail = ntail_ref[0]
        lax.fori_loop(0, n_tail, lambda i, c: (tail_copy(i).start(), c)[1], 0)
        lax.fori_loop(0, n_tail, lambda i, c: (tail_copy(i).wait(), c)[1], 0)
        wait_all(t, buf)


def _dispatch(h2_p, slots_p, h2_s, slots_s, n_units, seg, n_tail, tail_dst, rows_max):
    p_tiles = h2_p.shape[0] // TOKEN_TILE
    s_tiles = h2_s.shape[0] // TOKEN_TILE
    p_row = lambda t, *_: (jnp.minimum(t, p_tiles - 1), 0)
    s_row = lambda t, *_: (jnp.maximum(t - p_tiles, 0), 0)
    return pl.pallas_call(
        functools.partial(_dispatch_kernel, prompt_tiles=p_tiles),
        grid_spec=pltpu.PrefetchScalarGridSpec(
            num_scalar_prefetch=4,
            grid=(p_tiles + s_tiles,),
            in_specs=[pl.BlockSpec((TOKEN_TILE, D_MODEL), p_row),
                      pl.BlockSpec((TOKEN_TILE, LANES), p_row),
                      pl.BlockSpec((TOKEN_TILE, D_MODEL), s_row),
                      pl.BlockSpec((TOKEN_TILE, LANES), s_row)],
            out_specs=pl.BlockSpec(memory_space=pl.ANY),
            scratch_shapes=[pltpu.VMEM((2, LOCAL_ROWS // ROW_UNIT, ROW_UNIT, D_MODEL), BF16),
                            pltpu.VMEM((ROW_UNIT, D_MODEL), BF16),
                            pltpu.SemaphoreType.DMA((2,)), pltpu.SemaphoreType.DMA]),
        out_shape=jax.ShapeDtypeStruct((rows_max // ROW_UNIT, ROW_UNIT, D_MODEL), BF16),
        compiler_params=_params("arbitrary"),
        name="dispatch",
    )(n_units, seg, n_tail, tail_dst, h2_p, slots_p, h2_s, slots_s).reshape(rows_max, D_MODEL)


def _expert_kernel(tg_ref, ge_ref, ng_ref, nused_ref, rows_ref, x_ref, bgu_ref, bd_ref, wgu_hbm, wd_hbm,
                   y_ref, wgu_f32, wd_f32, wgu_bf, wd_bf, sem):
    i = pl.program_id(0)

    def fetch(g, b):
        e = ge_ref[g]
        return (pltpu.make_async_copy(wgu_hbm.at[e], wgu_f32.at[b], sem.at[0, b]),
                pltpu.make_async_copy(wd_hbm.at[e], wd_f32.at[b], sem.at[1, b]))

    @pl.when(i == 0)
    def _():
        for cp in fetch(0, 0):
            cp.start()

    @pl.when(i < nused_ref[0])
    def _():
        g = tg_ref[i]
        b = g % 2

        @pl.when((i == 0) | (g != tg_ref[jnp.maximum(i - 1, 0)]))
        def _():
            @pl.when(g + 1 < ng_ref[0])
            def _():
                for cp in fetch(g + 1, 1 - b):
                    cp.start()

            for cp in fetch(g, b):
                cp.wait()
            wgu_bf[...] = wgu_f32[b].astype(BF16)
            wd_bf[...] = wd_f32[b].astype(BF16)

        for part in range(EXPERT_TILE // EXPERT_PART):
            @pl.when(rows_ref[i] > part * EXPERT_PART)
            def _():
                sl = slice(part * EXPERT_PART, (part + 1) * EXPERT_PART)
                gu = _dot(x_ref[sl, :], wgu_bf[...]) + bgu_ref[...]
                gate = jnp.minimum(gu[:, :D_FF], SWIGLU_LIMIT)
                up = jnp.clip(gu[:, D_FF:], -SWIGLU_LIMIT, SWIGLU_LIMIT)
                act = (up + 1.0) * gate * jax.nn.sigmoid(SWIGLU_ALPHA * gate)
                y_ref[sl, :] = (_dot(act.astype(BF16), wd_bf[...]) + bd_ref[...]).astype(y_ref.dtype)


def _experts(xg, tile_group, group_expert, n_groups, n_used, tile_rows, wgu, bgu, wd, bd):
    rows_max = xg.shape[0]
    used = lambda i, nu: jnp.maximum(jnp.minimum(i, nu[0] - 1), 0)
    row = lambda i, tg, ge, ng, nu, tr: (used(i, nu), 0)
    exp = lambda i, tg, ge, ng, nu, tr: (ge[tg[used(i, nu)]], 0, 0)
    return pl.pallas_call(
        _expert_kernel,
        grid_spec=pltpu.PrefetchScalarGridSpec(
            num_scalar_prefetch=5,
            grid=(rows_max // EXPERT_TILE,),
            in_specs=[pl.BlockSpec((EXPERT_TILE, D_MODEL), row),
                      pl.BlockSpec((None, 1, 2 * D_FF), exp),
                      pl.BlockSpec((None, 1, D_MODEL), exp),
                      pl.BlockSpec(memory_space=pl.ANY),
                      pl.BlockSpec(memory_space=pl.ANY)],
            out_specs=pl.BlockSpec((EXPERT_TILE, D_MODEL), row),
            scratch_shapes=[pltpu.VMEM((2, D_MODEL, 2 * D_FF), F32), pltpu.VMEM((2, D_FF, D_MODEL), F32),
                            pltpu.VMEM((D_MODEL, 2 * D_FF), BF16), pltpu.VMEM((D_FF, D_MODEL), BF16),
                            pltpu.SemaphoreType.DMA((2, 2))]),
        out_shape=jax.ShapeDtypeStruct((rows_max, D_MODEL), BF16),
        compiler_params=_params("arbitrary"),
        name="experts",
    )(tile_group, group_expert, n_groups, n_used, tile_rows, xg, bgu, bd, wgu, wd)


def _combine_kernel(nu_ref, seg_ref, slot_ref, gatek_ref, x_ref, gmlp_ref, y_ref, o_ref, ys_ref, sem,
                    *, tile_offset):
    j = pl.program_id(0)
    t = j + tile_offset
    buf = j % 2
    tm = x_ref.shape[0]

    def fetch(tile, b):
        def body(e, c):
            n = seg_ref[tile, 0, e]

            @pl.when(n > 0)
            def _():
                pltpu.make_async_copy(y_ref.at[pl.ds(seg_ref[tile, 2, e], n)],
                                      ys_ref.at[b, pl.ds(seg_ref[tile, 1, e], n)], sem.at[b]).start()
            return c

        lax.fori_loop(0, N_EXPERTS, body, 0)
        n_units = nu_ref[tile]

        def zero_unit(i, c):
            ys_ref[b, i] = jnp.zeros((ROW_UNIT, D_MODEL), ys_ref.dtype)
            return c

        lax.fori_loop(n_units, LOCAL_ROWS // ROW_UNIT, zero_unit, 0)

    @pl.when(j == 0)
    def _():
        fetch(t, buf)

    @pl.when(j + 1 < pl.num_programs(0))
    def _():
        fetch(t + 1, 1 - buf)

    n_units = nu_ref[t]

    @pl.when(n_units > 0)
    def _():
        pltpu.make_async_copy(y_ref.at[pl.ds(0, n_units)], ys_ref.at[buf, pl.ds(0, n_units)],
                              sem.at[buf]).wait()

    slots = slot_ref[...].astype(jnp.int32)
    gates = gatek_ref[...]
    slot_cols = [slots[:, k:k + 1] for k in range(TOP_K)]
    gate_cols = [gates[:, k:k + 1] for k in range(TOP_K)]
    acc = jnp.zeros((tm, D_MODEL), F32)
    upc = SLOT_CHUNK // ROW_UNIT
    for c in range(LOCAL_ROWS // SLOT_CHUNK):
        rows_c = ys_ref[buf, c * upc:(c + 1) * upc].reshape(SLOT_CHUNK, D_MODEL)
        acc = acc + _dot(_slot_matrix(slot_cols, gate_cols, c), rows_c)
    o_ref[...] = x_ref[...] + _mod_rows(gmlp_ref, tm) * acc


def _combine(y, slots, gates, x1, gmlp, n_units, seg, tile_offset, per_token, tiles_per_seq):
    n = x1.shape[0]
    mod = _mod_spec(per_token, tiles_per_seq)
    wide = pl.BlockSpec((TOKEN_TILE, D_MODEL), lambda i, *_: (i, 0))
    narrow = pl.BlockSpec((TOKEN_TILE, LANES), lambda i, *_: (i, 0))
    return pl.pallas_call(
        functools.partial(_combine_kernel, tile_offset=tile_offset),
        grid_spec=pltpu.PrefetchScalarGridSpec(
            num_scalar_prefetch=2,
            grid=(n // TOKEN_TILE,),
            in_specs=[narrow, narrow, wide, mod, pl.BlockSpec(memory_space=pl.ANY)],
            out_specs=wide,
            scratch_shapes=[pltpu.VMEM((2, LOCAL_ROWS // ROW_UNIT, ROW_UNIT, D_MODEL), BF16),
                            pltpu.SemaphoreType.DMA((2,))]),
        out_shape=jax.ShapeDtypeStruct((n, D_MODEL), F32),
        compiler_params=_params("arbitrary"),
        name="combine",
    )(n_units, seg, slots, gates, x1, gmlp, y.reshape(-1, ROW_UNIT, D_MODEL))


def _route_tables(cnt, rows_max):
    units = (cnt + ROW_UNIT - 1) // ROW_UNIT
    group_units = jnp.sum(units, axis=0)
    upt = EXPERT_TILE // ROW_UNIT
    group_pad = (group_units + upt - 1) // upt * upt
    group_end = jnp.cumsum(group_pad)
    group_start = group_end - group_pad
    seg_start = group_start[None, :] + jnp.cumsum(units, axis=0) - units
    local_end = jnp.cumsum(units, axis=1)
    local_start = local_end - units
    n_units = local_end[:, -1].astype(jnp.int32)

    def pick(lo, hi, pos, value):
        return jnp.sum(jnp.where((pos >= lo) & (pos < hi), value, 0), axis=-1).astype(jnp.int32)

    seg = jnp.stack([units, local_start, seg_start], axis=1).astype(jnp.int32)

    n_tail_e = group_pad - group_units
    j = jnp.arange(N_EXPERTS * upt, dtype=jnp.int32)[:, None]
    tail_end = jnp.cumsum(n_tail_e)
    tail_start = tail_end - n_tail_e
    tail_hbm = pick(tail_start[None, :], tail_end[None, :], j,
                    (group_start + group_units)[None, :] + j - tail_start[None, :])
    n_tail = tail_end[-1:].astype(jnp.int32)

    r = jnp.arange(rows_max // EXPERT_TILE, dtype=jnp.int32)[:, None] * upt
    n_used = (group_end[-1:] // upt).astype(jnp.int32)
    nonempty = group_units > 0
    group_of_expert = jnp.cumsum(nonempty) - 1
    tile_rows = pick(group_start[None, :], group_end[None, :], r,
                     jnp.clip(((group_start + group_units)[None, :] - r) * ROW_UNIT, 0, EXPERT_TILE))
    tile_group = pick(group_start[None, :], group_end[None, :], r, group_of_expert[None, :])
    g = jnp.arange(N_EXPERTS, dtype=jnp.int32)
    group_expert = jnp.sum(jnp.where(nonempty[None, :] & (group_of_expert[None, :] == g[:, None]),
                                     g[None, :], 0), axis=-1).astype(jnp.int32)
    n_groups = jnp.sum(nonempty)[None].astype(jnp.int32)
    return (n_units, seg, n_tail, tail_hbm,
            tile_group, group_expert, n_groups, n_used, tile_rows)


def _rope_tables(pos):
    half = HEAD_DIM // 2
    lane = np.arange(LANES)
    inv = jnp.asarray(ROPE_THETA, F32) ** (-jnp.asarray(lane % half, F32) / half)
    sign = jnp.asarray(np.where(lane % HEAD_DIM < half, -1.0, 1.0), F32)
    ang = pos.astype(F32)[:, None] * inv[None, :]
    return jnp.cos(ang), jnp.sin(ang) * sign[None, :]


def kernel(x_prompt, x_sample, c_prompt, c_sample, state_gla, cache_swa_k, cache_swa_v, w_ada, b_ada,
           norm1_g, norm2_g, w_in, w_gk2, b_gk, gla_norm_g, q_norm_g, k_norm_g, attn_sinks, w_o,
           w_router, b_router, w_gate_up, b_gate_up, w_down, b_down):
    batch, seq, d = x_prompt.shape
    n_seq, n_real, _ = x_sample.shape
    depth = w_in.shape[0]
    assert depth == 1 and d == D_MODEL and n_real <= SAMPLE_ROWS
    assert seq % TOKEN_TILE == 0 and (n_seq * SAMPLE_ROWS) % TOKEN_TILE == 0 and n_seq % SAMPLE_SEQS == 0
    R = SAMPLE_ROWS
    kvw = SWA_KV_HEADS * HEAD_DIM
    tiles_per_seq = seq // TOKEN_TILE

    glr_col = COL_QS
    w_main = jnp.concatenate([w_in[0, :, :glr_col], w_in[0, :, glr_col + GATE_RANK:]], axis=1).astype(BF16)
    w_glr = jnp.pad(w_in[0, :, glr_col:glr_col + GATE_RANK], ((0, 0), (0, LANES - GATE_RANK))).astype(BF16)
    wgk = jnp.pad(w_gk2[0], ((0, LANES - GATE_RANK), (0, 0)))
    wgk_hi = wgk.astype(BF16)
    wgk = jnp.stack([wgk_hi, (wgk - wgk_hi.astype(F32)).astype(BF16)])
    bgk = b_gk[0].reshape(1, -1)
    gnorm = gla_norm_g[0].reshape(1, -1)
    gq = jnp.tile(q_norm_g[0], LANES // HEAD_DIM).reshape(1, LANES)
    gk = jnp.tile(k_norm_g[0], LANES // HEAD_DIM).reshape(1, LANES)
    seg = jnp.asarray(np.kron(np.eye(LANES // HEAD_DIM), np.ones((HEAD_DIM, HEAD_DIM))), BF16)
    sinks = attn_sinks[0]
    wo = w_o[0].astype(BF16)
    wr = jnp.pad(w_router[0], ((0, 0), (0, LANES - N_EXPERTS)))
    wr_hi = wr.astype(BF16)
    wr = jnp.stack([wr_hi, (wr - wr_hi.astype(F32)).astype(BF16)])
    br = jnp.pad(b_router[0], (0, LANES - N_EXPERTS), constant_values=-1e30).reshape(1, LANES)
    bgu = b_gate_up[0].reshape(N_EXPERTS, 1, 2 * D_FF)
    bd = b_down[0].reshape(N_EXPERTS, 1, D_MODEL)
    g1 = norm1_g[0].reshape(1, -1)
    g2 = norm2_g[0].reshape(1, -1)

    n_c = batch + n_seq
    c_all = jnp.pad(jnp.concatenate([c_prompt, c_sample], axis=0), ((0, -n_c % 8), (0, 0)))
    m_all = _ada(c_all, w_ada[0], b_ada[0])
    mp = [m_all[:batch, i * d:(i + 1) * d].reshape(batch, 1, d) for i in range(6)]
    ms = [m_all[batch:n_c, i * d:(i + 1) * d] for i in range(6)]

    xp = x_prompt.reshape(batch * seq, d)
    xs = jnp.pad(x_sample, ((0, 0), (0, R - n_real), (0, 0))).reshape(n_seq * R, d)
    cos_p, sin_p = _rope_tables(jnp.arange(seq))
    cos_s, sin_s = _rope_tables(PAST_LEN + jnp.tile(jnp.arange(R), n_seq))
    sink_rows = jnp.repeat(sinks, R).reshape(SWA_HEADS * R, 1)

    yp, glr_p = _inproj(xp, mp[0], mp[1], g1, w_main, w_glr, cos_p, sin_p, gq, gk, seg,
                        BF16, False, tiles_per_seq)
    ys, glr_s = _inproj(xs, ms[0], ms[1], g1, w_main, w_glr, cos_s, sin_s, gq, gk, seg, F32, True, 1)
    og_p, st_p = _gla_prompt(yp, glr_p, wgk, bgk, gnorm, batch, seq)
    og_s, st_s = _gla_sample(ys, glr_s, wgk, bgk, gnorm, state_gla[0], n_seq, n_real)
    os_p, kc_p, vc_p = _swa_prompt(yp, sinks, batch, seq)
    os_s, kc_s, vc_s = _swa_sample(ys, sink_rows, cache_swa_k[0].reshape(n_seq, WINDOW, kvw),
                                   cache_swa_v[0].reshape(n_seq, WINDOW, kvw), n_seq, n_real)
    x1_p, h2_p, sl_p, gt_p, cnt_p = _post(yp, og_p, os_p, xp, mp[2], mp[3], mp[4], g2, wo, wr, br,
                                          False, tiles_per_seq, R)
    x1_s, h2_s, sl_s, gt_s, cnt_s = _post(ys, og_s, os_s, xs, ms[2], ms[3], ms[4], g2, wo, wr, br,
                                          True, 1, n_real)

    cnt = jnp.concatenate([cnt_p, cnt_s], axis=0)[:, 0, :N_EXPERTS].astype(jnp.int32)
    n_tiles = cnt.shape[0]
    rows_bound = (TOP_K * (batch * seq + n_seq * n_real) + n_tiles * N_EXPERTS * (ROW_UNIT - 1)
                  + N_EXPERTS * (EXPERT_TILE - 1))
    rows_max = -(-rows_bound // EXPERT_TILE) * EXPERT_TILE
    (n_units, seg, n_tail, tail_hbm,
     tile_group, group_expert, n_groups, n_used, tile_rows) = _route_tables(cnt, rows_max)
    xg = _dispatch(h2_p, sl_p, h2_s, sl_s, n_units, seg, n_tail, tail_hbm, rows_max)
    yg = _experts(xg, tile_group, group_expert, n_groups, n_used, tile_rows,
                  w_gate_up[0], bgu, w_down[0], bd)
    p_tiles = batch * seq // TOKEN_TILE
    out_p = _combine(yg, sl_p, gt_p, x1_p, mp[5], n_units, seg, 0, False, tiles_per_seq)
    out_s = _combine(yg, sl_s, gt_s, x1_s, ms[5], n_units, seg, p_tiles, True, 1)

    cache_shape = (WINDOW, SWA_KV_HEADS, HEAD_DIM)
    return (out_p.reshape(batch, seq, d),
            out_s.reshape(n_seq, R, d)[:, :n_real],
            st_p[None],
            kc_p.reshape(1, batch, *cache_shape),
            vc_p.reshape(1, batch, *cache_shape),
            st_s[None],
            kc_s.reshape(1, n_seq, *cache_shape),
            vc_s.reshape(1, n_seq, *cache_shape))
```

```python
import functools

import numpy as np
import jax
import jax.numpy as jnp
from jax import lax
from jax.experimental import pallas as pl
from jax.experimental.pallas import tpu as pltpu

F32 = jnp.float32
BF16 = jnp.bfloat16
HIGHEST = lax.Precision.HIGHEST

D_MODEL = 1024
PAST_LEN = 16384
GLA_HEADS = 4
GLA_DK = 128
GLA_DV = 256
GATE_RANK = 16
GATE_TAU = 16.0
SWA_HEADS = 16
SWA_KV_HEADS = 4
HEAD_DIM = 64
SWA_GROUP = SWA_HEADS // SWA_KV_HEADS
WINDOW = 128
ROPE_THETA = 10000.0
N_EXPERTS = 32
TOP_K = 4
D_FF = 1024
SWIGLU_ALPHA = 1.702
SWIGLU_LIMIT = 7.0
NORM_EPS = 1e-6

LANES = 128
SUBLANES = 8
SAMPLE_ROWS = 8
SAMPLE_SEQS = 16
TOKEN_TILE = 512
SWA_STEP_BLOCKS = 4
GLA_BLOCK = 512
GLA_MASK_BLOCK = 128
VMEM_LIMIT = 56 * 1024 * 1024
ROW_UNIT = 16
SLOT_CHUNK = 512
EXPERT_TILE = 1024
EXPERT_PART = 256
LOCAL_ROWS = -(-(TOP_K * TOKEN_TILE + N_EXPERTS * (ROW_UNIT - 1)) // TOKEN_TILE) * TOKEN_TILE

COL_QG, COL_KG, COL_VG, COL_RG = 0, 512, 1024, 2048
COL_QS, COL_KS, COL_VS, COL_AG, COL_AS = 3072, 4096, 4352, 4608, 5632
D_MAIN = 6656
PROJ_CHUNK = 512
PROJ_AHEAD = 2


def _dot(a, b, precision=None):
    return jnp.dot(a, b, preferred_element_type=F32, precision=precision)


def _dot_nt(a, b, precision=None):
    return lax.dot_general(a, b, (((1,), (1,)), ((), ())), preferred_element_type=F32, precision=precision)


def _dot_tn(a, b, precision=None):
    return lax.dot_general(a, b, (((0,), (0,)), ((), ())), preferred_element_type=F32, precision=precision)


def _dot_sum(sel, x):
    hi = x.astype(BF16)
    r1 = x - hi.astype(F32)
    mid = r1.astype(BF16)
    lo = (r1 - mid.astype(F32)).astype(BF16)
    return _dot(sel, hi) + _dot(sel, mid) + _dot(sel, lo)


def _params(*sem):
    return pltpu.CompilerParams(dimension_semantics=sem, vmem_limit_bytes=VMEM_LIMIT)


def _rms(x, g):
    return x * lax.rsqrt(jnp.mean(x * x, axis=-1, keepdims=True) + NORM_EPS) * g


def _log_sigmoid(x):
    return jnp.minimum(x, 0.0) - jnp.log(1.0 + jnp.exp(-jnp.abs(x)))


def _ada_kernel(c_ref, w_ref, b_ref, o_ref):
    c = c_ref[...]
    s = c * jax.nn.sigmoid(c)
    s_hi = s.astype(BF16)
    s_lo = (s - s_hi.astype(F32)).astype(BF16)
    w = w_ref[...]
    w_hi = w.astype(BF16)
    w_lo = (w - w_hi.astype(F32)).astype(BF16)
    o_ref[...] = _dot(s_hi, w_hi) + (_dot(s_lo, w_hi) + _dot(s_hi, w_lo)) + b_ref[...]


def _ada(c_all, w_ada, b_ada):
    rows = c_all.shape[0]
    tn = 1536
    return pl.pallas_call(
        _ada_kernel,
        grid=(6 * D_MODEL // tn,),
        in_specs=[pl.BlockSpec((rows, D_MODEL), lambda j: (0, 0)),
                  pl.BlockSpec((D_MODEL, tn), lambda j: (0, j)),
                  pl.BlockSpec((1, tn), lambda j: (0, j))],
        out_specs=pl.BlockSpec((rows, tn), lambda j: (0, j)),
        out_shape=jax.ShapeDtypeStruct((rows, 6 * D_MODEL), F32),
        compiler_params=_params("parallel"),
        name="ada",
    )(c_all, w_ada, b_ada.reshape(1, -1))


def _inproj_kernel(x_ref, shift_ref, scale_ref, g_ref, w_ref, wg_ref, cos_ref, sin_ref, gq_ref, gk_ref,
                   seg_ref, o_ref, og_ref):
    rows = x_ref.shape[0]
    h = _rms(x_ref[...], g_ref[...]) * (1.0 + _mod_rows(scale_ref, rows)) + _mod_rows(shift_ref, rows)
    hb = h.astype(BF16)
    cos, sin, seg = cos_ref[...], sin_ref[...], seg_ref[...]
    def followup(c0):
        if COL_QS <= c0 < COL_QS + SWA_HEADS * HEAD_DIM:
            return "query"
        if COL_KS <= c0 < COL_KS + SWA_KV_HEADS * HEAD_DIM:
            return "key"
        if COL_AG <= c0 < COL_AG + D_MODEL or COL_AS <= c0 < COL_AS + D_MODEL:
            return "gate"
        return None

    def has_followup(j):
        return any(followup(c0) for c0 in range(j * PROJ_CHUNK, (j + 1) * PROJ_CHUNK, LANES))

    chunks = range(D_MAIN // PROJ_CHUNK)
    busy = [j for j in chunks if has_followup(j)]
    plain = [j for j in chunks if not has_followup(j)]
    order = [j for pair in zip(busy, plain) for j in pair] + busy[len(plain):] + plain[len(busy):]
    project = lambda j: _dot(hb, w_ref[:, j * PROJ_CHUNK:(j + 1) * PROJ_CHUNK])
    ahead = [project(j) for j in order[:PROJ_AHEAD]]
    for idx, j in enumerate(order):
        sl = slice(j * PROJ_CHUNK, (j + 1) * PROJ_CHUNK)
        r = ahead.pop(0)
        if idx + PROJ_AHEAD < len(order):
            ahead.append(project(order[idx + PROJ_AHEAD]))
        blocks = []
        for m in range(PROJ_CHUNK // LANES):
            c0 = j * PROJ_CHUNK + m * LANES
            blk = r[:, m * LANES:(m + 1) * LANES]
            kind = followup(c0)
            if kind == "query":
                blk = _head_norm_rope(blk, gq_ref[...], cos, sin, seg) * (HEAD_DIM ** -0.5)
            elif kind == "key":
                blk = _head_norm_rope(blk, gk_ref[...], cos, sin, seg)
            elif kind == "gate":
                blk = jax.nn.sigmoid(blk)
            blocks.append(blk)
        o_ref[:, sl] = jnp.concatenate(blocks, axis=1).astype(o_ref.dtype)
    og_ref[...] = _dot(hb, wg_ref[...])


def _mod_spec(per_token, tiles_per_seq):
    if per_token:
        return pl.BlockSpec((TOKEN_TILE // SAMPLE_ROWS, D_MODEL), lambda i, *_: (i, 0))
    return pl.BlockSpec((None, 1, D_MODEL), lambda i, *_: (i // tiles_per_seq, 0, 0))


def _mod_rows(ref, rows):
    m = ref[...]
    if m.shape[0] == 1:
        return m
    return jnp.broadcast_to(m[:, None, :], (m.shape[0], rows // m.shape[0], m.shape[1])).reshape(
        rows, m.shape[1])


def _inproj(x, shift, scale, g, w_main, w_glr, cos, sin, gq, gk, seg, out_dtype, per_token, tiles_per_seq):
    n = x.shape[0]
    mod = _mod_spec(per_token, tiles_per_seq)
    const = lambda i: (0, 0)
    rope = pl.BlockSpec((TOKEN_TILE, LANES), (lambda i: (i, 0)) if per_token else
                        (lambda i: (i % tiles_per_seq, 0)))
    return pl.pallas_call(
        _inproj_kernel,
        grid=(n // TOKEN_TILE,),
        in_specs=[pl.BlockSpec((TOKEN_TILE, D_MODEL), lambda i: (i, 0)), mod, mod,
                  pl.BlockSpec((1, D_MODEL), const),
                  pl.BlockSpec((D_MODEL, D_MAIN), const, pipeline_mode=pl.Buffered(1)),
                  pl.BlockSpec((D_MODEL, LANES), const, pipeline_mode=pl.Buffered(1)),
                  rope, rope,
                  pl.BlockSpec((1, LANES), const), pl.BlockSpec((1, LANES), const),
                  pl.BlockSpec((LANES, LANES), const)],
        out_specs=[pl.BlockSpec((TOKEN_TILE, D_MAIN), lambda i: (i, 0)),
                   pl.BlockSpec((TOKEN_TILE, LANES), lambda i: (i, 0))],
        out_shape=[jax.ShapeDtypeStruct((n, D_MAIN), out_dtype),
                   jax.ShapeDtypeStruct((n, LANES), F32)],
        compiler_params=_params("parallel"),
        name="inproj",
    )(x, shift, scale, g, w_main, w_glr, cos, sin, gq, gk, seg)


def _gla_log_gate(glr, wgk_ref, bgk):
    g_hi = glr.astype(BF16)
    g_lo = (glr - g_hi.astype(F32)).astype(BF16)
    w_hi, w_lo = wgk_ref[0], wgk_ref[1]
    x = _dot(g_hi, w_hi) + (_dot(g_lo, w_hi) + _dot(g_hi, w_lo))
    return _log_sigmoid(x + bgk) * (1.0 / GATE_TAU)


def _gla_out(o, r, g):
    r = r.astype(F32)
    return _rms(o, g) * (r * jax.nn.sigmoid(r))


def _gla_pair_levels(n):
    t = np.arange(n)[:, None]
    s = np.arange(n)[None, :]
    x = t ^ s
    top = np.where(x > 0, 1 << np.floor(np.log2(np.maximum(x, 1))).astype(np.int64), 0)
    return np.where(s > t, -1, top).astype(np.int32)


def _gla_block_ref(b, h):
    n, w = b.shape
    if 2 * h == n:
        return jnp.broadcast_to(b[h - 1:h, :], (n, w))
    if h >= SUBLANES // 2:
        picked = b.reshape(n // (2 * h), 2 * h, w)[:, h - 1:h, :]
        return jnp.broadcast_to(picked, (n // (2 * h), 2 * h, w)).reshape(n, w)
    r = lax.broadcasted_iota(jnp.int32, (n, 1), 0) % (2 * h)
    out = b
    for d in range(1, h + 1):
        out = jnp.where(r == h - 1 + d, pltpu.roll(b, d, 0), out)
    for d in range(1, h):
        out = jnp.where(r == h - 1 - d, pltpu.roll(b, n - d, 0), out)
    return out


def _gla_prompt_kernel(q_ref, k_ref, v_ref, r_ref, glr_ref, wgk_ref, bgk_ref, g_ref, lev_ref,
                       o_ref, s_ref, st_ref):
    c = pl.program_id(1)

    @pl.when(c == 0)
    def _():
        st_ref[...] = jnp.zeros_like(st_ref)

    n = GLA_BLOCK
    lev = lev_ref[...]
    causal = lax.broadcasted_iota(jnp.int32, (n, n), 1) <= lax.broadcasted_iota(jnp.int32, (n, n), 0)
    lg = _gla_log_gate(glr_ref[...], wgk_ref, bgk_ref[...])
    b = _dot_sum(jnp.where(causal, 1.0, 0.0).astype(BF16), lg)
    b_last = b[n - 1:n, :]
    q = q_ref[...].astype(F32) * (GLA_DK ** -0.5)
    k = k_ref[...].astype(F32)
    qe = (q * jnp.exp(b)).astype(BF16)
    kd = (k * jnp.exp(b_last - b)).astype(BF16)
    decay = jnp.exp(b_last)
    levels = [0] + [1 << p for p in range(n.bit_length() - 1)]
    q_lv, k_lv = [q.astype(BF16)], [k.astype(BF16)]
    for h in levels[1:]:
        e = jnp.exp(-jnp.abs(b - _gla_block_ref(b, h)))
        q_lv.append((q * e).astype(BF16))
        k_lv.append((k * e).astype(BF16))
    scaled = {level: (ql, kl) for level, ql, kl in zip(levels, q_lv, k_lv)}

    def intra(lo, hi, dk, v):
        size = hi - lo
        if size == GLA_MASK_BLOCK:
            attn = jnp.zeros((size, size), F32)
            for level in levels:
                if level < size:
                    ql, kl = scaled[level]
                    attn = jnp.where(lev == level, _dot_nt(ql[lo:hi, dk], kl[lo:hi, dk]), attn)
            return _dot(attn.astype(BF16), v[lo:hi, :])
        mid = lo + size // 2
        ql, kl = scaled[size // 2]
        cross = _dot_nt(ql[mid:hi, dk], kl[lo:mid, dk]).astype(BF16)
        return jnp.concatenate([intra(lo, mid, dk, v),
                                _dot(cross, v[lo:mid, :]) + intra(mid, hi, dk, v)], axis=0)

    for h in range(GLA_HEADS):
        dk = slice(h * GLA_DK, (h + 1) * GLA_DK)
        dv = slice(h * GLA_DV, (h + 1) * GLA_DV)
        v = v_ref[:, dv]
        st = st_ref[h]
        o = _dot_nt(qe[:, dk], st.astype(BF16)) + intra(0, n, dk, v)
        st_ref[h] = st * decay[:, dk] + _dot_tn(v, kd[:, dk])
        o_ref[:, dv] = _gla_out(o, r_ref[:, dv], g_ref[...]).astype(o_ref.dtype)

    @pl.when(c == pl.num_programs(1) - 1)
    def _():
        for h in range(GLA_HEADS):
            s_ref[h] = st_ref[h].T


def _gla_prompt(yp, glr, wgk, bgk, gnorm, batch, seq):
    nb = seq // GLA_BLOCK
    hk, hv = GLA_HEADS * GLA_DK, GLA_HEADS * GLA_DV
    tok = lambda b, c: b * nb + c
    const = lambda b, c: (0, 0)
    return pl.pallas_call(
        _gla_prompt_kernel,
        grid=(batch, nb),
        in_specs=[pl.BlockSpec((GLA_BLOCK, hk), lambda b, c: (tok(b, c), COL_QG // hk)),
                  pl.BlockSpec((GLA_BLOCK, hk), lambda b, c: (tok(b, c), COL_KG // hk)),
                  pl.BlockSpec((GLA_BLOCK, hv), lambda b, c: (tok(b, c), COL_VG // hv)),
                  pl.BlockSpec((GLA_BLOCK, hv), lambda b, c: (tok(b, c), COL_RG // hv)),
                  pl.BlockSpec((GLA_BLOCK, LANES), lambda b, c: (tok(b, c), 0)),
                  pl.BlockSpec((2, LANES, hk), lambda b, c: (0, 0, 0)),
                  pl.BlockSpec((1, hk), const),
                  pl.BlockSpec((1, GLA_DV), const),
                  pl.BlockSpec((GLA_MASK_BLOCK, GLA_MASK_BLOCK), const)],
        out_specs=[pl.BlockSpec((GLA_BLOCK, hv), lambda b, c: (tok(b, c), 0)),
                   pl.BlockSpec((None, GLA_HEADS, GLA_DK, GLA_DV), lambda b, c: (b, 0, 0, 0))],
        out_shape=[jax.ShapeDtypeStruct((batch * seq, hv), BF16),
                   jax.ShapeDtypeStruct((batch, GLA_HEADS, GLA_DK, GLA_DV), F32)],
        scratch_shapes=[pltpu.VMEM((GLA_HEADS, GLA_DV, GLA_DK), F32)],
        compiler_params=_params("parallel", "arbitrary"),
        name="gla_prompt",
    )(yp, yp, yp, yp, glr, wgk, bgk, gnorm, jnp.asarray(_gla_pair_levels(GLA_MASK_BLOCK)))


def _gla_sample_kernel(q_ref, k_ref, v_ref, r_ref, glr_ref, wgk_ref, bgk_ref, g_ref, s0_ref, lev_ref,
                       o_ref, s_ref, *, n_real):
    R = SAMPLE_ROWS
    rows = q_ref.shape[0]
    row = lax.broadcasted_iota(jnp.int32, (rows, rows), 0)
    col = lax.broadcasted_iota(jnp.int32, (rows, rows), 1)
    same = (row // R) == (col // R)
    causal = same & (col <= row)
    real = lax.broadcasted_iota(jnp.int32, (rows, 1), 0) % R < n_real
    lg = jnp.where(real, _gla_log_gate(glr_ref[...], wgk_ref, bgk_ref[...]), 0.0)
    b = _dot_sum(jnp.where(causal, 1.0, 0.0).astype(BF16), lg)
    b_last = _dot_sum(jnp.where(same, 1.0, 0.0).astype(BF16), lg)
    q = q_ref[...] * (GLA_DK ** -0.5)
    k = jnp.where(real, k_ref[...], 0.0)
    v = v_ref[...]
    qe = q * jnp.exp(b)
    kd = k * jnp.exp(b_last - b)
    lev = lev_ref[...]
    levels = [0] + [1 << p for p in range(R.bit_length() - 1)]
    q_lv, k_lv = [q.astype(BF16)], [k.astype(BF16)]
    for h in levels[1:]:
        e = jnp.exp(-jnp.abs(b - _gla_block_ref(b, h)))
        q_lv.append((q * e).astype(BF16))
        k_lv.append((k * e).astype(BF16))
    v_bf = v.astype(BF16)
    for h in range(GLA_HEADS):
        dk = slice(h * GLA_DK, (h + 1) * GLA_DK)
        dv = slice(h * GLA_DV, (h + 1) * GLA_DV)
        attn = jnp.zeros((rows, rows), F32)
        for level, ql, kl in zip(levels, q_lv, k_lv):
            attn = jnp.where(lev == level, _dot_nt(ql[:, dk], kl[:, dk]), attn)
        o_intra = _dot(attn.astype(BF16), v_bf[:, dv])
        decay_t = jnp.exp(b_last[:, dk]).T
        outs = []
        for s in range(rows // R):
            sl = slice(s * R, (s + 1) * R)
            s0 = s0_ref[s, h]
            outs.append(_dot(qe[sl, dk], s0) + o_intra[sl, :])
            s_ref[s, h] = s0 * decay_t[:, s * R:s * R + 1] + _dot_tn(kd[sl, dk], v[sl, dv])
        o_ref[:, dv] = _gla_out(jnp.concatenate(outs, axis=0), r_ref[:, dv], g_ref[...])


def _gla_sample(ys, glr, wgk, bgk, gnorm, state, n_seq, n_real):
    R = SAMPLE_SEQS * SAMPLE_ROWS
    hk, hv = GLA_HEADS * GLA_DK, GLA_HEADS * GLA_DV
    st_spec = pl.BlockSpec((SAMPLE_SEQS, GLA_HEADS, GLA_DK, GLA_DV), lambda b: (b, 0, 0, 0))
    lev = _gla_pair_levels(R)
    lev = np.where(lev >= SAMPLE_ROWS, -1, lev)
    return pl.pallas_call(
        functools.partial(_gla_sample_kernel, n_real=n_real),
        grid=(n_seq // SAMPLE_SEQS,),
        in_specs=[pl.BlockSpec((R, hk), lambda b: (b, COL_QG // hk)),
                  pl.BlockSpec((R, hk), lambda b: (b, COL_KG // hk)),
                  pl.BlockSpec((R, hv), lambda b: (b, COL_VG // hv)),
                  pl.BlockSpec((R, hv), lambda b: (b, COL_RG // hv)),
                  pl.BlockSpec((R, LANES), lambda b: (b, 0)),
                  pl.BlockSpec((2, LANES, hk), lambda b: (0, 0, 0)),
                  pl.BlockSpec((1, hk), lambda b: (0, 0)),
                  pl.BlockSpec((1, GLA_DV), lambda b: (0, 0)),
                  st_spec,
                  pl.BlockSpec((R, R), lambda b: (0, 0))],
        out_specs=[pl.BlockSpec((R, hv), lambda b: (b, 0)), st_spec],
        out_shape=[jax.ShapeDtypeStruct((n_seq * SAMPLE_ROWS, hv), F32),
                   jax.ShapeDtypeStruct((n_seq, GLA_HEADS, GLA_DK, GLA_DV), F32)],
        compiler_params=_params("parallel"),
        name="gla_sample",
    )(ys, ys, ys, ys, glr, wgk, bgk, gnorm, state, jnp.asarray(lev))


def _lane_lower(shape):
    return lax.broadcasted_iota(jnp.int32, shape, len(shape) - 1) % LANES < HEAD_DIM


def _head_norm_rope(x, g, cos, sin, seg):
    ss = _dot((x * x).astype(BF16), seg)
    y = x * lax.rsqrt(ss * (1.0 / HEAD_DIM) + NORM_EPS) * g
    half = HEAD_DIM // 2
    lane = lax.broadcasted_iota(jnp.int32, y.shape, 1)
    rot = jnp.where(lane % HEAD_DIM < half, pltpu.roll(y, LANES - half, 1), pltpu.roll(y, half, 1))
    return y * cos + rot * sin


def _both_halves(blk, half):
    sw = pltpu.roll(blk, HEAD_DIM, 1)
    lower = _lane_lower(blk.shape)
    return jnp.where(lower, blk, sw) if half == 0 else jnp.where(lower, sw, blk)


def _stack_heads(q_blocks):
    parts = []
    for qb in q_blocks:
        lower = _lane_lower(qb.shape)
        zero = jnp.zeros_like(qb)
        parts += [jnp.where(lower, qb, zero), jnp.where(lower, zero, qb)]
    return jnp.concatenate(parts, axis=0)


def _swa_prompt_kernel(sink_ref, q_ref, k_ref, v_ref, o_ref, ko_ref, vo_ref, kprev_ref, vprev_ref):
    n = pl.program_id(1)
    W = WINDOW

    @pl.when(n == 0)
    def _():
        kprev_ref[...] = jnp.zeros_like(kprev_ref)
        vprev_ref[...] = jnp.zeros_like(vprev_ref)

    def per_head(x):
        return [_both_halves(x[:, (kh // 2) * LANES:(kh // 2 + 1) * LANES], kh % 2).astype(BF16)
                for kh in range(SWA_KV_HEADS)]

    kb_prev = [kprev_ref[kh] for kh in range(SWA_KV_HEADS)]
    vb_prev = [vprev_ref[kh] for kh in range(SWA_KV_HEADS)]
    for sub in range(SWA_STEP_BLOCKS):
        tok = slice(sub * W, (sub + 1) * W)
        k_cur = k_ref[tok, :].astype(F32)
        v_cur = v_ref[tok, :].astype(F32)
        kb_cur, vb_cur = per_head(k_cur), per_head(v_cur)
        prev_fill = jnp.where(n > 0, 0.0, -jnp.inf) if sub == 0 else 0.0
        _swa_block(sink_ref, q_ref, o_ref, tok, kb_prev, vb_prev, kb_cur, vb_cur, prev_fill)
        kb_prev, vb_prev = kb_cur, vb_cur
    ko_ref[...] = k_cur
    vo_ref[...] = v_cur
    for kh in range(SWA_KV_HEADS):
        kprev_ref[kh] = kb_prev[kh]
        vprev_ref[kh] = vb_prev[kh]


def _swa_block(sink_ref, q_ref, o_ref, tok, kb_prev_all, vb_prev_all, kb_cur_all, vb_cur_all, prev_fill):
    W = WINDOW
    qi = lax.broadcasted_iota(jnp.int32, (W, W), 0)
    ki = lax.broadcasted_iota(jnp.int32, (W, W), 1)
    from_cur = ki <= qi
    for kh in range(SWA_KV_HEADS):
        kb_prev, kb_cur = kb_prev_all[kh], kb_cur_all[kh]
        vb_prev, vb_cur = vb_prev_all[kh], vb_cur_all[kh]
        qblocks = [q_ref[tok, (2 * kh + j) * LANES:(2 * kh + j + 1) * LANES] for j in range(2)]
        qs = _stack_heads(qblocks)
        s_prev = _dot_nt(qs, kb_prev)
        s_cur = _dot_nt(qs, kb_cur)
        outs = []
        for g in range(SWA_GROUP):
            rows = slice(g * W, (g + 1) * W)
            sg = jnp.where(from_cur, s_cur[rows, :], s_prev[rows, :] + prev_fill)
            sink = sink_ref[kh * SWA_GROUP + g]
            m = jnp.maximum(jnp.max(sg, axis=-1, keepdims=True), sink)
            p = jnp.exp(sg - m)
            denom = jnp.sum(p, axis=-1, keepdims=True) + jnp.exp(sink - m)
            p_cur = jnp.where(from_cur, p, 0.0).astype(BF16)
            p_prev = jnp.where(from_cur, 0.0, p).astype(BF16)
            outs.append((_dot(p_prev, vb_prev) + _dot(p_cur, vb_cur)) / denom)
        lower = _lane_lower((W, LANES))
        for j in range(2):
            c0 = (2 * kh + j) * LANES
            o_ref[tok, c0:c0 + LANES] = jnp.where(lower, outs[2 * j], outs[2 * j + 1]).astype(o_ref.dtype)


def _swa_prompt(yp, sinks, batch, seq):
    rows = SWA_STEP_BLOCKS * WINDOW
    nb = seq // rows
    kvw = SWA_KV_HEADS * HEAD_DIM
    tok = lambda b, n: b * nb + n
    cache_spec = pl.BlockSpec((None, WINDOW, kvw), lambda b, n: (b, 0, 0))
    return pl.pallas_call(
        _swa_prompt_kernel,
        grid=(batch, nb),
        in_specs=[pl.BlockSpec(memory_space=pltpu.SMEM),
                  pl.BlockSpec((rows, D_MODEL), lambda b, n: (tok(b, n), COL_QS // D_MODEL)),
                  pl.BlockSpec((rows, kvw), lambda b, n: (tok(b, n), COL_KS // kvw)),
                  pl.BlockSpec((rows, kvw), lambda b, n: (tok(b, n), COL_VS // kvw))],
        out_specs=[pl.BlockSpec((rows, D_MODEL), lambda b, n: (tok(b, n), 0)), cache_spec, cache_spec],
        out_shape=[jax.ShapeDtypeStruct((batch * seq, D_MODEL), BF16),
                   jax.ShapeDtypeStruct((batch, WINDOW, kvw), F32),
                   jax.ShapeDtypeStruct((batch, WINDOW, kvw), F32)],
        scratch_shapes=[pltpu.VMEM((SWA_KV_HEADS, WINDOW, LANES), BF16),
                        pltpu.VMEM((SWA_KV_HEADS, WINDOW, LANES), BF16)],
        compiler_params=_params("parallel", "arbitrary"),
        name="swa_prompt",
    )(sinks, yp, yp, yp)


def _shift_cache(cache, new, n_real):
    R = SAMPLE_ROWS
    rolled = pltpu.roll(cache, WINDOW - n_real, 0)
    tail_new = pltpu.roll(new, R - n_real, 0)
    row = lax.broadcasted_iota(jnp.int32, (R, cache.shape[1]), 0)
    tail = jnp.where(row < R - n_real, rolled[WINDOW - R:, :], tail_new)
    return jnp.concatenate([rolled[:WINDOW - R, :], tail], axis=0)


def _swa_sample_kernel(q_ref, k_ref, v_ref, kc_ref, vc_ref, sink_ref, o_ref, ko_ref, vo_ref, *, n_real):
    R, W = SAMPLE_ROWS, WINDOW
    k_new = k_ref[...]
    v_new = v_ref[...]
    q_pairs = [q_ref[:, j * LANES:(j + 1) * LANES] for j in range(SWA_HEADS // 2)]

    hr = SWA_HEADS * R
    t_c = lax.broadcasted_iota(jnp.int32, (hr, W), 0) % R
    mask_c = lax.broadcasted_iota(jnp.int32, (hr, W), 1) > t_c
    t_n = lax.broadcasted_iota(jnp.int32, (hr, R), 0) % R
    mask_n = lax.broadcasted_iota(jnp.int32, (hr, R), 1) <= t_n
    sink = sink_ref[...]
    lower = _lane_lower((R, LANES))
    zeros = jnp.zeros((R, LANES), F32)
    for s in range(q_ref.shape[0] // R):
        sl = slice(s * R, (s + 1) * R)
        kc, vc = kc_ref[s], vc_ref[s]
        kn, vn = k_new[sl, :], v_new[sl, :]
        ko_ref[s] = _shift_cache(kc, kn, n_real)
        vo_ref[s] = _shift_cache(vc, vn, n_real)
        q_rows = []
        for h in range(SWA_HEADS):
            kh = h // SWA_GROUP
            x = q_pairs[h // 2][sl, :]
            if h % 2 != kh % 2:
                x = pltpu.roll(x, HEAD_DIM, 1)
            x = jnp.where(lower, x, zeros) if kh % 2 == 0 else jnp.where(lower, zeros, x)
            q_rows.append(jnp.concatenate([x, zeros] if kh // 2 == 0 else [zeros, x], axis=1))
        qbd = jnp.concatenate(q_rows, axis=0)
        sc = jnp.where(mask_c, _dot_nt(qbd.astype(BF16), kc.astype(BF16)), -jnp.inf)
        sn = jnp.where(mask_n, _dot_nt(qbd, kn), -jnp.inf)
        m = jnp.maximum(jnp.maximum(jnp.max(sc, axis=-1, keepdims=True),
                                    jnp.max(sn, axis=-1, keepdims=True)), sink)
        pc, pn = jnp.exp(sc - m), jnp.exp(sn - m)
        denom = (jnp.sum(pc, axis=-1, keepdims=True) + jnp.sum(pn, axis=-1, keepdims=True)
                 + jnp.exp(sink - m))
        o = (_dot(pc.astype(BF16), vc.astype(BF16)) + _dot(pn, vn)) / denom
        for j in range(SWA_HEADS // 2):
            halves = []
            for h in (2 * j, 2 * j + 1):
                kh = h // SWA_GROUP
                y = o[h * R:(h + 1) * R, (kh // 2) * LANES:(kh // 2 + 1) * LANES]
                halves.append(pltpu.roll(y, HEAD_DIM, 1) if h % 2 != kh % 2 else y)
            o_ref[sl, j * LANES:(j + 1) * LANES] = jnp.where(lower, halves[0], halves[1])


def _swa_sample(ys, sink_rows, kcache, vcache, n_seq, n_real):
    R = SAMPLE_SEQS * SAMPLE_ROWS
    kvw = SWA_KV_HEADS * HEAD_DIM
    const = lambda b: (0, 0)
    cache_spec = pl.BlockSpec((SAMPLE_SEQS, WINDOW, kvw), lambda b: (b, 0, 0))
    return pl.pallas_call(
        functools.partial(_swa_sample_kernel, n_real=n_real),
        grid=(n_seq // SAMPLE_SEQS,),
        in_specs=[pl.BlockSpec((R, D_MODEL), lambda b: (b, COL_QS // D_MODEL)),
                  pl.BlockSpec((R, kvw), lambda b: (b, COL_KS // kvw)),
                  pl.BlockSpec((R, kvw), lambda b: (b, COL_VS // kvw)),
                  cache_spec, cache_spec,
                  pl.BlockSpec((SWA_HEADS * SAMPLE_ROWS, 1), const)],
        out_specs=[pl.BlockSpec((R, D_MODEL), lambda b: (b, 0)), cache_spec, cache_spec],
        out_shape=[jax.ShapeDtypeStruct((n_seq * SAMPLE_ROWS, D_MODEL), F32),
                   jax.ShapeDtypeStruct((n_seq, WINDOW, kvw), F32),
                   jax.ShapeDtypeStruct((n_seq, WINDOW, kvw), F32)],
        compiler_params=_params("parallel"),
        name="swa_sample",
    )(ys, ys, ys, kcache, vcache, sink_rows)


def _post_kernel(ag0_ref, ag1_ref, as0_ref, as1_ref, og_ref, os_ref, x_ref, gate_ref, shift_ref, scale_ref,
                 g2_ref, wo_ref, wr_ref, br_ref, x1_ref, h2_ref, slot_ref, gatek_ref, cnt_ref, *, n_valid):
    half = D_MODEL // 2
    merged = jnp.concatenate(
        [a_g[...].astype(F32) * og_ref[:, sl].astype(F32) + a_s[...].astype(F32) * os_ref[:, sl].astype(F32)
         for a_g, a_s, sl in ((ag0_ref, as0_ref, slice(0, half)), (ag1_ref, as1_ref, slice(half, D_MODEL)))],
        axis=1)
    y = _dot(merged.astype(BF16), wo_ref[...])
    rows = x_ref.shape[0]
    x1 = x_ref[...] + _mod_rows(gate_ref, rows) * y
    x1_ref[...] = x1
    h2 = _rms(x1, g2_ref[...]) * (1.0 + _mod_rows(scale_ref, rows)) + _mod_rows(shift_ref, rows)
    h2_hi = h2.astype(BF16)
    h2_ref[...] = h2_hi

    h2_lo = (h2 - h2_hi.astype(F32)).astype(BF16)
    w_hi, w_lo = wr_ref[0], wr_ref[1]
    logits = _dot(h2_hi, w_hi) + (_dot(h2_lo, w_hi) + _dot(h2_hi, w_lo)) + br_ref[...]
    lane_i = lax.broadcasted_iota(jnp.int32, logits.shape, 1)
    lane = lane_i.astype(F32)
    work = logits
    vals, hots = [], []
    for _ in range(TOP_K):
        m = jnp.max(work, axis=-1, keepdims=True)
        idx = jnp.min(jnp.where(work == m, lane, float(LANES)), axis=-1, keepdims=True)
        hot = lane == idx
        vals.append(m)
        hots.append(hot)
        work = jnp.where(hot, -jnp.inf, work)
    exps = [jnp.exp(v - vals[0]) for v in vals]
    denom = exps[0] + exps[1] + exps[2] + exps[3]

    tm = logits.shape[0]
    valid = lax.broadcasted_iota(jnp.int32, (tm, 1), 0) % SAMPLE_ROWS < n_valid
    sel = jnp.zeros_like(logits)
    for hot in hots:
        sel = jnp.where(hot, 1.0, sel)
    sel = jnp.where(valid, sel, 0.0)
    earlier = (lax.broadcasted_iota(jnp.int32, (tm, tm), 1)
               < lax.broadcasted_iota(jnp.int32, (tm, tm), 0))
    rank = _dot(jnp.where(earlier, 1.0, 0.0).astype(BF16), sel.astype(BF16))
    cnt = jnp.sum(sel, axis=0, keepdims=True)
    cnt_pad = jnp.floor((cnt + (ROW_UNIT - 1.0)) * (1.0 / ROW_UNIT)) * ROW_UNIT
    below = (lax.broadcasted_iota(jnp.int32, (LANES, LANES), 0)
             < lax.broadcasted_iota(jnp.int32, (LANES, LANES), 1))
    seg_start = _dot(jnp.broadcast_to(cnt_pad, (8, LANES)), jnp.where(below, 1.0, 0.0), HIGHEST)[0:1]
    pos = seg_start + rank
    slots = jnp.full_like(logits, -1.0)
    gates = jnp.zeros_like(logits)
    for k in range(TOP_K):
        s_k = jnp.sum(jnp.where(hots[k], pos, 0.0), axis=-1, keepdims=True)
        slots = jnp.where(lane_i == k, s_k, slots)
        gates = jnp.where(lane_i == k, exps[k] / denom, gates)
    slot_ref[...] = jnp.where(valid, slots, -1.0)
    gatek_ref[...] = gates
    cnt_ref[...] = cnt


def _post(y_all, o_gla, o_swa, x, gate, shift, scale, g2, wo, wr, br, per_token, tiles_per_seq, n_valid):
    n = x.shape[0]
    mod = _mod_spec(per_token, tiles_per_seq)
    row = lambda i: (i, 0)
    const = lambda i: (0, 0)
    wide = pl.BlockSpec((TOKEN_TILE, D_MODEL), row)
    narrow = pl.BlockSpec((TOKEN_TILE, LANES), row)
    half = D_MODEL // 2
    return pl.pallas_call(
        functools.partial(_post_kernel, n_valid=n_valid),
        grid=(n // TOKEN_TILE,),
        in_specs=[pl.BlockSpec((TOKEN_TILE, half), lambda i: (i, COL_AG // half)),
                  pl.BlockSpec((TOKEN_TILE, half), lambda i: (i, COL_AG // half + 1)),
                  pl.BlockSpec((TOKEN_TILE, half), lambda i: (i, COL_AS // half)),
                  pl.BlockSpec((TOKEN_TILE, half), lambda i: (i, COL_AS // half + 1)),
                  wide, wide, wide, mod, mod, mod,
                  pl.BlockSpec((1, D_MODEL), const),
                  pl.BlockSpec((D_MODEL, D_MODEL), const),
                  pl.BlockSpec((2, D_MODEL, LANES), lambda i: (0, 0, 0)),
                  pl.BlockSpec((1, LANES), const)],
        out_specs=[wide, wide, narrow, narrow, pl.BlockSpec((None, 1, LANES), lambda i: (i, 0, 0))],
        out_shape=[jax.ShapeDtypeStruct((n, D_MODEL), F32),
                   jax.ShapeDtypeStruct((n, D_MODEL), BF16),
                   jax.ShapeDtypeStruct((n, LANES), F32),
                   jax.ShapeDtypeStruct((n, LANES), F32),
                   jax.ShapeDtypeStruct((n // TOKEN_TILE, 1, LANES), F32)],
        compiler_params=_params("parallel"),
        name="post",
    )(y_all, y_all, y_all, y_all, o_gla, o_swa, x, gate, shift, scale, g2, wo, wr, br)


def _slot_matrix(slot_cols, weights, chunk):
    tm = slot_cols[0].shape[0]
    j = lax.broadcasted_iota(jnp.int32, (tm, SLOT_CHUNK), 1) + chunk * SLOT_CHUNK
    out = jnp.zeros((tm, SLOT_CHUNK), F32)
    for s, w in zip(slot_cols, weights):
        out = jnp.where(s == j, w, out)
    return out.astype(BF16)


def _dispatch_kernel(nu_ref, seg_ref, ntail_ref, tail_ref, hp_ref, slp_ref, hs_ref, sls_ref, xg_ref,
                     sorted_ref, zero_ref, sem, tail_sem, *, prompt_tiles):
    t = pl.program_id(0)
    last = pl.num_programs(0) - 1
    buf = t % 2

    def sort_tile(h_ref, slot_ref):
        slots = slot_ref[...].astype(jnp.int32)
        slot_cols = [slots[:, k:k + 1] for k in range(TOP_K)]
        h = h_ref[...]
        upc = SLOT_CHUNK // ROW_UNIT
        for c in range(LOCAL_ROWS // SLOT_CHUNK):
            onehot = _slot_matrix(slot_cols, [1.0] * TOP_K, c)
            sorted_ref[buf, c * upc:(c + 1) * upc] = _dot_tn(onehot, h).astype(BF16).reshape(
                upc, ROW_UNIT, D_MODEL)

    @pl.when(t < prompt_tiles)
    def _():
        sort_tile(hp_ref, slp_ref)

    @pl.when(t >= prompt_tiles)
    def _():
        sort_tile(hs_ref, sls_ref)

    def start_all(tile, b):
        def body(e, c):
            n = seg_ref[tile, 0, e]

            @pl.when(n > 0)
            def _():
                pltpu.make_async_copy(sorted_ref.at[b, pl.ds(seg_ref[tile, 1, e], n)],
                                      xg_ref.at[pl.ds(seg_ref[tile, 2, e], n)], sem.at[b]).start()
            return c

        lax.fori_loop(0, N_EXPERTS, body, 0)

    def wait_all(tile, b):
        n_units = nu_ref[tile]

        @pl.when(n_units > 0)
        def _():
            pltpu.make_async_copy(sorted_ref.at[b, pl.ds(0, n_units)],
                                  xg_ref.at[pl.ds(0, n_units)], sem.at[b]).wait()

    @pl.when(t > 0)
    def _():
        wait_all(t - 1, 1 - buf)

    start_all(t, buf)

    def tail_copy(i):
        return pltpu.make_async_copy(zero_ref, xg_ref.at[tail_ref[i]], tail_sem)

    @pl.when(t == last)
    def _():
        zero_ref[...] = jnp.zeros_like(zero_ref)
        n_tail = ntail_ref[0]
        lax.fori_loop(0, n_tail, lambda i, c: (tail_copy(i).start(), c)[1], 0)
        lax.fori_loop(0, n_tail, lambda i, c: (tail_copy(i).wait(), c)[1], 0)
        wait_all(t, buf)


def _dispatch(h2_p, slots_p, h2_s, slots_s, n_units, seg, n_tail, tail_dst, rows_max):
    p_tiles = h2_p.shape[0] // TOKEN_TILE
    s_tiles = h2_s.shape[0] // TOKEN_TILE
    p_row = lambda t, *_: (jnp.minimum(t, p_tiles - 1), 0)
    s_row = lambda t, *_: (jnp.maximum(t - p_tiles, 0), 0)
    return pl.pallas_call(
        functools.partial(_dispatch_kernel, prompt_tiles=p_tiles),
        grid_spec=pltpu.PrefetchScalarGridSpec(
            num_scalar_prefetch=4,
            grid=(p_tiles + s_tiles,),
            in_specs=[pl.BlockSpec((TOKEN_TILE, D_MODEL), p_row),
                      pl.BlockSpec((TOKEN_TILE, LANES), p_row),
                      pl.BlockSpec((TOKEN_TILE, D_MODEL), s_row),
                      pl.BlockSpec((TOKEN_TILE, LANES), s_row)],
            out_specs=pl.BlockSpec(memory_space=pl.ANY),
            scratch_shapes=[pltpu.VMEM((2, LOCAL_ROWS // ROW_UNIT, ROW_UNIT, D_MODEL), BF16),
                            pltpu.VMEM((ROW_UNIT, D_MODEL), BF16),
                            pltpu.SemaphoreType.DMA((2,)), pltpu.SemaphoreType.DMA]),
        out_shape=jax.ShapeDtypeStruct((rows_max // ROW_UNIT, ROW_UNIT, D_MODEL), BF16),
        compiler_params=_params("arbitrary"),
        name="dispatch",
    )(n_units, seg, n_tail, tail_dst, h2_p, slots_p, h2_s, slots_s).reshape(rows_max, D_MODEL)


def _expert_kernel(tg_ref, ge_ref, ng_ref, nused_ref, rows_ref, x_ref, bgu_ref, bd_ref, wgu_hbm, wd_hbm,
                   y_ref, wgu_f32, wd_f32, wgu_bf, wd_bf, sem):
    i = pl.program_id(0)

    def fetch(g, b):
        e = ge_ref[g]
        return (pltpu.make_async_copy(wgu_hbm.at[e], wgu_f32.at[b], sem.at[0, b]),
                pltpu.make_async_copy(wd_hbm.at[e], wd_f32.at[b], sem.at[1, b]))

    @pl.when(i == 0)
    def _():
        for cp in fetch(0, 0):
            cp.start()

    @pl.when(i < nused_ref[0])
    def _():
        g = tg_ref[i]
        b = g % 2

        @pl.when((i == 0) | (g != tg_ref[jnp.maximum(i - 1, 0)]))
        def _():
            @pl.when(g + 1 < ng_ref[0])
            def _():
                for cp in fetch(g + 1, 1 - b):
                    cp.start()

            for cp in fetch(g, b):
                cp.wait()
            wgu_bf[...] = wgu_f32[b].astype(BF16)
            wd_bf[...] = wd_f32[b].astype(BF16)

        for part in range(EXPERT_TILE // EXPERT_PART):
            @pl.when(rows_ref[i] > part * EXPERT_PART)
            def _():
                sl = slice(part * EXPERT_PART, (part + 1) * EXPERT_PART)
                gu = _dot(x_ref[sl, :], wgu_bf[...]) + bgu_ref[...]
                gate = jnp.minimum(gu[:, :D_FF], SWIGLU_LIMIT)
                up = jnp.clip(gu[:, D_FF:], -SWIGLU_LIMIT, SWIGLU_LIMIT)
                act = (up + 1.0) * gate * jax.nn.sigmoid(SWIGLU_ALPHA * gate)
                y_ref[sl, :] = (_dot(act.astype(BF16), wd_bf[...]) + bd_ref[...]).astype(y_ref.dtype)


def _experts(xg, tile_group, group_expert, n_groups, n_used, tile_rows, wgu, bgu, wd, bd):
    rows_max = xg.shape[0]
    used = lambda i, nu: jnp.maximum(jnp.minimum(i, nu[0] - 1), 0)
    row = lambda i, tg, ge, ng, nu, tr: (used(i, nu), 0)
    exp = lambda i, tg, ge, ng, nu, tr: (ge[tg[used(i, nu)]], 0, 0)
    return pl.pallas_call(
        _expert_kernel,
        grid_spec=pltpu.PrefetchScalarGridSpec(
            num_scalar_prefetch=5,
            grid=(rows_max // EXPERT_TILE,),
            in_specs=[pl.BlockSpec((EXPERT_TILE, D_MODEL), row),
                      pl.BlockSpec((None, 1, 2 * D_FF), exp),
                      pl.BlockSpec((None, 1, D_MODEL), exp),
                      pl.BlockSpec(memory_space=pl.ANY),
                      pl.BlockSpec(memory_space=pl.ANY)],
            out_specs=pl.BlockSpec((EXPERT_TILE, D_MODEL), row),
            scratch_shapes=[pltpu.VMEM((2, D_MODEL, 2 * D_FF), F32), pltpu.VMEM((2, D_FF, D_MODEL), F32),
                            pltpu.VMEM((D_MODEL, 2 * D_FF), BF16), pltpu.VMEM((D_FF, D_MODEL), BF16),
                            pltpu.SemaphoreType.DMA((2, 2))]),
        out_shape=jax.ShapeDtypeStruct((rows_max, D_MODEL), BF16),
        compiler_params=_params("arbitrary"),
        name="experts",
    )(tile_group, group_expert, n_groups, n_used, tile_rows, xg, bgu, bd, wgu, wd)


def _combine_kernel(nu_ref, seg_ref, slot_ref, gatek_ref, x_ref, gmlp_ref, y_ref, o_ref, ys_ref, sem,
                    *, tile_offset):
    j = pl.program_id(0)
    t = j + tile_offset
    buf = j % 2
    tm = x_ref.shape[0]

    def fetch(tile, b):
        def body(e, c):
            n = seg_ref[tile, 0, e]

            @pl.when(n > 0)
            def _():
                pltpu.make_async_copy(y_ref.at[pl.ds(seg_ref[tile, 2, e], n)],
                                      ys_ref.at[b, pl.ds(seg_ref[tile, 1, e], n)], sem.at[b]).start()
            return c

        lax.fori_loop(0, N_EXPERTS, body, 0)
        n_units = nu_ref[tile]

        def zero_unit(i, c):
            ys_ref[b, i] = jnp.zeros((ROW_UNIT, D_MODEL), ys_ref.dtype)
            return c

        lax.fori_loop(n_units, LOCAL_ROWS // ROW_UNIT, zero_unit, 0)

    @pl.when(j == 0)
    def _():
        fetch(t, buf)

    @pl.when(j + 1 < pl.num_programs(0))
    def _():
        fetch(t + 1, 1 - buf)

    n_units = nu_ref[t]

    @pl.when(n_units > 0)
    def _():
        pltpu.make_async_copy(y_ref.at[pl.ds(0, n_units)], ys_ref.at[buf, pl.ds(0, n_units)],
                              sem.at[buf]).wait()

    slots = slot_ref[...].astype(jnp.int32)
    gates = gatek_ref[...]
    slot_cols = [slots[:, k:k + 1] for k in range(TOP_K)]
    gate_cols = [gates[:, k:k + 1] for k in range(TOP_K)]
    acc = jnp.zeros((tm, D_MODEL), F32)
    upc = SLOT_CHUNK // ROW_UNIT
    for c in range(LOCAL_ROWS // SLOT_CHUNK):
        rows_c = ys_ref[buf, c * upc:(c + 1) * upc].reshape(SLOT_CHUNK, D_MODEL)
        acc = acc + _dot(_slot_matrix(slot_cols, gate_cols, c), rows_c)
    o_ref[...] = x_ref[...] + _mod_rows(gmlp_ref, tm) * acc


def _combine(y, slots, gates, x1, gmlp, n_units, seg, tile_offset, per_token, tiles_per_seq):
    n = x1.shape[0]
    mod = _mod_spec(per_token, tiles_per_seq)
    wide = pl.BlockSpec((TOKEN_TILE, D_MODEL), lambda i, *_: (i, 0))
    narrow = pl.BlockSpec((TOKEN_TILE, LANES), lambda i, *_: (i, 0))
    return pl.pallas_call(
        functools.partial(_combine_kernel, tile_offset=tile_offset),
        grid_spec=pltpu.PrefetchScalarGridSpec(
            num_scalar_prefetch=2,
            grid=(n // TOKEN_TILE,),
            in_specs=[narrow, narrow, wide, mod, pl.BlockSpec(memory_space=pl.ANY)],
            out_specs=wide,
            scratch_shapes=[pltpu.VMEM((2, LOCAL_ROWS // ROW_UNIT, ROW_UNIT, D_MODEL), BF16),
                            pltpu.SemaphoreType.DMA((2,))]),
        out_shape=jax.ShapeDtypeStruct((n, D_MODEL), F32),
        compiler_params=_params("arbitrary"),
        name="combine",
    )(n_units, seg, slots, gates, x1, gmlp, y.reshape(-1, ROW_UNIT, D_MODEL))


def _route_tables(cnt, rows_max):
    units = (cnt + ROW_UNIT - 1) // ROW_UNIT
    group_units = jnp.sum(units, axis=0)
    upt = EXPERT_TILE // ROW_UNIT
    group_pad = (group_units + upt - 1) // upt * upt
    group_end = jnp.cumsum(group_pad)
    group_start = group_end - group_pad
    seg_start = group_start[None, :] + jnp.cumsum(units, axis=0) - units
    local_end = jnp.cumsum(units, axis=1)
    local_start = local_end - units
    n_units = local_end[:, -1].astype(jnp.int32)

    def pick(lo, hi, pos, value):
        return jnp.sum(jnp.where((pos >= lo) & (pos < hi), value, 0), axis=-1).astype(jnp.int32)

    seg = jnp.stack([units, local_start, seg_start], axis=1).astype(jnp.int32)

    n_tail_e = group_pad - group_units
    j = jnp.arange(N_EXPERTS * upt, dtype=jnp.int32)[:, None]
    tail_end = jnp.cumsum(n_tail_e)
    tail_start = tail_end - n_tail_e
    tail_hbm = pick(tail_start[None, :], tail_end[None, :], j,
                    (group_start + group_units)[None, :] + j - tail_start[None, :])
    n_tail = tail_end[-1:].astype(jnp.int32)

    r = jnp.arange(rows_max // EXPERT_TILE, dtype=jnp.int32)[:, None] * upt
    n_used = (group_end[-1:] // upt).astype(jnp.int32)
    nonempty = group_units > 0
    group_of_expert = jnp.cumsum(nonempty) - 1
    tile_rows = pick(group_start[None, :], group_end[None, :], r,
                     jnp.clip(((group_start + group_units)[None, :] - r) * ROW_UNIT, 0, EXPERT_TILE))
    tile_group = pick(group_start[None, :], group_end[None, :], r, group_of_expert[None, :])
    g = jnp.arange(N_EXPERTS, dtype=jnp.int32)
    group_expert = jnp.sum(jnp.where(nonempty[None, :] & (group_of_expert[None, :] == g[:, None]),
                                     g[None, :], 0), axis=-1).astype(jnp.int32)
    n_groups = jnp.sum(nonempty)[None].astype(jnp.int32)
    return (n_units, seg, n_tail, tail_hbm,
            tile_group, group_expert, n_groups, n_used, tile_rows)


def _rope_tables(pos):
    half = HEAD_DIM // 2
    lane = np.arange(LANES)
    inv = jnp.asarray(ROPE_THETA, F32) ** (-jnp.asarray(lane % half, F32) / half)
    sign = jnp.asarray(np.where(lane % HEAD_DIM < half, -1.0, 1.0), F32)
    ang = pos.astype(F32)[:, None] * inv[None, :]
    return jnp.cos(ang), jnp.sin(ang) * sign[None, :]


def kernel(x_prompt, x_sample, c_prompt, c_sample, state_gla, cache_swa_k, cache_swa_v, w_ada, b_ada,
           norm1_g, norm2_g, w_in, w_gk2, b_gk, gla_norm_g, q_norm_g, k_norm_g, attn_sinks, w_o,
           w_router, b_router, w_gate_up, b_gate_up, w_down, b_down):
    batch, seq, d = x_prompt.shape
    n_seq, n_real, _ = x_sample.shape
    depth = w_in.shape[0]
    assert depth == 1 and d == D_MODEL and n_real <= SAMPLE_ROWS
    assert seq % TOKEN_TILE == 0 and (n_seq * SAMPLE_ROWS) % TOKEN_TILE == 0 and n_seq % SAMPLE_SEQS == 0
    R = SAMPLE_ROWS
    kvw = SWA_KV_HEADS * HEAD_DIM
    tiles_per_seq = seq // TOKEN_TILE

    glr_col = COL_QS
    w_main = jnp.concatenate([w_in[0, :, :glr_col], w_in[0, :, glr_col + GATE_RANK:]], axis=1).astype(BF16)
    w_glr = jnp.pad(w_in[0, :, glr_col:glr_col + GATE_RANK], ((0, 0), (0, LANES - GATE_RANK))).astype(BF16)
    wgk = jnp.pad(w_gk2[0], ((0, LANES - GATE_RANK), (0, 0)))
    wgk_hi = wgk.astype(BF16)
    wgk = jnp.stack([wgk_hi, (wgk - wgk_hi.astype(F32)).astype(BF16)])
    bgk = b_gk[0].reshape(1, -1)
    gnorm = gla_norm_g[0].reshape(1, -1)
    gq = jnp.tile(q_norm_g[0], LANES // HEAD_DIM).reshape(1, LANES)
    gk = jnp.tile(k_norm_g[0], LANES // HEAD_DIM).reshape(1, LANES)
    seg = jnp.asarray(np.kron(np.eye(LANES // HEAD_DIM), np.ones((HEAD_DIM, HEAD_DIM))), BF16)
    sinks = attn_sinks[0]
    wo = w_o[0].astype(BF16)
    wr = jnp.pad(w_router[0], ((0, 0), (0, LANES - N_EXPERTS)))
    wr_hi = wr.astype(BF16)
    wr = jnp.stack([wr_hi, (wr - wr_hi.astype(F32)).astype(BF16)])
    br = jnp.pad(b_router[0], (0, LANES - N_EXPERTS), constant_values=-1e30).reshape(1, LANES)
    bgu = b_gate_up[0].reshape(N_EXPERTS, 1, 2 * D_FF)
    bd = b_down[0].reshape(N_EXPERTS, 1, D_MODEL)
    g1 = norm1_g[0].reshape(1, -1)
    g2 = norm2_g[0].reshape(1, -1)

    n_c = batch + n_seq
    c_all = jnp.pad(jnp.concatenate([c_prompt, c_sample], axis=0), ((0, -n_c % 8), (0, 0)))
    m_all = _ada(c_all, w_ada[0], b_ada[0])
    mp = [m_all[:batch, i * d:(i + 1) * d].reshape(batch, 1, d) for i in range(6)]
    ms = [m_all[batch:n_c, i * d:(i + 1) * d] for i in range(6)]

    xp = x_prompt.reshape(batch * seq, d)
    xs = jnp.pad(x_sample, ((0, 0), (0, R - n_real), (0, 0))).reshape(n_seq * R, d)
    cos_p, sin_p = _rope_tables(jnp.arange(seq))
    cos_s, sin_s = _rope_tables(PAST_LEN + jnp.tile(jnp.arange(R), n_seq))
    sink_rows = jnp.repeat(sinks, R).reshape(SWA_HEADS * R, 1)

    yp, glr_p = _inproj(xp, mp[0], mp[1], g1, w_main, w_glr, cos_p, sin_p, gq, gk, seg,
                        BF16, False, tiles_per_seq)
    ys, glr_s = _inproj(xs, ms[0], ms[1], g1, w_main, w_glr, cos_s, sin_s, gq, gk, seg, F32, True, 1)
    og_p, st_p = _gla_prompt(yp, glr_p, wgk, bgk, gnorm, batch, seq)
    og_s, st_s = _gla_sample(ys, glr_s, wgk, bgk, gnorm, state_gla[0], n_seq, n_real)
    os_p, kc_p, vc_p = _swa_prompt(yp, sinks, batch, seq)
    os_s, kc_s, vc_s = _swa_sample(ys, sink_rows, cache_swa_k[0].reshape(n_seq, WINDOW, kvw),
                                   cache_swa_v[0].reshape(n_seq, WINDOW, kvw), n_seq, n_real)
    x1_p, h2_p, sl_p, gt_p, cnt_p = _post(yp, og_p, os_p, xp, mp[2], mp[3], mp[4], g2, wo, wr, br,
                                          False, tiles_per_seq, R)
    x1_s, h2_s, sl_s, gt_s, cnt_s = _post(ys, og_s, os_s, xs, ms[2], ms[3], ms[4], g2, wo, wr, br,
                                          True, 1, n_real)

    cnt = jnp.concatenate([cnt_p, cnt_s], axis=0)[:, 0, :N_EXPERTS].astype(jnp.int32)
    n_tiles = cnt.shape[0]
    rows_bound = (TOP_K * (batch * seq + n_seq * n_real) + n_tiles * N_EXPERTS * (ROW_UNIT - 1)
                  + N_EXPERTS * (EXPERT_TILE - 1))
    rows_max = -(-rows_bound // EXPERT_TILE) * EXPERT_TILE
    (n_units, seg, n_tail, tail_hbm,
     tile_group, group_expert, n_groups, n_used, tile_rows) = _route_tables(cnt, rows_max)
    xg = _dispatch(h2_p, sl_p, h2_s, sl_s, n_units, seg, n_tail, tail_hbm, rows_max)
    yg = _experts(xg, tile_group, group_expert, n_groups, n_used, tile_rows,
                  w_gate_up[0], bgu, w_down[0], bd)
    p_tiles = batch * seq // TOKEN_TILE
    out_p = _combine(yg, sl_p, gt_p, x1_p, mp[5], n_units, seg, 0, False, tiles_per_seq)
    out_s = _combine(yg, sl_s, gt_s, x1_s, ms[5], n_units, seg, p_tiles, True, 1)

    cache_shape = (WINDOW, SWA_KV_HEADS, HEAD_DIM)
    return (out_p.reshape(batch, seq, d),
            out_s.reshape(n_seq, R, d)[:, :n_real],
            st_p[None],
            kc_p.reshape(1, batch, *cache_shape),
            vc_p.reshape(1, batch, *cache_shape),
            st_s[None],
            kc_s.reshape(1, n_seq, *cache_shape),
            vc_s.reshape(1, n_seq, *cache_shape))
```

```python
import functools

import numpy as np
import jax
import jax.numpy as jnp
from jax import lax
from jax.experimental import pallas as pl
from jax.experimental.pallas import tpu as pltpu

F32 = jnp.float32
BF16 = jnp.bfloat16
HIGHEST = lax.Precision.HIGHEST

D_MODEL = 1024
PAST_LEN = 16384
GLA_HEADS = 4
GLA_DK = 128
GLA_DV = 256
GATE_RANK = 16
GATE_TAU = 16.0
SWA_HEADS = 16
SWA_KV_HEADS = 4
HEAD_DIM = 64
SWA_GROUP = SWA_HEADS // SWA_KV_HEADS
WINDOW = 128
ROPE_THETA = 10000.0
N_EXPERTS = 32
TOP_K = 4
D_FF = 1024
SWIGLU_ALPHA = 1.702
SWIGLU_LIMIT = 7.0
NORM_EPS = 1e-6

LANES = 128
SUBLANES = 8
SAMPLE_ROWS = 8
SAMPLE_SEQS = 16
TOKEN_TILE = 512
SWA_STEP_BLOCKS = 4
GLA_BLOCK = 512
GLA_MASK_BLOCK = 128
VMEM_LIMIT = 56 * 1024 * 1024
ROW_UNIT = 16
SLOT_CHUNK = 512
EXPERT_TILE = 1024
EXPERT_PART = 256
LOCAL_ROWS = -(-(TOP_K * TOKEN_TILE + N_EXPERTS * (ROW_UNIT - 1)) // TOKEN_TILE) * TOKEN_TILE

COL_VG, COL_RG, COL_QS, COL_AG, COL_AS = 0, 1024, 2048, 3072, 4096
COL_QG, COL_KG, COL_KS, COL_VS = 5120, 5632, 6144, 6400
D_MAIN = 6656
PROJ_CHUNK = 512
PROJ_AHEAD = 2


def _dot(a, b, precision=None):
    return jnp.dot(a, b, preferred_element_type=F32, precision=precision)


def _dot_nt(a, b, precision=None):
    return lax.dot_general(a, b, (((1,), (1,)), ((), ())), preferred_element_type=F32, precision=precision)


def _dot_tn(a, b, precision=None):
    return lax.dot_general(a, b, (((0,), (0,)), ((), ())), preferred_element_type=F32, precision=precision)


def _dot_sum(sel, x):
    hi = x.astype(BF16)
    r1 = x - hi.astype(F32)
    mid = r1.astype(BF16)
    lo = (r1 - mid.astype(F32)).astype(BF16)
    return _dot(sel, hi) + _dot(sel, mid) + _dot(sel, lo)


def _params(*sem):
    return pltpu.CompilerParams(dimension_semantics=sem, vmem_limit_bytes=VMEM_LIMIT)


def _rms(x, g):
    return x * lax.rsqrt(jnp.mean(x * x, axis=-1, keepdims=True) + NORM_EPS) * g


def _log_sigmoid(x):
    return jnp.minimum(x, 0.0) - jnp.log(1.0 + jnp.exp(-jnp.abs(x)))


def _ada_kernel(c_ref, w_ref, b_ref, o_ref):
    c = c_ref[...]
    s = c * jax.nn.sigmoid(c)
    s_hi = s.astype(BF16)
    s_lo = (s - s_hi.astype(F32)).astype(BF16)
    w = w_ref[...]
    w_hi = w.astype(BF16)
    w_lo = (w - w_hi.astype(F32)).astype(BF16)
    o_ref[...] = _dot(s_hi, w_hi) + (_dot(s_lo, w_hi) + _dot(s_hi, w_lo)) + b_ref[...]


def _ada(c_all, w_ada, b_ada):
    rows = c_all.shape[0]
    tn = 1536
    return pl.pallas_call(
        _ada_kernel,
        grid=(6 * D_MODEL // tn,),
        in_specs=[pl.BlockSpec((rows, D_MODEL), lambda j: (0, 0)),
                  pl.BlockSpec((D_MODEL, tn), lambda j: (0, j)),
                  pl.BlockSpec((1, tn), lambda j: (0, j))],
        out_specs=pl.BlockSpec((rows, tn), lambda j: (0, j)),
        out_shape=jax.ShapeDtypeStruct((rows, 6 * D_MODEL), F32),
        compiler_params=_params("parallel"),
        name="ada",
    )(c_all, w_ada, b_ada.reshape(1, -1))


def _inproj_kernel(x_ref, shift_ref, scale_ref, g_ref, w_ref, wg_ref, cos_ref, sin_ref, gq_ref, gk_ref,
                   seg_ref, o_ref, og_ref):
    rows = x_ref.shape[0]
    h = _rms(x_ref[...], g_ref[...]) * (1.0 + _mod_rows(scale_ref, rows)) + _mod_rows(shift_ref, rows)
    hb = h.astype(BF16)
    cos, sin, seg = cos_ref[...], sin_ref[...], seg_ref[...]
    def followup(c0):
        if COL_QS <= c0 < COL_QS + SWA_HEADS * HEAD_DIM:
            return "query"
        if COL_KS <= c0 < COL_KS + SWA_KV_HEADS * HEAD_DIM:
            return "key"
        if COL_AG <= c0 < COL_AG + D_MODEL or COL_AS <= c0 < COL_AS + D_MODEL:
            return "gate"
        return None

    def has_followup(j):
        return any(followup(c0) for c0 in range(j * PROJ_CHUNK, (j + 1) * PROJ_CHUNK, LANES))

    chunks = range(D_MAIN // PROJ_CHUNK)
    busy = [j for j in chunks if has_followup(j)]
    plain = [j for j in chunks if not has_followup(j)]
    order = [j for pair in zip(busy, plain) for j in pair] + busy[len(plain):] + plain[len(busy):]
    project = lambda j: _dot(hb, w_ref[:, j * PROJ_CHUNK:(j + 1) * PROJ_CHUNK])
    ahead = [project(j) for j in order[:PROJ_AHEAD]]
    for idx, j in enumerate(order):
        sl = slice(j * PROJ_CHUNK, (j + 1) * PROJ_CHUNK)
        r = ahead.pop(0)
        if idx + PROJ_AHEAD < len(order):
            ahead.append(project(order[idx + PROJ_AHEAD]))
        blocks = []
        for m in range(PROJ_CHUNK // LANES):
            c0 = j * PROJ_CHUNK + m * LANES
            blk = r[:, m * LANES:(m + 1) * LANES]
            kind = followup(c0)
            if kind == "query":
                blk = _head_norm_rope(blk, gq_ref[...], cos, sin, seg) * (HEAD_DIM ** -0.5)
            elif kind == "key":
                blk = _head_norm_rope(blk, gk_ref[...], cos, sin, seg)
            elif kind == "gate":
                blk = jax.nn.sigmoid(blk)
            blocks.append(blk)
        o_ref[:, sl] = jnp.concatenate(blocks, axis=1).astype(o_ref.dtype)
    og_ref[...] = _dot(hb, wg_ref[...])


def _mod_spec(per_token, tiles_per_seq):
    if per_token:
        return pl.BlockSpec((TOKEN_TILE // SAMPLE_ROWS, D_MODEL), lambda i, *_: (i, 0))
    return pl.BlockSpec((None, 1, D_MODEL), lambda i, *_: (i // tiles_per_seq, 0, 0))


def _mod_rows(ref, rows):
    m = ref[...]
    if m.shape[0] == 1:
        return m
    return jnp.broadcast_to(m[:, None, :], (m.shape[0], rows // m.shape[0], m.shape[1])).reshape(
        rows, m.shape[1])


def _inproj(x, shift, scale, g, w_main, w_glr, cos, sin, gq, gk, seg, out_dtype, per_token, tiles_per_seq):
    n = x.shape[0]
    mod = _mod_spec(per_token, tiles_per_seq)
    const = lambda i: (0, 0)
    rope = pl.BlockSpec((TOKEN_TILE, LANES), (lambda i: (i, 0)) if per_token else
                        (lambda i: (i % tiles_per_seq, 0)))
    return pl.pallas_call(
        _inproj_kernel,
        grid=(n // TOKEN_TILE,),
        in_specs=[pl.BlockSpec((TOKEN_TILE, D_MODEL), lambda i: (i, 0)), mod, mod,
                  pl.BlockSpec((1, D_MODEL), const),
                  pl.BlockSpec((D_MODEL, D_MAIN), const, pipeline_mode=pl.Buffered(1)),
                  pl.BlockSpec((D_MODEL, LANES), const, pipeline_mode=pl.Buffered(1)),
                  rope, rope,
                  pl.BlockSpec((1, LANES), const), pl.BlockSpec((1, LANES), const),
                  pl.BlockSpec((LANES, LANES), const)],
        out_specs=[pl.BlockSpec((TOKEN_TILE, D_MAIN), lambda i: (i, 0)),
                   pl.BlockSpec((TOKEN_TILE, LANES), lambda i: (i, 0))],
        out_shape=[jax.ShapeDtypeStruct((n, D_MAIN), out_dtype),
                   jax.ShapeDtypeStruct((n, LANES), F32)],
        compiler_params=_params("parallel"),
        name="inproj",
    )(x, shift, scale, g, w_main, w_glr, cos, sin, gq, gk, seg)


def _gla_log_gate(glr, wgk_ref, bgk):
    g_hi = glr.astype(BF16)
    g_lo = (glr - g_hi.astype(F32)).astype(BF16)
    w_hi, w_lo = wgk_ref[0], wgk_ref[1]
    x = _dot(g_hi, w_hi) + (_dot(g_lo, w_hi) + _dot(g_hi, w_lo))
    return _log_sigmoid(x + bgk) * (1.0 / GATE_TAU)


def _gla_out(o, r, g):
    r = r.astype(F32)
    return _rms(o, g) * (r * jax.nn.sigmoid(r))


def _gla_pair_levels(n):
    t = np.arange(n)[:, None]
    s = np.arange(n)[None, :]
    x = t ^ s
    top = np.where(x > 0, 1 << np.floor(np.log2(np.maximum(x, 1))).astype(np.int64), 0)
    return np.where(s > t, -1, top).astype(np.int32)


def _gla_block_ref(b, h):
    n, w = b.shape
    if 2 * h == n:
        return jnp.broadcast_to(b[h - 1:h, :], (n, w))
    if h >= SUBLANES // 2:
        picked = b.reshape(n // (2 * h), 2 * h, w)[:, h - 1:h, :]
        return jnp.broadcast_to(picked, (n // (2 * h), 2 * h, w)).reshape(n, w)
    r = lax.broadcasted_iota(jnp.int32, (n, 1), 0) % (2 * h)
    out = b
    for d in range(1, h + 1):
        out = jnp.where(r == h - 1 + d, pltpu.roll(b, d, 0), out)
    for d in range(1, h):
        out = jnp.where(r == h - 1 - d, pltpu.roll(b, n - d, 0), out)
    return out


def _gla_prompt_kernel(q_ref, k_ref, v_ref, r_ref, glr_ref, wgk_ref, bgk_ref, g_ref, lev_ref,
                       o_ref, s_ref, st_ref):
    c = pl.program_id(1)

    @pl.when(c == 0)
    def _():
        st_ref[...] = jnp.zeros_like(st_ref)

    n = GLA_BLOCK
    lev = lev_ref[...]
    causal = lax.broadcasted_iota(jnp.int32, (n, n), 1) <= lax.broadcasted_iota(jnp.int32, (n, n), 0)
    lg = _gla_log_gate(glr_ref[...], wgk_ref, bgk_ref[...])
    b = _dot_sum(jnp.where(causal, 1.0, 0.0).astype(BF16), lg)
    b_last = b[n - 1:n, :]
    q = q_ref[...].astype(F32) * (GLA_DK ** -0.5)
    k = k_ref[...].astype(F32)
    qe = (q * jnp.exp(b)).astype(BF16)
    kd = (k * jnp.exp(b_last - b)).astype(BF16)
    decay = jnp.exp(b_last)
    levels = [0] + [1 << p for p in range(n.bit_length() - 1)]
    q_lv, k_lv = [q.astype(BF16)], [k.astype(BF16)]
    for h in levels[1:]:
        e = jnp.exp(-jnp.abs(b - _gla_block_ref(b, h)))
        q_lv.append((q * e).astype(BF16))
        k_lv.append((k * e).astype(BF16))
    scaled = {level: (ql, kl) for level, ql, kl in zip(levels, q_lv, k_lv)}

    def intra(lo, hi, dk, v):
        size = hi - lo
        if size == GLA_MASK_BLOCK:
            attn = jnp.zeros((size, size), F32)
            for level in levels:
                if level < size:
                    ql, kl = scaled[level]
                    attn = jnp.where(lev == level, _dot_nt(ql[lo:hi, dk], kl[lo:hi, dk]), attn)
            return _dot(attn.astype(BF16), v[lo:hi, :])
        mid = lo + size // 2
        ql, kl = scaled[size // 2]
        cross = _dot_nt(ql[mid:hi, dk], kl[lo:mid, dk]).astype(BF16)
        return jnp.concatenate([intra(lo, mid, dk, v),
                                _dot(cross, v[lo:mid, :]) + intra(mid, hi, dk, v)], axis=0)

    for h in range(GLA_HEADS):
        dk = slice(h * GLA_DK, (h + 1) * GLA_DK)
        dv = slice(h * GLA_DV, (h + 1) * GLA_DV)
        v = v_ref[:, dv]
        st = st_ref[h]
        o = _dot_nt(qe[:, dk], st.astype(BF16)) + intra(0, n, dk, v)
        st_ref[h] = st * decay[:, dk] + _dot_tn(v, kd[:, dk])
        o_ref[:, dv] = _gla_out(o, r_ref[:, dv], g_ref[...]).astype(o_ref.dtype)

    @pl.when(c == pl.num_programs(1) - 1)
    def _():
        for h in range(GLA_HEADS):
            s_ref[h] = st_ref[h].T


def _gla_prompt(yp, glr, wgk, bgk, gnorm, batch, seq):
    nb = seq // GLA_BLOCK
    hk, hv = GLA_HEADS * GLA_DK, GLA_HEADS * GLA_DV
    tok = lambda b, c: b * nb + c
    const = lambda b, c: (0, 0)
    return pl.pallas_call(
        _gla_prompt_kernel,
        grid=(batch, nb),
        in_specs=[pl.BlockSpec((GLA_BLOCK, hk), lambda b, c: (tok(b, c), COL_QG // hk)),
                  pl.BlockSpec((GLA_BLOCK, hk), lambda b, c: (tok(b, c), COL_KG // hk)),
                  pl.BlockSpec((GLA_BLOCK, hv), lambda b, c: (tok(b, c), COL_VG // hv)),
                  pl.BlockSpec((GLA_BLOCK, hv), lambda b, c: (tok(b, c), COL_RG // hv)),
                  pl.BlockSpec((GLA_BLOCK, LANES), lambda b, c: (tok(b, c), 0)),
                  pl.BlockSpec((2, LANES, hk), lambda b, c: (0, 0, 0)),
                  pl.BlockSpec((1, hk), const),
                  pl.BlockSpec((1, GLA_DV), const),
                  pl.BlockSpec((GLA_MASK_BLOCK, GLA_MASK_BLOCK), const)],
        out_specs=[pl.BlockSpec((GLA_BLOCK, hv), lambda b, c: (tok(b, c), 0)),
                   pl.BlockSpec((None, GLA_HEADS, GLA_DK, GLA_DV), lambda b, c: (b, 0, 0, 0))],
        out_shape=[jax.ShapeDtypeStruct((batch * seq, hv), BF16),
                   jax.ShapeDtypeStruct((batch, GLA_HEADS, GLA_DK, GLA_DV), F32)],
        scratch_shapes=[pltpu.VMEM((GLA_HEADS, GLA_DV, GLA_DK), F32)],
        compiler_params=_params("parallel", "arbitrary"),
        name="gla_prompt",
    )(yp, yp, yp, yp, glr, wgk, bgk, gnorm, jnp.asarray(_gla_pair_levels(GLA_MASK_BLOCK)))


def _gla_sample_kernel(q_ref, k_ref, v_ref, r_ref, glr_ref, wgk_ref, bgk_ref, g_ref, s0_ref, lev_ref,
                       o_ref, s_ref, *, n_real):
    R = SAMPLE_ROWS
    rows = q_ref.shape[0]
    row = lax.broadcasted_iota(jnp.int32, (rows, rows), 0)
    col = lax.broadcasted_iota(jnp.int32, (rows, rows), 1)
    same = (row // R) == (col // R)
    causal = same & (col <= row)
    real = lax.broadcasted_iota(jnp.int32, (rows, 1), 0) % R < n_real
    lg = jnp.where(real, _gla_log_gate(glr_ref[...], wgk_ref, bgk_ref[...]), 0.0)
    b = _dot_sum(jnp.where(causal, 1.0, 0.0).astype(BF16), lg)
    b_last = _dot_sum(jnp.where(same, 1.0, 0.0).astype(BF16), lg)
    q = q_ref[...] * (GLA_DK ** -0.5)
    k = jnp.where(real, k_ref[...], 0.0)
    v = v_ref[...]
    qe = q * jnp.exp(b)
    kd = k * jnp.exp(b_last - b)
    lev = lev_ref[...]
    levels = [0] + [1 << p for p in range(R.bit_length() - 1)]
    q_lv, k_lv = [q.astype(BF16)], [k.astype(BF16)]
    for h in levels[1:]:
        e = jnp.exp(-jnp.abs(b - _gla_block_ref(b, h)))
        q_lv.append((q * e).astype(BF16))
        k_lv.append((k * e).astype(BF16))
    v_bf = v.astype(BF16)
    for h in range(GLA_HEADS):
        dk = slice(h * GLA_DK, (h + 1) * GLA_DK)
        dv = slice(h * GLA_DV, (h + 1) * GLA_DV)
        attn = jnp.zeros((rows, rows), F32)
        for level, ql, kl in zip(levels, q_lv, k_lv):
            attn = jnp.where(lev == level, _dot_nt(ql[:, dk], kl[:, dk]), attn)
        o_intra = _dot(attn.astype(BF16), v_bf[:, dv])
        decay_t = jnp.exp(b_last[:, dk]).T
        outs = []
        for s in range(rows // R):
            sl = slice(s * R, (s + 1) * R)
            s0 = s0_ref[s, h]
            outs.append(_dot(qe[sl, dk], s0) + o_intra[sl, :])
            s_ref[s, h] = s0 * decay_t[:, s * R:s * R + 1] + _dot_tn(kd[sl, dk], v[sl, dv])
        o_ref[:, dv] = _gla_out(jnp.concatenate(outs, axis=0), r_ref[:, dv], g_ref[...])


def _gla_sample(ys, glr, wgk, bgk, gnorm, state, n_seq, n_real):
    R = SAMPLE_SEQS * SAMPLE_ROWS
    hk, hv = GLA_HEADS * GLA_DK, GLA_HEADS * GLA_DV
    st_spec = pl.BlockSpec((SAMPLE_SEQS, GLA_HEADS, GLA_DK, GLA_DV), lambda b: (b, 0, 0, 0))
    lev = _gla_pair_levels(R)
    lev = np.where(lev >= SAMPLE_ROWS, -1, lev)
    return pl.pallas_call(
        functools.partial(_gla_sample_kernel, n_real=n_real),
        grid=(n_seq // SAMPLE_SEQS,),
        in_specs=[pl.BlockSpec((R, hk), lambda b: (b, COL_QG // hk)),
                  pl.BlockSpec((R, hk), lambda b: (b, COL_KG // hk)),
                  pl.BlockSpec((R, hv), lambda b: (b, COL_VG // hv)),
                  pl.BlockSpec((R, hv), lambda b: (b, COL_RG // hv)),
                  pl.BlockSpec((R, LANES), lambda b: (b, 0)),
                  pl.BlockSpec((2, LANES, hk), lambda b: (0, 0, 0)),
                  pl.BlockSpec((1, hk), lambda b: (0, 0)),
                  pl.BlockSpec((1, GLA_DV), lambda b: (0, 0)),
                  st_spec,
                  pl.BlockSpec((R, R), lambda b: (0, 0))],
        out_specs=[pl.BlockSpec((R, hv), lambda b: (b, 0)), st_spec],
        out_shape=[jax.ShapeDtypeStruct((n_seq * SAMPLE_ROWS, hv), F32),
                   jax.ShapeDtypeStruct((n_seq, GLA_HEADS, GLA_DK, GLA_DV), F32)],
        compiler_params=_params("parallel"),
        name="gla_sample",
    )(ys, ys, ys, ys, glr, wgk, bgk, gnorm, state, jnp.asarray(lev))


def _lane_lower(shape):
    return lax.broadcasted_iota(jnp.int32, shape, len(shape) - 1) % LANES < HEAD_DIM


def _head_norm_rope(x, g, cos, sin, seg):
    ss = _dot((x * x).astype(BF16), seg)
    y = x * lax.rsqrt(ss * (1.0 / HEAD_DIM) + NORM_EPS) * g
    half = HEAD_DIM // 2
    lane = lax.broadcasted_iota(jnp.int32, y.shape, 1)
    rot = jnp.where(lane % HEAD_DIM < half, pltpu.roll(y, LANES - half, 1), pltpu.roll(y, half, 1))
    return y * cos + rot * sin


def _both_halves(blk, half):
    sw = pltpu.roll(blk, HEAD_DIM, 1)
    lower = _lane_lower(blk.shape)
    return jnp.where(lower, blk, sw) if half == 0 else jnp.where(lower, sw, blk)


def _stack_heads(q_blocks):
    parts = []
    for qb in q_blocks:
        lower = _lane_lower(qb.shape)
        zero = jnp.zeros_like(qb)
        parts += [jnp.where(lower, qb, zero), jnp.where(lower, zero, qb)]
    return jnp.concatenate(parts, axis=0)


def _swa_prompt_kernel(sink_ref, q_ref, k_ref, v_ref, o_ref, ko_ref, vo_ref, kprev_ref, vprev_ref):
    n = pl.program_id(1)
    W = WINDOW

    @pl.when(n == 0)
    def _():
        kprev_ref[...] = jnp.zeros_like(kprev_ref)
        vprev_ref[...] = jnp.zeros_like(vprev_ref)

    def per_head(x):
        return [_both_halves(x[:, (kh // 2) * LANES:(kh // 2 + 1) * LANES], kh % 2).astype(BF16)
                for kh in range(SWA_KV_HEADS)]

    kb_prev = [kprev_ref[kh] for kh in range(SWA_KV_HEADS)]
    vb_prev = [vprev_ref[kh] for kh in range(SWA_KV_HEADS)]
    for sub in range(SWA_STEP_BLOCKS):
        tok = slice(sub * W, (sub + 1) * W)
        k_cur = k_ref[tok, :].astype(F32)
        v_cur = v_ref[tok, :].astype(F32)
        kb_cur, vb_cur = per_head(k_cur), per_head(v_cur)
        prev_fill = jnp.where(n > 0, 0.0, -jnp.inf) if sub == 0 else 0.0
        _swa_block(sink_ref, q_ref, o_ref, tok, kb_prev, vb_prev, kb_cur, vb_cur, prev_fill)
        kb_prev, vb_prev = kb_cur, vb_cur
    ko_ref[...] = k_cur
    vo_ref[...] = v_cur
    for kh in range(SWA_KV_HEADS):
        kprev_ref[kh] = kb_prev[kh]
        vprev_ref[kh] = vb_prev[kh]


def _swa_block(sink_ref, q_ref, o_ref, tok, kb_prev_all, vb_prev_all, kb_cur_all, vb_cur_all, prev_fill):
    W = WINDOW
    qi = lax.broadcasted_iota(jnp.int32, (W, W), 0)
    ki = lax.broadcasted_iota(jnp.int32, (W, W), 1)
    from_cur = ki <= qi
    for kh in range(SWA_KV_HEADS):
        kb_prev, kb_cur = kb_prev_all[kh], kb_cur_all[kh]
        vb_prev, vb_cur = vb_prev_all[kh], vb_cur_all[kh]
        qblocks = [q_ref[tok, (2 * kh + j) * LANES:(2 * kh + j + 1) * LANES] for j in range(2)]
        qs = _stack_heads(qblocks)
        s_prev = _dot_nt(qs, kb_prev)
        s_cur = _dot_nt(qs, kb_cur)
        outs = []
        for g in range(SWA_GROUP):
            rows = slice(g * W, (g + 1) * W)
            sg = jnp.where(from_cur, s_cur[rows, :], s_prev[rows, :] + prev_fill)
            sink = sink_ref[kh * SWA_GROUP + g]
            m = jnp.maximum(jnp.max(sg, axis=-1, keepdims=True), sink)
            p = jnp.exp(sg - m)
            denom = jnp.sum(p, axis=-1, keepdims=True) + jnp.exp(sink - m)
            p_cur = jnp.where(from_cur, p, 0.0).astype(BF16)
            p_prev = jnp.where(from_cur, 0.0, p).astype(BF16)
            outs.append((_dot(p_prev, vb_prev) + _dot(p_cur, vb_cur)) / denom)
        lower = _lane_lower((W, LANES))
        for j in range(2):
            c0 = (2 * kh + j) * LANES
            o_ref[tok, c0:c0 + LANES] = jnp.where(lower, outs[2 * j], outs[2 * j + 1]).astype(o_ref.dtype)


def _swa_prompt(yp, sinks, batch, seq):
    rows = SWA_STEP_BLOCKS * WINDOW
    nb = seq // rows
    kvw = SWA_KV_HEADS * HEAD_DIM
    tok = lambda b, n: b * nb + n
    cache_spec = pl.BlockSpec((None, WINDOW, kvw), lambda b, n: (b, 0, 0))
    return pl.pallas_call(
        _swa_prompt_kernel,
        grid=(batch, nb),
        in_specs=[pl.BlockSpec(memory_space=pltpu.SMEM),
                  pl.BlockSpec((rows, D_MODEL), lambda b, n: (tok(b, n), COL_QS // D_MODEL)),
                  pl.BlockSpec((rows, kvw), lambda b, n: (tok(b, n), COL_KS // kvw)),
                  pl.BlockSpec((rows, kvw), lambda b, n: (tok(b, n), COL_VS // kvw))],
        out_specs=[pl.BlockSpec((rows, D_MODEL), lambda b, n: (tok(b, n), 0)), cache_spec, cache_spec],
        out_shape=[jax.ShapeDtypeStruct((batch * seq, D_MODEL), BF16),
                   jax.ShapeDtypeStruct((batch, WINDOW, kvw), F32),
                   jax.ShapeDtypeStruct((batch, WINDOW, kvw), F32)],
        scratch_shapes=[pltpu.VMEM((SWA_KV_HEADS, WINDOW, LANES), BF16),
                        pltpu.VMEM((SWA_KV_HEADS, WINDOW, LANES), BF16)],
        compiler_params=_params("parallel", "arbitrary"),
        name="swa_prompt",
    )(sinks, yp, yp, yp)


def _shift_cache(cache, new, n_real):
    R = SAMPLE_ROWS
    rolled = pltpu.roll(cache, WINDOW - n_real, 0)
    tail_new = pltpu.roll(new, R - n_real, 0)
    row = lax.broadcasted_iota(jnp.int32, (R, cache.shape[1]), 0)
    tail = jnp.where(row < R - n_real, rolled[WINDOW - R:, :], tail_new)
    return jnp.concatenate([rolled[:WINDOW - R, :], tail], axis=0)


def _swa_sample_kernel(q_ref, k_ref, v_ref, kc_ref, vc_ref, sink_ref, o_ref, ko_ref, vo_ref, *, n_real):
    R, W = SAMPLE_ROWS, WINDOW
    k_new = k_ref[...]
    v_new = v_ref[...]
    q_pairs = [q_ref[:, j * LANES:(j + 1) * LANES] for j in range(SWA_HEADS // 2)]

    hr = SWA_HEADS * R
    t_c = lax.broadcasted_iota(jnp.int32, (hr, W), 0) % R
    mask_c = lax.broadcasted_iota(jnp.int32, (hr, W), 1) > t_c
    t_n = lax.broadcasted_iota(jnp.int32, (hr, R), 0) % R
    mask_n = lax.broadcasted_iota(jnp.int32, (hr, R), 1) <= t_n
    sink = sink_ref[...]
    lower = _lane_lower((R, LANES))
    zeros = jnp.zeros((R, LANES), F32)
    for s in range(q_ref.shape[0] // R):
        sl = slice(s * R, (s + 1) * R)
        kc, vc = kc_ref[s], vc_ref[s]
        kn, vn = k_new[sl, :], v_new[sl, :]
        ko_ref[s] = _shift_cache(kc, kn, n_real)
        vo_ref[s] = _shift_cache(vc, vn, n_real)
        q_rows = []
        for h in range(SWA_HEADS):
            kh = h // SWA_GROUP
            x = q_pairs[h // 2][sl, :]
            if h % 2 != kh % 2:
                x = pltpu.roll(x, HEAD_DIM, 1)
            x = jnp.where(lower, x, zeros) if kh % 2 == 0 else jnp.where(lower, zeros, x)
            q_rows.append(jnp.concatenate([x, zeros] if kh // 2 == 0 else [zeros, x], axis=1))
        qbd = jnp.concatenate(q_rows, axis=0)
        sc = jnp.where(mask_c, _dot_nt(qbd.astype(BF16), kc.astype(BF16)), -jnp.inf)
        sn = jnp.where(mask_n, _dot_nt(qbd, kn), -jnp.inf)
        m = jnp.maximum(jnp.maximum(jnp.max(sc, axis=-1, keepdims=True),
                                    jnp.max(sn, axis=-1, keepdims=True)), sink)
        pc, pn = jnp.exp(sc - m), jnp.exp(sn - m)
        denom = (jnp.sum(pc, axis=-1, keepdims=True) + jnp.sum(pn, axis=-1, keepdims=True)
                 + jnp.exp(sink - m))
        o = (_dot(pc.astype(BF16), vc.astype(BF16)) + _dot(pn, vn)) / denom
        for j in range(SWA_HEADS // 2):
            halves = []
            for h in (2 * j, 2 * j + 1):
                kh = h // SWA_GROUP
                y = o[h * R:(h + 1) * R, (kh // 2) * LANES:(kh // 2 + 1) * LANES]
                halves.append(pltpu.roll(y, HEAD_DIM, 1) if h % 2 != kh % 2 else y)
            o_ref[sl, j * LANES:(j + 1) * LANES] = jnp.where(lower, halves[0], halves[1])


def _swa_sample(ys, sink_rows, kcache, vcache, n_seq, n_real):
    R = SAMPLE_SEQS * SAMPLE_ROWS
    kvw = SWA_KV_HEADS * HEAD_DIM
    const = lambda b: (0, 0)
    cache_spec = pl.BlockSpec((SAMPLE_SEQS, WINDOW, kvw), lambda b: (b, 0, 0))
    return pl.pallas_call(
        functools.partial(_swa_sample_kernel, n_real=n_real),
        grid=(n_seq // SAMPLE_SEQS,),
        in_specs=[pl.BlockSpec((R, D_MODEL), lambda b: (b, COL_QS // D_MODEL)),
                  pl.BlockSpec((R, kvw), lambda b: (b, COL_KS // kvw)),
                  pl.BlockSpec((R, kvw), lambda b: (b, COL_VS // kvw)),
                  cache_spec, cache_spec,
                  pl.BlockSpec((SWA_HEADS * SAMPLE_ROWS, 1), const)],
        out_specs=[pl.BlockSpec((R, D_MODEL), lambda b: (b, 0)), cache_spec, cache_spec],
        out_shape=[jax.ShapeDtypeStruct((n_seq * SAMPLE_ROWS, D_MODEL), F32),
                   jax.ShapeDtypeStruct((n_seq, WINDOW, kvw), F32),
                   jax.ShapeDtypeStruct((n_seq, WINDOW, kvw), F32)],
        compiler_params=_params("parallel"),
        name="swa_sample",
    )(ys, ys, ys, kcache, vcache, sink_rows)


def _post_kernel(ag_ref, as_ref, og_ref, os_ref, x_ref, gate_ref, shift_ref, scale_ref, g2_ref,
                 wo_ref, wr_ref, br_ref, x1_ref, h2_ref, slot_ref, gatek_ref, cnt_ref, *, n_valid):
    merged = (ag_ref[...].astype(F32) * og_ref[...].astype(F32)
              + as_ref[...].astype(F32) * os_ref[...].astype(F32))
    y = _dot(merged.astype(BF16), wo_ref[...])
    rows = x_ref.shape[0]
    x1 = x_ref[...] + _mod_rows(gate_ref, rows) * y
    x1_ref[...] = x1
    h2 = _rms(x1, g2_ref[...]) * (1.0 + _mod_rows(scale_ref, rows)) + _mod_rows(shift_ref, rows)
    h2_hi = h2.astype(BF16)
    h2_ref[...] = h2_hi

    h2_lo = (h2 - h2_hi.astype(F32)).astype(BF16)
    w_hi, w_lo = wr_ref[0], wr_ref[1]
    logits = _dot(h2_hi, w_hi) + (_dot(h2_lo, w_hi) + _dot(h2_hi, w_lo)) + br_ref[...]
    lane_i = lax.broadcasted_iota(jnp.int32, logits.shape, 1)
    lane = lane_i.astype(F32)
    work = logits
    vals, hots = [], []
    for _ in range(TOP_K):
        m = jnp.max(work, axis=-1, keepdims=True)
        idx = jnp.min(jnp.where(work == m, lane, float(LANES)), axis=-1, keepdims=True)
        hot = lane == idx
        vals.append(m)
        hots.append(hot)
        work = jnp.where(hot, -jnp.inf, work)
    exps = [jnp.exp(v - vals[0]) for v in vals]
    denom = exps[0] + exps[1] + exps[2] + exps[3]

    tm = logits.shape[0]
    valid = lax.broadcasted_iota(jnp.int32, (tm, 1), 0) % SAMPLE_ROWS < n_valid
    sel = jnp.zeros_like(logits)
    for hot in hots:
        sel = jnp.where(hot, 1.0, sel)
    sel = jnp.where(valid, sel, 0.0)
    earlier = (lax.broadcasted_iota(jnp.int32, (tm, tm), 1)
               < lax.broadcasted_iota(jnp.int32, (tm, tm), 0))
    rank = _dot(jnp.where(earlier, 1.0, 0.0).astype(BF16), sel.astype(BF16))
    cnt = jnp.sum(sel, axis=0, keepdims=True)
    cnt_pad = jnp.floor((cnt + (ROW_UNIT - 1.0)) * (1.0 / ROW_UNIT)) * ROW_UNIT
    below = (lax.broadcasted_iota(jnp.int32, (LANES, LANES), 0)
             < lax.broadcasted_iota(jnp.int32, (LANES, LANES), 1))
    seg_start = _dot(jnp.broadcast_to(cnt_pad, (8, LANES)), jnp.where(below, 1.0, 0.0), HIGHEST)[0:1]
    pos = seg_start + rank
    slots = jnp.full_like(logits, -1.0)
    gates = jnp.zeros_like(logits)
    for k in range(TOP_K):
        s_k = jnp.sum(jnp.where(hots[k], pos, 0.0), axis=-1, keepdims=True)
        slots = jnp.where(lane_i == k, s_k, slots)
        gates = jnp.where(lane_i == k, exps[k] / denom, gates)
    slot_ref[...] = jnp.where(valid, slots, -1.0)
    gatek_ref[...] = gates
    cnt_ref[...] = cnt


def _post(y_all, o_gla, o_swa, x, gate, shift, scale, g2, wo, wr, br, per_token, tiles_per_seq, n_valid):
    n = x.shape[0]
    mod = _mod_spec(per_token, tiles_per_seq)
    row = lambda i: (i, 0)
    const = lambda i: (0, 0)
    wide = pl.BlockSpec((TOKEN_TILE, D_MODEL), row)
    narrow = pl.BlockSpec((TOKEN_TILE, LANES), row)
    return pl.pallas_call(
        functools.partial(_post_kernel, n_valid=n_valid),
        grid=(n // TOKEN_TILE,),
        in_specs=[pl.BlockSpec((TOKEN_TILE, D_MODEL), lambda i: (i, COL_AG // D_MODEL)),
                  pl.BlockSpec((TOKEN_TILE, D_MODEL), lambda i: (i, COL_AS // D_MODEL)),
                  wide, wide, wide, mod, mod, mod,
                  pl.BlockSpec((1, D_MODEL), const),
                  pl.BlockSpec((D_MODEL, D_MODEL), const),
                  pl.BlockSpec((2, D_MODEL, LANES), lambda i: (0, 0, 0)),
                  pl.BlockSpec((1, LANES), const)],
        out_specs=[wide, wide, narrow, narrow, pl.BlockSpec((None, 1, LANES), lambda i: (i, 0, 0))],
        out_shape=[jax.ShapeDtypeStruct((n, D_MODEL), F32),
                   jax.ShapeDtypeStruct((n, D_MODEL), BF16),
                   jax.ShapeDtypeStruct((n, LANES), F32),
                   jax.ShapeDtypeStruct((n, LANES), F32),
                   jax.ShapeDtypeStruct((n // TOKEN_TILE, 1, LANES), F32)],
        compiler_params=_params("parallel"),
        name="post",
    )(y_all, y_all, o_gla, o_swa, x, gate, shift, scale, g2, wo, wr, br)


def _slot_matrix(slot_cols, weights, chunk):
    tm = slot_cols[0].shape[0]
    j = lax.broadcasted_iota(jnp.int32, (tm, SLOT_CHUNK), 1) + chunk * SLOT_CHUNK
    out = jnp.zeros((tm, SLOT_CHUNK), F32)
    for s, w in zip(slot_cols, weights):
        out = jnp.where(s == j, w, out)
    return out.astype(BF16)


def _dispatch_kernel(nu_ref, seg_ref, ntail_ref, tail_ref, hp_ref, slp_ref, hs_ref, sls_ref, xg_ref,
                     sorted_ref, zero_ref, sem, tail_sem, *, prompt_tiles):
    t = pl.program_id(0)
    last = pl.num_programs(0) - 1
    buf = t % 2

    def sort_tile(h_ref, slot_ref):
        slots = slot_ref[...].astype(jnp.int32)
        slot_cols = [slots[:, k:k + 1] for k in range(TOP_K)]
        h = h_ref[...]
        upc = SLOT_CHUNK // ROW_UNIT
        for c in range(LOCAL_ROWS // SLOT_CHUNK):
            onehot = _slot_matrix(slot_cols, [1.0] * TOP_K, c)
            sorted_ref[buf, c * upc:(c + 1) * upc] = _dot_tn(onehot, h).astype(BF16).reshape(
                upc, ROW_UNIT, D_MODEL)

    @pl.when(t < prompt_tiles)
    def _():
        sort_tile(hp_ref, slp_ref)

    @pl.when(t >= prompt_tiles)
    def _():
        sort_tile(hs_ref, sls_ref)

    def start_all(tile, b):
        def body(e, c):
            n = seg_ref[tile, 0, e]

            @pl.when(n > 0)
            def _():
                pltpu.make_async_copy(sorted_ref.at[b, pl.ds(seg_ref[tile, 1, e], n)],
                                      xg_ref.at[pl.ds(seg_ref[tile, 2, e], n)], sem.at[b]).start()
            return c

        lax.fori_loop(0, N_EXPERTS, body, 0)

    def wait_all(tile, b):
        n_units = nu_ref[tile]

        @pl.when(n_units > 0)
        def _():
            pltpu.make_async_copy(sorted_ref.at[b, pl.ds(0, n_units)],
                                  xg_ref.at[pl.ds(0, n_units)], sem.at[b]).wait()

    @pl.when(t > 0)
    def _():
        wait_all(t - 1, 1 - buf)

    start_all(t, buf)

    def tail_copy(i):
        return pltpu.make_async_copy(zero_ref, xg_ref.at[tail_ref[i]], tail_sem)

    @pl.when(t == last)
    def _():
        zero_ref[...] = jnp.zeros_like(zero_ref)
        n_tail = ntail_ref[0]
        lax.fori_loop(0, n_tail, lambda i, c: (tail_copy(i).start(), c)[1], 0)
        lax.fori_loop(0, n_tail, lambda i, c: (tail_copy(i).wait(), c)[1], 0)
        wait_all(t, buf)


def _dispatch(h2_p, slots_p, h2_s, slots_s, n_units, seg, n_tail, tail_dst, rows_max):
    p_tiles = h2_p.shape[0] // TOKEN_TILE
    s_tiles = h2_s.shape[0] // TOKEN_TILE
    p_row = lambda t, *_: (jnp.minimum(t, p_tiles - 1), 0)
    s_row = lambda t, *_: (jnp.maximum(t - p_tiles, 0), 0)
    return pl.pallas_call(
        functools.partial(_dispatch_kernel, prompt_tiles=p_tiles),
        grid_spec=pltpu.PrefetchScalarGridSpec(
            num_scalar_prefetch=4,
            grid=(p_tiles + s_tiles,),
            in_specs=[pl.BlockSpec((TOKEN_TILE, D_MODEL), p_row),
                      pl.BlockSpec((TOKEN_TILE, LANES), p_row),
                      pl.BlockSpec((TOKEN_TILE, D_MODEL), s_row),
                      pl.BlockSpec((TOKEN_TILE, LANES), s_row)],
            out_specs=pl.BlockSpec(memory_space=pl.ANY),
            scratch_shapes=[pltpu.VMEM((2, LOCAL_ROWS // ROW_UNIT, ROW_UNIT, D_MODEL), BF16),
                            pltpu.VMEM((ROW_UNIT, D_MODEL), BF16),
                            pltpu.SemaphoreType.DMA((2,)), pltpu.SemaphoreType.DMA]),
        out_shape=jax.ShapeDtypeStruct((rows_max // ROW_UNIT, ROW_UNIT, D_MODEL), BF16),
        compiler_params=_params("arbitrary"),
        name="dispatch",
    )(n_units, seg, n_tail, tail_dst, h2_p, slots_p, h2_s, slots_s).reshape(rows_max, D_MODEL)


def _expert_kernel(tg_ref, ge_ref, ng_ref, nused_ref, rows_ref, x_ref, bgu_ref, bd_ref, wgu_hbm, wd_hbm,
                   y_ref, wgu_f32, wd_f32, wgu_bf, wd_bf, sem):
    i = pl.program_id(0)

    def fetch(g, b):
        e = ge_ref[g]
        return (pltpu.make_async_copy(wgu_hbm.at[e], wgu_f32.at[b], sem.at[0, b]),
                pltpu.make_async_copy(wd_hbm.at[e], wd_f32.at[b], sem.at[1, b]))

    @pl.when(i == 0)
    def _():
        for cp in fetch(0, 0):
            cp.start()

    @pl.when(i < nused_ref[0])
    def _():
        g = tg_ref[i]
        b = g % 2

        @pl.when((i == 0) | (g != tg_ref[jnp.maximum(i - 1, 0)]))
        def _():
            @pl.when(g + 1 < ng_ref[0])
            def _():
                for cp in fetch(g + 1, 1 - b):
                    cp.start()

            for cp in fetch(g, b):
                cp.wait()
            wgu_bf[...] = wgu_f32[b].astype(BF16)
            wd_bf[...] = wd_f32[b].astype(BF16)

        for part in range(EXPERT_TILE // EXPERT_PART):
            @pl.when(rows_ref[i] > part * EXPERT_PART)
            def _():
                sl = slice(part * EXPERT_PART, (part + 1) * EXPERT_PART)
                gu = _dot(x_ref[sl, :], wgu_bf[...]) + bgu_ref[...]
                gate = jnp.minimum(gu[:, :D_FF], SWIGLU_LIMIT)
                up = jnp.clip(gu[:, D_FF:], -SWIGLU_LIMIT, SWIGLU_LIMIT)
                act = (up + 1.0) * gate * jax.nn.sigmoid(SWIGLU_ALPHA * gate)
                y_ref[sl, :] = (_dot(act.astype(BF16), wd_bf[...]) + bd_ref[...]).astype(y_ref.dtype)


def _experts(xg, tile_group, group_expert, n_groups, n_used, tile_rows, wgu, bgu, wd, bd):
    rows_max = xg.shape[0]
    used = lambda i, nu: jnp.maximum(jnp.minimum(i, nu[0] - 1), 0)
    row = lambda i, tg, ge, ng, nu, tr: (used(i, nu), 0)
    exp = lambda i, tg, ge, ng, nu, tr: (ge[tg[used(i, nu)]], 0, 0)
    return pl.pallas_call(
        _expert_kernel,
        grid_spec=pltpu.PrefetchScalarGridSpec(
            num_scalar_prefetch=5,
            grid=(rows_max // EXPERT_TILE,),
            in_specs=[pl.BlockSpec((EXPERT_TILE, D_MODEL), row),
                      pl.BlockSpec((None, 1, 2 * D_FF), exp),
                      pl.BlockSpec((None, 1, D_MODEL), exp),
                      pl.BlockSpec(memory_space=pl.ANY),
                      pl.BlockSpec(memory_space=pl.ANY)],
            out_specs=pl.BlockSpec((EXPERT_TILE, D_MODEL), row),
            scratch_shapes=[pltpu.VMEM((2, D_MODEL, 2 * D_FF), F32), pltpu.VMEM((2, D_FF, D_MODEL), F32),
                            pltpu.VMEM((D_MODEL, 2 * D_FF), BF16), pltpu.VMEM((D_FF, D_MODEL), BF16),
                            pltpu.SemaphoreType.DMA((2, 2))]),
        out_shape=jax.ShapeDtypeStruct((rows_max, D_MODEL), BF16),
        compiler_params=_params("arbitrary"),
        name="experts",
    )(tile_group, group_expert, n_groups, n_used, tile_rows, xg, bgu, bd, wgu, wd)


def _combine_kernel(nu_ref, seg_ref, slot_ref, gatek_ref, x_ref, gmlp_ref, y_ref, o_ref, ys_ref, sem,
                    *, tile_offset):
    j = pl.program_id(0)
    t = j + tile_offset
    buf = j % 2
    tm = x_ref.shape[0]

    def fetch(tile, b):
        def body(e, c):
            n = seg_ref[tile, 0, e]

            @pl.when(n > 0)
            def _():
                pltpu.make_async_copy(y_ref.at[pl.ds(seg_ref[tile, 2, e], n)],
                                      ys_ref.at[b, pl.ds(seg_ref[tile, 1, e], n)], sem.at[b]).start()
            return c

        lax.fori_loop(0, N_EXPERTS, body, 0)
        n_units = nu_ref[tile]

        def zero_unit(i, c):
            ys_ref[b, i] = jnp.zeros((ROW_UNIT, D_MODEL), ys_ref.dtype)
            return c

        lax.fori_loop(n_units, LOCAL_ROWS // ROW_UNIT, zero_unit, 0)

    @pl.when(j == 0)
    def _():
        fetch(t, buf)

    @pl.when(j + 1 < pl.num_programs(0))
    def _():
        fetch(t + 1, 1 - buf)

    n_units = nu_ref[t]

    @pl.when(n_units > 0)
    def _():
        pltpu.make_async_copy(y_ref.at[pl.ds(0, n_units)], ys_ref.at[buf, pl.ds(0, n_units)],
                              sem.at[buf]).wait()

    slots = slot_ref[...].astype(jnp.int32)
    gates = gatek_ref[...]
    slot_cols = [slots[:, k:k + 1] for k in range(TOP_K)]
    gate_cols = [gates[:, k:k + 1] for k in range(TOP_K)]
    acc = jnp.zeros((tm, D_MODEL), F32)
    upc = SLOT_CHUNK // ROW_UNIT
    for c in range(LOCAL_ROWS // SLOT_CHUNK):
        rows_c = ys_ref[buf, c * upc:(c + 1) * upc].reshape(SLOT_CHUNK, D_MODEL)
        acc = acc + _dot(_slot_matrix(slot_cols, gate_cols, c), rows_c)
    o_ref[...] = x_ref[...] + _mod_rows(gmlp_ref, tm) * acc


def _combine(y, slots, gates, x1, gmlp, n_units, seg, tile_offset, per_token, tiles_per_seq):
    n = x1.shape[0]
    mod = _mod_spec(per_token, tiles_per_seq)
    wide = pl.BlockSpec((TOKEN_TILE, D_MODEL), lambda i, *_: (i, 0))
    narrow = pl.BlockSpec((TOKEN_TILE, LANES), lambda i, *_: (i, 0))
    return pl.pallas_call(
        functools.partial(_combine_kernel, tile_offset=tile_offset),
        grid_spec=pltpu.PrefetchScalarGridSpec(
            num_scalar_prefetch=2,
            grid=(n // TOKEN_TILE,),
            in_specs=[narrow, narrow, wide, mod, pl.BlockSpec(memory_space=pl.ANY)],
            out_specs=wide,
            scratch_shapes=[pltpu.VMEM((2, LOCAL_ROWS // ROW_UNIT, ROW_UNIT, D_MODEL), BF16),
                            pltpu.SemaphoreType.DMA((2,))]),
        out_shape=jax.ShapeDtypeStruct((n, D_MODEL), F32),
        compiler_params=_params("arbitrary"),
        name="combine",
    )(n_units, seg, slots, gates, x1, gmlp, y.reshape(-1, ROW_UNIT, D_MODEL))


def _route_tables(cnt, rows_max):
    units = (cnt + ROW_UNIT - 1) // ROW_UNIT
    group_units = jnp.sum(units, axis=0)
    upt = EXPERT_TILE // ROW_UNIT
    group_pad = (group_units + upt - 1) // upt * upt
    group_end = jnp.cumsum(group_pad)
    group_start = group_end - group_pad
    seg_start = group_start[None, :] + jnp.cumsum(units, axis=0) - units
    local_end = jnp.cumsum(units, axis=1)
    local_start = local_end - units
    n_units = local_end[:, -1].astype(jnp.int32)

    def pick(lo, hi, pos, value):
        return jnp.sum(jnp.where((pos >= lo) & (pos < hi), value, 0), axis=-1).astype(jnp.int32)

    seg = jnp.stack([units, local_start, seg_start], axis=1).astype(jnp.int32)

    n_tail_e = group_pad - group_units
    j = jnp.arange(N_EXPERTS * upt, dtype=jnp.int32)[:, None]
    tail_end = jnp.cumsum(n_tail_e)
    tail_start = tail_end - n_tail_e
    tail_hbm = pick(tail_start[None, :], tail_end[None, :], j,
                    (group_start + group_units)[None, :] + j - tail_start[None, :])
    n_tail = tail_end[-1:].astype(jnp.int32)

    r = jnp.arange(rows_max // EXPERT_TILE, dtype=jnp.int32)[:, None] * upt
    n_used = (group_end[-1:] // upt).astype(jnp.int32)
    nonempty = group_units > 0
    group_of_expert = jnp.cumsum(nonempty) - 1
    tile_rows = pick(group_start[None, :], group_end[None, :], r,
                     jnp.clip(((group_start + group_units)[None, :] - r) * ROW_UNIT, 0, EXPERT_TILE))
    tile_group = pick(group_start[None, :], group_end[None, :], r, group_of_expert[None, :])
    g = jnp.arange(N_EXPERTS, dtype=jnp.int32)
    group_expert = jnp.sum(jnp.where(nonempty[None, :] & (group_of_expert[None, :] == g[:, None]),
                                     g[None, :], 0), axis=-1).astype(jnp.int32)
    n_groups = jnp.sum(nonempty)[None].astype(jnp.int32)
    return (n_units, seg, n_tail, tail_hbm,
            tile_group, group_expert, n_groups, n_used, tile_rows)


def _rope_tables(pos):
    half = HEAD_DIM // 2
    lane = np.arange(LANES)
    inv = jnp.asarray(ROPE_THETA, F32) ** (-jnp.asarray(lane % half, F32) / half)
    sign = jnp.asarray(np.where(lane % HEAD_DIM < half, -1.0, 1.0), F32)
    ang = pos.astype(F32)[:, None] * inv[None, :]
    return jnp.cos(ang), jnp.sin(ang) * sign[None, :]


def kernel(x_prompt, x_sample, c_prompt, c_sample, state_gla, cache_swa_k, cache_swa_v, w_ada, b_ada,
           norm1_g, norm2_g, w_in, w_gk2, b_gk, gla_norm_g, q_norm_g, k_norm_g, attn_sinks, w_o,
           w_router, b_router, w_gate_up, b_gate_up, w_down, b_down):
    batch, seq, d = x_prompt.shape
    n_seq, n_real, _ = x_sample.shape
    depth = w_in.shape[0]
    assert depth == 1 and d == D_MODEL and n_real <= SAMPLE_ROWS
    assert seq % TOKEN_TILE == 0 and (n_seq * SAMPLE_ROWS) % TOKEN_TILE == 0 and n_seq % SAMPLE_SEQS == 0
    R = SAMPLE_ROWS
    kvw = SWA_KV_HEADS * HEAD_DIM
    tiles_per_seq = seq // TOKEN_TILE

    qg, kg, vg, rg, glr_w, qs, ks, vs, ag, as_ = jnp.split(
        w_in[0], [512, 1024, 2048, 3072, 3088, 4112, 4368, 4624, 5648], axis=1)
    w_main = jnp.concatenate([vg, rg, qs, ag, as_, qg, kg, ks, vs], axis=1).astype(BF16)
    w_glr = jnp.pad(glr_w, ((0, 0), (0, LANES - GATE_RANK))).astype(BF16)
    wgk = jnp.pad(w_gk2[0], ((0, LANES - GATE_RANK), (0, 0)))
    wgk_hi = wgk.astype(BF16)
    wgk = jnp.stack([wgk_hi, (wgk - wgk_hi.astype(F32)).astype(BF16)])
    bgk = b_gk[0].reshape(1, -1)
    gnorm = gla_norm_g[0].reshape(1, -1)
    gq = jnp.tile(q_norm_g[0], LANES // HEAD_DIM).reshape(1, LANES)
    gk = jnp.tile(k_norm_g[0], LANES // HEAD_DIM).reshape(1, LANES)
    seg = jnp.asarray(np.kron(np.eye(LANES // HEAD_DIM), np.ones((HEAD_DIM, HEAD_DIM))), BF16)
    sinks = attn_sinks[0]
    wo = w_o[0].astype(BF16)
    wr = jnp.pad(w_router[0], ((0, 0), (0, LANES - N_EXPERTS)))
    wr_hi = wr.astype(BF16)
    wr = jnp.stack([wr_hi, (wr - wr_hi.astype(F32)).astype(BF16)])
    br = jnp.pad(b_router[0], (0, LANES - N_EXPERTS), constant_values=-1e30).reshape(1, LANES)
    bgu = b_gate_up[0].reshape(N_EXPERTS, 1, 2 * D_FF)
    bd = b_down[0].reshape(N_EXPERTS, 1, D_MODEL)
    g1 = norm1_g[0].reshape(1, -1)
    g2 = norm2_g[0].reshape(1, -1)

    n_c = batch + n_seq
    c_all = jnp.pad(jnp.concatenate([c_prompt, c_sample], axis=0), ((0, -n_c % 8), (0, 0)))
    m_all = _ada(c_all, w_ada[0], b_ada[0])
    mp = [m_all[:batch, i * d:(i + 1) * d].reshape(batch, 1, d) for i in range(6)]
    ms = [m_all[batch:n_c, i * d:(i + 1) * d] for i in range(6)]

    xp = x_prompt.reshape(batch * seq, d)
    xs = jnp.pad(x_sample, ((0, 0), (0, R - n_real), (0, 0))).reshape(n_seq * R, d)
    cos_p, sin_p = _rope_tables(jnp.arange(seq))
    cos_s, sin_s = _rope_tables(PAST_LEN + jnp.tile(jnp.arange(R), n_seq))
    sink_rows = jnp.repeat(sinks, R).reshape(SWA_HEADS * R, 1)

    yp, glr_p = _inproj(xp, mp[0], mp[1], g1, w_main, w_glr, cos_p, sin_p, gq, gk, seg,
                        BF16, False, tiles_per_seq)
    ys, glr_s = _inproj(xs, ms[0], ms[1], g1, w_main, w_glr, cos_s, sin_s, gq, gk, seg, F32, True, 1)
    og_p, st_p = _gla_prompt(yp, glr_p, wgk, bgk, gnorm, batch, seq)
    og_s, st_s = _gla_sample(ys, glr_s, wgk, bgk, gnorm, state_gla[0], n_seq, n_real)
    os_p, kc_p, vc_p = _swa_prompt(yp, sinks, batch, seq)
    os_s, kc_s, vc_s = _swa_sample(ys, sink_rows, cache_swa_k[0].reshape(n_seq, WINDOW, kvw),
                                   cache_swa_v[0].reshape(n_seq, WINDOW, kvw), n_seq, n_real)
    x1_p, h2_p, sl_p, gt_p, cnt_p = _post(yp, og_p, os_p, xp, mp[2], mp[3], mp[4], g2, wo, wr, br,
                                          False, tiles_per_seq, R)
    x1_s, h2_s, sl_s, gt_s, cnt_s = _post(ys, og_s, os_s, xs, ms[2], ms[3], ms[4], g2, wo, wr, br,
                                          True, 1, n_real)

    cnt = jnp.concatenate([cnt_p, cnt_s], axis=0)[:, 0, :N_EXPERTS].astype(jnp.int32)
    n_tiles = cnt.shape[0]
    rows_bound = (TOP_K * (batch * seq + n_seq * n_real) + n_tiles * N_EXPERTS * (ROW_UNIT - 1)
                  + N_EXPERTS * (EXPERT_TILE - 1))
    rows_max = -(-rows_bound // EXPERT_TILE) * EXPERT_TILE
    (n_units, seg, n_tail, tail_hbm,
     tile_group, group_expert, n_groups, n_used, tile_rows) = _route_tables(cnt, rows_max)
    xg = _dispatch(h2_p, sl_p, h2_s, sl_s, n_units, seg, n_tail, tail_hbm, rows_max)
    yg = _experts(xg, tile_group, group_expert, n_groups, n_used, tile_rows,
                  w_gate_up[0], bgu, w_down[0], bd)
    p_tiles = batch * seq // TOKEN_TILE
    out_p = _combine(yg, sl_p, gt_p, x1_p, mp[5], n_units, seg, 0, False, tiles_per_seq)
    out_s = _combine(yg, sl_s, gt_s, x1_s, ms[5], n_units, seg, p_tiles, True, 1)

    cache_shape = (WINDOW, SWA_KV_HEADS, HEAD_DIM)
    return (out_p.reshape(batch, seq, d),
            out_s.reshape(n_seq, R, d)[:, :n_real],
            st_p[None],
            kc_p.reshape(1, batch, *cache_shape),
            vc_p.reshape(1, batch, *cache_shape),
            st_s[None],
            kc_s.reshape(1, n_seq, *cache_shape),
            vc_s.reshape(1, n_seq, *cache_shape))
```

```python
import functools

import numpy as np
import jax
import jax.numpy as jnp
from jax import lax
from jax.experimental import pallas as pl
from jax.experimental.pallas import tpu as pltpu

F32 = jnp.float32
BF16 = jnp.bfloat16
HIGHEST = lax.Precision.HIGHEST

D_MODEL = 1024
PAST_LEN = 16384
GLA_HEADS = 4
GLA_DK = 128
GLA_DV = 256
GATE_RANK = 16
GATE_TAU = 16.0
SWA_HEADS = 16
SWA_KV_HEADS = 4
HEAD_DIM = 64
SWA_GROUP = SWA_HEADS // SWA_KV_HEADS
WINDOW = 128
ROPE_THETA = 10000.0
N_EXPERTS = 32
TOP_K = 4
D_FF = 1024
SWIGLU_ALPHA = 1.702
SWIGLU_LIMIT = 7.0
NORM_EPS = 1e-6

LANES = 128
SUBLANES = 8
SAMPLE_ROWS = 8
SAMPLE_SEQS = 16
TOKEN_TILE = 512
SWA_STEP_BLOCKS = 2
GLA_BLOCK = 256
GLA_MASK_BLOCK = 128
VMEM_LIMIT = 56 * 1024 * 1024
ROW_UNIT = 16
EXPERT_TILE = 1024
EXPERT_PART = 256
LOCAL_ROWS = -(-(TOP_K * TOKEN_TILE + N_EXPERTS * (ROW_UNIT - 1)) // TOKEN_TILE) * TOKEN_TILE

COL_VG, COL_RG, COL_QS, COL_AG, COL_AS = 0, 1024, 2048, 3072, 4096
COL_QG, COL_KG, COL_KS, COL_VS = 5120, 5632, 6144, 6400
D_MAIN = 6656
PROJ_CHUNK = 512
PROJ_AHEAD = 2


def _dot(a, b, precision=None):
    return jnp.dot(a, b, preferred_element_type=F32, precision=precision)


def _dot_nt(a, b, precision=None):
    return lax.dot_general(a, b, (((1,), (1,)), ((), ())), preferred_element_type=F32, precision=precision)


def _dot_tn(a, b, precision=None):
    return lax.dot_general(a, b, (((0,), (0,)), ((), ())), preferred_element_type=F32, precision=precision)


def _dot_sum(sel, x):
    hi = x.astype(BF16)
    r1 = x - hi.astype(F32)
    mid = r1.astype(BF16)
    lo = (r1 - mid.astype(F32)).astype(BF16)
    return _dot(sel, hi) + _dot(sel, mid) + _dot(sel, lo)


def _params(*sem):
    return pltpu.CompilerParams(dimension_semantics=sem, vmem_limit_bytes=VMEM_LIMIT)


def _rms(x, g):
    return x * lax.rsqrt(jnp.mean(x * x, axis=-1, keepdims=True) + NORM_EPS) * g


def _log_sigmoid(x):
    return jnp.minimum(x, 0.0) - jnp.log(1.0 + jnp.exp(-jnp.abs(x)))


def _ada_kernel(c_ref, w_ref, b_ref, o_ref):
    c = c_ref[...]
    s = c * jax.nn.sigmoid(c)
    s_hi = s.astype(BF16)
    s_lo = (s - s_hi.astype(F32)).astype(BF16)
    w = w_ref[...]
    w_hi = w.astype(BF16)
    w_lo = (w - w_hi.astype(F32)).astype(BF16)
    o_ref[...] = _dot(s_hi, w_hi) + (_dot(s_lo, w_hi) + _dot(s_hi, w_lo)) + b_ref[...]


def _ada(c_all, w_ada, b_ada):
    rows = c_all.shape[0]
    tn = 768
    return pl.pallas_call(
        _ada_kernel,
        grid=(6 * D_MODEL // tn,),
        in_specs=[pl.BlockSpec((rows, D_MODEL), lambda j: (0, 0)),
                  pl.BlockSpec((D_MODEL, tn), lambda j: (0, j)),
                  pl.BlockSpec((1, tn), lambda j: (0, j))],
        out_specs=pl.BlockSpec((rows, tn), lambda j: (0, j)),
        out_shape=jax.ShapeDtypeStruct((rows, 6 * D_MODEL), F32),
        compiler_params=_params("parallel"),
        name="ada",
    )(c_all, w_ada, b_ada.reshape(1, -1))


def _inproj_kernel(x_ref, shift_ref, scale_ref, g_ref, w_ref, wg_ref, cos_ref, sin_ref, gq_ref, gk_ref,
                   seg_ref, o_ref, og_ref):
    rows = x_ref.shape[0]
    h = _rms(x_ref[...], g_ref[...]) * (1.0 + _mod_rows(scale_ref, rows)) + _mod_rows(shift_ref, rows)
    hb = h.astype(BF16)
    cos, sin, seg = cos_ref[...], sin_ref[...], seg_ref[...]
    def has_followup(j):
        c0, c1 = j * PROJ_CHUNK, (j + 1) * PROJ_CHUNK
        return c0 < COL_AS + D_MODEL and c1 > COL_QS or c0 < COL_KS + SWA_KV_HEADS * HEAD_DIM and c1 > COL_KS

    chunks = range(D_MAIN // PROJ_CHUNK)
    busy = [j for j in chunks if has_followup(j)]
    plain = [j for j in chunks if not has_followup(j)]
    order = [j for pair in zip(busy, plain) for j in pair] + busy[len(plain):] + plain[len(busy):]
    project = lambda j: _dot(hb, w_ref[:, j * PROJ_CHUNK:(j + 1) * PROJ_CHUNK])
    ahead = [project(j) for j in order[:PROJ_AHEAD]]
    for idx, j in enumerate(order):
        sl = slice(j * PROJ_CHUNK, (j + 1) * PROJ_CHUNK)
        r = ahead.pop(0)
        if idx + PROJ_AHEAD < len(order):
            ahead.append(project(order[idx + PROJ_AHEAD]))
        blocks = []
        for m in range(PROJ_CHUNK // LANES):
            c0 = j * PROJ_CHUNK + m * LANES
            blk = r[:, m * LANES:(m + 1) * LANES]
            if COL_QS <= c0 < COL_QS + SWA_HEADS * HEAD_DIM:
                blk = _head_norm_rope(blk, gq_ref[...], cos, sin, seg) * (HEAD_DIM ** -0.5)
            elif COL_KS <= c0 < COL_KS + SWA_KV_HEADS * HEAD_DIM:
                blk = _head_norm_rope(blk, gk_ref[...], cos, sin, seg)
            elif COL_AG <= c0 < COL_AS + D_MODEL:
                blk = jax.nn.sigmoid(blk)
            blocks.append(blk)
        o_ref[:, sl] = jnp.concatenate(blocks, axis=1).astype(o_ref.dtype)
    og_ref[...] = _dot(hb, wg_ref[...])


def _mod_spec(per_token, tiles_per_seq):
    if per_token:
        return pl.BlockSpec((TOKEN_TILE // SAMPLE_ROWS, D_MODEL), lambda i, *_: (i, 0))
    return pl.BlockSpec((None, 1, D_MODEL), lambda i, *_: (i // tiles_per_seq, 0, 0))


def _mod_rows(ref, rows):
    m = ref[...]
    if m.shape[0] == 1:
        return m
    return jnp.broadcast_to(m[:, None, :], (m.shape[0], rows // m.shape[0], m.shape[1])).reshape(
        rows, m.shape[1])


def _inproj(x, shift, scale, g, w_main, w_glr, cos, sin, gq, gk, seg, out_dtype, per_token, tiles_per_seq):
    n = x.shape[0]
    mod = _mod_spec(per_token, tiles_per_seq)
    const = lambda i: (0, 0)
    rope = pl.BlockSpec((TOKEN_TILE, LANES), (lambda i: (i, 0)) if per_token else
                        (lambda i: (i % tiles_per_seq, 0)))
    return pl.pallas_call(
        _inproj_kernel,
        grid=(n // TOKEN_TILE,),
        in_specs=[pl.BlockSpec((TOKEN_TILE, D_MODEL), lambda i: (i, 0)), mod, mod,
                  pl.BlockSpec((1, D_MODEL), const),
                  pl.BlockSpec((D_MODEL, D_MAIN), const, pipeline_mode=pl.Buffered(1)),
                  pl.BlockSpec((D_MODEL, LANES), const, pipeline_mode=pl.Buffered(1)),
                  rope, rope,
                  pl.BlockSpec((1, LANES), const), pl.BlockSpec((1, LANES), const),
                  pl.BlockSpec((LANES, LANES), const)],
        out_specs=[pl.BlockSpec((TOKEN_TILE, D_MAIN), lambda i: (i, 0)),
                   pl.BlockSpec((TOKEN_TILE, LANES), lambda i: (i, 0))],
        out_shape=[jax.ShapeDtypeStruct((n, D_MAIN), out_dtype),
                   jax.ShapeDtypeStruct((n, LANES), F32)],
        compiler_params=_params("parallel"),
        name="inproj",
    )(x, shift, scale, g, w_main, w_glr, cos, sin, gq, gk, seg)


def _gla_log_gate(glr, wgk_ref, bgk):
    g_hi = glr.astype(BF16)
    g_lo = (glr - g_hi.astype(F32)).astype(BF16)
    w_hi, w_lo = wgk_ref[0], wgk_ref[1]
    x = _dot(g_hi, w_hi) + (_dot(g_lo, w_hi) + _dot(g_hi, w_lo))
    return _log_sigmoid(x + bgk) * (1.0 / GATE_TAU)


def _gla_out(o, r, g):
    r = r.astype(F32)
    return _rms(o, g) * (r * jax.nn.sigmoid(r))


def _gla_pair_levels(n):
    t = np.arange(n)[:, None]
    s = np.arange(n)[None, :]
    x = t ^ s
    top = np.where(x > 0, 1 << np.floor(np.log2(np.maximum(x, 1))).astype(np.int64), 0)
    return np.where(s > t, -1, top).astype(np.int32)


def _gla_block_ref(b, h):
    n, w = b.shape
    if 2 * h == n:
        return jnp.broadcast_to(b[h - 1:h, :], (n, w))
    if h >= SUBLANES // 2:
        picked = b.reshape(n // (2 * h), 2 * h, w)[:, h - 1:h, :]
        return jnp.broadcast_to(picked, (n // (2 * h), 2 * h, w)).reshape(n, w)
    r = lax.broadcasted_iota(jnp.int32, (n, 1), 0) % (2 * h)
    out = b
    for d in range(1, h + 1):
        out = jnp.where(r == h - 1 + d, pltpu.roll(b, d, 0), out)
    for d in range(1, h):
        out = jnp.where(r == h - 1 - d, pltpu.roll(b, n - d, 0), out)
    return out


def _gla_prompt_kernel(q_ref, k_ref, v_ref, r_ref, glr_ref, wgk_ref, bgk_ref, g_ref, lev_ref,
                       o_ref, s_ref, st_ref):
    c = pl.program_id(1)

    @pl.when(c == 0)
    def _():
        st_ref[...] = jnp.zeros_like(st_ref)

    n = GLA_BLOCK
    lev = lev_ref[...]
    causal = lax.broadcasted_iota(jnp.int32, (n, n), 1) <= lax.broadcasted_iota(jnp.int32, (n, n), 0)
    lg = _gla_log_gate(glr_ref[...], wgk_ref, bgk_ref[...])
    b = _dot_sum(jnp.where(causal, 1.0, 0.0).astype(BF16), lg)
    b_last = b[n - 1:n, :]
    q = q_ref[...].astype(F32) * (GLA_DK ** -0.5)
    k = k_ref[...].astype(F32)
    qe = (q * jnp.exp(b)).astype(BF16)
    kd = (k * jnp.exp(b_last - b)).astype(BF16)
    decay = jnp.exp(b_last)
    levels = [0] + [1 << p for p in range(n.bit_length() - 1)]
    q_lv, k_lv = [q.astype(BF16)], [k.astype(BF16)]
    for h in levels[1:]:
        e = jnp.exp(-jnp.abs(b - _gla_block_ref(b, h)))
        q_lv.append((q * e).astype(BF16))
        k_lv.append((k * e).astype(BF16))
    scaled = {level: (ql, kl) for level, ql, kl in zip(levels, q_lv, k_lv)}

    def intra(lo, hi, dk, v):
        size = hi - lo
        if size == GLA_MASK_BLOCK:
            attn = jnp.zeros((size, size), F32)
            for level in levels:
                if level < size:
                    ql, kl = scaled[level]
                    attn = jnp.where(lev == level, _dot_nt(ql[lo:hi, dk], kl[lo:hi, dk]), attn)
            return _dot(attn.astype(BF16), v[lo:hi, :])
        mid = lo + size // 2
        ql, kl = scaled[size // 2]
        cross = _dot_nt(ql[mid:hi, dk], kl[lo:mid, dk]).astype(BF16)
        return jnp.concatenate([intra(lo, mid, dk, v),
                                _dot(cross, v[lo:mid, :]) + intra(mid, hi, dk, v)], axis=0)

    for h in range(GLA_HEADS):
        dk = slice(h * GLA_DK, (h + 1) * GLA_DK)
        dv = slice(h * GLA_DV, (h + 1) * GLA_DV)
        v = v_ref[:, dv]
        st = st_ref[h]
        o = _dot_nt(qe[:, dk], st.astype(BF16)) + intra(0, n, dk, v)
        st_ref[h] = st * decay[:, dk] + _dot_tn(v, kd[:, dk])
        o_ref[:, dv] = _gla_out(o, r_ref[:, dv], g_ref[...]).astype(o_ref.dtype)

    @pl.when(c == pl.num_programs(1) - 1)
    def _():
        for h in range(GLA_HEADS):
            s_ref[h] = st_ref[h].T


def _gla_prompt(yp, glr, wgk, bgk, gnorm, batch, seq):
    nb = seq // GLA_BLOCK
    hk, hv = GLA_HEADS * GLA_DK, GLA_HEADS * GLA_DV
    tok = lambda b, c: b * nb + c
    const = lambda b, c: (0, 0)
    return pl.pallas_call(
        _gla_prompt_kernel,
        grid=(batch, nb),
        in_specs=[pl.BlockSpec((GLA_BLOCK, hk), lambda b, c: (tok(b, c), COL_QG // hk)),
                  pl.BlockSpec((GLA_BLOCK, hk), lambda b, c: (tok(b, c), COL_KG // hk)),
                  pl.BlockSpec((GLA_BLOCK, hv), lambda b, c: (tok(b, c), COL_VG // hv)),
                  pl.BlockSpec((GLA_BLOCK, hv), lambda b, c: (tok(b, c), COL_RG // hv)),
                  pl.BlockSpec((GLA_BLOCK, LANES), lambda b, c: (tok(b, c), 0)),
                  pl.BlockSpec((2, LANES, hk), lambda b, c: (0, 0, 0)),
                  pl.BlockSpec((1, hk), const),
                  pl.BlockSpec((1, GLA_DV), const),
                  pl.BlockSpec((GLA_MASK_BLOCK, GLA_MASK_BLOCK), const)],
        out_specs=[pl.BlockSpec((GLA_BLOCK, hv), lambda b, c: (tok(b, c), 0)),
                   pl.BlockSpec((None, GLA_HEADS, GLA_DK, GLA_DV), lambda b, c: (b, 0, 0, 0))],
        out_shape=[jax.ShapeDtypeStruct((batch * seq, hv), BF16),
                   jax.ShapeDtypeStruct((batch, GLA_HEADS, GLA_DK, GLA_DV), F32)],
        scratch_shapes=[pltpu.VMEM((GLA_HEADS, GLA_DV, GLA_DK), F32)],
        compiler_params=_params("parallel", "arbitrary"),
        name="gla_prompt",
    )(yp, yp, yp, yp, glr, wgk, bgk, gnorm, jnp.asarray(_gla_pair_levels(GLA_MASK_BLOCK)))


def _gla_sample_kernel(q_ref, k_ref, v_ref, r_ref, glr_ref, wgk_ref, bgk_ref, g_ref, s0_ref, lev_ref,
                       o_ref, s_ref, *, n_real):
    R = SAMPLE_ROWS
    rows = q_ref.shape[0]
    row = lax.broadcasted_iota(jnp.int32, (rows, rows), 0)
    col = lax.broadcasted_iota(jnp.int32, (rows, rows), 1)
    same = (row // R) == (col // R)
    causal = same & (col <= row)
    real = lax.broadcasted_iota(jnp.int32, (rows, 1), 0) % R < n_real
    lg = jnp.where(real, _gla_log_gate(glr_ref[...], wgk_ref, bgk_ref[...]), 0.0)
    b = _dot_sum(jnp.where(causal, 1.0, 0.0).astype(BF16), lg)
    b_last = _dot_sum(jnp.where(same, 1.0, 0.0).astype(BF16), lg)
    q = q_ref[...] * (GLA_DK ** -0.5)
    k = jnp.where(real, k_ref[...], 0.0)
    v = v_ref[...]
    qe = q * jnp.exp(b)
    kd = k * jnp.exp(b_last - b)
    lev = lev_ref[...]
    levels = [0] + [1 << p for p in range(R.bit_length() - 1)]
    q_lv, k_lv = [q.astype(BF16)], [k.astype(BF16)]
    for h in levels[1:]:
        e = jnp.exp(-jnp.abs(b - _gla_block_ref(b, h)))
        q_lv.append((q * e).astype(BF16))
        k_lv.append((k * e).astype(BF16))
    v_bf = v.astype(BF16)
    for h in range(GLA_HEADS):
        dk = slice(h * GLA_DK, (h + 1) * GLA_DK)
        dv = slice(h * GLA_DV, (h + 1) * GLA_DV)
        attn = jnp.zeros((rows, rows), F32)
        for level, ql, kl in zip(levels, q_lv, k_lv):
            attn = jnp.where(lev == level, _dot_nt(ql[:, dk], kl[:, dk]), attn)
        o_intra = _dot(attn.astype(BF16), v_bf[:, dv])
        decay_t = jnp.exp(b_last[:, dk]).T
        outs = []
        for s in range(rows // R):
            sl = slice(s * R, (s + 1) * R)
            s0 = s0_ref[s, h]
            outs.append(_dot(qe[sl, dk], s0) + o_intra[sl, :])
            s_ref[s, h] = s0 * decay_t[:, s * R:s * R + 1] + _dot_tn(kd[sl, dk], v[sl, dv])
        o_ref[:, dv] = _gla_out(jnp.concatenate(outs, axis=0), r_ref[:, dv], g_ref[...])


def _gla_sample(ys, glr, wgk, bgk, gnorm, state, n_seq, n_real):
    R = SAMPLE_SEQS * SAMPLE_ROWS
    hk, hv = GLA_HEADS * GLA_DK, GLA_HEADS * GLA_DV
    st_spec = pl.BlockSpec((SAMPLE_SEQS, GLA_HEADS, GLA_DK, GLA_DV), lambda b: (b, 0, 0, 0))
    lev = _gla_pair_levels(R)
    lev = np.where(lev >= SAMPLE_ROWS, -1, lev)
    return pl.pallas_call(
        functools.partial(_gla_sample_kernel, n_real=n_real),
        grid=(n_seq // SAMPLE_SEQS,),
        in_specs=[pl.BlockSpec((R, hk), lambda b: (b, COL_QG // hk)),
                  pl.BlockSpec((R, hk), lambda b: (b, COL_KG // hk)),
                  pl.BlockSpec((R, hv), lambda b: (b, COL_VG // hv)),
                  pl.BlockSpec((R, hv), lambda b: (b, COL_RG // hv)),
                  pl.BlockSpec((R, LANES), lambda b: (b, 0)),
                  pl.BlockSpec((2, LANES, hk), lambda b: (0, 0, 0)),
                  pl.BlockSpec((1, hk), lambda b: (0, 0)),
                  pl.BlockSpec((1, GLA_DV), lambda b: (0, 0)),
                  st_spec,
                  pl.BlockSpec((R, R), lambda b: (0, 0))],
        out_specs=[pl.BlockSpec((R, hv), lambda b: (b, 0)), st_spec],
        out_shape=[jax.ShapeDtypeStruct((n_seq * SAMPLE_ROWS, hv), F32),
                   jax.ShapeDtypeStruct((n_seq, GLA_HEADS, GLA_DK, GLA_DV), F32)],
        compiler_params=_params("parallel"),
        name="gla_sample",
    )(ys, ys, ys, ys, glr, wgk, bgk, gnorm, state, jnp.asarray(lev))


def _lane_lower(shape):
    return lax.broadcasted_iota(jnp.int32, shape, len(shape) - 1) % LANES < HEAD_DIM


def _head_norm_rope(x, g, cos, sin, seg):
    ss = _dot((x * x).astype(BF16), seg)
    y = x * lax.rsqrt(ss * (1.0 / HEAD_DIM) + NORM_EPS) * g
    half = HEAD_DIM // 2
    lane = lax.broadcasted_iota(jnp.int32, y.shape, 1)
    rot = jnp.where(lane % HEAD_DIM < half, pltpu.roll(y, LANES - half, 1), pltpu.roll(y, half, 1))
    return y * cos + rot * sin


def _both_halves(blk, half):
    sw = pltpu.roll(blk, HEAD_DIM, 1)
    lower = _lane_lower(blk.shape)
    return jnp.where(lower, blk, sw) if half == 0 else jnp.where(lower, sw, blk)


def _stack_heads(q_blocks):
    parts = []
    for qb in q_blocks:
        lower = _lane_lower(qb.shape)
        zero = jnp.zeros_like(qb)
        parts += [jnp.where(lower, qb, zero), jnp.where(lower, zero, qb)]
    return jnp.concatenate(parts, axis=0)


def _swa_prompt_kernel(sink_ref, q_ref, k_ref, v_ref, o_ref, ko_ref, vo_ref, kprev_ref, vprev_ref):
    n = pl.program_id(1)
    W = WINDOW

    @pl.when(n == 0)
    def _():
        kprev_ref[...] = jnp.zeros_like(kprev_ref)
        vprev_ref[...] = jnp.zeros_like(vprev_ref)

    k_prev, v_prev = kprev_ref[...], vprev_ref[...]
    for sub in range(SWA_STEP_BLOCKS):
        tok = slice(sub * W, (sub + 1) * W)
        k_cur = k_ref[tok, :].astype(F32)
        v_cur = v_ref[tok, :].astype(F32)
        prev_fill = jnp.where(n > 0, 0.0, -jnp.inf) if sub == 0 else 0.0
        _swa_block(sink_ref, q_ref, o_ref, tok, k_prev, v_prev, k_cur, v_cur, prev_fill)
        k_prev, v_prev = k_cur, v_cur
    ko_ref[...] = k_prev
    vo_ref[...] = v_prev
    kprev_ref[...] = k_prev
    vprev_ref[...] = v_prev


def _swa_block(sink_ref, q_ref, o_ref, tok, k_prev, v_prev, k_cur, v_cur, prev_fill):
    W = WINDOW
    qi = lax.broadcasted_iota(jnp.int32, (W, W), 0)
    ki = lax.broadcasted_iota(jnp.int32, (W, W), 1)
    from_cur = ki <= qi
    for kh in range(SWA_KV_HEADS):
        blk = slice((kh // 2) * LANES, (kh // 2 + 1) * LANES)
        kb_prev = _both_halves(k_prev[:, blk], kh % 2).astype(BF16)
        kb_cur = _both_halves(k_cur[:, blk], kh % 2).astype(BF16)
        vb_prev = _both_halves(v_prev[:, blk], kh % 2).astype(BF16)
        vb_cur = _both_halves(v_cur[:, blk], kh % 2).astype(BF16)
        qblocks = [q_ref[tok, (2 * kh + j) * LANES:(2 * kh + j + 1) * LANES] for j in range(2)]
        qs = _stack_heads(qblocks)
        s_prev = _dot_nt(qs, kb_prev)
        s_cur = _dot_nt(qs, kb_cur)
        outs = []
        for g in range(SWA_GROUP):
            rows = slice(g * W, (g + 1) * W)
            sg = jnp.where(from_cur, s_cur[rows, :], s_prev[rows, :] + prev_fill)
            sink = sink_ref[kh * SWA_GROUP + g]
            m = jnp.maximum(jnp.max(sg, axis=-1, keepdims=True), sink)
            p = jnp.exp(sg - m)
            denom = jnp.sum(p, axis=-1, keepdims=True) + jnp.exp(sink - m)
            p_cur = jnp.where(from_cur, p, 0.0).astype(BF16)
            p_prev = jnp.where(from_cur, 0.0, p).astype(BF16)
            outs.append((_dot(p_prev, vb_prev) + _dot(p_cur, vb_cur)) / denom)
        lower = _lane_lower((W, LANES))
        for j in range(2):
            c0 = (2 * kh + j) * LANES
            o_ref[tok, c0:c0 + LANES] = jnp.where(lower, outs[2 * j], outs[2 * j + 1]).astype(o_ref.dtype)


def _swa_prompt(yp, sinks, batch, seq):
    rows = SWA_STEP_BLOCKS * WINDOW
    nb = seq // rows
    kvw = SWA_KV_HEADS * HEAD_DIM
    tok = lambda b, n: b * nb + n
    cache_spec = pl.BlockSpec((None, WINDOW, kvw), lambda b, n: (b, 0, 0))
    return pl.pallas_call(
        _swa_prompt_kernel,
        grid=(batch, nb),
        in_specs=[pl.BlockSpec(memory_space=pltpu.SMEM),
                  pl.BlockSpec((rows, D_MODEL), lambda b, n: (tok(b, n), COL_QS // D_MODEL)),
                  pl.BlockSpec((rows, kvw), lambda b, n: (tok(b, n), COL_KS // kvw)),
                  pl.BlockSpec((rows, kvw), lambda b, n: (tok(b, n), COL_VS // kvw))],
        out_specs=[pl.BlockSpec((rows, D_MODEL), lambda b, n: (tok(b, n), 0)), cache_spec, cache_spec],
        out_shape=[jax.ShapeDtypeStruct((batch * seq, D_MODEL), BF16),
                   jax.ShapeDtypeStruct((batch, WINDOW, kvw), F32),
                   jax.ShapeDtypeStruct((batch, WINDOW, kvw), F32)],
        scratch_shapes=[pltpu.VMEM((WINDOW, kvw), F32), pltpu.VMEM((WINDOW, kvw), F32)],
        compiler_params=_params("parallel", "arbitrary"),
        name="swa_prompt",
    )(sinks, yp, yp, yp)


def _shift_cache(cache, new, n_real):
    R = SAMPLE_ROWS
    rolled = pltpu.roll(cache, WINDOW - n_real, 0)
    tail_new = pltpu.roll(new, R - n_real, 0)
    row = lax.broadcasted_iota(jnp.int32, (R, cache.shape[1]), 0)
    tail = jnp.where(row < R - n_real, rolled[WINDOW - R:, :], tail_new)
    return jnp.concatenate([rolled[:WINDOW - R, :], tail], axis=0)


def _swa_sample_kernel(q_ref, k_ref, v_ref, kc_ref, vc_ref, sink_ref, o_ref, ko_ref, vo_ref, *, n_real):
    R, W = SAMPLE_ROWS, WINDOW
    k_new = k_ref[...]
    v_new = v_ref[...]
    q_pairs = [q_ref[:, j * LANES:(j + 1) * LANES] for j in range(SWA_HEADS // 2)]

    hr = SWA_HEADS * R
    t_c = lax.broadcasted_iota(jnp.int32, (hr, W), 0) % R
    mask_c = lax.broadcasted_iota(jnp.int32, (hr, W), 1) > t_c
    t_n = lax.broadcasted_iota(jnp.int32, (hr, R), 0) % R
    mask_n = lax.broadcasted_iota(jnp.int32, (hr, R), 1) <= t_n
    sink = sink_ref[...]
    lower = _lane_lower((R, LANES))
    zeros = jnp.zeros((R, LANES), F32)
    for s in range(q_ref.shape[0] // R):
        sl = slice(s * R, (s + 1) * R)
        kc, vc = kc_ref[s], vc_ref[s]
        kn, vn = k_new[sl, :], v_new[sl, :]
        ko_ref[s] = _shift_cache(kc, kn, n_real)
        vo_ref[s] = _shift_cache(vc, vn, n_real)
        q_rows = []
        for h in range(SWA_HEADS):
            kh = h // SWA_GROUP
            x = q_pairs[h // 2][sl, :]
            if h % 2 != kh % 2:
                x = pltpu.roll(x, HEAD_DIM, 1)
            x = jnp.where(lower, x, zeros) if kh % 2 == 0 else jnp.where(lower, zeros, x)
            q_rows.append(jnp.concatenate([x, zeros] if kh // 2 == 0 else [zeros, x], axis=1))
        qbd = jnp.concatenate(q_rows, axis=0)
        sc = jnp.where(mask_c, _dot_nt(qbd.astype(BF16), kc.astype(BF16)), -jnp.inf)
        sn = jnp.where(mask_n, _dot_nt(qbd, kn), -jnp.inf)
        m = jnp.maximum(jnp.maximum(jnp.max(sc, axis=-1, keepdims=True),
                                    jnp.max(sn, axis=-1, keepdims=True)), sink)
        pc, pn = jnp.exp(sc - m), jnp.exp(sn - m)
        denom = (jnp.sum(pc, axis=-1, keepdims=True) + jnp.sum(pn, axis=-1, keepdims=True)
                 + jnp.exp(sink - m))
        o = (_dot(pc.astype(BF16), vc.astype(BF16)) + _dot(pn, vn)) / denom
        for j in range(SWA_HEADS // 2):
            halves = []
            for h in (2 * j, 2 * j + 1):
                kh = h // SWA_GROUP
                y = o[h * R:(h + 1) * R, (kh // 2) * LANES:(kh // 2 + 1) * LANES]
                halves.append(pltpu.roll(y, HEAD_DIM, 1) if h % 2 != kh % 2 else y)
            o_ref[sl, j * LANES:(j + 1) * LANES] = jnp.where(lower, halves[0], halves[1])


def _swa_sample(ys, sink_rows, kcache, vcache, n_seq, n_real):
    R = SAMPLE_SEQS * SAMPLE_ROWS
    kvw = SWA_KV_HEADS * HEAD_DIM
    const = lambda b: (0, 0)
    cache_spec = pl.BlockSpec((SAMPLE_SEQS, WINDOW, kvw), lambda b: (b, 0, 0))
    return pl.pallas_call(
        functools.partial(_swa_sample_kernel, n_real=n_real),
        grid=(n_seq // SAMPLE_SEQS,),
        in_specs=[pl.BlockSpec((R, D_MODEL), lambda b: (b, COL_QS // D_MODEL)),
                  pl.BlockSpec((R, kvw), lambda b: (b, COL_KS // kvw)),
                  pl.BlockSpec((R, kvw), lambda b: (b, COL_VS // kvw)),
                  cache_spec, cache_spec,
                  pl.BlockSpec((SWA_HEADS * SAMPLE_ROWS, 1), const)],
        out_specs=[pl.BlockSpec((R, D_MODEL), lambda b: (b, 0)), cache_spec, cache_spec],
        out_shape=[jax.ShapeDtypeStruct((n_seq * SAMPLE_ROWS, D_MODEL), F32),
                   jax.ShapeDtypeStruct((n_seq, WINDOW, kvw), F32),
                   jax.ShapeDtypeStruct((n_seq, WINDOW, kvw), F32)],
        compiler_params=_params("parallel"),
        name="swa_sample",
    )(ys, ys, ys, kcache, vcache, sink_rows)


def _post_kernel(ag_ref, as_ref, og_ref, os_ref, x_ref, gate_ref, shift_ref, scale_ref, g2_ref,
                 wo_ref, wr_ref, br_ref, x1_ref, h2_ref, slot_ref, gatek_ref, cnt_ref, *, n_valid):
    merged = (ag_ref[...].astype(F32) * og_ref[...].astype(F32)
              + as_ref[...].astype(F32) * os_ref[...].astype(F32))
    y = _dot(merged.astype(BF16), wo_ref[...])
    rows = x_ref.shape[0]
    x1 = x_ref[...] + _mod_rows(gate_ref, rows) * y
    x1_ref[...] = x1
    h2 = _rms(x1, g2_ref[...]) * (1.0 + _mod_rows(scale_ref, rows)) + _mod_rows(shift_ref, rows)
    h2_hi = h2.astype(BF16)
    h2_ref[...] = h2_hi

    h2_lo = (h2 - h2_hi.astype(F32)).astype(BF16)
    w_hi, w_lo = wr_ref[0], wr_ref[1]
    logits = _dot(h2_hi, w_hi) + (_dot(h2_lo, w_hi) + _dot(h2_hi, w_lo)) + br_ref[...]
    lane_i = lax.broadcasted_iota(jnp.int32, logits.shape, 1)
    lane = lane_i.astype(F32)
    work = logits
    vals, hots = [], []
    for _ in range(TOP_K):
        m = jnp.max(work, axis=-1, keepdims=True)
        idx = jnp.min(jnp.where(work == m, lane, float(LANES)), axis=-1, keepdims=True)
        hot = lane == idx
        vals.append(m)
        hots.append(hot)
        work = jnp.where(hot, -jnp.inf, work)
    exps = [jnp.exp(v - vals[0]) for v in vals]
    denom = exps[0] + exps[1] + exps[2] + exps[3]

    tm = logits.shape[0]
    valid = lax.broadcasted_iota(jnp.int32, (tm, 1), 0) % SAMPLE_ROWS < n_valid
    sel = jnp.zeros_like(logits)
    for hot in hots:
        sel = jnp.where(hot, 1.0, sel)
    sel = jnp.where(valid, sel, 0.0)
    earlier = (lax.broadcasted_iota(jnp.int32, (tm, tm), 1)
               < lax.broadcasted_iota(jnp.int32, (tm, tm), 0))
    rank = _dot(jnp.where(earlier, 1.0, 0.0).astype(BF16), sel.astype(BF16))
    cnt = jnp.sum(sel, axis=0, keepdims=True)
    cnt_pad = jnp.floor((cnt + (ROW_UNIT - 1.0)) * (1.0 / ROW_UNIT)) * ROW_UNIT
    below = (lax.broadcasted_iota(jnp.int32, (LANES, LANES), 0)
             < lax.broadcasted_iota(jnp.int32, (LANES, LANES), 1))
    seg_start = _dot(jnp.broadcast_to(cnt_pad, (8, LANES)), jnp.where(below, 1.0, 0.0), HIGHEST)[0:1]
    pos = seg_start + rank
    slots = jnp.full_like(logits, -1.0)
    gates = jnp.zeros_like(logits)
    for k in range(TOP_K):
        s_k = jnp.sum(jnp.where(hots[k], pos, 0.0), axis=-1, keepdims=True)
        slots = jnp.where(lane_i == k, s_k, slots)
        gates = jnp.where(lane_i == k, exps[k] / denom, gates)
    slot_ref[...] = jnp.where(valid, slots, -1.0)
    gatek_ref[...] = gates
    cnt_ref[...] = cnt


def _post(y_all, o_gla, o_swa, x, gate, shift, scale, g2, wo, wr, br, per_token, tiles_per_seq, n_valid):
    n = x.shape[0]
    mod = _mod_spec(per_token, tiles_per_seq)
    row = lambda i: (i, 0)
    const = lambda i: (0, 0)
    wide = pl.BlockSpec((TOKEN_TILE, D_MODEL), row)
    narrow = pl.BlockSpec((TOKEN_TILE, LANES), row)
    return pl.pallas_call(
        functools.partial(_post_kernel, n_valid=n_valid),
        grid=(n // TOKEN_TILE,),
        in_specs=[pl.BlockSpec((TOKEN_TILE, D_MODEL), lambda i: (i, COL_AG // D_MODEL)),
                  pl.BlockSpec((TOKEN_TILE, D_MODEL), lambda i: (i, COL_AS // D_MODEL)),
                  wide, wide, wide, mod, mod, mod,
                  pl.BlockSpec((1, D_MODEL), const),
                  pl.BlockSpec((D_MODEL, D_MODEL), const),
                  pl.BlockSpec((2, D_MODEL, LANES), lambda i: (0, 0, 0)),
                  pl.BlockSpec((1, LANES), const)],
        out_specs=[wide, wide, narrow, narrow, pl.BlockSpec((None, 1, LANES), lambda i: (i, 0, 0))],
        out_shape=[jax.ShapeDtypeStruct((n, D_MODEL), F32),
                   jax.ShapeDtypeStruct((n, D_MODEL), BF16),
                   jax.ShapeDtypeStruct((n, LANES), F32),
                   jax.ShapeDtypeStruct((n, LANES), F32),
                   jax.ShapeDtypeStruct((n // TOKEN_TILE, 1, LANES), F32)],
        compiler_params=_params("parallel"),
        name="post",
    )(y_all, y_all, o_gla, o_swa, x, gate, shift, scale, g2, wo, wr, br)


def _slot_matrix(slot_cols, weights, chunk):
    tm = slot_cols[0].shape[0]
    j = lax.broadcasted_iota(jnp.int32, (tm, tm), 1) + chunk * tm
    out = jnp.zeros((tm, tm), F32)
    for s, w in zip(slot_cols, weights):
        out = jnp.where(s == j, w, out)
    return out.astype(BF16)


def _dispatch_kernel(nu_ref, seg_ref, ntail_ref, tail_ref, hp_ref, slp_ref, hs_ref, sls_ref, xg_ref,
                     sorted_ref, zero_ref, sem, tail_sem, *, prompt_tiles):
    t = pl.program_id(0)
    last = pl.num_programs(0) - 1
    buf = t % 2

    def sort_tile(h_ref, slot_ref):
        tm = h_ref.shape[0]
        slots = slot_ref[...].astype(jnp.int32)
        slot_cols = [slots[:, k:k + 1] for k in range(TOP_K)]
        h = h_ref[...]
        for c in range(LOCAL_ROWS // tm):
            onehot = _slot_matrix(slot_cols, [1.0] * TOP_K, c)
            upc = tm // ROW_UNIT
            sorted_ref[buf, c * upc:(c + 1) * upc] = _dot_tn(onehot, h).astype(BF16).reshape(
                upc, ROW_UNIT, D_MODEL)

    @pl.when(t < prompt_tiles)
    def _():
        sort_tile(hp_ref, slp_ref)

    @pl.when(t >= prompt_tiles)
    def _():
        sort_tile(hs_ref, sls_ref)

    def start_all(tile, b):
        def body(e, c):
            n = seg_ref[tile, 0, e]

            @pl.when(n > 0)
            def _():
                pltpu.make_async_copy(sorted_ref.at[b, pl.ds(seg_ref[tile, 1, e], n)],
                                      xg_ref.at[pl.ds(seg_ref[tile, 2, e], n)], sem.at[b]).start()
            return c

        lax.fori_loop(0, N_EXPERTS, body, 0)

    def wait_all(tile, b):
        n_units = nu_ref[tile]

        @pl.when(n_units > 0)
        def _():
            pltpu.make_async_copy(sorted_ref.at[b, pl.ds(0, n_units)],
                                  xg_ref.at[pl.ds(0, n_units)], sem.at[b]).wait()

    @pl.when(t > 0)
    def _():
        wait_all(t - 1, 1 - buf)

    start_all(t, buf)

    def tail_copy(i):
        return pltpu.make_async_copy(zero_ref, xg_ref.at[tail_ref[i]], tail_sem)

    @pl.when(t == last)
    def _():
        zero_ref[...] = jnp.zeros_like(zero_ref)
        n_tail = ntail_ref[0]
        lax.fori_loop(0, n_tail, lambda i, c: (tail_copy(i).start(), c)[1], 0)
        lax.fori_loop(0, n_tail, lambda i, c: (tail_copy(i).wait(), c)[1], 0)
        wait_all(t, buf)


def _dispatch(h2_p, slots_p, h2_s, slots_s, n_units, seg, n_tail, tail_dst, rows_max):
    p_tiles = h2_p.shape[0] // TOKEN_TILE
    s_tiles = h2_s.shape[0] // TOKEN_TILE
    p_row = lambda t, *_: (jnp.minimum(t, p_tiles - 1), 0)
    s_row = lambda t, *_: (jnp.maximum(t - p_tiles, 0), 0)
    return pl.pallas_call(
        functools.partial(_dispatch_kernel, prompt_tiles=p_tiles),
        grid_spec=pltpu.PrefetchScalarGridSpec(
            num_scalar_prefetch=4,
            grid=(p_tiles + s_tiles,),
            in_specs=[pl.BlockSpec((TOKEN_TILE, D_MODEL), p_row),
                      pl.BlockSpec((TOKEN_TILE, LANES), p_row),
                      pl.BlockSpec((TOKEN_TILE, D_MODEL), s_row),
                      pl.BlockSpec((TOKEN_TILE, LANES), s_row)],
            out_specs=pl.BlockSpec(memory_space=pl.ANY),
            scratch_shapes=[pltpu.VMEM((2, LOCAL_ROWS // ROW_UNIT, ROW_UNIT, D_MODEL), BF16),
                            pltpu.VMEM((ROW_UNIT, D_MODEL), BF16),
                            pltpu.SemaphoreType.DMA((2,)), pltpu.SemaphoreType.DMA]),
        out_shape=jax.ShapeDtypeStruct((rows_max // ROW_UNIT, ROW_UNIT, D_MODEL), BF16),
        compiler_params=_params("arbitrary"),
        name="dispatch",
    )(n_units, seg, n_tail, tail_dst, h2_p, slots_p, h2_s, slots_s).reshape(rows_max, D_MODEL)


def _expert_kernel(tg_ref, ge_ref, ng_ref, nused_ref, rows_ref, x_ref, bgu_ref, bd_ref, wgu_hbm, wd_hbm,
                   y_ref, wgu_f32, wd_f32, wgu_bf, wd_bf, sem):
    i = pl.program_id(0)

    def fetch(g, b):
        e = ge_ref[g]
        return (pltpu.make_async_copy(wgu_hbm.at[e], wgu_f32.at[b], sem.at[0, b]),
                pltpu.make_async_copy(wd_hbm.at[e], wd_f32.at[b], sem.at[1, b]))

    @pl.when(i == 0)
    def _():
        for cp in fetch(0, 0):
            cp.start()

    @pl.when(i < nused_ref[0])
    def _():
        g = tg_ref[i]
        b = g % 2

        @pl.when((i == 0) | (g != tg_ref[jnp.maximum(i - 1, 0)]))
        def _():
            @pl.when(g + 1 < ng_ref[0])
            def _():
                for cp in fetch(g + 1, 1 - b):
                    cp.start()

            for cp in fetch(g, b):
                cp.wait()
            wgu_bf[...] = wgu_f32[b].astype(BF16)
            wd_bf[...] = wd_f32[b].astype(BF16)

        for part in range(EXPERT_TILE // EXPERT_PART):
            @pl.when(rows_ref[i] > part * EXPERT_PART)
            def _():
                sl = slice(part * EXPERT_PART, (part + 1) * EXPERT_PART)
                gu = _dot(x_ref[sl, :], wgu_bf[...]) + bgu_ref[...]
                gate = jnp.minimum(gu[:, :D_FF], SWIGLU_LIMIT)
                up = jnp.clip(gu[:, D_FF:], -SWIGLU_LIMIT, SWIGLU_LIMIT)
                act = (up + 1.0) * gate * jax.nn.sigmoid(SWIGLU_ALPHA * gate)
                y_ref[sl, :] = (_dot(act.astype(BF16), wd_bf[...]) + bd_ref[...]).astype(y_ref.dtype)


def _experts(xg, tile_group, group_expert, n_groups, n_used, tile_rows, wgu, bgu, wd, bd):
    rows_max = xg.shape[0]
    used = lambda i, nu: jnp.maximum(jnp.minimum(i, nu[0] - 1), 0)
    row = lambda i, tg, ge, ng, nu, tr: (used(i, nu), 0)
    exp = lambda i, tg, ge, ng, nu, tr: (ge[tg[used(i, nu)]], 0, 0)
    return pl.pallas_call(
        _expert_kernel,
        grid_spec=pltpu.PrefetchScalarGridSpec(
            num_scalar_prefetch=5,
            grid=(rows_max // EXPERT_TILE,),
            in_specs=[pl.BlockSpec((EXPERT_TILE, D_MODEL), row),
                      pl.BlockSpec((None, 1, 2 * D_FF), exp),
                      pl.BlockSpec((None, 1, D_MODEL), exp),
                      pl.BlockSpec(memory_space=pl.ANY),
                      pl.BlockSpec(memory_space=pl.ANY)],
            out_specs=pl.BlockSpec((EXPERT_TILE, D_MODEL), row),
            scratch_shapes=[pltpu.VMEM((2, D_MODEL, 2 * D_FF), F32), pltpu.VMEM((2, D_FF, D_MODEL), F32),
                            pltpu.VMEM((D_MODEL, 2 * D_FF), BF16), pltpu.VMEM((D_FF, D_MODEL), BF16),
                            pltpu.SemaphoreType.DMA((2, 2))]),
        out_shape=jax.ShapeDtypeStruct((rows_max, D_MODEL), BF16),
        compiler_params=_params("arbitrary"),
        name="experts",
    )(tile_group, group_expert, n_groups, n_used, tile_rows, xg, bgu, bd, wgu, wd)


def _combine_kernel(nu_ref, seg_ref, slot_ref, gatek_ref, x_ref, gmlp_ref, y_ref, o_ref, ys_ref, sem,
                    *, tile_offset):
    j = pl.program_id(0)
    t = j + tile_offset
    buf = j % 2
    tm = x_ref.shape[0]

    def fetch(tile, b):
        def body(e, c):
            n = seg_ref[tile, 0, e]

            @pl.when(n > 0)
            def _():
                pltpu.make_async_copy(y_ref.at[pl.ds(seg_ref[tile, 2, e], n)],
                                      ys_ref.at[b, pl.ds(seg_ref[tile, 1, e], n)], sem.at[b]).start()
            return c

        lax.fori_loop(0, N_EXPERTS, body, 0)
        n_units = nu_ref[tile]

        def zero_unit(i, c):
            ys_ref[b, i] = jnp.zeros((ROW_UNIT, D_MODEL), ys_ref.dtype)
            return c

        lax.fori_loop(n_units, LOCAL_ROWS // ROW_UNIT, zero_unit, 0)

    @pl.when(j == 0)
    def _():
        fetch(t, buf)

    @pl.when(j + 1 < pl.num_programs(0))
    def _():
        fetch(t + 1, 1 - buf)

    n_units = nu_ref[t]

    @pl.when(n_units > 0)
    def _():
        pltpu.make_async_copy(y_ref.at[pl.ds(0, n_units)], ys_ref.at[buf, pl.ds(0, n_units)],
                              sem.at[buf]).wait()

    slots = slot_ref[...].astype(jnp.int32)
    gates = gatek_ref[...]
    slot_cols = [slots[:, k:k + 1] for k in range(TOP_K)]
    gate_cols = [gates[:, k:k + 1] for k in range(TOP_K)]
    acc = jnp.zeros((tm, D_MODEL), F32)
    for c in range(LOCAL_ROWS // tm):
        upc = tm // ROW_UNIT
        rows_c = ys_ref[buf, c * upc:(c + 1) * upc].reshape(tm, D_MODEL)
        acc = acc + _dot(_slot_matrix(slot_cols, gate_cols, c), rows_c)
    o_ref[...] = x_ref[...] + _mod_rows(gmlp_ref, tm) * acc


def _combine(y, slots, gates, x1, gmlp, n_units, seg, tile_offset, per_token, tiles_per_seq):
    n = x1.shape[0]
    mod = _mod_spec(per_token, tiles_per_seq)
    wide = pl.BlockSpec((TOKEN_TILE, D_MODEL), lambda i, *_: (i, 0))
    narrow = pl.BlockSpec((TOKEN_TILE, LANES), lambda i, *_: (i, 0))
    return pl.pallas_call(
        functools.partial(_combine_kernel, tile_offset=tile_offset),
        grid_spec=pltpu.PrefetchScalarGridSpec(
            num_scalar_prefetch=2,
            grid=(n // TOKEN_TILE,),
            in_specs=[narrow, narrow, wide, mod, pl.BlockSpec(memory_space=pl.ANY)],
            out_specs=wide,
            scratch_shapes=[pltpu.VMEM((2, LOCAL_ROWS // ROW_UNIT, ROW_UNIT, D_MODEL), BF16),
                            pltpu.SemaphoreType.DMA((2,))]),
        out_shape=jax.ShapeDtypeStruct((n, D_MODEL), F32),
        compiler_params=_params("arbitrary"),
        name="combine",
    )(n_units, seg, slots, gates, x1, gmlp, y.reshape(-1, ROW_UNIT, D_MODEL))


def _route_tables(cnt, rows_max):
    units = (cnt + ROW_UNIT - 1) // ROW_UNIT
    group_units = jnp.sum(units, axis=0)
    upt = EXPERT_TILE // ROW_UNIT
    group_pad = (group_units + upt - 1) // upt * upt
    group_end = jnp.cumsum(group_pad)
    group_start = group_end - group_pad
    seg_start = group_start[None, :] + jnp.cumsum(units, axis=0) - units
    local_end = jnp.cumsum(units, axis=1)
    local_start = local_end - units
    n_units = local_end[:, -1].astype(jnp.int32)

    def pick(lo, hi, pos, value):
        return jnp.sum(jnp.where((pos >= lo) & (pos < hi), value, 0), axis=-1).astype(jnp.int32)

    seg = jnp.stack([units, local_start, seg_start], axis=1).astype(jnp.int32)

    upp = EXPERT_PART // ROW_UNIT
    n_tail_e = (group_units + upp - 1) // upp * upp - group_units
    j = jnp.arange(N_EXPERTS * upp, dtype=jnp.int32)[:, None]
    tail_end = jnp.cumsum(n_tail_e)
    tail_start = tail_end - n_tail_e
    tail_hbm = pick(tail_start[None, :], tail_end[None, :], j,
                    (group_start + group_units)[None, :] + j - tail_start[None, :])
    n_tail = tail_end[-1:].astype(jnp.int32)

    r = jnp.arange(rows_max // EXPERT_TILE, dtype=jnp.int32)[:, None] * upt
    n_used = (group_end[-1:] // upt).astype(jnp.int32)
    nonempty = group_units > 0
    group_of_expert = jnp.cumsum(nonempty) - 1
    tile_rows = pick(group_start[None, :], group_end[None, :], r,
                     jnp.clip(((group_start + group_units)[None, :] - r) * ROW_UNIT, 0, EXPERT_TILE))
    tile_group = pick(group_start[None, :], group_end[None, :], r, group_of_expert[None, :])
    g = jnp.arange(N_EXPERTS, dtype=jnp.int32)
    group_expert = jnp.sum(jnp.where(nonempty[None, :] & (group_of_expert[None, :] == g[:, None]),
                                     g[None, :], 0), axis=-1).astype(jnp.int32)
    n_groups = jnp.sum(nonempty)[None].astype(jnp.int32)
    return (n_units, seg, n_tail, tail_hbm,
            tile_group, group_expert, n_groups, n_used, tile_rows)


def _rope_tables(pos):
    half = HEAD_DIM // 2
    lane = np.arange(LANES)
    inv = jnp.asarray(ROPE_THETA, F32) ** (-jnp.asarray(lane % half, F32) / half)
    sign = jnp.asarray(np.where(lane % HEAD_DIM < half, -1.0, 1.0), F32)
    ang = pos.astype(F32)[:, None] * inv[None, :]
    return jnp.cos(ang), jnp.sin(ang) * sign[None, :]


def kernel(x_prompt, x_sample, c_prompt, c_sample, state_gla, cache_swa_k, cache_swa_v, w_ada, b_ada,
           norm1_g, norm2_g, w_in, w_gk2, b_gk, gla_norm_g, q_norm_g, k_norm_g, attn_sinks, w_o,
           w_router, b_router, w_gate_up, b_gate_up, w_down, b_down):
    batch, seq, d = x_prompt.shape
    n_seq, n_real, _ = x_sample.shape
    depth = w_in.shape[0]
    assert depth == 1 and d == D_MODEL and n_real <= SAMPLE_ROWS
    assert seq % TOKEN_TILE == 0 and (n_seq * SAMPLE_ROWS) % TOKEN_TILE == 0 and n_seq % SAMPLE_SEQS == 0
    R = SAMPLE_ROWS
    kvw = SWA_KV_HEADS * HEAD_DIM
    tiles_per_seq = seq // TOKEN_TILE

    qg, kg, vg, rg, glr_w, qs, ks, vs, ag, as_ = jnp.split(
        w_in[0], [512, 1024, 2048, 3072, 3088, 4112, 4368, 4624, 5648], axis=1)
    w_main = jnp.concatenate([vg, rg, qs, ag, as_, qg, kg, ks, vs], axis=1).astype(BF16)
    w_glr = jnp.pad(glr_w, ((0, 0), (0, LANES - GATE_RANK))).astype(BF16)
    wgk = jnp.pad(w_gk2[0], ((0, LANES - GATE_RANK), (0, 0)))
    wgk_hi = wgk.astype(BF16)
    wgk = jnp.stack([wgk_hi, (wgk - wgk_hi.astype(F32)).astype(BF16)])
    bgk = b_gk[0].reshape(1, -1)
    gnorm = gla_norm_g[0].reshape(1, -1)
    gq = jnp.tile(q_norm_g[0], LANES // HEAD_DIM).reshape(1, LANES)
    gk = jnp.tile(k_norm_g[0], LANES // HEAD_DIM).reshape(1, LANES)
    seg = jnp.asarray(np.kron(np.eye(LANES // HEAD_DIM), np.ones((HEAD_DIM, HEAD_DIM))), BF16)
    sinks = attn_sinks[0]
    wo = w_o[0].astype(BF16)
    wr = jnp.pad(w_router[0], ((0, 0), (0, LANES - N_EXPERTS)))
    wr_hi = wr.astype(BF16)
    wr = jnp.stack([wr_hi, (wr - wr_hi.astype(F32)).astype(BF16)])
    br = jnp.pad(b_router[0], (0, LANES - N_EXPERTS), constant_values=-1e30).reshape(1, LANES)
    bgu = b_gate_up[0].reshape(N_EXPERTS, 1, 2 * D_FF)
    bd = b_down[0].reshape(N_EXPERTS, 1, D_MODEL)
    g1 = norm1_g[0].reshape(1, -1)
    g2 = norm2_g[0].reshape(1, -1)

    n_c = batch + n_seq
    c_all = jnp.pad(jnp.concatenate([c_prompt, c_sample], axis=0), ((0, -n_c % 8), (0, 0)))
    m_all = _ada(c_all, w_ada[0], b_ada[0])
    mp = [m_all[:batch, i * d:(i + 1) * d].reshape(batch, 1, d) for i in range(6)]
    ms = [m_all[batch:n_c, i * d:(i + 1) * d] for i in range(6)]

    xp = x_prompt.reshape(batch * seq, d)
    xs = jnp.pad(x_sample, ((0, 0), (0, R - n_real), (0, 0))).reshape(n_seq * R, d)
    cos_p, sin_p = _rope_tables(jnp.arange(seq))
    cos_s, sin_s = _rope_tables(PAST_LEN + jnp.tile(jnp.arange(R), n_seq))
    sink_rows = jnp.repeat(sinks, R).reshape(SWA_HEADS * R, 1)

    yp, glr_p = _inproj(xp, mp[0], mp[1], g1, w_main, w_glr, cos_p, sin_p, gq, gk, seg,
                        BF16, False, tiles_per_seq)
    ys, glr_s = _inproj(xs, ms[0], ms[1], g1, w_main, w_glr, cos_s, sin_s, gq, gk, seg, F32, True, 1)
    og_p, st_p = _gla_prompt(yp, glr_p, wgk, bgk, gnorm, batch, seq)
    og_s, st_s = _gla_sample(ys, glr_s, wgk, bgk, gnorm, state_gla[0], n_seq, n_real)
    os_p, kc_p, vc_p = _swa_prompt(yp, sinks, batch, seq)
    os_s, kc_s, vc_s = _swa_sample(ys, sink_rows, cache_swa_k[0].reshape(n_seq, WINDOW, kvw),
                                   cache_swa_v[0].reshape(n_seq, WINDOW, kvw), n_seq, n_real)
    x1_p, h2_p, sl_p, gt_p, cnt_p = _post(yp, og_p, os_p, xp, mp[2], mp[3], mp[4], g2, wo, wr, br,
                                          False, tiles_per_seq, R)
    x1_s, h2_s, sl_s, gt_s, cnt_s = _post(ys, og_s, os_s, xs, ms[2], ms[3], ms[4], g2, wo, wr, br,
                                          True, 1, n_real)

    cnt = jnp.concatenate([cnt_p, cnt_s], axis=0)[:, 0, :N_EXPERTS].astype(jnp.int32)
    n_tiles = cnt.shape[0]
    rows_bound = (TOP_K * (batch * seq + n_seq * n_real) + n_tiles * N_EXPERTS * (ROW_UNIT - 1)
                  + N_EXPERTS * (EXPERT_TILE - 1))
    rows_max = -(-rows_bound // EXPERT_TILE) * EXPERT_TILE
    (n_units, seg, n_tail, tail_hbm,
     tile_group, group_expert, n_groups, n_used, tile_rows) = _route_tables(cnt, rows_max)
    xg = _dispatch(h2_p, sl_p, h2_s, sl_s, n_units, seg, n_tail, tail_hbm, rows_max)
    yg = _experts(xg, tile_group, group_expert, n_groups, n_used, tile_rows,
                  w_gate_up[0], bgu, w_down[0], bd)
    p_tiles = batch * seq // TOKEN_TILE
    out_p = _combine(yg, sl_p, gt_p, x1_p, mp[5], n_units, seg, 0, False, tiles_per_seq)
    out_s = _combine(yg, sl_s, gt_s, x1_s, ms[5], n_units, seg, p_tiles, True, 1)

    cache_shape = (WINDOW, SWA_KV_HEADS, HEAD_DIM)
    return (out_p.reshape(batch, seq, d),
            out_s.reshape(n_seq, R, d)[:, :n_real],
            st_p[None],
            kc_p.reshape(1, batch, *cache_shape),
            vc_p.reshape(1, batch, *cache_shape),
            st_s[None],
            kc_s.reshape(1, n_seq, *cache_shape),
            vc_s.reshape(1, n_seq, *cache_shape))
```

```python
import functools

import numpy as np
import jax
import jax.numpy as jnp
from jax import lax
from jax.experimental import pallas as pl
from jax.experimental.pallas import tpu as pltpu

F32 = jnp.float32
BF16 = jnp.bfloat16
HIGHEST = lax.Precision.HIGHEST

D_MODEL = 1024
PAST_LEN = 16384
GLA_HEADS = 4
GLA_DK = 128
GLA_DV = 256
GATE_RANK = 16
GATE_TAU = 16.0
SWA_HEADS = 16
SWA_KV_HEADS = 4
HEAD_DIM = 64
SWA_GROUP = SWA_HEADS // SWA_KV_HEADS
WINDOW = 128
ROPE_THETA = 10000.0
N_EXPERTS = 32
TOP_K = 4
D_FF = 1024
SWIGLU_ALPHA = 1.702
SWIGLU_LIMIT = 7.0
NORM_EPS = 1e-6

LANES = 128
SUBLANES = 8
SAMPLE_ROWS = 8
SAMPLE_SEQS = 16
TOKEN_TILE = 512
SWA_STEP_BLOCKS = 2
GLA_BLOCK = 256
GLA_MASK_BLOCK = 128
VMEM_LIMIT = 56 * 1024 * 1024
ROW_UNIT = 16
EXPERT_TILE = 1024
EXPERT_PART = 256
LOCAL_ROWS = -(-(TOP_K * TOKEN_TILE + N_EXPERTS * (ROW_UNIT - 1)) // TOKEN_TILE) * TOKEN_TILE

COL_VG, COL_RG, COL_QS, COL_AG, COL_AS = 0, 1024, 2048, 3072, 4096
COL_QG, COL_KG, COL_KS, COL_VS = 5120, 5632, 6144, 6400
D_MAIN = 6656
PROJ_CHUNK = 512
PROJ_AHEAD = 2


def _dot(a, b, precision=None):
    return jnp.dot(a, b, preferred_element_type=F32, precision=precision)


def _dot_nt(a, b, precision=None):
    return lax.dot_general(a, b, (((1,), (1,)), ((), ())), preferred_element_type=F32, precision=precision)


def _dot_tn(a, b, precision=None):
    return lax.dot_general(a, b, (((0,), (0,)), ((), ())), preferred_element_type=F32, precision=precision)


def _dot_sum(sel, x):
    hi = x.astype(BF16)
    r1 = x - hi.astype(F32)
    mid = r1.astype(BF16)
    lo = (r1 - mid.astype(F32)).astype(BF16)
    return _dot(sel, hi) + _dot(sel, mid) + _dot(sel, lo)


def _params(*sem):
    return pltpu.CompilerParams(dimension_semantics=sem, vmem_limit_bytes=VMEM_LIMIT)


def _rms(x, g):
    return x * lax.rsqrt(jnp.mean(x * x, axis=-1, keepdims=True) + NORM_EPS) * g


def _log_sigmoid(x):
    return jnp.minimum(x, 0.0) - jnp.log(1.0 + jnp.exp(-jnp.abs(x)))


def _ada_kernel(c_ref, w_ref, b_ref, o_ref):
    c = c_ref[...]
    s = c * jax.nn.sigmoid(c)
    s_hi = s.astype(BF16)
    s_lo = (s - s_hi.astype(F32)).astype(BF16)
    w = w_ref[...]
    w_hi = w.astype(BF16)
    w_lo = (w - w_hi.astype(F32)).astype(BF16)
    o_ref[...] = _dot(s_hi, w_hi) + (_dot(s_lo, w_hi) + _dot(s_hi, w_lo)) + b_ref[...]


def _ada(c_all, w_ada, b_ada):
    rows = c_all.shape[0]
    tn = 768
    return pl.pallas_call(
        _ada_kernel,
        grid=(6 * D_MODEL // tn,),
        in_specs=[pl.BlockSpec((rows, D_MODEL), lambda j: (0, 0)),
                  pl.BlockSpec((D_MODEL, tn), lambda j: (0, j)),
                  pl.BlockSpec((1, tn), lambda j: (0, j))],
        out_specs=pl.BlockSpec((rows, tn), lambda j: (0, j)),
        out_shape=jax.ShapeDtypeStruct((rows, 6 * D_MODEL), F32),
        compiler_params=_params("parallel"),
        name="ada",
    )(c_all, w_ada, b_ada.reshape(1, -1))


W_IN_GLR = 3072
W_IN_PIECES = ((1024, 1024), (2048, 1024), (3088, 1024), (4624, 1024), (5648, 1024),
               (0, 512), (512, 512), (4112, 256), (4368, 256))


def _pack_w_kernel(w_ref, o_ref, og_ref):
    dst = 0
    for src, width in W_IN_PIECES:
        o_ref[:, dst:dst + width] = w_ref[:, src:src + width].astype(BF16)
        dst += width
    glr = w_ref[:, W_IN_GLR:W_IN_GLR + LANES]
    lane = lax.broadcasted_iota(jnp.int32, glr.shape, 1)
    og_ref[...] = jnp.where(lane < GATE_RANK, glr, 0.0).astype(BF16)


def _pack_w_in(w):
    rows, cols = w.shape
    assert sum(width for _, width in W_IN_PIECES) == D_MAIN and cols == D_MAIN + GATE_RANK
    tr = 128
    return pl.pallas_call(
        _pack_w_kernel,
        grid=(rows // tr,),
        in_specs=[pl.BlockSpec((tr, cols), lambda i: (i, 0))],
        out_specs=[pl.BlockSpec((tr, D_MAIN), lambda i: (i, 0)), pl.BlockSpec((tr, LANES), lambda i: (i, 0))],
        out_shape=[jax.ShapeDtypeStruct((rows, D_MAIN), BF16), jax.ShapeDtypeStruct((rows, LANES), BF16)],
        compiler_params=_params("parallel"),
        name="pack_w",
    )(w)


def _inproj_kernel(x_ref, shift_ref, scale_ref, g_ref, w_ref, wg_ref, cos_ref, sin_ref, gq_ref, gk_ref,
                   seg_ref, o_ref, og_ref):
    rows = x_ref.shape[0]
    h = _rms(x_ref[...], g_ref[...]) * (1.0 + _mod_rows(scale_ref, rows)) + _mod_rows(shift_ref, rows)
    hb = h.astype(BF16)
    cos, sin, seg = cos_ref[...], sin_ref[...], seg_ref[...]
    def has_followup(j):
        c0, c1 = j * PROJ_CHUNK, (j + 1) * PROJ_CHUNK
        return c0 < COL_AS + D_MODEL and c1 > COL_QS or c0 < COL_KS + SWA_KV_HEADS * HEAD_DIM and c1 > COL_KS

    chunks = range(D_MAIN // PROJ_CHUNK)
    busy = [j for j in chunks if has_followup(j)]
    plain = [j for j in chunks if not has_followup(j)]
    order = [j for pair in zip(busy, plain) for j in pair] + busy[len(plain):] + plain[len(busy):]
    project = lambda j: _dot(hb, w_ref[:, j * PROJ_CHUNK:(j + 1) * PROJ_CHUNK])
    ahead = [project(j) for j in order[:PROJ_AHEAD]]
    for idx, j in enumerate(order):
        sl = slice(j * PROJ_CHUNK, (j + 1) * PROJ_CHUNK)
        r = ahead.pop(0)
        if idx + PROJ_AHEAD < len(order):
            ahead.append(project(order[idx + PROJ_AHEAD]))
        blocks = []
        for m in range(PROJ_CHUNK // LANES):
            c0 = j * PROJ_CHUNK + m * LANES
            blk = r[:, m * LANES:(m + 1) * LANES]
            if COL_QS <= c0 < COL_QS + SWA_HEADS * HEAD_DIM:
                blk = _head_norm_rope(blk, gq_ref[...], cos, sin, seg) * (HEAD_DIM ** -0.5)
            elif COL_KS <= c0 < COL_KS + SWA_KV_HEADS * HEAD_DIM:
                blk = _head_norm_rope(blk, gk_ref[...], cos, sin, seg)
            elif COL_AG <= c0 < COL_AS + D_MODEL:
                blk = jax.nn.sigmoid(blk)
            blocks.append(blk)
        o_ref[:, sl] = jnp.concatenate(blocks, axis=1).astype(o_ref.dtype)
    og_ref[...] = _dot(hb, wg_ref[...])


def _mod_spec(per_token, tiles_per_seq):
    if per_token:
        return pl.BlockSpec((TOKEN_TILE // SAMPLE_ROWS, D_MODEL), lambda i, *_: (i, 0))
    return pl.BlockSpec((None, 1, D_MODEL), lambda i, *_: (i // tiles_per_seq, 0, 0))


def _mod_rows(ref, rows):
    m = ref[...]
    if m.shape[0] == 1:
        return m
    return jnp.broadcast_to(m[:, None, :], (m.shape[0], rows // m.shape[0], m.shape[1])).reshape(
        rows, m.shape[1])


def _inproj(x, shift, scale, g, w_main, w_glr, cos, sin, gq, gk, seg, out_dtype, per_token, tiles_per_seq):
    n = x.shape[0]
    mod = _mod_spec(per_token, tiles_per_seq)
    const = lambda i: (0, 0)
    rope = pl.BlockSpec((TOKEN_TILE, LANES), (lambda i: (i, 0)) if per_token else
                        (lambda i: (i % tiles_per_seq, 0)))
    return pl.pallas_call(
        _inproj_kernel,
        grid=(n // TOKEN_TILE,),
        in_specs=[pl.BlockSpec((TOKEN_TILE, D_MODEL), lambda i: (i, 0)), mod, mod,
                  pl.BlockSpec((1, D_MODEL), const),
                  pl.BlockSpec((D_MODEL, D_MAIN), const, pipeline_mode=pl.Buffered(1)),
                  pl.BlockSpec((D_MODEL, LANES), const, pipeline_mode=pl.Buffered(1)),
                  rope, rope,
                  pl.BlockSpec((1, LANES), const), pl.BlockSpec((1, LANES), const),
                  pl.BlockSpec((LANES, LANES), const)],
        out_specs=[pl.BlockSpec((TOKEN_TILE, D_MAIN), lambda i: (i, 0)),
                   pl.BlockSpec((TOKEN_TILE, LANES), lambda i: (i, 0))],
        out_shape=[jax.ShapeDtypeStruct((n, D_MAIN), out_dtype),
                   jax.ShapeDtypeStruct((n, LANES), F32)],
        compiler_params=_params("parallel"),
        name="inproj",
    )(x, shift, scale, g, w_main, w_glr, cos, sin, gq, gk, seg)


def _gla_log_gate(glr, wgk_ref, bgk):
    g_hi = glr.astype(BF16)
    g_lo = (glr - g_hi.astype(F32)).astype(BF16)
    w_hi, w_lo = wgk_ref[0], wgk_ref[1]
    x = _dot(g_hi, w_hi) + (_dot(g_lo, w_hi) + _dot(g_hi, w_lo))
    return _log_sigmoid(x + bgk) * (1.0 / GATE_TAU)


def _gla_out(o, r, g):
    r = r.astype(F32)
    return _rms(o, g) * (r * jax.nn.sigmoid(r))


def _gla_pair_levels(n):
    t = np.arange(n)[:, None]
    s = np.arange(n)[None, :]
    x = t ^ s
    top = np.where(x > 0, 1 << np.floor(np.log2(np.maximum(x, 1))).astype(np.int64), 0)
    return np.where(s > t, -1, top).astype(np.int32)


def _gla_block_ref(b, h):
    n, w = b.shape
    if 2 * h == n:
        return jnp.broadcast_to(b[h - 1:h, :], (n, w))
    if h >= SUBLANES // 2:
        picked = b.reshape(n // (2 * h), 2 * h, w)[:, h - 1:h, :]
        return jnp.broadcast_to(picked, (n // (2 * h), 2 * h, w)).reshape(n, w)
    r = lax.broadcasted_iota(jnp.int32, (n, 1), 0) % (2 * h)
    out = b
    for d in range(1, h + 1):
        out = jnp.where(r == h - 1 + d, pltpu.roll(b, d, 0), out)
    for d in range(1, h):
        out = jnp.where(r == h - 1 - d, pltpu.roll(b, n - d, 0), out)
    return out


def _gla_prompt_kernel(q_ref, k_ref, v_ref, r_ref, glr_ref, wgk_ref, bgk_ref, g_ref, lev_ref,
                       o_ref, s_ref, st_ref):
    c = pl.program_id(1)

    @pl.when(c == 0)
    def _():
        st_ref[...] = jnp.zeros_like(st_ref)

    n = GLA_BLOCK
    lev = lev_ref[...]
    causal = lax.broadcasted_iota(jnp.int32, (n, n), 1) <= lax.broadcasted_iota(jnp.int32, (n, n), 0)
    lg = _gla_log_gate(glr_ref[...], wgk_ref, bgk_ref[...])
    b = _dot_sum(jnp.where(causal, 1.0, 0.0).astype(BF16), lg)
    b_last = b[n - 1:n, :]
    q = q_ref[...].astype(F32) * (GLA_DK ** -0.5)
    k = k_ref[...].astype(F32)
    qe = (q * jnp.exp(b)).astype(BF16)
    kd = (k * jnp.exp(b_last - b)).astype(BF16)
    decay = jnp.exp(b_last)
    levels = [0] + [1 << p for p in range(n.bit_length() - 1)]
    q_lv, k_lv = [q.astype(BF16)], [k.astype(BF16)]
    for h in levels[1:]:
        e = jnp.exp(-jnp.abs(b - _gla_block_ref(b, h)))
        q_lv.append((q * e).astype(BF16))
        k_lv.append((k * e).astype(BF16))
    scaled = {level: (ql, kl) for level, ql, kl in zip(levels, q_lv, k_lv)}

    def intra(lo, hi, dk, v):
        size = hi - lo
        if size == GLA_MASK_BLOCK:
            attn = jnp.zeros((size, size), F32)
            for level in levels:
                if level < size:
                    ql, kl = scaled[level]
                    attn = jnp.where(lev == level, _dot_nt(ql[lo:hi, dk], kl[lo:hi, dk]), attn)
            return _dot(attn.astype(BF16), v[lo:hi, :])
        mid = lo + size // 2
        ql, kl = scaled[size // 2]
        cross = _dot_nt(ql[mid:hi, dk], kl[lo:mid, dk]).astype(BF16)
        return jnp.concatenate([intra(lo, mid, dk, v),
                                _dot(cross, v[lo:mid, :]) + intra(mid, hi, dk, v)], axis=0)

    for h in range(GLA_HEADS):
        dk = slice(h * GLA_DK, (h + 1) * GLA_DK)
        dv = slice(h * GLA_DV, (h + 1) * GLA_DV)
        v = v_ref[:, dv]
        st = st_ref[h]
        o = _dot_nt(qe[:, dk], st.astype(BF16)) + intra(0, n, dk, v)
        st_ref[h] = st * decay[:, dk] + _dot_tn(v, kd[:, dk])
        o_ref[:, dv] = _gla_out(o, r_ref[:, dv], g_ref[...]).astype(o_ref.dtype)

    @pl.when(c == pl.num_programs(1) - 1)
    def _():
        for h in range(GLA_HEADS):
            s_ref[h] = st_ref[h].T


def _gla_prompt(yp, glr, wgk, bgk, gnorm, batch, seq):
    nb = seq // GLA_BLOCK
    hk, hv = GLA_HEADS * GLA_DK, GLA_HEADS * GLA_DV
    tok = lambda b, c: b * nb + c
    const = lambda b, c: (0, 0)
    return pl.pallas_call(
        _gla_prompt_kernel,
        grid=(batch, nb),
        in_specs=[pl.BlockSpec((GLA_BLOCK, hk), lambda b, c: (tok(b, c), COL_QG // hk)),
                  pl.BlockSpec((GLA_BLOCK, hk), lambda b, c: (tok(b, c), COL_KG // hk)),
                  pl.BlockSpec((GLA_BLOCK, hv), lambda b, c: (tok(b, c), COL_VG // hv)),
                  pl.BlockSpec((GLA_BLOCK, hv), lambda b, c: (tok(b, c), COL_RG // hv)),
                  pl.BlockSpec((GLA_BLOCK, LANES), lambda b, c: (tok(b, c), 0)),
                  pl.BlockSpec((2, LANES, hk), lambda b, c: (0, 0, 0)),
                  pl.BlockSpec((1, hk), const),
                  pl.BlockSpec((1, GLA_DV), const),
                  pl.BlockSpec((GLA_MASK_BLOCK, GLA_MASK_BLOCK), const)],
        out_specs=[pl.BlockSpec((GLA_BLOCK, hv), lambda b, c: (tok(b, c), 0)),
                   pl.BlockSpec((None, GLA_HEADS, GLA_DK, GLA_DV), lambda b, c: (b, 0, 0, 0))],
        out_shape=[jax.ShapeDtypeStruct((batch * seq, hv), BF16),
                   jax.ShapeDtypeStruct((batch, GLA_HEADS, GLA_DK, GLA_DV), F32)],
        scratch_shapes=[pltpu.VMEM((GLA_HEADS, GLA_DV, GLA_DK), F32)],
        compiler_params=_params("parallel", "arbitrary"),
        name="gla_prompt",
    )(yp, yp, yp, yp, glr, wgk, bgk, gnorm, jnp.asarray(_gla_pair_levels(GLA_MASK_BLOCK)))


def _gla_sample_kernel(q_ref, k_ref, v_ref, r_ref, glr_ref, wgk_ref, bgk_ref, g_ref, s0_ref, lev_ref,
                       o_ref, s_ref, *, n_real):
    R = SAMPLE_ROWS
    rows = q_ref.shape[0]
    row = lax.broadcasted_iota(jnp.int32, (rows, rows), 0)
    col = lax.broadcasted_iota(jnp.int32, (rows, rows), 1)
    same = (row // R) == (col // R)
    causal = same & (col <= row)
    real = lax.broadcasted_iota(jnp.int32, (rows, 1), 0) % R < n_real
    lg = jnp.where(real, _gla_log_gate(glr_ref[...], wgk_ref, bgk_ref[...]), 0.0)
    b = _dot_sum(jnp.where(causal, 1.0, 0.0).astype(BF16), lg)
    b_last = _dot_sum(jnp.where(same, 1.0, 0.0).astype(BF16), lg)
    q = q_ref[...] * (GLA_DK ** -0.5)
    k = jnp.where(real, k_ref[...], 0.0)
    v = v_ref[...]
    qe = q * jnp.exp(b)
    kd = k * jnp.exp(b_last - b)
    lev = lev_ref[...]
    levels = [0] + [1 << p for p in range(R.bit_length() - 1)]
    q_lv, k_lv = [q.astype(BF16)], [k.astype(BF16)]
    for h in levels[1:]:
        e = jnp.exp(-jnp.abs(b - _gla_block_ref(b, h)))
        q_lv.append((q * e).astype(BF16))
        k_lv.append((k * e).astype(BF16))
    v_bf = v.astype(BF16)
    for h in range(GLA_HEADS):
        dk = slice(h * GLA_DK, (h + 1) * GLA_DK)
        dv = slice(h * GLA_DV, (h + 1) * GLA_DV)
        attn = jnp.zeros((rows, rows), F32)
        for level, ql, kl in zip(levels, q_lv, k_lv):
            attn = jnp.where(lev == level, _dot_nt(ql[:, dk], kl[:, dk]), attn)
        o_intra = _dot(attn.astype(BF16), v_bf[:, dv])
        decay_t = jnp.exp(b_last[:, dk]).T
        outs = []
        for s in range(rows // R):
            sl = slice(s * R, (s + 1) * R)
            s0 = s0_ref[s, h]
            outs.append(_dot(qe[sl, dk], s0) + o_intra[sl, :])
            s_ref[s, h] = s0 * decay_t[:, s * R:s * R + 1] + _dot_tn(kd[sl, dk], v[sl, dv])
        o_ref[:, dv] = _gla_out(jnp.concatenate(outs, axis=0), r_ref[:, dv], g_ref[...])


def _gla_sample(ys, glr, wgk, bgk, gnorm, state, n_seq, n_real):
    R = SAMPLE_SEQS * SAMPLE_ROWS
    hk, hv = GLA_HEADS * GLA_DK, GLA_HEADS * GLA_DV
    st_spec = pl.BlockSpec((SAMPLE_SEQS, GLA_HEADS, GLA_DK, GLA_DV), lambda b: (b, 0, 0, 0))
    lev = _gla_pair_levels(R)
    lev = np.where(lev >= SAMPLE_ROWS, -1, lev)
    return pl.pallas_call(
        functools.partial(_gla_sample_kernel, n_real=n_real),
        grid=(n_seq // SAMPLE_SEQS,),
        in_specs=[pl.BlockSpec((R, hk), lambda b: (b, COL_QG // hk)),
                  pl.BlockSpec((R, hk), lambda b: (b, COL_KG // hk)),
                  pl.BlockSpec((R, hv), lambda b: (b, COL_VG // hv)),
                  pl.BlockSpec((R, hv), lambda b: (b, COL_RG // hv)),
                  pl.BlockSpec((R, LANES), lambda b: (b, 0)),
                  pl.BlockSpec((2, LANES, hk), lambda b: (0, 0, 0)),
                  pl.BlockSpec((1, hk), lambda b: (0, 0)),
                  pl.BlockSpec((1, GLA_DV), lambda b: (0, 0)),
                  st_spec,
                  pl.BlockSpec((R, R), lambda b: (0, 0))],
        out_specs=[pl.BlockSpec((R, hv), lambda b: (b, 0)), st_spec],
        out_shape=[jax.ShapeDtypeStruct((n_seq * SAMPLE_ROWS, hv), F32),
                   jax.ShapeDtypeStruct((n_seq, GLA_HEADS, GLA_DK, GLA_DV), F32)],
        compiler_params=_params("parallel"),
        name="gla_sample",
    )(ys, ys, ys, ys, glr, wgk, bgk, gnorm, state, jnp.asarray(lev))


def _lane_lower(shape):
    return lax.broadcasted_iota(jnp.int32, shape, len(shape) - 1) % LANES < HEAD_DIM


def _head_norm_rope(x, g, cos, sin, seg):
    ss = _dot((x * x).astype(BF16), seg)
    y = x * lax.rsqrt(ss * (1.0 / HEAD_DIM) + NORM_EPS) * g
    half = HEAD_DIM // 2
    lane = lax.broadcasted_iota(jnp.int32, y.shape, 1)
    rot = jnp.where(lane % HEAD_DIM < half, pltpu.roll(y, LANES - half, 1), pltpu.roll(y, half, 1))
    return y * cos + rot * sin


def _both_halves(blk, half):
    sw = pltpu.roll(blk, HEAD_DIM, 1)
    lower = _lane_lower(blk.shape)
    return jnp.where(lower, blk, sw) if half == 0 else jnp.where(lower, sw, blk)


def _stack_heads(q_blocks):
    parts = []
    for qb in q_blocks:
        lower = _lane_lower(qb.shape)
        zero = jnp.zeros_like(qb)
        parts += [jnp.where(lower, qb, zero), jnp.where(lower, zero, qb)]
    return jnp.concatenate(parts, axis=0)


def _swa_prompt_kernel(sink_ref, q_ref, k_ref, v_ref, o_ref, ko_ref, vo_ref, kprev_ref, vprev_ref):
    n = pl.program_id(1)
    W = WINDOW

    @pl.when(n == 0)
    def _():
        kprev_ref[...] = jnp.zeros_like(kprev_ref)
        vprev_ref[...] = jnp.zeros_like(vprev_ref)

    k_prev, v_prev = kprev_ref[...], vprev_ref[...]
    for sub in range(SWA_STEP_BLOCKS):
        tok = slice(sub * W, (sub + 1) * W)
        k_cur = k_ref[tok, :].astype(F32)
        v_cur = v_ref[tok, :].astype(F32)
        prev_fill = jnp.where(n > 0, 0.0, -jnp.inf) if sub == 0 else 0.0
        _swa_block(sink_ref, q_ref, o_ref, tok, k_prev, v_prev, k_cur, v_cur, prev_fill)
        k_prev, v_prev = k_cur, v_cur
    ko_ref[...] = k_prev
    vo_ref[...] = v_prev
    kprev_ref[...] = k_prev
    vprev_ref[...] = v_prev


def _swa_block(sink_ref, q_ref, o_ref, tok, k_prev, v_prev, k_cur, v_cur, prev_fill):
    W = WINDOW
    qi = lax.broadcasted_iota(jnp.int32, (W, W), 0)
    ki = lax.broadcasted_iota(jnp.int32, (W, W), 1)
    from_cur = ki <= qi
    for kh in range(SWA_KV_HEADS):
        blk = slice((kh // 2) * LANES, (kh // 2 + 1) * LANES)
        kb_prev = _both_halves(k_prev[:, blk], kh % 2).astype(BF16)
        kb_cur = _both_halves(k_cur[:, blk], kh % 2).astype(BF16)
        vb_prev = _both_halves(v_prev[:, blk], kh % 2).astype(BF16)
        vb_cur = _both_halves(v_cur[:, blk], kh % 2).astype(BF16)
        qblocks = [q_ref[tok, (2 * kh + j) * LANES:(2 * kh + j + 1) * LANES] for j in range(2)]
        qs = _stack_heads(qblocks)
        s_prev = _dot_nt(qs, kb_prev)
        s_cur = _dot_nt(qs, kb_cur)
        outs = []
        for g in range(SWA_GROUP):
            rows = slice(g * W, (g + 1) * W)
            sg = jnp.where(from_cur, s_cur[rows, :], s_prev[rows, :] + prev_fill)
            sink = sink_ref[kh * SWA_GROUP + g]
            m = jnp.maximum(jnp.max(sg, axis=-1, keepdims=True), sink)
            p = jnp.exp(sg - m)
            denom = jnp.sum(p, axis=-1, keepdims=True) + jnp.exp(sink - m)
            p_cur = jnp.where(from_cur, p, 0.0).astype(BF16)
            p_prev = jnp.where(from_cur, 0.0, p).astype(BF16)
            outs.append((_dot(p_prev, vb_prev) + _dot(p_cur, vb_cur)) / denom)
        lower = _lane_lower((W, LANES))
        for j in range(2):
            c0 = (2 * kh + j) * LANES
            o_ref[tok, c0:c0 + LANES] = jnp.where(lower, outs[2 * j], outs[2 * j + 1]).astype(o_ref.dtype)


def _swa_prompt(yp, sinks, batch, seq):
    rows = SWA_STEP_BLOCKS * WINDOW
    nb = seq // rows
    kvw = SWA_KV_HEADS * HEAD_DIM
    tok = lambda b, n: b * nb + n
    cache_spec = pl.BlockSpec((None, WINDOW, kvw), lambda b, n: (b, 0, 0))
    return pl.pallas_call(
        _swa_prompt_kernel,
        grid=(batch, nb),
        in_specs=[pl.BlockSpec(memory_space=pltpu.SMEM),
                  pl.BlockSpec((rows, D_MODEL), lambda b, n: (tok(b, n), COL_QS // D_MODEL)),
                  pl.BlockSpec((rows, kvw), lambda b, n: (tok(b, n), COL_KS // kvw)),
                  pl.BlockSpec((rows, kvw), lambda b, n: (tok(b, n), COL_VS // kvw))],
        out_specs=[pl.BlockSpec((rows, D_MODEL), lambda b, n: (tok(b, n), 0)), cache_spec, cache_spec],
        out_shape=[jax.ShapeDtypeStruct((batch * seq, D_MODEL), BF16),
                   jax.ShapeDtypeStruct((batch, WINDOW, kvw), F32),
                   jax.ShapeDtypeStruct((batch, WINDOW, kvw), F32)],
        scratch_shapes=[pltpu.VMEM((WINDOW, kvw), F32), pltpu.VMEM((WINDOW, kvw), F32)],
        compiler_params=_params("parallel", "arbitrary"),
        name="swa_prompt",
    )(sinks, yp, yp, yp)


def _shift_cache(cache, new, n_real):
    R = SAMPLE_ROWS
    rolled = pltpu.roll(cache, WINDOW - n_real, 0)
    tail_new = pltpu.roll(new, R - n_real, 0)
    row = lax.broadcasted_iota(jnp.int32, (R, cache.shape[1]), 0)
    tail = jnp.where(row < R - n_real, rolled[WINDOW - R:, :], tail_new)
    return jnp.concatenate([rolled[:WINDOW - R, :], tail], axis=0)


def _swa_sample_kernel(q_ref, k_ref, v_ref, kc_ref, vc_ref, sink_ref, o_ref, ko_ref, vo_ref, *, n_real):
    R, W = SAMPLE_ROWS, WINDOW
    k_new = k_ref[...]
    v_new = v_ref[...]
    q_pairs = [q_ref[:, j * LANES:(j + 1) * LANES] for j in range(SWA_HEADS // 2)]

    hr = SWA_HEADS * R
    t_c = lax.broadcasted_iota(jnp.int32, (hr, W), 0) % R
    mask_c = lax.broadcasted_iota(jnp.int32, (hr, W), 1) > t_c
    t_n = lax.broadcasted_iota(jnp.int32, (hr, R), 0) % R
    mask_n = lax.broadcasted_iota(jnp.int32, (hr, R), 1) <= t_n
    sink = sink_ref[...]
    lower = _lane_lower((R, LANES))
    zeros = jnp.zeros((R, LANES), F32)
    for s in range(q_ref.shape[0] // R):
        sl = slice(s * R, (s + 1) * R)
        kc, vc = kc_ref[s], vc_ref[s]
        kn, vn = k_new[sl, :], v_new[sl, :]
        ko_ref[s] = _shift_cache(kc, kn, n_real)
        vo_ref[s] = _shift_cache(vc, vn, n_real)
        q_rows = []
        for h in range(SWA_HEADS):
            kh = h // SWA_GROUP
            x = q_pairs[h // 2][sl, :]
            if h % 2 != kh % 2:
                x = pltpu.roll(x, HEAD_DIM, 1)
            x = jnp.where(lower, x, zeros) if kh % 2 == 0 else jnp.where(lower, zeros, x)
            q_rows.append(jnp.concatenate([x, zeros] if kh // 2 == 0 else [zeros, x], axis=1))
        qbd = jnp.concatenate(q_rows, axis=0)
        sc = jnp.where(mask_c, _dot_nt(qbd.astype(BF16), kc.astype(BF16)), -jnp.inf)
        sn = jnp.where(mask_n, _dot_nt(qbd, kn), -jnp.inf)
        m = jnp.maximum(jnp.maximum(jnp.max(sc, axis=-1, keepdims=True),
                                    jnp.max(sn, axis=-1, keepdims=True)), sink)
        pc, pn = jnp.exp(sc - m), jnp.exp(sn - m)
        denom = (jnp.sum(pc, axis=-1, keepdims=True) + jnp.sum(pn, axis=-1, keepdims=True)
                 + jnp.exp(sink - m))
        o = (_dot(pc.astype(BF16), vc.astype(BF16)) + _dot(pn, vn)) / denom
        for j in range(SWA_HEADS // 2):
            halves = []
            for h in (2 * j, 2 * j + 1):
                kh = h // SWA_GROUP
                y = o[h * R:(h + 1) * R, (kh // 2) * LANES:(kh // 2 + 1) * LANES]
                halves.append(pltpu.roll(y, HEAD_DIM, 1) if h % 2 != kh % 2 else y)
            o_ref[sl, j * LANES:(j + 1) * LANES] = jnp.where(lower, halves[0], halves[1])


def _swa_sample(ys, sink_rows, kcache, vcache, n_seq, n_real):
    R = SAMPLE_SEQS * SAMPLE_ROWS
    kvw = SWA_KV_HEADS * HEAD_DIM
    const = lambda b: (0, 0)
    cache_spec = pl.BlockSpec((SAMPLE_SEQS, WINDOW, kvw), lambda b: (b, 0, 0))
    return pl.pallas_call(
        functools.partial(_swa_sample_kernel, n_real=n_real),
        grid=(n_seq // SAMPLE_SEQS,),
        in_specs=[pl.BlockSpec((R, D_MODEL), lambda b: (b, COL_QS // D_MODEL)),
                  pl.BlockSpec((R, kvw), lambda b: (b, COL_KS // kvw)),
                  pl.BlockSpec((R, kvw), lambda b: (b, COL_VS // kvw)),
                  cache_spec, cache_spec,
                  pl.BlockSpec((SWA_HEADS * SAMPLE_ROWS, 1), const)],
        out_specs=[pl.BlockSpec((R, D_MODEL), lambda b: (b, 0)), cache_spec, cache_spec],
        out_shape=[jax.ShapeDtypeStruct((n_seq * SAMPLE_ROWS, D_MODEL), F32),
                   jax.ShapeDtypeStruct((n_seq, WINDOW, kvw), F32),
                   jax.ShapeDtypeStruct((n_seq, WINDOW, kvw), F32)],
        compiler_params=_params("parallel"),
        name="swa_sample",
    )(ys, ys, ys, kcache, vcache, sink_rows)


def _post_kernel(ag_ref, as_ref, og_ref, os_ref, x_ref, gate_ref, shift_ref, scale_ref, g2_ref,
                 wo_ref, wr_ref, br_ref, x1_ref, h2_ref, slot_ref, gatek_ref, cnt_ref, *, n_valid):
    merged = (ag_ref[...].astype(F32) * og_ref[...].astype(F32)
              + as_ref[...].astype(F32) * os_ref[...].astype(F32))
    y = _dot(merged.astype(BF16), wo_ref[...])
    rows = x_ref.shape[0]
    x1 = x_ref[...] + _mod_rows(gate_ref, rows) * y
    x1_ref[...] = x1
    h2 = _rms(x1, g2_ref[...]) * (1.0 + _mod_rows(scale_ref, rows)) + _mod_rows(shift_ref, rows)
    h2_hi = h2.astype(BF16)
    h2_ref[...] = h2_hi

    h2_lo = (h2 - h2_hi.astype(F32)).astype(BF16)
    w_hi, w_lo = wr_ref[0], wr_ref[1]
    logits = _dot(h2_hi, w_hi) + (_dot(h2_lo, w_hi) + _dot(h2_hi, w_lo)) + br_ref[...]
    lane_i = lax.broadcasted_iota(jnp.int32, logits.shape, 1)
    lane = lane_i.astype(F32)
    work = logits
    vals, hots = [], []
    for _ in range(TOP_K):
        m = jnp.max(work, axis=-1, keepdims=True)
        idx = jnp.min(jnp.where(work == m, lane, float(LANES)), axis=-1, keepdims=True)
        hot = lane == idx
        vals.append(m)
        hots.append(hot)
        work = jnp.where(hot, -jnp.inf, work)
    exps = [jnp.exp(v - vals[0]) for v in vals]
    denom = exps[0] + exps[1] + exps[2] + exps[3]

    tm = logits.shape[0]
    valid = lax.broadcasted_iota(jnp.int32, (tm, 1), 0) % SAMPLE_ROWS < n_valid
    sel = jnp.zeros_like(logits)
    for hot in hots:
        sel = jnp.where(hot, 1.0, sel)
    sel = jnp.where(valid, sel, 0.0)
    earlier = (lax.broadcasted_iota(jnp.int32, (tm, tm), 1)
               < lax.broadcasted_iota(jnp.int32, (tm, tm), 0))
    rank = _dot(jnp.where(earlier, 1.0, 0.0).astype(BF16), sel.astype(BF16))
    cnt = jnp.sum(sel, axis=0, keepdims=True)
    cnt_pad = jnp.floor((cnt + (ROW_UNIT - 1.0)) * (1.0 / ROW_UNIT)) * ROW_UNIT
    below = (lax.broadcasted_iota(jnp.int32, (LANES, LANES), 0)
             < lax.broadcasted_iota(jnp.int32, (LANES, LANES), 1))
    seg_start = _dot(jnp.broadcast_to(cnt_pad, (8, LANES)), jnp.where(below, 1.0, 0.0), HIGHEST)[0:1]
    pos = seg_start + rank
    slots = jnp.full_like(logits, -1.0)
    gates = jnp.zeros_like(logits)
    for k in range(TOP_K):
        s_k = jnp.sum(jnp.where(hots[k], pos, 0.0), axis=-1, keepdims=True)
        slots = jnp.where(lane_i == k, s_k, slots)
        gates = jnp.where(lane_i == k, exps[k] / denom, gates)
    slot_ref[...] = jnp.where(valid, slots, -1.0)
    gatek_ref[...] = gates
    cnt_ref[...] = cnt


def _post(y_all, o_gla, o_swa, x, gate, shift, scale, g2, wo, wr, br, per_token, tiles_per_seq, n_valid):
    n = x.shape[0]
    mod = _mod_spec(per_token, tiles_per_seq)
    row = lambda i: (i, 0)
    const = lambda i: (0, 0)
    wide = pl.BlockSpec((TOKEN_TILE, D_MODEL), row)
    narrow = pl.BlockSpec((TOKEN_TILE, LANES), row)
    return pl.pallas_call(
        functools.partial(_post_kernel, n_valid=n_valid),
        grid=(n // TOKEN_TILE,),
        in_specs=[pl.BlockSpec((TOKEN_TILE, D_MODEL), lambda i: (i, COL_AG // D_MODEL)),
                  pl.BlockSpec((TOKEN_TILE, D_MODEL), lambda i: (i, COL_AS // D_MODEL)),
                  wide, wide, wide, mod, mod, mod,
                  pl.BlockSpec((1, D_MODEL), const),
                  pl.BlockSpec((D_MODEL, D_MODEL), const),
                  pl.BlockSpec((2, D_MODEL, LANES), lambda i: (0, 0, 0)),
                  pl.BlockSpec((1, LANES), const)],
        out_specs=[wide, wide, narrow, narrow, pl.BlockSpec((None, 1, LANES), lambda i: (i, 0, 0))],
        out_shape=[jax.ShapeDtypeStruct((n, D_MODEL), F32),
                   jax.ShapeDtypeStruct((n, D_MODEL), BF16),
                   jax.ShapeDtypeStruct((n, LANES), F32),
                   jax.ShapeDtypeStruct((n, LANES), F32),
                   jax.ShapeDtypeStruct((n // TOKEN_TILE, 1, LANES), F32)],
        compiler_params=_params("parallel"),
        name="post",
    )(y_all, y_all, o_gla, o_swa, x, gate, shift, scale, g2, wo, wr, br)


def _slot_matrix(slot_cols, weights, chunk):
    tm = slot_cols[0].shape[0]
    j = lax.broadcasted_iota(jnp.int32, (tm, tm), 1) + chunk * tm
    out = jnp.zeros((tm, tm), F32)
    for s, w in zip(slot_cols, weights):
        out = jnp.where(s == j, w, out)
    return out.astype(BF16)


def _dispatch_kernel(nu_ref, seg_ref, ntail_ref, tail_ref, hp_ref, slp_ref, hs_ref, sls_ref, xg_ref,
                     sorted_ref, zero_ref, sem, tail_sem, *, prompt_tiles):
    t = pl.program_id(0)
    last = pl.num_programs(0) - 1
    buf = t % 2

    def sort_tile(h_ref, slot_ref):
        tm = h_ref.shape[0]
        slots = slot_ref[...].astype(jnp.int32)
        slot_cols = [slots[:, k:k + 1] for k in range(TOP_K)]
        h = h_ref[...]
        for c in range(LOCAL_ROWS // tm):
            onehot = _slot_matrix(slot_cols, [1.0] * TOP_K, c)
            upc = tm // ROW_UNIT
            sorted_ref[buf, c * upc:(c + 1) * upc] = _dot_tn(onehot, h).astype(BF16).reshape(
                upc, ROW_UNIT, D_MODEL)

    @pl.when(t < prompt_tiles)
    def _():
        sort_tile(hp_ref, slp_ref)

    @pl.when(t >= prompt_tiles)
    def _():
        sort_tile(hs_ref, sls_ref)

    def start_all(tile, b):
        def body(e, c):
            n = seg_ref[tile, 0, e]

            @pl.when(n > 0)
            def _():
                pltpu.make_async_copy(sorted_ref.at[b, pl.ds(seg_ref[tile, 1, e], n)],
                                      xg_ref.at[pl.ds(seg_ref[tile, 2, e], n)], sem.at[b]).start()
            return c

        lax.fori_loop(0, N_EXPERTS, body, 0)

    def wait_all(tile, b):
        n_units = nu_ref[tile]

        @pl.when(n_units > 0)
        def _():
            pltpu.make_async_copy(sorted_ref.at[b, pl.ds(0, n_units)],
                                  xg_ref.at[pl.ds(0, n_units)], sem.at[b]).wait()

    @pl.when(t > 0)
    def _():
        wait_all(t - 1, 1 - buf)

    start_all(t, buf)

    def tail_copy(i):
        return pltpu.make_async_copy(zero_ref, xg_ref.at[tail_ref[i]], tail_sem)

    @pl.when(t == last)
    def _():
        zero_ref[...] = jnp.zeros_like(zero_ref)
        n_tail = ntail_ref[0]
        lax.fori_loop(0, n_tail, lambda i, c: (tail_copy(i).start(), c)[1], 0)
        lax.fori_loop(0, n_tail, lambda i, c: (tail_copy(i).wait(), c)[1], 0)
        wait_all(t, buf)


def _dispatch(h2_p, slots_p, h2_s, slots_s, n_units, seg, n_tail, tail_dst, rows_max):
    p_tiles = h2_p.shape[0] // TOKEN_TILE
    s_tiles = h2_s.shape[0] // TOKEN_TILE
    p_row = lambda t, *_: (jnp.minimum(t, p_tiles - 1), 0)
    s_row = lambda t, *_: (jnp.maximum(t - p_tiles, 0), 0)
    return pl.pallas_call(
        functools.partial(_dispatch_kernel, prompt_tiles=p_tiles),
        grid_spec=pltpu.PrefetchScalarGridSpec(
            num_scalar_prefetch=4,
            grid=(p_tiles + s_tiles,),
            in_specs=[pl.BlockSpec((TOKEN_TILE, D_MODEL), p_row),
                      pl.BlockSpec((TOKEN_TILE, LANES), p_row),
                      pl.BlockSpec((TOKEN_TILE, D_MODEL), s_row),
                      pl.BlockSpec((TOKEN_TILE, LANES), s_row)],
            out_specs=pl.BlockSpec(memory_space=pl.ANY),
            scratch_shapes=[pltpu.VMEM((2, LOCAL_ROWS // ROW_UNIT, ROW_UNIT, D_MODEL), BF16),
                            pltpu.VMEM((ROW_UNIT, D_MODEL), BF16),
                            pltpu.SemaphoreType.DMA((2,)), pltpu.SemaphoreType.DMA]),
        out_shape=jax.ShapeDtypeStruct((rows_max // ROW_UNIT, ROW_UNIT, D_MODEL), BF16),
        compiler_params=_params("arbitrary"),
        name="dispatch",
    )(n_units, seg, n_tail, tail_dst, h2_p, slots_p, h2_s, slots_s).reshape(rows_max, D_MODEL)


def _expert_kernel(tg_ref, ge_ref, ng_ref, nused_ref, rows_ref, x_ref, bgu_ref, bd_ref, wgu_hbm, wd_hbm,
                   y_ref, wgu_f32, wd_f32, wgu_bf, wd_bf, sem):
    i = pl.program_id(0)

    def fetch(g, b):
        e = ge_ref[g]
        return (pltpu.make_async_copy(wgu_hbm.at[e], wgu_f32.at[b], sem.at[0, b]),
                pltpu.make_async_copy(wd_hbm.at[e], wd_f32.at[b], sem.at[1, b]))

    @pl.when(i == 0)
    def _():
        for cp in fetch(0, 0):
            cp.start()

    @pl.when(i < nused_ref[0])
    def _():
        g = tg_ref[i]
        b = g % 2

        @pl.when((i == 0) | (g != tg_ref[jnp.maximum(i - 1, 0)]))
        def _():
            @pl.when(g + 1 < ng_ref[0])
            def _():
                for cp in fetch(g + 1, 1 - b):
                    cp.start()

            for cp in fetch(g, b):
                cp.wait()
            wgu_bf[...] = wgu_f32[b].astype(BF16)
            wd_bf[...] = wd_f32[b].astype(BF16)

        for part in range(EXPERT_TILE // EXPERT_PART):
            @pl.when(rows_ref[i] > part * EXPERT_PART)
            def _():
                sl = slice(part * EXPERT_PART, (part + 1) * EXPERT_PART)
                gu = _dot(x_ref[sl, :], wgu_bf[...]) + bgu_ref[...]
                gate = jnp.minimum(gu[:, :D_FF], SWIGLU_LIMIT)
                up = jnp.clip(gu[:, D_FF:], -SWIGLU_LIMIT, SWIGLU_LIMIT)
                act = (up + 1.0) * gate * jax.nn.sigmoid(SWIGLU_ALPHA * gate)
                y_ref[sl, :] = (_dot(act.astype(BF16), wd_bf[...]) + bd_ref[...]).astype(y_ref.dtype)


def _experts(xg, tile_group, group_expert, n_groups, n_used, tile_rows, wgu, bgu, wd, bd):
    rows_max = xg.shape[0]
    used = lambda i, nu: jnp.maximum(jnp.minimum(i, nu[0] - 1), 0)
    row = lambda i, tg, ge, ng, nu, tr: (used(i, nu), 0)
    exp = lambda i, tg, ge, ng, nu, tr: (ge[tg[used(i, nu)]], 0, 0)
    return pl.pallas_call(
        _expert_kernel,
        grid_spec=pltpu.PrefetchScalarGridSpec(
            num_scalar_prefetch=5,
            grid=(rows_max // EXPERT_TILE,),
            in_specs=[pl.BlockSpec((EXPERT_TILE, D_MODEL), row),
                      pl.BlockSpec((None, 1, 2 * D_FF), exp),
                      pl.BlockSpec((None, 1, D_MODEL), exp),
                      pl.BlockSpec(memory_space=pl.ANY),
                      pl.BlockSpec(memory_space=pl.ANY)],
            out_specs=pl.BlockSpec((EXPERT_TILE, D_MODEL), row),
            scratch_shapes=[pltpu.VMEM((2, D_MODEL, 2 * D_FF), F32), pltpu.VMEM((2, D_FF, D_MODEL), F32),
                            pltpu.VMEM((D_MODEL, 2 * D_FF), BF16), pltpu.VMEM((D_FF, D_MODEL), BF16),
                            pltpu.SemaphoreType.DMA((2, 2))]),
        out_shape=jax.ShapeDtypeStruct((rows_max, D_MODEL), BF16),
        compiler_params=_params("arbitrary"),
        name="experts",
    )(tile_group, group_expert, n_groups, n_used, tile_rows, xg, bgu, bd, wgu, wd)


def _combine_kernel(nu_ref, seg_ref, slot_ref, gatek_ref, x_ref, gmlp_ref, y_ref, o_ref, ys_ref, sem,
                    *, tile_offset):
    j = pl.program_id(0)
    t = j + tile_offset
    buf = j % 2
    tm = x_ref.shape[0]

    def fetch(tile, b):
        def body(e, c):
            n = seg_ref[tile, 0, e]

            @pl.when(n > 0)
            def _():
                pltpu.make_async_copy(y_ref.at[pl.ds(seg_ref[tile, 2, e], n)],
                                      ys_ref.at[b, pl.ds(seg_ref[tile, 1, e], n)], sem.at[b]).start()
            return c

        lax.fori_loop(0, N_EXPERTS, body, 0)
        n_units = nu_ref[tile]

        def zero_unit(i, c):
            ys_ref[b, i] = jnp.zeros((ROW_UNIT, D_MODEL), ys_ref.dtype)
            return c

        lax.fori_loop(n_units, LOCAL_ROWS // ROW_UNIT, zero_unit, 0)

    @pl.when(j == 0)
    def _():
        fetch(t, buf)

    @pl.when(j + 1 < pl.num_programs(0))
    def _():
        fetch(t + 1, 1 - buf)

    n_units = nu_ref[t]

    @pl.when(n_units > 0)
    def _():
        pltpu.make_async_copy(y_ref.at[pl.ds(0, n_units)], ys_ref.at[buf, pl.ds(0, n_units)],
                              sem.at[buf]).wait()

    slots = slot_ref[...].astype(jnp.int32)
    gates = gatek_ref[...]
    slot_cols = [slots[:, k:k + 1] for k in range(TOP_K)]
    gate_cols = [gates[:, k:k + 1] for k in range(TOP_K)]
    acc = jnp.zeros((tm, D_MODEL), F32)
    for c in range(LOCAL_ROWS // tm):
        upc = tm // ROW_UNIT
        rows_c = ys_ref[buf, c * upc:(c + 1) * upc].reshape(tm, D_MODEL)
        acc = acc + _dot(_slot_matrix(slot_cols, gate_cols, c), rows_c)
    o_ref[...] = x_ref[...] + _mod_rows(gmlp_ref, tm) * acc


def _combine(y, slots, gates, x1, gmlp, n_units, seg, tile_offset, per_token, tiles_per_seq):
    n = x1.shape[0]
    mod = _mod_spec(per_token, tiles_per_seq)
    wide = pl.BlockSpec((TOKEN_TILE, D_MODEL), lambda i, *_: (i, 0))
    narrow = pl.BlockSpec((TOKEN_TILE, LANES), lambda i, *_: (i, 0))
    return pl.pallas_call(
        functools.partial(_combine_kernel, tile_offset=tile_offset),
        grid_spec=pltpu.PrefetchScalarGridSpec(
            num_scalar_prefetch=2,
            grid=(n // TOKEN_TILE,),
            in_specs=[narrow, narrow, wide, mod, pl.BlockSpec(memory_space=pl.ANY)],
            out_specs=wide,
            scratch_shapes=[pltpu.VMEM((2, LOCAL_ROWS // ROW_UNIT, ROW_UNIT, D_MODEL), BF16),
                            pltpu.SemaphoreType.DMA((2,))]),
        out_shape=jax.ShapeDtypeStruct((n, D_MODEL), F32),
        compiler_params=_params("arbitrary"),
        name="combine",
    )(n_units, seg, slots, gates, x1, gmlp, y.reshape(-1, ROW_UNIT, D_MODEL))


def _route_tables(cnt, rows_max):
    units = (cnt + ROW_UNIT - 1) // ROW_UNIT
    group_units = jnp.sum(units, axis=0)
    upt = EXPERT_TILE // ROW_UNIT
    group_pad = (group_units + upt - 1) // upt * upt
    group_end = jnp.cumsum(group_pad)
    group_start = group_end - group_pad
    seg_start = group_start[None, :] + jnp.cumsum(units, axis=0) - units
    local_end = jnp.cumsum(units, axis=1)
    local_start = local_end - units
    n_units = local_end[:, -1].astype(jnp.int32)

    def pick(lo, hi, pos, value):
        return jnp.sum(jnp.where((pos >= lo) & (pos < hi), value, 0), axis=-1).astype(jnp.int32)

    seg = jnp.stack([units, local_start, seg_start], axis=1).astype(jnp.int32)

    upp = EXPERT_PART // ROW_UNIT
    n_tail_e = (group_units + upp - 1) // upp * upp - group_units
    j = jnp.arange(N_EXPERTS * upp, dtype=jnp.int32)[:, None]
    tail_end = jnp.cumsum(n_tail_e)
    tail_start = tail_end - n_tail_e
    tail_hbm = pick(tail_start[None, :], tail_end[None, :], j,
                    (group_start + group_units)[None, :] + j - tail_start[None, :])
    n_tail = tail_end[-1:].astype(jnp.int32)

    r = jnp.arange(rows_max // EXPERT_TILE, dtype=jnp.int32)[:, None] * upt
    n_used = (group_end[-1:] // upt).astype(jnp.int32)
    nonempty = group_units > 0
    group_of_expert = jnp.cumsum(nonempty) - 1
    tile_rows = pick(group_start[None, :], group_end[None, :], r,
                     jnp.clip(((group_start + group_units)[None, :] - r) * ROW_UNIT, 0, EXPERT_TILE))
    tile_group = pick(group_start[None, :], group_end[None, :], r, group_of_expert[None, :])
    g = jnp.arange(N_EXPERTS, dtype=jnp.int32)
    group_expert = jnp.sum(jnp.where(nonempty[None, :] & (group_of_expert[None, :] == g[:, None]),
                                     g[None, :], 0), axis=-1).astype(jnp.int32)
    n_groups = jnp.sum(nonempty)[None].astype(jnp.int32)
    return (n_units, seg, n_tail, tail_hbm,
            tile_group, group_expert, n_groups, n_used, tile_rows)


def _rope_tables(pos):
    half = HEAD_DIM // 2
    lane = np.arange(LANES)
    inv = jnp.asarray(ROPE_THETA, F32) ** (-jnp.asarray(lane % half, F32) / half)
    sign = jnp.asarray(np.where(lane % HEAD_DIM < half, -1.0, 1.0), F32)
    ang = pos.astype(F32)[:, None] * inv[None, :]
    return jnp.cos(ang), jnp.sin(ang) * sign[None, :]


def kernel(x_prompt, x_sample, c_prompt, c_sample, state_gla, cache_swa_k, cache_swa_v, w_ada, b_ada,
           norm1_g, norm2_g, w_in, w_gk2, b_gk, gla_norm_g, q_norm_g, k_norm_g, attn_sinks, w_o,
           w_router, b_router, w_gate_up, b_gate_up, w_down, b_down):
    batch, seq, d = x_prompt.shape
    n_seq, n_real, _ = x_sample.shape
    depth = w_in.shape[0]
    assert depth == 1 and d == D_MODEL and n_real <= SAMPLE_ROWS
    assert seq % TOKEN_TILE == 0 and (n_seq * SAMPLE_ROWS) % TOKEN_TILE == 0 and n_seq % SAMPLE_SEQS == 0
    R = SAMPLE_ROWS
    kvw = SWA_KV_HEADS * HEAD_DIM
    tiles_per_seq = seq // TOKEN_TILE

    w_main, w_glr = _pack_w_in(w_in[0])
    wgk = jnp.pad(w_gk2[0], ((0, LANES - GATE_RANK), (0, 0)))
    wgk_hi = wgk.astype(BF16)
    wgk = jnp.stack([wgk_hi, (wgk - wgk_hi.astype(F32)).astype(BF16)])
    bgk = b_gk[0].reshape(1, -1)
    gnorm = gla_norm_g[0].reshape(1, -1)
    gq = jnp.tile(q_norm_g[0], LANES // HEAD_DIM).reshape(1, LANES)
    gk = jnp.tile(k_norm_g[0], LANES // HEAD_DIM).reshape(1, LANES)
    seg = jnp.asarray(np.kron(np.eye(LANES // HEAD_DIM), np.ones((HEAD_DIM, HEAD_DIM))), BF16)
    sinks = attn_sinks[0]
    wo = w_o[0].astype(BF16)
    wr = jnp.pad(w_router[0], ((0, 0), (0, LANES - N_EXPERTS)))
    wr_hi = wr.astype(BF16)
    wr = jnp.stack([wr_hi, (wr - wr_hi.astype(F32)).astype(BF16)])
    br = jnp.pad(b_router[0], (0, LANES - N_EXPERTS), constant_values=-1e30).reshape(1, LANES)
    bgu = b_gate_up[0].reshape(N_EXPERTS, 1, 2 * D_FF)
    bd = b_down[0].reshape(N_EXPERTS, 1, D_MODEL)
    g1 = norm1_g[0].reshape(1, -1)
    g2 = norm2_g[0].reshape(1, -1)

    n_c = batch + n_seq
    c_all = jnp.pad(jnp.concatenate([c_prompt, c_sample], axis=0), ((0, -n_c % 8), (0, 0)))
    m_all = _ada(c_all, w_ada[0], b_ada[0])
    mp = [m_all[:batch, i * d:(i + 1) * d].reshape(batch, 1, d) for i in range(6)]
    ms = [m_all[batch:n_c, i * d:(i + 1) * d] for i in range(6)]

    xp = x_prompt.reshape(batch * seq, d)
    xs = jnp.pad(x_sample, ((0, 0), (0, R - n_real), (0, 0))).reshape(n_seq * R, d)
    cos_p, sin_p = _rope_tables(jnp.arange(seq))
    cos_s, sin_s = _rope_tables(PAST_LEN + jnp.tile(jnp.arange(R), n_seq))
    sink_rows = jnp.repeat(sinks, R).reshape(SWA_HEADS * R, 1)

    yp, glr_p = _inproj(xp, mp[0], mp[1], g1, w_main, w_glr, cos_p, sin_p, gq, gk, seg,
                        BF16, False, tiles_per_seq)
    ys, glr_s = _inproj(xs, ms[0], ms[1], g1, w_main, w_glr, cos_s, sin_s, gq, gk, seg, F32, True, 1)
    og_p, st_p = _gla_prompt(yp, glr_p, wgk, bgk, gnorm, batch, seq)
    og_s, st_s = _gla_sample(ys, glr_s, wgk, bgk, gnorm, state_gla[0], n_seq, n_real)
    os_p, kc_p, vc_p = _swa_prompt(yp, sinks, batch, seq)
    os_s, kc_s, vc_s = _swa_sample(ys, sink_rows, cache_swa_k[0].reshape(n_seq, WINDOW, kvw),
                                   cache_swa_v[0].reshape(n_seq, WINDOW, kvw), n_seq, n_real)
    x1_p, h2_p, sl_p, gt_p, cnt_p = _post(yp, og_p, os_p, xp, mp[2], mp[3], mp[4], g2, wo, wr, br,
                                          False, tiles_per_seq, R)
    x1_s, h2_s, sl_s, gt_s, cnt_s = _post(ys, og_s, os_s, xs, ms[2], ms[3], ms[4], g2, wo, wr, br,
                                          True, 1, n_real)

    cnt = jnp.concatenate([cnt_p, cnt_s], axis=0)[:, 0, :N_EXPERTS].astype(jnp.int32)
    n_tiles = cnt.shape[0]
    rows_bound = (TOP_K * (batch * seq + n_seq * n_real) + n_tiles * N_EXPERTS * (ROW_UNIT - 1)
                  + N_EXPERTS * (EXPERT_TILE - 1))
    rows_max = -(-rows_bound // EXPERT_TILE) * EXPERT_TILE
    (n_units, seg, n_tail, tail_hbm,
     tile_group, group_expert, n_groups, n_used, tile_rows) = _route_tables(cnt, rows_max)
    xg = _dispatch(h2_p, sl_p, h2_s, sl_s, n_units, seg, n_tail, tail_hbm, rows_max)
    yg = _experts(xg, tile_group, group_expert, n_groups, n_used, tile_rows,
                  w_gate_up[0], bgu, w_down[0], bd)
    p_tiles = batch * seq // TOKEN_TILE
    out_p = _combine(yg, sl_p, gt_p, x1_p, mp[5], n_units, seg, 0, False, tiles_per_seq)
    out_s = _combine(yg, sl_s, gt_s, x1_s, ms[5], n_units, seg, p_tiles, True, 1)

    cache_shape = (WINDOW, SWA_KV_HEADS, HEAD_DIM)
    return (out_p.reshape(batch, seq, d),
            out_s.reshape(n_seq, R, d)[:, :n_real],
            st_p[None],
            kc_p.reshape(1, batch, *cache_shape),
            vc_p.reshape(1, batch, *cache_shape),
            st_s[None],
            kc_s.reshape(1, n_seq, *cache_shape),
            vc_s.reshape(1, n_seq, *cache_shape))
```

```python
import functools

import numpy as np
import jax
import jax.numpy as jnp
from jax import lax
from jax.experimental import pallas as pl
from jax.experimental.pallas import tpu as pltpu

F32 = jnp.float32
BF16 = jnp.bfloat16
HIGHEST = lax.Precision.HIGHEST

D_MODEL = 1024
PAST_LEN = 16384
GLA_HEADS = 4
GLA_DK = 128
GLA_DV = 256
GATE_RANK = 16
GATE_TAU = 16.0
SWA_HEADS = 16
SWA_KV_HEADS = 4
HEAD_DIM = 64
SWA_GROUP = SWA_HEADS // SWA_KV_HEADS
WINDOW = 128
ROPE_THETA = 10000.0
N_EXPERTS = 32
TOP_K = 4
D_FF = 1024
SWIGLU_ALPHA = 1.702
SWIGLU_LIMIT = 7.0
NORM_EPS = 1e-6

LANES = 128
SUBLANES = 8
SAMPLE_ROWS = 8
SAMPLE_SEQS = 16
TOKEN_TILE = 512
SWA_STEP_BLOCKS = 2
GLA_BLOCK = 256
GLA_MASK_BLOCK = 128
VMEM_LIMIT = 56 * 1024 * 1024
ROW_UNIT = 16
EXPERT_TILE = 1024
EXPERT_PART = 256
LOCAL_ROWS = -(-(TOP_K * TOKEN_TILE + N_EXPERTS * (ROW_UNIT - 1)) // TOKEN_TILE) * TOKEN_TILE

COL_VG, COL_RG, COL_QS, COL_AG, COL_AS = 0, 1024, 2048, 3072, 4096
COL_QG, COL_KG, COL_KS, COL_VS = 5120, 5632, 6144, 6400
D_MAIN = 6656
PROJ_CHUNK = 512
PROJ_AHEAD = 2


def _dot(a, b, precision=None):
    return jnp.dot(a, b, preferred_element_type=F32, precision=precision)


def _dot_nt(a, b, precision=None):
    return lax.dot_general(a, b, (((1,), (1,)), ((), ())), preferred_element_type=F32, precision=precision)


def _dot_tn(a, b, precision=None):
    return lax.dot_general(a, b, (((0,), (0,)), ((), ())), preferred_element_type=F32, precision=precision)


def _dot_sum(sel, x):
    hi = x.astype(BF16)
    r1 = x - hi.astype(F32)
    mid = r1.astype(BF16)
    lo = (r1 - mid.astype(F32)).astype(BF16)
    return _dot(sel, hi) + _dot(sel, mid) + _dot(sel, lo)


def _params(*sem):
    return pltpu.CompilerParams(dimension_semantics=sem, vmem_limit_bytes=VMEM_LIMIT)


def _rms(x, g):
    return x * lax.rsqrt(jnp.mean(x * x, axis=-1, keepdims=True) + NORM_EPS) * g


def _log_sigmoid(x):
    return jnp.minimum(x, 0.0) - jnp.log(1.0 + jnp.exp(-jnp.abs(x)))


def _ada_kernel(c_ref, w_ref, b_ref, o_ref):
    c = c_ref[...]
    s = c * jax.nn.sigmoid(c)
    s_hi = s.astype(BF16)
    s_lo = (s - s_hi.astype(F32)).astype(BF16)
    w = w_ref[...]
    w_hi = w.astype(BF16)
    w_lo = (w - w_hi.astype(F32)).astype(BF16)
    o_ref[...] = _dot(s_hi, w_hi) + (_dot(s_lo, w_hi) + _dot(s_hi, w_lo)) + b_ref[...]


def _ada(c_all, w_ada, b_ada):
    rows = c_all.shape[0]
    tn = 768
    return pl.pallas_call(
        _ada_kernel,
        grid=(6 * D_MODEL // tn,),
        in_specs=[pl.BlockSpec((rows, D_MODEL), lambda j: (0, 0)),
                  pl.BlockSpec((D_MODEL, tn), lambda j: (0, j)),
                  pl.BlockSpec((1, tn), lambda j: (0, j))],
        out_specs=pl.BlockSpec((rows, tn), lambda j: (0, j)),
        out_shape=jax.ShapeDtypeStruct((rows, 6 * D_MODEL), F32),
        compiler_params=_params("parallel"),
        name="ada",
    )(c_all, w_ada, b_ada.reshape(1, -1))


W_IN_GLR = 3072
W_IN_PIECES = ((1024, 1024), (2048, 1024), (3088, 1024), (4624, 1024), (5648, 1024),
               (0, 512), (512, 512), (4112, 256), (4368, 256))


def _pack_w_kernel(w_ref, o_ref, og_ref):
    dst = 0
    for src, width in W_IN_PIECES:
        o_ref[:, dst:dst + width] = w_ref[:, src:src + width].astype(BF16)
        dst += width
    glr = w_ref[:, W_IN_GLR:W_IN_GLR + LANES]
    lane = lax.broadcasted_iota(jnp.int32, glr.shape, 1)
    og_ref[...] = jnp.where(lane < GATE_RANK, glr, jnp.zeros_like(glr)).astype(BF16)


def _pack_w_in(w):
    _, rows, cols = w.shape
    assert sum(width for _, width in W_IN_PIECES) == D_MAIN and cols == D_MAIN + GATE_RANK
    tr = 128
    return pl.pallas_call(
        _pack_w_kernel,
        grid=(rows // tr,),
        in_specs=[pl.BlockSpec((None, tr, cols), lambda i: (0, i, 0))],
        out_specs=[pl.BlockSpec((tr, D_MAIN), lambda i: (i, 0)), pl.BlockSpec((tr, LANES), lambda i: (i, 0))],
        out_shape=[jax.ShapeDtypeStruct((rows, D_MAIN), BF16), jax.ShapeDtypeStruct((rows, LANES), BF16)],
        compiler_params=_params("parallel"),
        name="pack_w",
    )(w)


def _inproj_kernel(x_ref, shift_ref, scale_ref, g_ref, w_ref, wg_ref, cos_ref, sin_ref, gq_ref, gk_ref,
                   seg_ref, o_ref, og_ref):
    rows = x_ref.shape[0]
    h = _rms(x_ref[...], g_ref[...]) * (1.0 + _mod_rows(scale_ref, rows)) + _mod_rows(shift_ref, rows)
    hb = h.astype(BF16)
    cos, sin, seg = cos_ref[...], sin_ref[...], seg_ref[...]
    def has_followup(j):
        c0, c1 = j * PROJ_CHUNK, (j + 1) * PROJ_CHUNK
        return c0 < COL_AS + D_MODEL and c1 > COL_QS or c0 < COL_KS + SWA_KV_HEADS * HEAD_DIM and c1 > COL_KS

    chunks = range(D_MAIN // PROJ_CHUNK)
    busy = [j for j in chunks if has_followup(j)]
    plain = [j for j in chunks if not has_followup(j)]
    order = [j for pair in zip(busy, plain) for j in pair] + busy[len(plain):] + plain[len(busy):]
    project = lambda j: _dot(hb, w_ref[:, j * PROJ_CHUNK:(j + 1) * PROJ_CHUNK])
    ahead = [project(j) for j in order[:PROJ_AHEAD]]
    for idx, j in enumerate(order):
        sl = slice(j * PROJ_CHUNK, (j + 1) * PROJ_CHUNK)
        r = ahead.pop(0)
        if idx + PROJ_AHEAD < len(order):
            ahead.append(project(order[idx + PROJ_AHEAD]))
        blocks = []
        for m in range(PROJ_CHUNK // LANES):
            c0 = j * PROJ_CHUNK + m * LANES
            blk = r[:, m * LANES:(m + 1) * LANES]
            if COL_QS <= c0 < COL_QS + SWA_HEADS * HEAD_DIM:
                blk = _head_norm_rope(blk, gq_ref[...], cos, sin, seg) * (HEAD_DIM ** -0.5)
            elif COL_KS <= c0 < COL_KS + SWA_KV_HEADS * HEAD_DIM:
                blk = _head_norm_rope(blk, gk_ref[...], cos, sin, seg)
            elif COL_AG <= c0 < COL_AS + D_MODEL:
                blk = jax.nn.sigmoid(blk)
            blocks.append(blk)
        o_ref[:, sl] = jnp.concatenate(blocks, axis=1).astype(o_ref.dtype)
    og_ref[...] = _dot(hb, wg_ref[...])


def _mod_spec(per_token, tiles_per_seq):
    if per_token:
        return pl.BlockSpec((TOKEN_TILE // SAMPLE_ROWS, D_MODEL), lambda i, *_: (i, 0))
    return pl.BlockSpec((None, 1, D_MODEL), lambda i, *_: (i // tiles_per_seq, 0, 0))


def _mod_rows(ref, rows):
    m = ref[...]
    if m.shape[0] == 1:
        return m
    return jnp.broadcast_to(m[:, None, :], (m.shape[0], rows // m.shape[0], m.shape[1])).reshape(
        rows, m.shape[1])


def _inproj(x, shift, scale, g, w_main, w_glr, cos, sin, gq, gk, seg, out_dtype, per_token, tiles_per_seq):
    n = x.shape[0]
    mod = _mod_spec(per_token, tiles_per_seq)
    const = lambda i: (0, 0)
    rope = pl.BlockSpec((TOKEN_TILE, LANES), (lambda i: (i, 0)) if per_token else
                        (lambda i: (i % tiles_per_seq, 0)))
    return pl.pallas_call(
        _inproj_kernel,
        grid=(n // TOKEN_TILE,),
        in_specs=[pl.BlockSpec((TOKEN_TILE, D_MODEL), lambda i: (i, 0)), mod, mod,
                  pl.BlockSpec((1, D_MODEL), const),
                  pl.BlockSpec((D_MODEL, D_MAIN), const, pipeline_mode=pl.Buffered(1)),
                  pl.BlockSpec((D_MODEL, LANES), const, pipeline_mode=pl.Buffered(1)),
                  rope, rope,
                  pl.BlockSpec((1, LANES), const), pl.BlockSpec((1, LANES), const),
                  pl.BlockSpec((LANES, LANES), const)],
        out_specs=[pl.BlockSpec((TOKEN_TILE, D_MAIN), lambda i: (i, 0)),
                   pl.BlockSpec((TOKEN_TILE, LANES), lambda i: (i, 0))],
        out_shape=[jax.ShapeDtypeStruct((n, D_MAIN), out_dtype),
                   jax.ShapeDtypeStruct((n, LANES), F32)],
        compiler_params=_params("parallel"),
        name="inproj",
    )(x, shift, scale, g, w_main, w_glr, cos, sin, gq, gk, seg)


def _gla_log_gate(glr, wgk_ref, bgk):
    g_hi = glr.astype(BF16)
    g_lo = (glr - g_hi.astype(F32)).astype(BF16)
    w_hi, w_lo = wgk_ref[0], wgk_ref[1]
    x = _dot(g_hi, w_hi) + (_dot(g_lo, w_hi) + _dot(g_hi, w_lo))
    return _log_sigmoid(x + bgk) * (1.0 / GATE_TAU)


def _gla_out(o, r, g):
    r = r.astype(F32)
    return _rms(o, g) * (r * jax.nn.sigmoid(r))


def _gla_pair_levels(n):
    t = np.arange(n)[:, None]
    s = np.arange(n)[None, :]
    x = t ^ s
    top = np.where(x > 0, 1 << np.floor(np.log2(np.maximum(x, 1))).astype(np.int64), 0)
    return np.where(s > t, -1, top).astype(np.int32)


def _gla_block_ref(b, h):
    n, w = b.shape
    if 2 * h == n:
        return jnp.broadcast_to(b[h - 1:h, :], (n, w))
    if h >= SUBLANES // 2:
        picked = b.reshape(n // (2 * h), 2 * h, w)[:, h - 1:h, :]
        return jnp.broadcast_to(picked, (n // (2 * h), 2 * h, w)).reshape(n, w)
    r = lax.broadcasted_iota(jnp.int32, (n, 1), 0) % (2 * h)
    out = b
    for d in range(1, h + 1):
        out = jnp.where(r == h - 1 + d, pltpu.roll(b, d, 0), out)
    for d in range(1, h):
        out = jnp.where(r == h - 1 - d, pltpu.roll(b, n - d, 0), out)
    return out


def _gla_prompt_kernel(q_ref, k_ref, v_ref, r_ref, glr_ref, wgk_ref, bgk_ref, g_ref, lev_ref,
                       o_ref, s_ref, st_ref):
    c = pl.program_id(1)

    @pl.when(c == 0)
    def _():
        st_ref[...] = jnp.zeros_like(st_ref)

    n = GLA_BLOCK
    lev = lev_ref[...]
    causal = lax.broadcasted_iota(jnp.int32, (n, n), 1) <= lax.broadcasted_iota(jnp.int32, (n, n), 0)
    lg = _gla_log_gate(glr_ref[...], wgk_ref, bgk_ref[...])
    b = _dot_sum(jnp.where(causal, 1.0, 0.0).astype(BF16), lg)
    b_last = b[n - 1:n, :]
    q = q_ref[...].astype(F32) * (GLA_DK ** -0.5)
    k = k_ref[...].astype(F32)
    qe = (q * jnp.exp(b)).astype(BF16)
    kd = (k * jnp.exp(b_last - b)).astype(BF16)
    decay = jnp.exp(b_last)
    levels = [0] + [1 << p for p in range(n.bit_length() - 1)]
    q_lv, k_lv = [q.astype(BF16)], [k.astype(BF16)]
    for h in levels[1:]:
        e = jnp.exp(-jnp.abs(b - _gla_block_ref(b, h)))
        q_lv.append((q * e).astype(BF16))
        k_lv.append((k * e).astype(BF16))
    scaled = {level: (ql, kl) for level, ql, kl in zip(levels, q_lv, k_lv)}

    def intra(lo, hi, dk, v):
        size = hi - lo
        if size == GLA_MASK_BLOCK:
            attn = jnp.zeros((size, size), F32)
            for level in levels:
                if level < size:
                    ql, kl = scaled[level]
                    attn = jnp.where(lev == level, _dot_nt(ql[lo:hi, dk], kl[lo:hi, dk]), attn)
            return _dot(attn.astype(BF16), v[lo:hi, :])
        mid = lo + size // 2
        ql, kl = scaled[size // 2]
        cross = _dot_nt(ql[mid:hi, dk], kl[lo:mid, dk]).astype(BF16)
        return jnp.concatenate([intra(lo, mid, dk, v),
                                _dot(cross, v[lo:mid, :]) + intra(mid, hi, dk, v)], axis=0)

    for h in range(GLA_HEADS):
        dk = slice(h * GLA_DK, (h + 1) * GLA_DK)
        dv = slice(h * GLA_DV, (h + 1) * GLA_DV)
        v = v_ref[:, dv]
        st = st_ref[h]
        o = _dot_nt(qe[:, dk], st.astype(BF16)) + intra(0, n, dk, v)
        st_ref[h] = st * decay[:, dk] + _dot_tn(v, kd[:, dk])
        o_ref[:, dv] = _gla_out(o, r_ref[:, dv], g_ref[...]).astype(o_ref.dtype)

    @pl.when(c == pl.num_programs(1) - 1)
    def _():
        for h in range(GLA_HEADS):
            s_ref[h] = st_ref[h].T


def _gla_prompt(yp, glr, wgk, bgk, gnorm, batch, seq):
    nb = seq // GLA_BLOCK
    hk, hv = GLA_HEADS * GLA_DK, GLA_HEADS * GLA_DV
    tok = lambda b, c: b * nb + c
    const = lambda b, c: (0, 0)
    return pl.pallas_call(
        _gla_prompt_kernel,
        grid=(batch, nb),
        in_specs=[pl.BlockSpec((GLA_BLOCK, hk), lambda b, c: (tok(b, c), COL_QG // hk)),
                  pl.BlockSpec((GLA_BLOCK, hk), lambda b, c: (tok(b, c), COL_KG // hk)),
                  pl.BlockSpec((GLA_BLOCK, hv), lambda b, c: (tok(b, c), COL_VG // hv)),
                  pl.BlockSpec((GLA_BLOCK, hv), lambda b, c: (tok(b, c), COL_RG // hv)),
                  pl.BlockSpec((GLA_BLOCK, LANES), lambda b, c: (tok(b, c), 0)),
                  pl.BlockSpec((2, LANES, hk), lambda b, c: (0, 0, 0)),
                  pl.BlockSpec((1, hk), const),
                  pl.BlockSpec((1, GLA_DV), const),
                  pl.BlockSpec((GLA_MASK_BLOCK, GLA_MASK_BLOCK), const)],
        out_specs=[pl.BlockSpec((GLA_BLOCK, hv), lambda b, c: (tok(b, c), 0)),
                   pl.BlockSpec((None, GLA_HEADS, GLA_DK, GLA_DV), lambda b, c: (b, 0, 0, 0))],
        out_shape=[jax.ShapeDtypeStruct((batch * seq, hv), BF16),
                   jax.ShapeDtypeStruct((batch, GLA_HEADS, GLA_DK, GLA_DV), F32)],
        scratch_shapes=[pltpu.VMEM((GLA_HEADS, GLA_DV, GLA_DK), F32)],
        compiler_params=_params("parallel", "arbitrary"),
        name="gla_prompt",
    )(yp, yp, yp, yp, glr, wgk, bgk, gnorm, jnp.asarray(_gla_pair_levels(GLA_MASK_BLOCK)))


def _gla_sample_kernel(q_ref, k_ref, v_ref, r_ref, glr_ref, wgk_ref, bgk_ref, g_ref, s0_ref, lev_ref,
                       o_ref, s_ref, *, n_real):
    R = SAMPLE_ROWS
    rows = q_ref.shape[0]
    row = lax.broadcasted_iota(jnp.int32, (rows, rows), 0)
    col = lax.broadcasted_iota(jnp.int32, (rows, rows), 1)
    same = (row // R) == (col // R)
    causal = same & (col <= row)
    real = lax.broadcasted_iota(jnp.int32, (rows, 1), 0) % R < n_real
    lg = jnp.where(real, _gla_log_gate(glr_ref[...], wgk_ref, bgk_ref[...]), 0.0)
    b = _dot_sum(jnp.where(causal, 1.0, 0.0).astype(BF16), lg)
    b_last = _dot_sum(jnp.where(same, 1.0, 0.0).astype(BF16), lg)
    q = q_ref[...] * (GLA_DK ** -0.5)
    k = jnp.where(real, k_ref[...], 0.0)
    v = v_ref[...]
    qe = q * jnp.exp(b)
    kd = k * jnp.exp(b_last - b)
    lev = lev_ref[...]
    levels = [0] + [1 << p for p in range(R.bit_length() - 1)]
    q_lv, k_lv = [q.astype(BF16)], [k.astype(BF16)]
    for h in levels[1:]:
        e = jnp.exp(-jnp.abs(b - _gla_block_ref(b, h)))
        q_lv.append((q * e).astype(BF16))
        k_lv.append((k * e).astype(BF16))
    v_bf = v.astype(BF16)
    for h in range(GLA_HEADS):
        dk = slice(h * GLA_DK, (h + 1) * GLA_DK)
        dv = slice(h * GLA_DV, (h + 1) * GLA_DV)
        attn = jnp.zeros((rows, rows), F32)
        for level, ql, kl in zip(levels, q_lv, k_lv):
            attn = jnp.where(lev == level, _dot_nt(ql[:, dk], kl[:, dk]), attn)
        o_intra = _dot(attn.astype(BF16), v_bf[:, dv])
        decay_t = jnp.exp(b_last[:, dk]).T
        outs = []
        for s in range(rows // R):
            sl = slice(s * R, (s + 1) * R)
            s0 = s0_ref[s, h]
            outs.append(_dot(qe[sl, dk], s0) + o_intra[sl, :])
            s_ref[s, h] = s0 * decay_t[:, s * R:s * R + 1] + _dot_tn(kd[sl, dk], v[sl, dv])
        o_ref[:, dv] = _gla_out(jnp.concatenate(outs, axis=0), r_ref[:, dv], g_ref[...])


def _gla_sample(ys, glr, wgk, bgk, gnorm, state, n_seq, n_real):
    R = SAMPLE_SEQS * SAMPLE_ROWS
    hk, hv = GLA_HEADS * GLA_DK, GLA_HEADS * GLA_DV
    st_spec = pl.BlockSpec((SAMPLE_SEQS, GLA_HEADS, GLA_DK, GLA_DV), lambda b: (b, 0, 0, 0))
    lev = _gla_pair_levels(R)
    lev = np.where(lev >= SAMPLE_ROWS, -1, lev)
    return pl.pallas_call(
        functools.partial(_gla_sample_kernel, n_real=n_real),
        grid=(n_seq // SAMPLE_SEQS,),
        in_specs=[pl.BlockSpec((R, hk), lambda b: (b, COL_QG // hk)),
                  pl.BlockSpec((R, hk), lambda b: (b, COL_KG // hk)),
                  pl.BlockSpec((R, hv), lambda b: (b, COL_VG // hv)),
                  pl.BlockSpec((R, hv), lambda b: (b, COL_RG // hv)),
                  pl.BlockSpec((R, LANES), lambda b: (b, 0)),
                  pl.BlockSpec((2, LANES, hk), lambda b: (0, 0, 0)),
                  pl.BlockSpec((1, hk), lambda b: (0, 0)),
                  pl.BlockSpec((1, GLA_DV), lambda b: (0, 0)),
                  st_spec,
                  pl.BlockSpec((R, R), lambda b: (0, 0))],
        out_specs=[pl.BlockSpec((R, hv), lambda b: (b, 0)), st_spec],
        out_shape=[jax.ShapeDtypeStruct((n_seq * SAMPLE_ROWS, hv), F32),
                   jax.ShapeDtypeStruct((n_seq, GLA_HEADS, GLA_DK, GLA_DV), F32)],
        compiler_params=_params("parallel"),
        name="gla_sample",
    )(ys, ys, ys, ys, glr, wgk, bgk, gnorm, state, jnp.asarray(lev))


def _lane_lower(shape):
    return lax.broadcasted_iota(jnp.int32, shape, len(shape) - 1) % LANES < HEAD_DIM


def _head_norm_rope(x, g, cos, sin, seg):
    ss = _dot((x * x).astype(BF16), seg)
    y = x * lax.rsqrt(ss * (1.0 / HEAD_DIM) + NORM_EPS) * g
    half = HEAD_DIM // 2
    lane = lax.broadcasted_iota(jnp.int32, y.shape, 1)
    rot = jnp.where(lane % HEAD_DIM < half, pltpu.roll(y, LANES - half, 1), pltpu.roll(y, half, 1))
    return y * cos + rot * sin


def _both_halves(blk, half):
    sw = pltpu.roll(blk, HEAD_DIM, 1)
    lower = _lane_lower(blk.shape)
    return jnp.where(lower, blk, sw) if half == 0 else jnp.where(lower, sw, blk)


def _stack_heads(q_blocks):
    parts = []
    for qb in q_blocks:
        lower = _lane_lower(qb.shape)
        zero = jnp.zeros_like(qb)
        parts += [jnp.where(lower, qb, zero), jnp.where(lower, zero, qb)]
    return jnp.concatenate(parts, axis=0)


def _swa_prompt_kernel(sink_ref, q_ref, k_ref, v_ref, o_ref, ko_ref, vo_ref, kprev_ref, vprev_ref):
    n = pl.program_id(1)
    W = WINDOW

    @pl.when(n == 0)
    def _():
        kprev_ref[...] = jnp.zeros_like(kprev_ref)
        vprev_ref[...] = jnp.zeros_like(vprev_ref)

    k_prev, v_prev = kprev_ref[...], vprev_ref[...]
    for sub in range(SWA_STEP_BLOCKS):
        tok = slice(sub * W, (sub + 1) * W)
        k_cur = k_ref[tok, :].astype(F32)
        v_cur = v_ref[tok, :].astype(F32)
        prev_fill = jnp.where(n > 0, 0.0, -jnp.inf) if sub == 0 else 0.0
        _swa_block(sink_ref, q_ref, o_ref, tok, k_prev, v_prev, k_cur, v_cur, prev_fill)
        k_prev, v_prev = k_cur, v_cur
    ko_ref[...] = k_prev
    vo_ref[...] = v_prev
    kprev_ref[...] = k_prev
    vprev_ref[...] = v_prev


def _swa_block(sink_ref, q_ref, o_ref, tok, k_prev, v_prev, k_cur, v_cur, prev_fill):
    W = WINDOW
    qi = lax.broadcasted_iota(jnp.int32, (W, W), 0)
    ki = lax.broadcasted_iota(jnp.int32, (W, W), 1)
    from_cur = ki <= qi
    for kh in range(SWA_KV_HEADS):
        blk = slice((kh // 2) * LANES, (kh // 2 + 1) * LANES)
        kb_prev = _both_halves(k_prev[:, blk], kh % 2).astype(BF16)
        kb_cur = _both_halves(k_cur[:, blk], kh % 2).astype(BF16)
        vb_prev = _both_halves(v_prev[:, blk], kh % 2).astype(BF16)
        vb_cur = _both_halves(v_cur[:, blk], kh % 2).astype(BF16)
        qblocks = [q_ref[tok, (2 * kh + j) * LANES:(2 * kh + j + 1) * LANES] for j in range(2)]
        qs = _stack_heads(qblocks)
        s_prev = _dot_nt(qs, kb_prev)
        s_cur = _dot_nt(qs, kb_cur)
        outs = []
        for g in range(SWA_GROUP):
            rows = slice(g * W, (g + 1) * W)
            sg = jnp.where(from_cur, s_cur[rows, :], s_prev[rows, :] + prev_fill)
            sink = sink_ref[kh * SWA_GROUP + g]
            m = jnp.maximum(jnp.max(sg, axis=-1, keepdims=True), sink)
            p = jnp.exp(sg - m)
            denom = jnp.sum(p, axis=-1, keepdims=True) + jnp.exp(sink - m)
            p_cur = jnp.where(from_cur, p, 0.0).astype(BF16)
            p_prev = jnp.where(from_cur, 0.0, p).astype(BF16)
            outs.append((_dot(p_prev, vb_prev) + _dot(p_cur, vb_cur)) / denom)
        lower = _lane_lower((W, LANES))
        for j in range(2):
            c0 = (2 * kh + j) * LANES
            o_ref[tok, c0:c0 + LANES] = jnp.where(lower, outs[2 * j], outs[2 * j + 1]).astype(o_ref.dtype)


def _swa_prompt(yp, sinks, batch, seq):
    rows = SWA_STEP_BLOCKS * WINDOW
    nb = seq // rows
    kvw = SWA_KV_HEADS * HEAD_DIM
    tok = lambda b, n: b * nb + n
    cache_spec = pl.BlockSpec((None, WINDOW, kvw), lambda b, n: (b, 0, 0))
    return pl.pallas_call(
        _swa_prompt_kernel,
        grid=(batch, nb),
        in_specs=[pl.BlockSpec(memory_space=pltpu.SMEM),
                  pl.BlockSpec((rows, D_MODEL), lambda b, n: (tok(b, n), COL_QS // D_MODEL)),
                  pl.BlockSpec((rows, kvw), lambda b, n: (tok(b, n), COL_KS // kvw)),
                  pl.BlockSpec((rows, kvw), lambda b, n: (tok(b, n), COL_VS // kvw))],
        out_specs=[pl.BlockSpec((rows, D_MODEL), lambda b, n: (tok(b, n), 0)), cache_spec, cache_spec],
        out_shape=[jax.ShapeDtypeStruct((batch * seq, D_MODEL), BF16),
                   jax.ShapeDtypeStruct((batch, WINDOW, kvw), F32),
                   jax.ShapeDtypeStruct((batch, WINDOW, kvw), F32)],
        scratch_shapes=[pltpu.VMEM((WINDOW, kvw), F32), pltpu.VMEM((WINDOW, kvw), F32)],
        compiler_params=_params("parallel", "arbitrary"),
        name="swa_prompt",
    )(sinks, yp, yp, yp)


def _shift_cache(cache, new, n_real):
    R = SAMPLE_ROWS
    rolled = pltpu.roll(cache, WINDOW - n_real, 0)
    tail_new = pltpu.roll(new, R - n_real, 0)
    row = lax.broadcasted_iota(jnp.int32, (R, cache.shape[1]), 0)
    tail = jnp.where(row < R - n_real, rolled[WINDOW - R:, :], tail_new)
    return jnp.concatenate([rolled[:WINDOW - R, :], tail], axis=0)


def _swa_sample_kernel(q_ref, k_ref, v_ref, kc_ref, vc_ref, sink_ref, o_ref, ko_ref, vo_ref, *, n_real):
    R, W = SAMPLE_ROWS, WINDOW
    k_new = k_ref[...]
    v_new = v_ref[...]
    q_pairs = [q_ref[:, j * LANES:(j + 1) * LANES] for j in range(SWA_HEADS // 2)]

    hr = SWA_HEADS * R
    t_c = lax.broadcasted_iota(jnp.int32, (hr, W), 0) % R
    mask_c = lax.broadcasted_iota(jnp.int32, (hr, W), 1) > t_c
    t_n = lax.broadcasted_iota(jnp.int32, (hr, R), 0) % R
    mask_n = lax.broadcasted_iota(jnp.int32, (hr, R), 1) <= t_n
    sink = sink_ref[...]
    lower = _lane_lower((R, LANES))
    zeros = jnp.zeros((R, LANES), F32)
    for s in range(q_ref.shape[0] // R):
        sl = slice(s * R, (s + 1) * R)
        kc, vc = kc_ref[s], vc_ref[s]
        kn, vn = k_new[sl, :], v_new[sl, :]
        ko_ref[s] = _shift_cache(kc, kn, n_real)
        vo_ref[s] = _shift_cache(vc, vn, n_real)
        q_rows = []
        for h in range(SWA_HEADS):
            kh = h // SWA_GROUP
            x = q_pairs[h // 2][sl, :]
            if h % 2 != kh % 2:
                x = pltpu.roll(x, HEAD_DIM, 1)
            x = jnp.where(lower, x, zeros) if kh % 2 == 0 else jnp.where(lower, zeros, x)
            q_rows.append(jnp.concatenate([x, zeros] if kh // 2 == 0 else [zeros, x], axis=1))
        qbd = jnp.concatenate(q_rows, axis=0)
        sc = jnp.where(mask_c, _dot_nt(qbd.astype(BF16), kc.astype(BF16)), -jnp.inf)
        sn = jnp.where(mask_n, _dot_nt(qbd, kn), -jnp.inf)
        m = jnp.maximum(jnp.maximum(jnp.max(sc, axis=-1, keepdims=True),
                                    jnp.max(sn, axis=-1, keepdims=True)), sink)
        pc, pn = jnp.exp(sc - m), jnp.exp(sn - m)
        denom = (jnp.sum(pc, axis=-1, keepdims=True) + jnp.sum(pn, axis=-1, keepdims=True)
                 + jnp.exp(sink - m))
        o = (_dot(pc.astype(BF16), vc.astype(BF16)) + _dot(pn, vn)) / denom
        for j in range(SWA_HEADS // 2):
            halves = []
            for h in (2 * j, 2 * j + 1):
                kh = h // SWA_GROUP
                y = o[h * R:(h + 1) * R, (kh // 2) * LANES:(kh // 2 + 1) * LANES]
                halves.append(pltpu.roll(y, HEAD_DIM, 1) if h % 2 != kh % 2 else y)
            o_ref[sl, j * LANES:(j + 1) * LANES] = jnp.where(lower, halves[0], halves[1])


def _swa_sample(ys, sink_rows, kcache, vcache, n_seq, n_real):
    R = SAMPLE_SEQS * SAMPLE_ROWS
    kvw = SWA_KV_HEADS * HEAD_DIM
    const = lambda b: (0, 0)
    cache_spec = pl.BlockSpec((SAMPLE_SEQS, WINDOW, kvw), lambda b: (b, 0, 0))
    return pl.pallas_call(
        functools.partial(_swa_sample_kernel, n_real=n_real),
        grid=(n_seq // SAMPLE_SEQS,),
        in_specs=[pl.BlockSpec((R, D_MODEL), lambda b: (b, COL_QS // D_MODEL)),
                  pl.BlockSpec((R, kvw), lambda b: (b, COL_KS // kvw)),
                  pl.BlockSpec((R, kvw), lambda b: (b, COL_VS // kvw)),
                  cache_spec, cache_spec,
                  pl.BlockSpec((SWA_HEADS * SAMPLE_ROWS, 1), const)],
        out_specs=[pl.BlockSpec((R, D_MODEL), lambda b: (b, 0)), cache_spec, cache_spec],
        out_shape=[jax.ShapeDtypeStruct((n_seq * SAMPLE_ROWS, D_MODEL), F32),
                   jax.ShapeDtypeStruct((n_seq, WINDOW, kvw), F32),
                   jax.ShapeDtypeStruct((n_seq, WINDOW, kvw), F32)],
        compiler_params=_params("parallel"),
        name="swa_sample",
    )(ys, ys, ys, kcache, vcache, sink_rows)


def _post_kernel(ag_ref, as_ref, og_ref, os_ref, x_ref, gate_ref, shift_ref, scale_ref, g2_ref,
                 wo_ref, wr_ref, br_ref, x1_ref, h2_ref, slot_ref, gatek_ref, cnt_ref, *, n_valid):
    merged = (ag_ref[...].astype(F32) * og_ref[...].astype(F32)
              + as_ref[...].astype(F32) * os_ref[...].astype(F32))
    y = _dot(merged.astype(BF16), wo_ref[...])
    rows = x_ref.shape[0]
    x1 = x_ref[...] + _mod_rows(gate_ref, rows) * y
    x1_ref[...] = x1
    h2 = _rms(x1, g2_ref[...]) * (1.0 + _mod_rows(scale_ref, rows)) + _mod_rows(shift_ref, rows)
    h2_hi = h2.astype(BF16)
    h2_ref[...] = h2_hi

    h2_lo = (h2 - h2_hi.astype(F32)).astype(BF16)
    w_hi, w_lo = wr_ref[0], wr_ref[1]
    logits = _dot(h2_hi, w_hi) + (_dot(h2_lo, w_hi) + _dot(h2_hi, w_lo)) + br_ref[...]
    lane_i = lax.broadcasted_iota(jnp.int32, logits.shape, 1)
    lane = lane_i.astype(F32)
    work = logits
    vals, hots = [], []
    for _ in range(TOP_K):
        m = jnp.max(work, axis=-1, keepdims=True)
        idx = jnp.min(jnp.where(work == m, lane, float(LANES)), axis=-1, keepdims=True)
        hot = lane == idx
        vals.append(m)
        hots.append(hot)
        work = jnp.where(hot, -jnp.inf, work)
    exps = [jnp.exp(v - vals[0]) for v in vals]
    denom = exps[0] + exps[1] + exps[2] + exps[3]

    tm = logits.shape[0]
    valid = lax.broadcasted_iota(jnp.int32, (tm, 1), 0) % SAMPLE_ROWS < n_valid
    sel = jnp.zeros_like(logits)
    for hot in hots:
        sel = jnp.where(hot, 1.0, sel)
    sel = jnp.where(valid, sel, 0.0)
    earlier = (lax.broadcasted_iota(jnp.int32, (tm, tm), 1)
               < lax.broadcasted_iota(jnp.int32, (tm, tm), 0))
    rank = _dot(jnp.where(earlier, 1.0, 0.0).astype(BF16), sel.astype(BF16))
    cnt = jnp.sum(sel, axis=0, keepdims=True)
    cnt_pad = jnp.floor((cnt + (ROW_UNIT - 1.0)) * (1.0 / ROW_UNIT)) * ROW_UNIT
    below = (lax.broadcasted_iota(jnp.int32, (LANES, LANES), 0)
             < lax.broadcasted_iota(jnp.int32, (LANES, LANES), 1))
    seg_start = _dot(jnp.broadcast_to(cnt_pad, (8, LANES)), jnp.where(below, 1.0, 0.0), HIGHEST)[0:1]
    pos = seg_start + rank
    slots = jnp.full_like(logits, -1.0)
    gates = jnp.zeros_like(logits)
    for k in range(TOP_K):
        s_k = jnp.sum(jnp.where(hots[k], pos, 0.0), axis=-1, keepdims=True)
        slots = jnp.where(lane_i == k, s_k, slots)
        gates = jnp.where(lane_i == k, exps[k] / denom, gates)
    slot_ref[...] = jnp.where(valid, slots, -1.0)
    gatek_ref[...] = gates
    cnt_ref[...] = cnt


def _post(y_all, o_gla, o_swa, x, gate, shift, scale, g2, wo, wr, br, per_token, tiles_per_seq, n_valid):
    n = x.shape[0]
    mod = _mod_spec(per_token, tiles_per_seq)
    row = lambda i: (i, 0)
    const = lambda i: (0, 0)
    wide = pl.BlockSpec((TOKEN_TILE, D_MODEL), row)
    narrow = pl.BlockSpec((TOKEN_TILE, LANES), row)
    return pl.pallas_call(
        functools.partial(_post_kernel, n_valid=n_valid),
        grid=(n // TOKEN_TILE,),
        in_specs=[pl.BlockSpec((TOKEN_TILE, D_MODEL), lambda i: (i, COL_AG // D_MODEL)),
                  pl.BlockSpec((TOKEN_TILE, D_MODEL), lambda i: (i, COL_AS // D_MODEL)),
                  wide, wide, wide, mod, mod, mod,
                  pl.BlockSpec((1, D_MODEL), const),
                  pl.BlockSpec((D_MODEL, D_MODEL), const),
                  pl.BlockSpec((2, D_MODEL, LANES), lambda i: (0, 0, 0)),
                  pl.BlockSpec((1, LANES), const)],
        out_specs=[wide, wide, narrow, narrow, pl.BlockSpec((None, 1, LANES), lambda i: (i, 0, 0))],
        out_shape=[jax.ShapeDtypeStruct((n, D_MODEL), F32),
                   jax.ShapeDtypeStruct((n, D_MODEL), BF16),
                   jax.ShapeDtypeStruct((n, LANES), F32),
                   jax.ShapeDtypeStruct((n, LANES), F32),
                   jax.ShapeDtypeStruct((n // TOKEN_TILE, 1, LANES), F32)],
        compiler_params=_params("parallel"),
        name="post",
    )(y_all, y_all, o_gla, o_swa, x, gate, shift, scale, g2, wo, wr, br)


def _slot_matrix(slot_cols, weights, chunk):
    tm = slot_cols[0].shape[0]
    j = lax.broadcasted_iota(jnp.int32, (tm, tm), 1) + chunk * tm
    out = jnp.zeros((tm, tm), F32)
    for s, w in zip(slot_cols, weights):
        out = jnp.where(s == j, w, out)
    return out.astype(BF16)


def _dispatch_kernel(nu_ref, seg_ref, ntail_ref, tail_ref, hp_ref, slp_ref, hs_ref, sls_ref, xg_ref,
                     sorted_ref, zero_ref, sem, tail_sem, *, prompt_tiles):
    t = pl.program_id(0)
    last = pl.num_programs(0) - 1
    buf = t % 2

    def sort_tile(h_ref, slot_ref):
        tm = h_ref.shape[0]
        slots = slot_ref[...].astype(jnp.int32)
        slot_cols = [slots[:, k:k + 1] for k in range(TOP_K)]
        h = h_ref[...]
        for c in range(LOCAL_ROWS // tm):
            onehot = _slot_matrix(slot_cols, [1.0] * TOP_K, c)
            upc = tm // ROW_UNIT
            sorted_ref[buf, c * upc:(c + 1) * upc] = _dot_tn(onehot, h).astype(BF16).reshape(
                upc, ROW_UNIT, D_MODEL)

    @pl.when(t < prompt_tiles)
    def _():
        sort_tile(hp_ref, slp_ref)

    @pl.when(t >= prompt_tiles)
    def _():
        sort_tile(hs_ref, sls_ref)

    def start_all(tile, b):
        def body(e, c):
            n = seg_ref[tile, 0, e]

            @pl.when(n > 0)
            def _():
                pltpu.make_async_copy(sorted_ref.at[b, pl.ds(seg_ref[tile, 1, e], n)],
                                      xg_ref.at[pl.ds(seg_ref[tile, 2, e], n)], sem.at[b]).start()
            return c

        lax.fori_loop(0, N_EXPERTS, body, 0)

    def wait_all(tile, b):
        n_units = nu_ref[tile]

        @pl.when(n_units > 0)
        def _():
            pltpu.make_async_copy(sorted_ref.at[b, pl.ds(0, n_units)],
                                  xg_ref.at[pl.ds(0, n_units)], sem.at[b]).wait()

    @pl.when(t > 0)
    def _():
        wait_all(t - 1, 1 - buf)

    start_all(t, buf)

    def tail_copy(i):
        return pltpu.make_async_copy(zero_ref, xg_ref.at[tail_ref[i]], tail_sem)

    @pl.when(t == last)
    def _():
        zero_ref[...] = jnp.zeros_like(zero_ref)
        n_tail = ntail_ref[0]
        lax.fori_loop(0, n_tail, lambda i, c: (tail_copy(i).start(), c)[1], 0)
        lax.fori_loop(0, n_tail, lambda i, c: (tail_copy(i).wait(), c)[1], 0)
        wait_all(t, buf)


def _dispatch(h2_p, slots_p, h2_s, slots_s, n_units, seg, n_tail, tail_dst, rows_max):
    p_tiles = h2_p.shape[0] // TOKEN_TILE
    s_tiles = h2_s.shape[0] // TOKEN_TILE
    p_row = lambda t, *_: (jnp.minimum(t, p_tiles - 1), 0)
    s_row = lambda t, *_: (jnp.maximum(t - p_tiles, 0), 0)
    return pl.pallas_call(
        functools.partial(_dispatch_kernel, prompt_tiles=p_tiles),
        grid_spec=pltpu.PrefetchScalarGridSpec(
            num_scalar_prefetch=4,
            grid=(p_tiles + s_tiles,),
            in_specs=[pl.BlockSpec((TOKEN_TILE, D_MODEL), p_row),
                      pl.BlockSpec((TOKEN_TILE, LANES), p_row),
                      pl.BlockSpec((TOKEN_TILE, D_MODEL), s_row),
                      pl.BlockSpec((TOKEN_TILE, LANES), s_row)],
            out_specs=pl.BlockSpec(memory_space=pl.ANY),
            scratch_shapes=[pltpu.VMEM((2, LOCAL_ROWS // ROW_UNIT, ROW_UNIT, D_MODEL), BF16),
                            pltpu.VMEM((ROW_UNIT, D_MODEL), BF16),
                            pltpu.SemaphoreType.DMA((2,)), pltpu.SemaphoreType.DMA]),
        out_shape=jax.ShapeDtypeStruct((rows_max // ROW_UNIT, ROW_UNIT, D_MODEL), BF16),
        compiler_params=_params("arbitrary"),
        name="dispatch",
    )(n_units, seg, n_tail, tail_dst, h2_p, slots_p, h2_s, slots_s).reshape(rows_max, D_MODEL)


def _expert_kernel(tg_ref, ge_ref, ng_ref, nused_ref, rows_ref, x_ref, bgu_ref, bd_ref, wgu_hbm, wd_hbm,
                   y_ref, wgu_f32, wd_f32, wgu_bf, wd_bf, sem):
    i = pl.program_id(0)

    def fetch(g, b):
        e = ge_ref[g]
        return (pltpu.make_async_copy(wgu_hbm.at[e], wgu_f32.at[b], sem.at[0, b]),
                pltpu.make_async_copy(wd_hbm.at[e], wd_f32.at[b], sem.at[1, b]))

    @pl.when(i == 0)
    def _():
        for cp in fetch(0, 0):
            cp.start()

    @pl.when(i < nused_ref[0])
    def _():
        g = tg_ref[i]
        b = g % 2

        @pl.when((i == 0) | (g != tg_ref[jnp.maximum(i - 1, 0)]))
        def _():
            @pl.when(g + 1 < ng_ref[0])
            def _():
                for cp in fetch(g + 1, 1 - b):
                    cp.start()

            for cp in fetch(g, b):
                cp.wait()
            wgu_bf[...] = wgu_f32[b].astype(BF16)
            wd_bf[...] = wd_f32[b].astype(BF16)

        for part in range(EXPERT_TILE // EXPERT_PART):
            @pl.when(rows_ref[i] > part * EXPERT_PART)
            def _():
                sl = slice(part * EXPERT_PART, (part + 1) * EXPERT_PART)
                gu = _dot(x_ref[sl, :], wgu_bf[...]) + bgu_ref[...]
                gate = jnp.minimum(gu[:, :D_FF], SWIGLU_LIMIT)
                up = jnp.clip(gu[:, D_FF:], -SWIGLU_LIMIT, SWIGLU_LIMIT)
                act = (up + 1.0) * gate * jax.nn.sigmoid(SWIGLU_ALPHA * gate)
                y_ref[sl, :] = (_dot(act.astype(BF16), wd_bf[...]) + bd_ref[...]).astype(y_ref.dtype)


def _experts(xg, tile_group, group_expert, n_groups, n_used, tile_rows, wgu, bgu, wd, bd):
    rows_max = xg.shape[0]
    used = lambda i, nu: jnp.maximum(jnp.minimum(i, nu[0] - 1), 0)
    row = lambda i, tg, ge, ng, nu, tr: (used(i, nu), 0)
    exp = lambda i, tg, ge, ng, nu, tr: (ge[tg[used(i, nu)]], 0, 0)
    return pl.pallas_call(
        _expert_kernel,
        grid_spec=pltpu.PrefetchScalarGridSpec(
            num_scalar_prefetch=5,
            grid=(rows_max // EXPERT_TILE,),
            in_specs=[pl.BlockSpec((EXPERT_TILE, D_MODEL), row),
                      pl.BlockSpec((None, 1, 2 * D_FF), exp),
                      pl.BlockSpec((None, 1, D_MODEL), exp),
                      pl.BlockSpec(memory_space=pl.ANY),
                      pl.BlockSpec(memory_space=pl.ANY)],
            out_specs=pl.BlockSpec((EXPERT_TILE, D_MODEL), row),
            scratch_shapes=[pltpu.VMEM((2, D_MODEL, 2 * D_FF), F32), pltpu.VMEM((2, D_FF, D_MODEL), F32),
                            pltpu.VMEM((D_MODEL, 2 * D_FF), BF16), pltpu.VMEM((D_FF, D_MODEL), BF16),
                            pltpu.SemaphoreType.DMA((2, 2))]),
        out_shape=jax.ShapeDtypeStruct((rows_max, D_MODEL), BF16),
        compiler_params=_params("arbitrary"),
        name="experts",
    )(tile_group, group_expert, n_groups, n_used, tile_rows, xg, bgu, bd, wgu, wd)


def _combine_kernel(nu_ref, seg_ref, slot_ref, gatek_ref, x_ref, gmlp_ref, y_ref, o_ref, ys_ref, sem,
                    *, tile_offset):
    j = pl.program_id(0)
    t = j + tile_offset
    buf = j % 2
    tm = x_ref.shape[0]

    def fetch(tile, b):
        def body(e, c):
            n = seg_ref[tile, 0, e]

            @pl.when(n > 0)
            def _():
                pltpu.make_async_copy(y_ref.at[pl.ds(seg_ref[tile, 2, e], n)],
                                      ys_ref.at[b, pl.ds(seg_ref[tile, 1, e], n)], sem.at[b]).start()
            return c

        lax.fori_loop(0, N_EXPERTS, body, 0)
        n_units = nu_ref[tile]

        def zero_unit(i, c):
            ys_ref[b, i] = jnp.zeros((ROW_UNIT, D_MODEL), ys_ref.dtype)
            return c

        lax.fori_loop(n_units, LOCAL_ROWS // ROW_UNIT, zero_unit, 0)

    @pl.when(j == 0)
    def _():
        fetch(t, buf)

    @pl.when(j + 1 < pl.num_programs(0))
    def _():
        fetch(t + 1, 1 - buf)

    n_units = nu_ref[t]

    @pl.when(n_units > 0)
    def _():
        pltpu.make_async_copy(y_ref.at[pl.ds(0, n_units)], ys_ref.at[buf, pl.ds(0, n_units)],
                              sem.at[buf]).wait()

    slots = slot_ref[...].astype(jnp.int32)
    gates = gatek_ref[...]
    slot_cols = [slots[:, k:k + 1] for k in range(TOP_K)]
    gate_cols = [gates[:, k:k + 1] for k in range(TOP_K)]
    acc = jnp.zeros((tm, D_MODEL), F32)
    for c in range(LOCAL_ROWS // tm):
        upc = tm // ROW_UNIT
        rows_c = ys_ref[buf, c * upc:(c + 1) * upc].reshape(tm, D_MODEL)
        acc = acc + _dot(_slot_matrix(slot_cols, gate_cols, c), rows_c)
    o_ref[...] = x_ref[...] + _mod_rows(gmlp_ref, tm) * acc


def _combine(y, slots, gates, x1, gmlp, n_units, seg, tile_offset, per_token, tiles_per_seq):
    n = x1.shape[0]
    mod = _mod_spec(per_token, tiles_per_seq)
    wide = pl.BlockSpec((TOKEN_TILE, D_MODEL), lambda i, *_: (i, 0))
    narrow = pl.BlockSpec((TOKEN_TILE, LANES), lambda i, *_: (i, 0))
    return pl.pallas_call(
        functools.partial(_combine_kernel, tile_offset=tile_offset),
        grid_spec=pltpu.PrefetchScalarGridSpec(
            num_scalar_prefetch=2,
            grid=(n // TOKEN_TILE,),
            in_specs=[narrow, narrow, wide, mod, pl.BlockSpec(memory_space=pl.ANY)],
            out_specs=wide,
            scratch_shapes=[pltpu.VMEM((2, LOCAL_ROWS // ROW_UNIT, ROW_UNIT, D_MODEL), BF16),
                            pltpu.SemaphoreType.DMA((2,))]),
        out_shape=jax.ShapeDtypeStruct((n, D_MODEL), F32),
        compiler_params=_params("arbitrary"),
        name="combine",
    )(n_units, seg, slots, gates, x1, gmlp, y.reshape(-1, ROW_UNIT, D_MODEL))


def _route_tables(cnt, rows_max):
    units = (cnt + ROW_UNIT - 1) // ROW_UNIT
    group_units = jnp.sum(units, axis=0)
    upt = EXPERT_TILE // ROW_UNIT
    group_pad = (group_units + upt - 1) // upt * upt
    group_end = jnp.cumsum(group_pad)
    group_start = group_end - group_pad
    seg_start = group_start[None, :] + jnp.cumsum(units, axis=0) - units
    local_end = jnp.cumsum(units, axis=1)
    local_start = local_end - units
    n_units = local_end[:, -1].astype(jnp.int32)

    def pick(lo, hi, pos, value):
        return jnp.sum(jnp.where((pos >= lo) & (pos < hi), value, 0), axis=-1).astype(jnp.int32)

    seg = jnp.stack([units, local_start, seg_start], axis=1).astype(jnp.int32)

    upp = EXPERT_PART // ROW_UNIT
    n_tail_e = (group_units + upp - 1) // upp * upp - group_units
    j = jnp.arange(N_EXPERTS * upp, dtype=jnp.int32)[:, None]
    tail_end = jnp.cumsum(n_tail_e)
    tail_start = tail_end - n_tail_e
    tail_hbm = pick(tail_start[None, :], tail_end[None, :], j,
                    (group_start + group_units)[None, :] + j - tail_start[None, :])
    n_tail = tail_end[-1:].astype(jnp.int32)

    r = jnp.arange(rows_max // EXPERT_TILE, dtype=jnp.int32)[:, None] * upt
    n_used = (group_end[-1:] // upt).astype(jnp.int32)
    nonempty = group_units > 0
    group_of_expert = jnp.cumsum(nonempty) - 1
    tile_rows = pick(group_start[None, :], group_end[None, :], r,
                     jnp.clip(((group_start + group_units)[None, :] - r) * ROW_UNIT, 0, EXPERT_TILE))
    tile_group = pick(group_start[None, :], group_end[None, :], r, group_of_expert[None, :])
    g = jnp.arange(N_EXPERTS, dtype=jnp.int32)
    group_expert = jnp.sum(jnp.where(nonempty[None, :] & (group_of_expert[None, :] == g[:, None]),
                                     g[None, :], 0), axis=-1).astype(jnp.int32)
    n_groups = jnp.sum(nonempty)[None].astype(jnp.int32)
    return (n_units, seg, n_tail, tail_hbm,
            tile_group, group_expert, n_groups, n_used, tile_rows)


def _rope_tables(pos):
    half = HEAD_DIM // 2
    lane = np.arange(LANES)
    inv = jnp.asarray(ROPE_THETA, F32) ** (-jnp.asarray(lane % half, F32) / half)
    sign = jnp.asarray(np.where(lane % HEAD_DIM < half, -1.0, 1.0), F32)
    ang = pos.astype(F32)[:, None] * inv[None, :]
    return jnp.cos(ang), jnp.sin(ang) * sign[None, :]


def kernel(x_prompt, x_sample, c_prompt, c_sample, state_gla, cache_swa_k, cache_swa_v, w_ada, b_ada,
           norm1_g, norm2_g, w_in, w_gk2, b_gk, gla_norm_g, q_norm_g, k_norm_g, attn_sinks, w_o,
           w_router, b_router, w_gate_up, b_gate_up, w_down, b_down):
    batch, seq, d = x_prompt.shape
    n_seq, n_real, _ = x_sample.shape
    depth = w_in.shape[0]
    assert depth == 1 and d == D_MODEL and n_real <= SAMPLE_ROWS
    assert seq % TOKEN_TILE == 0 and (n_seq * SAMPLE_ROWS) % TOKEN_TILE == 0 and n_seq % SAMPLE_SEQS == 0
    R = SAMPLE_ROWS
    kvw = SWA_KV_HEADS * HEAD_DIM
    tiles_per_seq = seq // TOKEN_TILE

    w_main, w_glr = _pack_w_in(w_in.astype(BF16))
    wgk = jnp.pad(w_gk2[0], ((0, LANES - GATE_RANK), (0, 0)))
    wgk_hi = wgk.astype(BF16)
    wgk = jnp.stack([wgk_hi, (wgk - wgk_hi.astype(F32)).astype(BF16)])
    bgk = b_gk[0].reshape(1, -1)
    gnorm = gla_norm_g[0].reshape(1, -1)
    gq = jnp.tile(q_norm_g[0], LANES // HEAD_DIM).reshape(1, LANES)
    gk = jnp.tile(k_norm_g[0], LANES // HEAD_DIM).reshape(1, LANES)
    seg = jnp.asarray(np.kron(np.eye(LANES // HEAD_DIM), np.ones((HEAD_DIM, HEAD_DIM))), BF16)
    sinks = attn_sinks[0]
    wo = w_o[0].astype(BF16)
    wr = jnp.pad(w_router[0], ((0, 0), (0, LANES - N_EXPERTS)))
    wr_hi = wr.astype(BF16)
    wr = jnp.stack([wr_hi, (wr - wr_hi.astype(F32)).astype(BF16)])
    br = jnp.pad(b_router[0], (0, LANES - N_EXPERTS), constant_values=-1e30).reshape(1, LANES)
    bgu = b_gate_up[0].reshape(N_EXPERTS, 1, 2 * D_FF)
    bd = b_down[0].reshape(N_EXPERTS, 1, D_MODEL)
    g1 = norm1_g[0].reshape(1, -1)
    g2 = norm2_g[0].reshape(1, -1)

    n_c = batch + n_seq
    c_all = jnp.pad(jnp.concatenate([c_prompt, c_sample], axis=0), ((0, -n_c % 8), (0, 0)))
    m_all = _ada(c_all, w_ada[0], b_ada[0])
    mp = [m_all[:batch, i * d:(i + 1) * d].reshape(batch, 1, d) for i in range(6)]
    ms = [m_all[batch:n_c, i * d:(i + 1) * d] for i in range(6)]

    xp = x_prompt.reshape(batch * seq, d)
    xs = jnp.pad(x_sample, ((0, 0), (0, R - n_real), (0, 0))).reshape(n_seq * R, d)
    cos_p, sin_p = _rope_tables(jnp.arange(seq))
    cos_s, sin_s = _rope_tables(PAST_LEN + jnp.tile(jnp.arange(R), n_seq))
    sink_rows = jnp.repeat(sinks, R).reshape(SWA_HEADS * R, 1)

    yp, glr_p = _inproj(xp, mp[0], mp[1], g1, w_main, w_glr, cos_p, sin_p, gq, gk, seg,
                        BF16, False, tiles_per_seq)
    ys, glr_s = _inproj(xs, ms[0], ms[1], g1, w_main, w_glr, cos_s, sin_s, gq, gk, seg, F32, True, 1)
    og_p, st_p = _gla_prompt(yp, glr_p, wgk, bgk, gnorm, batch, seq)
    og_s, st_s = _gla_sample(ys, glr_s, wgk, bgk, gnorm, state_gla[0], n_seq, n_real)
    os_p, kc_p, vc_p = _swa_prompt(yp, sinks, batch, seq)
    os_s, kc_s, vc_s = _swa_sample(ys, sink_rows, cache_swa_k[0].reshape(n_seq, WINDOW, kvw),
                                   cache_swa_v[0].reshape(n_seq, WINDOW, kvw), n_seq, n_real)
    x1_p, h2_p, sl_p, gt_p, cnt_p = _post(yp, og_p, os_p, xp, mp[2], mp[3], mp[4], g2, wo, wr, br,
                                          False, tiles_per_seq, R)
    x1_s, h2_s, sl_s, gt_s, cnt_s = _post(ys, og_s, os_s, xs, ms[2], ms[3], ms[4], g2, wo, wr, br,
                                          True, 1, n_real)

    cnt = jnp.concatenate([cnt_p, cnt_s], axis=0)[:, 0, :N_EXPERTS].astype(jnp.int32)
    n_tiles = cnt.shape[0]
    rows_bound = (TOP_K * (batch * seq + n_seq * n_real) + n_tiles * N_EXPERTS * (ROW_UNIT - 1)
                  + N_EXPERTS * (EXPERT_TILE - 1))
    rows_max = -(-rows_bound // EXPERT_TILE) * EXPERT_TILE
    (n_units, seg, n_tail, tail_hbm,
     tile_group, group_expert, n_groups, n_used, tile_rows) = _route_tables(cnt, rows_max)
    xg = _dispatch(h2_p, sl_p, h2_s, sl_s, n_units, seg, n_tail, tail_hbm, rows_max)
    yg = _experts(xg, tile_group, group_expert, n_groups, n_used, tile_rows,
                  w_gate_up[0], bgu, w_down[0], bd)
    p_tiles = batch * seq // TOKEN_TILE
    out_p = _combine(yg, sl_p, gt_p, x1_p, mp[5], n_units, seg, 0, False, tiles_per_seq)
    out_s = _combine(yg, sl_s, gt_s, x1_s, ms[5], n_units, seg, p_tiles, True, 1)

    cache_shape = (WINDOW, SWA_KV_HEADS, HEAD_DIM)
    return (out_p.reshape(batch, seq, d),
            out_s.reshape(n_seq, R, d)[:, :n_real],
            st_p[None],
            kc_p.reshape(1, batch, *cache_shape),
            vc_p.reshape(1, batch, *cache_shape),
            st_s[None],
            kc_s.reshape(1, n_seq, *cache_shape),
            vc_s.reshape(1, n_seq, *cache_shape))
```

```python
import functools

import numpy as np
import jax
import jax.numpy as jnp
from jax import lax
from jax.experimental import pallas as pl
from jax.experimental.pallas import tpu as pltpu

F32 = jnp.float32
BF16 = jnp.bfloat16
HIGHEST = lax.Precision.HIGHEST

D_MODEL = 1024
PAST_LEN = 16384
GLA_HEADS = 4
GLA_DK = 128
GLA_DV = 256
GATE_RANK = 16
GATE_TAU = 16.0
SWA_HEADS = 16
SWA_KV_HEADS = 4
HEAD_DIM = 64
SWA_GROUP = SWA_HEADS // SWA_KV_HEADS
WINDOW = 128
ROPE_THETA = 10000.0
N_EXPERTS = 32
TOP_K = 4
D_FF = 1024
SWIGLU_ALPHA = 1.702
SWIGLU_LIMIT = 7.0
NORM_EPS = 1e-6

LANES = 128
SUBLANES = 8
SAMPLE_ROWS = 8
SAMPLE_SEQS = 16
TOKEN_TILE = 512
SWA_STEP_BLOCKS = 4
GLA_BLOCK = 512
GLA_MASK_BLOCK = 128
VMEM_LIMIT = 56 * 1024 * 1024
ROW_UNIT = 16
EXPERT_TILE = 1024
EXPERT_PART = 256
LOCAL_ROWS = -(-(TOP_K * TOKEN_TILE + N_EXPERTS * (ROW_UNIT - 1)) // TOKEN_TILE) * TOKEN_TILE

COL_VG, COL_RG, COL_QS, COL_AG, COL_AS = 0, 1024, 2048, 3072, 4096
COL_QG, COL_KG, COL_KS, COL_VS = 5120, 5632, 6144, 6400
D_MAIN = 6656
PROJ_CHUNK = 256
PROJ_AHEAD = 4


def _dot(a, b, precision=None):
    return jnp.dot(a, b, preferred_element_type=F32, precision=precision)


def _dot_nt(a, b, precision=None):
    return lax.dot_general(a, b, (((1,), (1,)), ((), ())), preferred_element_type=F32, precision=precision)


def _dot_tn(a, b, precision=None):
    return lax.dot_general(a, b, (((0,), (0,)), ((), ())), preferred_element_type=F32, precision=precision)


def _dot_sum(sel, x):
    hi = x.astype(BF16)
    r1 = x - hi.astype(F32)
    mid = r1.astype(BF16)
    lo = (r1 - mid.astype(F32)).astype(BF16)
    return _dot(sel, hi) + _dot(sel, mid) + _dot(sel, lo)


def _params(*sem):
    return pltpu.CompilerParams(dimension_semantics=sem, vmem_limit_bytes=VMEM_LIMIT)


def _rms(x, g):
    return x * lax.rsqrt(jnp.mean(x * x, axis=-1, keepdims=True) + NORM_EPS) * g


def _log_sigmoid(x):
    return jnp.minimum(x, 0.0) - jnp.log(1.0 + jnp.exp(-jnp.abs(x)))


def _ada_kernel(c_ref, w_ref, b_ref, o_ref):
    c = c_ref[...]
    s = c * jax.nn.sigmoid(c)
    s_hi = s.astype(BF16)
    s_lo = (s - s_hi.astype(F32)).astype(BF16)
    w = w_ref[...]
    w_hi = w.astype(BF16)
    w_lo = (w - w_hi.astype(F32)).astype(BF16)
    o_ref[...] = _dot(s_hi, w_hi) + (_dot(s_lo, w_hi) + _dot(s_hi, w_lo)) + b_ref[...]


def _ada(c_all, w_ada, b_ada):
    rows = c_all.shape[0]
    tn = 768
    return pl.pallas_call(
        _ada_kernel,
        grid=(6 * D_MODEL // tn,),
        in_specs=[pl.BlockSpec((rows, D_MODEL), lambda j: (0, 0)),
                  pl.BlockSpec((D_MODEL, tn), lambda j: (0, j)),
                  pl.BlockSpec((1, tn), lambda j: (0, j))],
        out_specs=pl.BlockSpec((rows, tn), lambda j: (0, j)),
        out_shape=jax.ShapeDtypeStruct((rows, 6 * D_MODEL), F32),
        compiler_params=_params("parallel"),
        name="ada",
    )(c_all, w_ada, b_ada.reshape(1, -1))


W_IN_GLR = 3072
W_IN_PIECES = ((1024, 1024), (2048, 1024), (3088, 1024), (4624, 1024), (5648, 1024),
               (0, 512), (512, 512), (4112, 256), (4368, 256))


def _pack_w_kernel(w_ref, o_ref, og_ref):
    dst = 0
    for src, width in W_IN_PIECES:
        o_ref[:, dst:dst + width] = w_ref[:, src:src + width].astype(BF16)
        dst += width
    glr = w_ref[:, W_IN_GLR:W_IN_GLR + LANES]
    lane = lax.broadcasted_iota(jnp.int32, glr.shape, 1)
    og_ref[...] = jnp.where(lane < GATE_RANK, glr, 0.0).astype(BF16)


def _pack_w_in(w):
    rows, cols = w.shape
    assert sum(width for _, width in W_IN_PIECES) == D_MAIN and cols == D_MAIN + GATE_RANK
    tr = 128
    return pl.pallas_call(
        _pack_w_kernel,
        grid=(rows // tr,),
        in_specs=[pl.BlockSpec((tr, cols), lambda i: (i, 0))],
        out_specs=[pl.BlockSpec((tr, D_MAIN), lambda i: (i, 0)), pl.BlockSpec((tr, LANES), lambda i: (i, 0))],
        out_shape=[jax.ShapeDtypeStruct((rows, D_MAIN), BF16), jax.ShapeDtypeStruct((rows, LANES), BF16)],
        compiler_params=_params("parallel"),
        name="pack_w",
    )(w)


def _inproj_kernel(x_ref, shift_ref, scale_ref, g_ref, w_ref, wg_ref, cos_ref, sin_ref, gq_ref, gk_ref,
                   seg_ref, o_ref, og_ref):
    rows = x_ref.shape[0]
    h = _rms(x_ref[...], g_ref[...]) * (1.0 + _mod_rows(scale_ref, rows)) + _mod_rows(shift_ref, rows)
    hb = h.astype(BF16)
    cos, sin, seg = cos_ref[...], sin_ref[...], seg_ref[...]
    def has_followup(j):
        c0, c1 = j * PROJ_CHUNK, (j + 1) * PROJ_CHUNK
        return c0 < COL_AS + D_MODEL and c1 > COL_QS or c0 < COL_KS + SWA_KV_HEADS * HEAD_DIM and c1 > COL_KS

    chunks = range(D_MAIN // PROJ_CHUNK)
    busy = [j for j in chunks if has_followup(j)]
    plain = [j for j in chunks if not has_followup(j)]
    order = [j for pair in zip(busy, plain) for j in pair] + busy[len(plain):] + plain[len(busy):]
    project = lambda j: _dot(hb, w_ref[:, j * PROJ_CHUNK:(j + 1) * PROJ_CHUNK])
    ahead = [project(j) for j in order[:PROJ_AHEAD]]
    for idx, j in enumerate(order):
        sl = slice(j * PROJ_CHUNK, (j + 1) * PROJ_CHUNK)
        r = ahead.pop(0)
        if idx + PROJ_AHEAD < len(order):
            ahead.append(project(order[idx + PROJ_AHEAD]))
        blocks = []
        for m in range(PROJ_CHUNK // LANES):
            c0 = j * PROJ_CHUNK + m * LANES
            blk = r[:, m * LANES:(m + 1) * LANES]
            if COL_QS <= c0 < COL_QS + SWA_HEADS * HEAD_DIM:
                blk = _head_norm_rope(blk, gq_ref[...], cos, sin, seg) * (HEAD_DIM ** -0.5)
            elif COL_KS <= c0 < COL_KS + SWA_KV_HEADS * HEAD_DIM:
                blk = _head_norm_rope(blk, gk_ref[...], cos, sin, seg)
            elif COL_AG <= c0 < COL_AS + D_MODEL:
                blk = jax.nn.sigmoid(blk)
            blocks.append(blk)
        o_ref[:, sl] = jnp.concatenate(blocks, axis=1).astype(o_ref.dtype)
    og_ref[...] = _dot(hb, wg_ref[...])


def _mod_spec(per_token, tiles_per_seq):
    if per_token:
        return pl.BlockSpec((TOKEN_TILE // SAMPLE_ROWS, D_MODEL), lambda i, *_: (i, 0))
    return pl.BlockSpec((None, 1, D_MODEL), lambda i, *_: (i // tiles_per_seq, 0, 0))


def _mod_rows(ref, rows):
    m = ref[...]
    if m.shape[0] == 1:
        return m
    return jnp.broadcast_to(m[:, None, :], (m.shape[0], rows // m.shape[0], m.shape[1])).reshape(
        rows, m.shape[1])


def _inproj(x, shift, scale, g, w_main, w_glr, cos, sin, gq, gk, seg, out_dtype, per_token, tiles_per_seq):
    n = x.shape[0]
    mod = _mod_spec(per_token, tiles_per_seq)
    const = lambda i: (0, 0)
    rope = pl.BlockSpec((TOKEN_TILE, LANES), (lambda i: (i, 0)) if per_token else
                        (lambda i: (i % tiles_per_seq, 0)))
    return pl.pallas_call(
        _inproj_kernel,
        grid=(n // TOKEN_TILE,),
        in_specs=[pl.BlockSpec((TOKEN_TILE, D_MODEL), lambda i: (i, 0)), mod, mod,
                  pl.BlockSpec((1, D_MODEL), const),
                  pl.BlockSpec((D_MODEL, D_MAIN), const, pipeline_mode=pl.Buffered(1)),
                  pl.BlockSpec((D_MODEL, LANES), const, pipeline_mode=pl.Buffered(1)),
                  rope, rope,
                  pl.BlockSpec((1, LANES), const), pl.BlockSpec((1, LANES), const),
                  pl.BlockSpec((LANES, LANES), const)],
        out_specs=[pl.BlockSpec((TOKEN_TILE, D_MAIN), lambda i: (i, 0)),
                   pl.BlockSpec((TOKEN_TILE, LANES), lambda i: (i, 0))],
        out_shape=[jax.ShapeDtypeStruct((n, D_MAIN), out_dtype),
                   jax.ShapeDtypeStruct((n, LANES), F32)],
        compiler_params=_params("parallel"),
        name="inproj",
    )(x, shift, scale, g, w_main, w_glr, cos, sin, gq, gk, seg)


def _gla_log_gate(glr, wgk_ref, bgk):
    g_hi = glr.astype(BF16)
    g_lo = (glr - g_hi.astype(F32)).astype(BF16)
    w_hi, w_lo = wgk_ref[0], wgk_ref[1]
    x = _dot(g_hi, w_hi) + (_dot(g_lo, w_hi) + _dot(g_hi, w_lo))
    return _log_sigmoid(x + bgk) * (1.0 / GATE_TAU)


def _gla_out(o, r, g):
    r = r.astype(F32)
    return _rms(o, g) * (r * jax.nn.sigmoid(r))


def _gla_pair_levels(n):
    t = np.arange(n)[:, None]
    s = np.arange(n)[None, :]
    x = t ^ s
    top = np.where(x > 0, 1 << np.floor(np.log2(np.maximum(x, 1))).astype(np.int64), 0)
    return np.where(s > t, -1, top).astype(np.int32)


def _gla_block_ref(b, h):
    n, w = b.shape
    if 2 * h == n:
        return jnp.broadcast_to(b[h - 1:h, :], (n, w))
    if h >= SUBLANES // 2:
        picked = b.reshape(n // (2 * h), 2 * h, w)[:, h - 1:h, :]
        return jnp.broadcast_to(picked, (n // (2 * h), 2 * h, w)).reshape(n, w)
    r = lax.broadcasted_iota(jnp.int32, (n, 1), 0) % (2 * h)
    out = b
    for d in range(1, h + 1):
        out = jnp.where(r == h - 1 + d, pltpu.roll(b, d, 0), out)
    for d in range(1, h):
        out = jnp.where(r == h - 1 - d, pltpu.roll(b, n - d, 0), out)
    return out


def _gla_prompt_kernel(q_ref, k_ref, v_ref, r_ref, glr_ref, wgk_ref, bgk_ref, g_ref, lev_ref,
                       o_ref, s_ref, st_ref):
    c = pl.program_id(1)

    @pl.when(c == 0)
    def _():
        st_ref[...] = jnp.zeros_like(st_ref)

    n = GLA_BLOCK
    lev = lev_ref[...]
    causal = lax.broadcasted_iota(jnp.int32, (n, n), 1) <= lax.broadcasted_iota(jnp.int32, (n, n), 0)
    lg = _gla_log_gate(glr_ref[...], wgk_ref, bgk_ref[...])
    b = _dot_sum(jnp.where(causal, 1.0, 0.0).astype(BF16), lg)
    b_last = b[n - 1:n, :]
    q = q_ref[...].astype(F32) * (GLA_DK ** -0.5)
    k = k_ref[...].astype(F32)
    qe = (q * jnp.exp(b)).astype(BF16)
    kd = (k * jnp.exp(b_last - b)).astype(BF16)
    decay = jnp.exp(b_last)
    levels = [0] + [1 << p for p in range(n.bit_length() - 1)]
    q_lv, k_lv = [q.astype(BF16)], [k.astype(BF16)]
    for h in levels[1:]:
        e = jnp.exp(-jnp.abs(b - _gla_block_ref(b, h)))
        q_lv.append((q * e).astype(BF16))
        k_lv.append((k * e).astype(BF16))
    scaled = {level: (ql, kl) for level, ql, kl in zip(levels, q_lv, k_lv)}

    def intra(lo, hi, dk, v):
        size = hi - lo
        if size == GLA_MASK_BLOCK:
            attn = jnp.zeros((size, size), F32)
            for level in levels:
                if level < size:
                    ql, kl = scaled[level]
                    attn = jnp.where(lev == level, _dot_nt(ql[lo:hi, dk], kl[lo:hi, dk]), attn)
            return _dot(attn.astype(BF16), v[lo:hi, :])
        mid = lo + size // 2
        ql, kl = scaled[size // 2]
        cross = _dot_nt(ql[mid:hi, dk], kl[lo:mid, dk]).astype(BF16)
        return jnp.concatenate([intra(lo, mid, dk, v),
                                _dot(cross, v[lo:mid, :]) + intra(mid, hi, dk, v)], axis=0)

    for h in range(GLA_HEADS):
        dk = slice(h * GLA_DK, (h + 1) * GLA_DK)
        dv = slice(h * GLA_DV, (h + 1) * GLA_DV)
        v = v_ref[:, dv]
        st = st_ref[h]
        o = _dot_nt(qe[:, dk], st.astype(BF16)) + intra(0, n, dk, v)
        st_ref[h] = st * decay[:, dk] + _dot_tn(v, kd[:, dk])
        o_ref[:, dv] = _gla_out(o, r_ref[:, dv], g_ref[...]).astype(o_ref.dtype)

    @pl.when(c == pl.num_programs(1) - 1)
    def _():
        for h in range(GLA_HEADS):
            s_ref[h] = st_ref[h].T


def _gla_prompt(yp, glr, wgk, bgk, gnorm, batch, seq):
    nb = seq // GLA_BLOCK
    hk, hv = GLA_HEADS * GLA_DK, GLA_HEADS * GLA_DV
    tok = lambda b, c: b * nb + c
    const = lambda b, c: (0, 0)
    return pl.pallas_call(
        _gla_prompt_kernel,
        grid=(batch, nb),
        in_specs=[pl.BlockSpec((GLA_BLOCK, hk), lambda b, c: (tok(b, c), COL_QG // hk)),
                  pl.BlockSpec((GLA_BLOCK, hk), lambda b, c: (tok(b, c), COL_KG // hk)),
                  pl.BlockSpec((GLA_BLOCK, hv), lambda b, c: (tok(b, c), COL_VG // hv)),
                  pl.BlockSpec((GLA_BLOCK, hv), lambda b, c: (tok(b, c), COL_RG // hv)),
                  pl.BlockSpec((GLA_BLOCK, LANES), lambda b, c: (tok(b, c), 0)),
                  pl.BlockSpec((2, LANES, hk), lambda b, c: (0, 0, 0)),
                  pl.BlockSpec((1, hk), const),
                  pl.BlockSpec((1, GLA_DV), const),
                  pl.BlockSpec((GLA_MASK_BLOCK, GLA_MASK_BLOCK), const)],
        out_specs=[pl.BlockSpec((GLA_BLOCK, hv), lambda b, c: (tok(b, c), 0)),
                   pl.BlockSpec((None, GLA_HEADS, GLA_DK, GLA_DV), lambda b, c: (b, 0, 0, 0))],
        out_shape=[jax.ShapeDtypeStruct((batch * seq, hv), BF16),
                   jax.ShapeDtypeStruct((batch, GLA_HEADS, GLA_DK, GLA_DV), F32)],
        scratch_shapes=[pltpu.VMEM((GLA_HEADS, GLA_DV, GLA_DK), F32)],
        compiler_params=_params("parallel", "arbitrary"),
        name="gla_prompt",
    )(yp, yp, yp, yp, glr, wgk, bgk, gnorm, jnp.asarray(_gla_pair_levels(GLA_MASK_BLOCK)))


def _gla_sample_kernel(q_ref, k_ref, v_ref, r_ref, glr_ref, wgk_ref, bgk_ref, g_ref, s0_ref, lev_ref,
                       o_ref, s_ref, *, n_real):
    R = SAMPLE_ROWS
    rows = q_ref.shape[0]
    row = lax.broadcasted_iota(jnp.int32, (rows, rows), 0)
    col = lax.broadcasted_iota(jnp.int32, (rows, rows), 1)
    same = (row // R) == (col // R)
    causal = same & (col <= row)
    real = lax.broadcasted_iota(jnp.int32, (rows, 1), 0) % R < n_real
    lg = jnp.where(real, _gla_log_gate(glr_ref[...], wgk_ref, bgk_ref[...]), 0.0)
    b = _dot_sum(jnp.where(causal, 1.0, 0.0).astype(BF16), lg)
    b_last = _dot_sum(jnp.where(same, 1.0, 0.0).astype(BF16), lg)
    q = q_ref[...] * (GLA_DK ** -0.5)
    k = jnp.where(real, k_ref[...], 0.0)
    v = v_ref[...]
    qe = q * jnp.exp(b)
    kd = k * jnp.exp(b_last - b)
    lev = lev_ref[...]
    levels = [0] + [1 << p for p in range(R.bit_length() - 1)]
    q_lv, k_lv = [q.astype(BF16)], [k.astype(BF16)]
    for h in levels[1:]:
        e = jnp.exp(-jnp.abs(b - _gla_block_ref(b, h)))
        q_lv.append((q * e).astype(BF16))
        k_lv.append((k * e).astype(BF16))
    v_bf = v.astype(BF16)
    for h in range(GLA_HEADS):
        dk = slice(h * GLA_DK, (h + 1) * GLA_DK)
        dv = slice(h * GLA_DV, (h + 1) * GLA_DV)
        attn = jnp.zeros((rows, rows), F32)
        for level, ql, kl in zip(levels, q_lv, k_lv):
            attn = jnp.where(lev == level, _dot_nt(ql[:, dk], kl[:, dk]), attn)
        o_intra = _dot(attn.astype(BF16), v_bf[:, dv])
        decay_t = jnp.exp(b_last[:, dk]).T
        outs = []
        for s in range(rows // R):
            sl = slice(s * R, (s + 1) * R)
            s0 = s0_ref[s, h]
            outs.append(_dot(qe[sl, dk], s0) + o_intra[sl, :])
            s_ref[s, h] = s0 * decay_t[:, s * R:s * R + 1] + _dot_tn(kd[sl, dk], v[sl, dv])
        o_ref[:, dv] = _gla_out(jnp.concatenate(outs, axis=0), r_ref[:, dv], g_ref[...])


def _gla_sample(ys, glr, wgk, bgk, gnorm, state, n_seq, n_real):
    R = SAMPLE_SEQS * SAMPLE_ROWS
    hk, hv = GLA_HEADS * GLA_DK, GLA_HEADS * GLA_DV
    st_spec = pl.BlockSpec((SAMPLE_SEQS, GLA_HEADS, GLA_DK, GLA_DV), lambda b: (b, 0, 0, 0))
    lev = _gla_pair_levels(R)
    lev = np.where(lev >= SAMPLE_ROWS, -1, lev)
    return pl.pallas_call(
        functools.partial(_gla_sample_kernel, n_real=n_real),
        grid=(n_seq // SAMPLE_SEQS,),
        in_specs=[pl.BlockSpec((R, hk), lambda b: (b, COL_QG // hk)),
                  pl.BlockSpec((R, hk), lambda b: (b, COL_KG // hk)),
                  pl.BlockSpec((R, hv), lambda b: (b, COL_VG // hv)),
                  pl.BlockSpec((R, hv), lambda b: (b, COL_RG // hv)),
                  pl.BlockSpec((R, LANES), lambda b: (b, 0)),
                  pl.BlockSpec((2, LANES, hk), lambda b: (0, 0, 0)),
                  pl.BlockSpec((1, hk), lambda b: (0, 0)),
                  pl.BlockSpec((1, GLA_DV), lambda b: (0, 0)),
                  st_spec,
                  pl.BlockSpec((R, R), lambda b: (0, 0))],
        out_specs=[pl.BlockSpec((R, hv), lambda b: (b, 0)), st_spec],
        out_shape=[jax.ShapeDtypeStruct((n_seq * SAMPLE_ROWS, hv), F32),
                   jax.ShapeDtypeStruct((n_seq, GLA_HEADS, GLA_DK, GLA_DV), F32)],
        compiler_params=_params("parallel"),
        name="gla_sample",
    )(ys, ys, ys, ys, glr, wgk, bgk, gnorm, state, jnp.asarray(lev))


def _lane_lower(shape):
    return lax.broadcasted_iota(jnp.int32, shape, len(shape) - 1) % LANES < HEAD_DIM


def _head_norm_rope(x, g, cos, sin, seg):
    ss = _dot((x * x).astype(BF16), seg)
    y = x * lax.rsqrt(ss * (1.0 / HEAD_DIM) + NORM_EPS) * g
    half = HEAD_DIM // 2
    lane = lax.broadcasted_iota(jnp.int32, y.shape, 1)
    rot = jnp.where(lane % HEAD_DIM < half, pltpu.roll(y, LANES - half, 1), pltpu.roll(y, half, 1))
    return y * cos + rot * sin


def _both_halves(blk, half):
    sw = pltpu.roll(blk, HEAD_DIM, 1)
    lower = _lane_lower(blk.shape)
    return jnp.where(lower, blk, sw) if half == 0 else jnp.where(lower, sw, blk)


def _stack_heads(q_blocks):
    parts = []
    for qb in q_blocks:
        lower = _lane_lower(qb.shape)
        zero = jnp.zeros_like(qb)
        parts += [jnp.where(lower, qb, zero), jnp.where(lower, zero, qb)]
    return jnp.concatenate(parts, axis=0)


def _swa_prompt_kernel(sink_ref, q_ref, k_ref, v_ref, o_ref, ko_ref, vo_ref, kprev_ref, vprev_ref):
    n = pl.program_id(1)
    W = WINDOW

    @pl.when(n == 0)
    def _():
        kprev_ref[...] = jnp.zeros_like(kprev_ref)
        vprev_ref[...] = jnp.zeros_like(vprev_ref)

    k_prev, v_prev = kprev_ref[...], vprev_ref[...]
    for sub in range(SWA_STEP_BLOCKS):
        tok = slice(sub * W, (sub + 1) * W)
        k_cur = k_ref[tok, :].astype(F32)
        v_cur = v_ref[tok, :].astype(F32)
        prev_fill = jnp.where(n > 0, 0.0, -jnp.inf) if sub == 0 else 0.0
        _swa_block(sink_ref, q_ref, o_ref, tok, k_prev, v_prev, k_cur, v_cur, prev_fill)
        k_prev, v_prev = k_cur, v_cur
    ko_ref[...] = k_prev
    vo_ref[...] = v_prev
    kprev_ref[...] = k_prev
    vprev_ref[...] = v_prev


def _swa_block(sink_ref, q_ref, o_ref, tok, k_prev, v_prev, k_cur, v_cur, prev_fill):
    W = WINDOW
    qi = lax.broadcasted_iota(jnp.int32, (W, W), 0)
    ki = lax.broadcasted_iota(jnp.int32, (W, W), 1)
    from_cur = ki <= qi
    for kh in range(SWA_KV_HEADS):
        blk = slice((kh // 2) * LANES, (kh // 2 + 1) * LANES)
        kb_prev = _both_halves(k_prev[:, blk], kh % 2).astype(BF16)
        kb_cur = _both_halves(k_cur[:, blk], kh % 2).astype(BF16)
        vb_prev = _both_halves(v_prev[:, blk], kh % 2).astype(BF16)
        vb_cur = _both_halves(v_cur[:, blk], kh % 2).astype(BF16)
        qblocks = [q_ref[tok, (2 * kh + j) * LANES:(2 * kh + j + 1) * LANES] for j in range(2)]
        qs = _stack_heads(qblocks)
        s_prev = _dot_nt(qs, kb_prev)
        s_cur = _dot_nt(qs, kb_cur)
        outs = []
        for g in range(SWA_GROUP):
            rows = slice(g * W, (g + 1) * W)
            sg = jnp.where(from_cur, s_cur[rows, :], s_prev[rows, :] + prev_fill)
            sink = sink_ref[kh * SWA_GROUP + g]
            m = jnp.maximum(jnp.max(sg, axis=-1, keepdims=True), sink)
            p = jnp.exp(sg - m)
            denom = jnp.sum(p, axis=-1, keepdims=True) + jnp.exp(sink - m)
            p_cur = jnp.where(from_cur, p, 0.0).astype(BF16)
            p_prev = jnp.where(from_cur, 0.0, p).astype(BF16)
            outs.append((_dot(p_prev, vb_prev) + _dot(p_cur, vb_cur)) / denom)
        lower = _lane_lower((W, LANES))
        for j in range(2):
            c0 = (2 * kh + j) * LANES
            o_ref[tok, c0:c0 + LANES] = jnp.where(lower, outs[2 * j], outs[2 * j + 1]).astype(o_ref.dtype)


def _swa_prompt(yp, sinks, batch, seq):
    rows = SWA_STEP_BLOCKS * WINDOW
    nb = seq // rows
    kvw = SWA_KV_HEADS * HEAD_DIM
    tok = lambda b, n: b * nb + n
    cache_spec = pl.BlockSpec((None, WINDOW, kvw), lambda b, n: (b, 0, 0))
    return pl.pallas_call(
        _swa_prompt_kernel,
        grid=(batch, nb),
        in_specs=[pl.BlockSpec(memory_space=pltpu.SMEM),
                  pl.BlockSpec((rows, D_MODEL), lambda b, n: (tok(b, n), COL_QS // D_MODEL)),
                  pl.BlockSpec((rows, kvw), lambda b, n: (tok(b, n), COL_KS // kvw)),
                  pl.BlockSpec((rows, kvw), lambda b, n: (tok(b, n), COL_VS // kvw))],
        out_specs=[pl.BlockSpec((rows, D_MODEL), lambda b, n: (tok(b, n), 0)), cache_spec, cache_spec],
        out_shape=[jax.ShapeDtypeStruct((batch * seq, D_MODEL), BF16),
                   jax.ShapeDtypeStruct((batch, WINDOW, kvw), F32),
                   jax.ShapeDtypeStruct((batch, WINDOW, kvw), F32)],
        scratch_shapes=[pltpu.VMEM((WINDOW, kvw), F32), pltpu.VMEM((WINDOW, kvw), F32)],
        compiler_params=_params("parallel", "arbitrary"),
        name="swa_prompt",
    )(sinks, yp, yp, yp)


def _shift_cache(cache, new, n_real):
    R = SAMPLE_ROWS
    rolled = pltpu.roll(cache, WINDOW - n_real, 0)
    tail_new = pltpu.roll(new, R - n_real, 0)
    row = lax.broadcasted_iota(jnp.int32, (R, cache.shape[1]), 0)
    tail = jnp.where(row < R - n_real, rolled[WINDOW - R:, :], tail_new)
    return jnp.concatenate([rolled[:WINDOW - R, :], tail], axis=0)


def _swa_sample_kernel(q_ref, k_ref, v_ref, kc_ref, vc_ref, sink_ref, o_ref, ko_ref, vo_ref, *, n_real):
    R, W = SAMPLE_ROWS, WINDOW
    k_new = k_ref[...]
    v_new = v_ref[...]
    q_pairs = [q_ref[:, j * LANES:(j + 1) * LANES] for j in range(SWA_HEADS // 2)]

    hr = SWA_HEADS * R
    t_c = lax.broadcasted_iota(jnp.int32, (hr, W), 0) % R
    mask_c = lax.broadcasted_iota(jnp.int32, (hr, W), 1) > t_c
    t_n = lax.broadcasted_iota(jnp.int32, (hr, R), 0) % R
    mask_n = lax.broadcasted_iota(jnp.int32, (hr, R), 1) <= t_n
    sink = sink_ref[...]
    lower = _lane_lower((R, LANES))
    zeros = jnp.zeros((R, LANES), F32)
    for s in range(q_ref.shape[0] // R):
        sl = slice(s * R, (s + 1) * R)
        kc, vc = kc_ref[s], vc_ref[s]
        kn, vn = k_new[sl, :], v_new[sl, :]
        ko_ref[s] = _shift_cache(kc, kn, n_real)
        vo_ref[s] = _shift_cache(vc, vn, n_real)
        q_rows = []
        for h in range(SWA_HEADS):
            kh = h // SWA_GROUP
            x = q_pairs[h // 2][sl, :]
            if h % 2 != kh % 2:
                x = pltpu.roll(x, HEAD_DIM, 1)
            x = jnp.where(lower, x, zeros) if kh % 2 == 0 else jnp.where(lower, zeros, x)
            q_rows.append(jnp.concatenate([x, zeros] if kh // 2 == 0 else [zeros, x], axis=1))
        qbd = jnp.concatenate(q_rows, axis=0)
        sc = jnp.where(mask_c, _dot_nt(qbd.astype(BF16), kc.astype(BF16)), -jnp.inf)
        sn = jnp.where(mask_n, _dot_nt(qbd, kn), -jnp.inf)
        m = jnp.maximum(jnp.maximum(jnp.max(sc, axis=-1, keepdims=True),
                                    jnp.max(sn, axis=-1, keepdims=True)), sink)
        pc, pn = jnp.exp(sc - m), jnp.exp(sn - m)
        denom = (jnp.sum(pc, axis=-1, keepdims=True) + jnp.sum(pn, axis=-1, keepdims=True)
                 + jnp.exp(sink - m))
        o = (_dot(pc.astype(BF16), vc.astype(BF16)) + _dot(pn, vn)) / denom
        for j in range(SWA_HEADS // 2):
            halves = []
            for h in (2 * j, 2 * j + 1):
                kh = h // SWA_GROUP
                y = o[h * R:(h + 1) * R, (kh // 2) * LANES:(kh // 2 + 1) * LANES]
                halves.append(pltpu.roll(y, HEAD_DIM, 1) if h % 2 != kh % 2 else y)
            o_ref[sl, j * LANES:(j + 1) * LANES] = jnp.where(lower, halves[0], halves[1])


def _swa_sample(ys, sink_rows, kcache, vcache, n_seq, n_real):
    R = SAMPLE_SEQS * SAMPLE_ROWS
    kvw = SWA_KV_HEADS * HEAD_DIM
    const = lambda b: (0, 0)
    cache_spec = pl.BlockSpec((SAMPLE_SEQS, WINDOW, kvw), lambda b: (b, 0, 0))
    return pl.pallas_call(
        functools.partial(_swa_sample_kernel, n_real=n_real),
        grid=(n_seq // SAMPLE_SEQS,),
        in_specs=[pl.BlockSpec((R, D_MODEL), lambda b: (b, COL_QS // D_MODEL)),
                  pl.BlockSpec((R, kvw), lambda b: (b, COL_KS // kvw)),
                  pl.BlockSpec((R, kvw), lambda b: (b, COL_VS // kvw)),
                  cache_spec, cache_spec,
                  pl.BlockSpec((SWA_HEADS * SAMPLE_ROWS, 1), const)],
        out_specs=[pl.BlockSpec((R, D_MODEL), lambda b: (b, 0)), cache_spec, cache_spec],
        out_shape=[jax.ShapeDtypeStruct((n_seq * SAMPLE_ROWS, D_MODEL), F32),
                   jax.ShapeDtypeStruct((n_seq, WINDOW, kvw), F32),
                   jax.ShapeDtypeStruct((n_seq, WINDOW, kvw), F32)],
        compiler_params=_params("parallel"),
        name="swa_sample",
    )(ys, ys, ys, kcache, vcache, sink_rows)


def _post_kernel(ag_ref, as_ref, og_ref, os_ref, x_ref, gate_ref, shift_ref, scale_ref, g2_ref,
                 wo_ref, wr_ref, br_ref, x1_ref, h2_ref, slot_ref, gatek_ref, cnt_ref, *, n_valid):
    merged = (ag_ref[...].astype(F32) * og_ref[...].astype(F32)
              + as_ref[...].astype(F32) * os_ref[...].astype(F32))
    y = _dot(merged.astype(BF16), wo_ref[...])
    rows = x_ref.shape[0]
    x1 = x_ref[...] + _mod_rows(gate_ref, rows) * y
    x1_ref[...] = x1
    h2 = _rms(x1, g2_ref[...]) * (1.0 + _mod_rows(scale_ref, rows)) + _mod_rows(shift_ref, rows)
    h2_hi = h2.astype(BF16)
    h2_ref[...] = h2_hi

    h2_lo = (h2 - h2_hi.astype(F32)).astype(BF16)
    w_hi, w_lo = wr_ref[0], wr_ref[1]
    logits = _dot(h2_hi, w_hi) + (_dot(h2_lo, w_hi) + _dot(h2_hi, w_lo)) + br_ref[...]
    lane_i = lax.broadcasted_iota(jnp.int32, logits.shape, 1)
    lane = lane_i.astype(F32)
    work = logits
    vals, hots = [], []
    for _ in range(TOP_K):
        m = jnp.max(work, axis=-1, keepdims=True)
        idx = jnp.min(jnp.where(work == m, lane, float(LANES)), axis=-1, keepdims=True)
        hot = lane == idx
        vals.append(m)
        hots.append(hot)
        work = jnp.where(hot, -jnp.inf, work)
    exps = [jnp.exp(v - vals[0]) for v in vals]
    denom = exps[0] + exps[1] + exps[2] + exps[3]

    tm = logits.shape[0]
    valid = lax.broadcasted_iota(jnp.int32, (tm, 1), 0) % SAMPLE_ROWS < n_valid
    sel = jnp.zeros_like(logits)
    for hot in hots:
        sel = jnp.where(hot, 1.0, sel)
    sel = jnp.where(valid, sel, 0.0)
    earlier = (lax.broadcasted_iota(jnp.int32, (tm, tm), 1)
               < lax.broadcasted_iota(jnp.int32, (tm, tm), 0))
    rank = _dot(jnp.where(earlier, 1.0, 0.0).astype(BF16), sel.astype(BF16))
    cnt = jnp.sum(sel, axis=0, keepdims=True)
    cnt_pad = jnp.floor((cnt + (ROW_UNIT - 1.0)) * (1.0 / ROW_UNIT)) * ROW_UNIT
    below = (lax.broadcasted_iota(jnp.int32, (LANES, LANES), 0)
             < lax.broadcasted_iota(jnp.int32, (LANES, LANES), 1))
    seg_start = _dot(jnp.broadcast_to(cnt_pad, (8, LANES)), jnp.where(below, 1.0, 0.0), HIGHEST)[0:1]
    pos = seg_start + rank
    slots = jnp.full_like(logits, -1.0)
    gates = jnp.zeros_like(logits)
    for k in range(TOP_K):
        s_k = jnp.sum(jnp.where(hots[k], pos, 0.0), axis=-1, keepdims=True)
        slots = jnp.where(lane_i == k, s_k, slots)
        gates = jnp.where(lane_i == k, exps[k] / denom, gates)
    slot_ref[...] = jnp.where(valid, slots, -1.0)
    gatek_ref[...] = gates
    cnt_ref[...] = cnt


def _post(y_all, o_gla, o_swa, x, gate, shift, scale, g2, wo, wr, br, per_token, tiles_per_seq, n_valid):
    n = x.shape[0]
    mod = _mod_spec(per_token, tiles_per_seq)
    row = lambda i: (i, 0)
    const = lambda i: (0, 0)
    wide = pl.BlockSpec((TOKEN_TILE, D_MODEL), row)
    narrow = pl.BlockSpec((TOKEN_TILE, LANES), row)
    return pl.pallas_call(
        functools.partial(_post_kernel, n_valid=n_valid),
        grid=(n // TOKEN_TILE,),
        in_specs=[pl.BlockSpec((TOKEN_TILE, D_MODEL), lambda i: (i, COL_AG // D_MODEL)),
                  pl.BlockSpec((TOKEN_TILE, D_MODEL), lambda i: (i, COL_AS // D_MODEL)),
                  wide, wide, wide, mod, mod, mod,
                  pl.BlockSpec((1, D_MODEL), const),
                  pl.BlockSpec((D_MODEL, D_MODEL), const),
                  pl.BlockSpec((2, D_MODEL, LANES), lambda i: (0, 0, 0)),
                  pl.BlockSpec((1, LANES), const)],
        out_specs=[wide, wide, narrow, narrow, pl.BlockSpec((None, 1, LANES), lambda i: (i, 0, 0))],
        out_shape=[jax.ShapeDtypeStruct((n, D_MODEL), F32),
                   jax.ShapeDtypeStruct((n, D_MODEL), BF16),
                   jax.ShapeDtypeStruct((n, LANES), F32),
                   jax.ShapeDtypeStruct((n, LANES), F32),
                   jax.ShapeDtypeStruct((n // TOKEN_TILE, 1, LANES), F32)],
        compiler_params=_params("parallel"),
        name="post",
    )(y_all, y_all, o_gla, o_swa, x, gate, shift, scale, g2, wo, wr, br)


def _slot_matrix(slot_cols, weights, chunk):
    tm = slot_cols[0].shape[0]
    j = lax.broadcasted_iota(jnp.int32, (tm, tm), 1) + chunk * tm
    out = jnp.zeros((tm, tm), F32)
    for s, w in zip(slot_cols, weights):
        out = jnp.where(s == j, w, out)
    return out.astype(BF16)


def _dispatch_kernel(nu_ref, seg_ref, ntail_ref, tail_ref, hp_ref, slp_ref, hs_ref, sls_ref, xg_ref,
                     sorted_ref, zero_ref, sem, tail_sem, *, prompt_tiles):
    t = pl.program_id(0)
    last = pl.num_programs(0) - 1
    buf = t % 2

    def sort_tile(h_ref, slot_ref):
        tm = h_ref.shape[0]
        slots = slot_ref[...].astype(jnp.int32)
        slot_cols = [slots[:, k:k + 1] for k in range(TOP_K)]
        h = h_ref[...]
        for c in range(LOCAL_ROWS // tm):
            onehot = _slot_matrix(slot_cols, [1.0] * TOP_K, c)
            upc = tm // ROW_UNIT
            sorted_ref[buf, c * upc:(c + 1) * upc] = _dot_tn(onehot, h).astype(BF16).reshape(
                upc, ROW_UNIT, D_MODEL)

    @pl.when(t < prompt_tiles)
    def _():
        sort_tile(hp_ref, slp_ref)

    @pl.when(t >= prompt_tiles)
    def _():
        sort_tile(hs_ref, sls_ref)

    def start_all(tile, b):
        def body(e, c):
            n = seg_ref[tile, 0, e]

            @pl.when(n > 0)
            def _():
                pltpu.make_async_copy(sorted_ref.at[b, pl.ds(seg_ref[tile, 1, e], n)],
                                      xg_ref.at[pl.ds(seg_ref[tile, 2, e], n)], sem.at[b]).start()
            return c

        lax.fori_loop(0, N_EXPERTS, body, 0)

    def wait_all(tile, b):
        n_units = nu_ref[tile]

        @pl.when(n_units > 0)
        def _():
            pltpu.make_async_copy(sorted_ref.at[b, pl.ds(0, n_units)],
                                  xg_ref.at[pl.ds(0, n_units)], sem.at[b]).wait()

    @pl.when(t > 0)
    def _():
        wait_all(t - 1, 1 - buf)

    start_all(t, buf)

    def tail_copy(i):
        return pltpu.make_async_copy(zero_ref, xg_ref.at[tail_ref[i]], tail_sem)

    @pl.when(t == last)
    def _():
        zero_ref[...] = jnp.zeros_like(zero_ref)
        n_tail = ntail_ref[0]
        lax.fori_loop(0, n_tail, lambda i, c: (tail_copy(i).start(), c)[1], 0)
        lax.fori_loop(0, n_tail, lambda i, c: (tail_copy(i).wait(), c)[1], 0)
        wait_all(t, buf)


def _dispatch(h2_p, slots_p, h2_s, slots_s, n_units, seg, n_tail, tail_dst, rows_max):
    p_tiles = h2_p.shape[0] // TOKEN_TILE
    s_tiles = h2_s.shape[0] // TOKEN_TILE
    p_row = lambda t, *_: (jnp.minimum(t, p_tiles - 1), 0)
    s_row = lambda t, *_: (jnp.maximum(t - p_tiles, 0), 0)
    return pl.pallas_call(
        functools.partial(_dispatch_kernel, prompt_tiles=p_tiles),
        grid_spec=pltpu.PrefetchScalarGridSpec(
            num_scalar_prefetch=4,
            grid=(p_tiles + s_tiles,),
            in_specs=[pl.BlockSpec((TOKEN_TILE, D_MODEL), p_row),
                      pl.BlockSpec((TOKEN_TILE, LANES), p_row),
                      pl.BlockSpec((TOKEN_TILE, D_MODEL), s_row),
                      pl.BlockSpec((TOKEN_TILE, LANES), s_row)],
            out_specs=pl.BlockSpec(memory_space=pl.ANY),
            scratch_shapes=[pltpu.VMEM((2, LOCAL_ROWS // ROW_UNIT, ROW_UNIT, D_MODEL), BF16),
                            pltpu.VMEM((ROW_UNIT, D_MODEL), BF16),
                            pltpu.SemaphoreType.DMA((2,)), pltpu.SemaphoreType.DMA]),
        out_shape=jax.ShapeDtypeStruct((rows_max // ROW_UNIT, ROW_UNIT, D_MODEL), BF16),
        compiler_params=_params("arbitrary"),
        name="dispatch",
    )(n_units, seg, n_tail, tail_dst, h2_p, slots_p, h2_s, slots_s).reshape(rows_max, D_MODEL)


def _expert_kernel(tg_ref, ge_ref, ng_ref, nused_ref, rows_ref, x_ref, bgu_ref, bd_ref, wgu_hbm, wd_hbm,
                   y_ref, wgu_f32, wd_f32, wgu_bf, wd_bf, sem):
    i = pl.program_id(0)

    def fetch(g, b):
        e = ge_ref[g]
        return (pltpu.make_async_copy(wgu_hbm.at[e], wgu_f32.at[b], sem.at[0, b]),
                pltpu.make_async_copy(wd_hbm.at[e], wd_f32.at[b], sem.at[1, b]))

    @pl.when(i == 0)
    def _():
        for cp in fetch(0, 0):
            cp.start()

    @pl.when(i < nused_ref[0])
    def _():
        g = tg_ref[i]
        b = g % 2

        @pl.when((i == 0) | (g != tg_ref[jnp.maximum(i - 1, 0)]))
        def _():
            @pl.when(g + 1 < ng_ref[0])
            def _():
                for cp in fetch(g + 1, 1 - b):
                    cp.start()

            for cp in fetch(g, b):
                cp.wait()
            wgu_bf[...] = wgu_f32[b].astype(BF16)
            wd_bf[...] = wd_f32[b].astype(BF16)

        for part in range(EXPERT_TILE // EXPERT_PART):
            @pl.when(rows_ref[i] > part * EXPERT_PART)
            def _():
                sl = slice(part * EXPERT_PART, (part + 1) * EXPERT_PART)
                gu = _dot(x_ref[sl, :], wgu_bf[...]) + bgu_ref[...]
                gate = jnp.minimum(gu[:, :D_FF], SWIGLU_LIMIT)
                up = jnp.clip(gu[:, D_FF:], -SWIGLU_LIMIT, SWIGLU_LIMIT)
                act = (up + 1.0) * gate * jax.nn.sigmoid(SWIGLU_ALPHA * gate)
                y_ref[sl, :] = (_dot(act.astype(BF16), wd_bf[...]) + bd_ref[...]).astype(y_ref.dtype)


def _experts(xg, tile_group, group_expert, n_groups, n_used, tile_rows, wgu, bgu, wd, bd):
    rows_max = xg.shape[0]
    used = lambda i, nu: jnp.maximum(jnp.minimum(i, nu[0] - 1), 0)
    row = lambda i, tg, ge, ng, nu, tr: (used(i, nu), 0)
    exp = lambda i, tg, ge, ng, nu, tr: (ge[tg[used(i, nu)]], 0, 0)
    return pl.pallas_call(
        _expert_kernel,
        grid_spec=pltpu.PrefetchScalarGridSpec(
            num_scalar_prefetch=5,
            grid=(rows_max // EXPERT_TILE,),
            in_specs=[pl.BlockSpec((EXPERT_TILE, D_MODEL), row),
                      pl.BlockSpec((None, 1, 2 * D_FF), exp),
                      pl.BlockSpec((None, 1, D_MODEL), exp),
                      pl.BlockSpec(memory_space=pl.ANY),
                      pl.BlockSpec(memory_space=pl.ANY)],
            out_specs=pl.BlockSpec((EXPERT_TILE, D_MODEL), row),
            scratch_shapes=[pltpu.VMEM((2, D_MODEL, 2 * D_FF), F32), pltpu.VMEM((2, D_FF, D_MODEL), F32),
                            pltpu.VMEM((D_MODEL, 2 * D_FF), BF16), pltpu.VMEM((D_FF, D_MODEL), BF16),
                            pltpu.SemaphoreType.DMA((2, 2))]),
        out_shape=jax.ShapeDtypeStruct((rows_max, D_MODEL), BF16),
        compiler_params=_params("arbitrary"),
        name="experts",
    )(tile_group, group_expert, n_groups, n_used, tile_rows, xg, bgu, bd, wgu, wd)


def _combine_kernel(nu_ref, seg_ref, slot_ref, gatek_ref, x_ref, gmlp_ref, y_ref, o_ref, ys_ref, sem,
                    *, tile_offset):
    j = pl.program_id(0)
    t = j + tile_offset
    buf = j % 2
    tm = x_ref.shape[0]

    def fetch(tile, b):
        def body(e, c):
            n = seg_ref[tile, 0, e]

            @pl.when(n > 0)
            def _():
                pltpu.make_async_copy(y_ref.at[pl.ds(seg_ref[tile, 2, e], n)],
                                      ys_ref.at[b, pl.ds(seg_ref[tile, 1, e], n)], sem.at[b]).start()
            return c

        lax.fori_loop(0, N_EXPERTS, body, 0)
        n_units = nu_ref[tile]

        def zero_unit(i, c):
            ys_ref[b, i] = jnp.zeros((ROW_UNIT, D_MODEL), ys_ref.dtype)
            return c

        lax.fori_loop(n_units, LOCAL_ROWS // ROW_UNIT, zero_unit, 0)

    @pl.when(j == 0)
    def _():
        fetch(t, buf)

    @pl.when(j + 1 < pl.num_programs(0))
    def _():
        fetch(t + 1, 1 - buf)

    n_units = nu_ref[t]

    @pl.when(n_units > 0)
    def _():
        pltpu.make_async_copy(y_ref.at[pl.ds(0, n_units)], ys_ref.at[buf, pl.ds(0, n_units)],
                              sem.at[buf]).wait()

    slots = slot_ref[...].astype(jnp.int32)
    gates = gatek_ref[...]
    slot_cols = [slots[:, k:k + 1] for k in range(TOP_K)]
    gate_cols = [gates[:, k:k + 1] for k in range(TOP_K)]
    acc = jnp.zeros((tm, D_MODEL), F32)
    for c in range(LOCAL_ROWS // tm):
        upc = tm // ROW_UNIT
        rows_c = ys_ref[buf, c * upc:(c + 1) * upc].reshape(tm, D_MODEL)
        acc = acc + _dot(_slot_matrix(slot_cols, gate_cols, c), rows_c)
    o_ref[...] = x_ref[...] + _mod_rows(gmlp_ref, tm) * acc


def _combine(y, slots, gates, x1, gmlp, n_units, seg, tile_offset, per_token, tiles_per_seq):
    n = x1.shape[0]
    mod = _mod_spec(per_token, tiles_per_seq)
    wide = pl.BlockSpec((TOKEN_TILE, D_MODEL), lambda i, *_: (i, 0))
    narrow = pl.BlockSpec((TOKEN_TILE, LANES), lambda i, *_: (i, 0))
    return pl.pallas_call(
        functools.partial(_combine_kernel, tile_offset=tile_offset),
        grid_spec=pltpu.PrefetchScalarGridSpec(
            num_scalar_prefetch=2,
            grid=(n // TOKEN_TILE,),
            in_specs=[narrow, narrow, wide, mod, pl.BlockSpec(memory_space=pl.ANY)],
            out_specs=wide,
            scratch_shapes=[pltpu.VMEM((2, LOCAL_ROWS // ROW_UNIT, ROW_UNIT, D_MODEL), BF16),
                            pltpu.SemaphoreType.DMA((2,))]),
        out_shape=jax.ShapeDtypeStruct((n, D_MODEL), F32),
        compiler_params=_params("arbitrary"),
        name="combine",
    )(n_units, seg, slots, gates, x1, gmlp, y.reshape(-1, ROW_UNIT, D_MODEL))


def _route_tables(cnt, rows_max):
    units = (cnt + ROW_UNIT - 1) // ROW_UNIT
    group_units = jnp.sum(units, axis=0)
    upt = EXPERT_TILE // ROW_UNIT
    group_pad = (group_units + upt - 1) // upt * upt
    group_end = jnp.cumsum(group_pad)
    group_start = group_end - group_pad
    seg_start = group_start[None, :] + jnp.cumsum(units, axis=0) - units
    local_end = jnp.cumsum(units, axis=1)
    local_start = local_end - units
    n_units = local_end[:, -1].astype(jnp.int32)

    def pick(lo, hi, pos, value):
        return jnp.sum(jnp.where((pos >= lo) & (pos < hi), value, 0), axis=-1).astype(jnp.int32)

    seg = jnp.stack([units, local_start, seg_start], axis=1).astype(jnp.int32)

    upp = EXPERT_PART // ROW_UNIT
    n_tail_e = (group_units + upp - 1) // upp * upp - group_units
    j = jnp.arange(N_EXPERTS * upp, dtype=jnp.int32)[:, None]
    tail_end = jnp.cumsum(n_tail_e)
    tail_start = tail_end - n_tail_e
    tail_hbm = pick(tail_start[None, :], tail_end[None, :], j,
                    (group_start + group_units)[None, :] + j - tail_start[None, :])
    n_tail = tail_end[-1:].astype(jnp.int32)

    r = jnp.arange(rows_max // EXPERT_TILE, dtype=jnp.int32)[:, None] * upt
    n_used = (group_end[-1:] // upt).astype(jnp.int32)
    nonempty = group_units > 0
    group_of_expert = jnp.cumsum(nonempty) - 1
    tile_rows = pick(group_start[None, :], group_end[None, :], r,
                     jnp.clip(((group_start + group_units)[None, :] - r) * ROW_UNIT, 0, EXPERT_TILE))
    tile_group = pick(group_start[None, :], group_end[None, :], r, group_of_expert[None, :])
    g = jnp.arange(N_EXPERTS, dtype=jnp.int32)
    group_expert = jnp.sum(jnp.where(nonempty[None, :] & (group_of_expert[None, :] == g[:, None]),
                                     g[None, :], 0), axis=-1).astype(jnp.int32)
    n_groups = jnp.sum(nonempty)[None].astype(jnp.int32)
    return (n_units, seg, n_tail, tail_hbm,
            tile_group, group_expert, n_groups, n_used, tile_rows)


def _rope_tables(pos):
    half = HEAD_DIM // 2
    lane = np.arange(LANES)
    inv = jnp.asarray(ROPE_THETA, F32) ** (-jnp.asarray(lane % half, F32) / half)
    sign = jnp.asarray(np.where(lane % HEAD_DIM < half, -1.0, 1.0), F32)
    ang = pos.astype(F32)[:, None] * inv[None, :]
    return jnp.cos(ang), jnp.sin(ang) * sign[None, :]


def kernel(x_prompt, x_sample, c_prompt, c_sample, state_gla, cache_swa_k, cache_swa_v, w_ada, b_ada,
           norm1_g, norm2_g, w_in, w_gk2, b_gk, gla_norm_g, q_norm_g, k_norm_g, attn_sinks, w_o,
           w_router, b_router, w_gate_up, b_gate_up, w_down, b_down):
    batch, seq, d = x_prompt.shape
    n_seq, n_real, _ = x_sample.shape
    depth = w_in.shape[0]
    assert depth == 1 and d == D_MODEL and n_real <= SAMPLE_ROWS
    assert seq % TOKEN_TILE == 0 and (n_seq * SAMPLE_ROWS) % TOKEN_TILE == 0 and n_seq % SAMPLE_SEQS == 0
    R = SAMPLE_ROWS
    kvw = SWA_KV_HEADS * HEAD_DIM
    tiles_per_seq = seq // TOKEN_TILE

    w_main, w_glr = _pack_w_in(w_in[0])
    wgk = jnp.pad(w_gk2[0], ((0, LANES - GATE_RANK), (0, 0)))
    wgk_hi = wgk.astype(BF16)
    wgk = jnp.stack([wgk_hi, (wgk - wgk_hi.astype(F32)).astype(BF16)])
    bgk = b_gk[0].reshape(1, -1)
    gnorm = gla_norm_g[0].reshape(1, -1)
    gq = jnp.tile(q_norm_g[0], LANES // HEAD_DIM).reshape(1, LANES)
    gk = jnp.tile(k_norm_g[0], LANES // HEAD_DIM).reshape(1, LANES)
    seg = jnp.asarray(np.kron(np.eye(LANES // HEAD_DIM), np.ones((HEAD_DIM, HEAD_DIM))), BF16)
    sinks = attn_sinks[0]
    wo = w_o[0].astype(BF16)
    wr = jnp.pad(w_router[0], ((0, 0), (0, LANES - N_EXPERTS)))
    wr_hi = wr.astype(BF16)
    wr = jnp.stack([wr_hi, (wr - wr_hi.astype(F32)).astype(BF16)])
    br = jnp.pad(b_router[0], (0, LANES - N_EXPERTS), constant_values=-1e30).reshape(1, LANES)
    bgu = b_gate_up[0].reshape(N_EXPERTS, 1, 2 * D_FF)
    bd = b_down[0].reshape(N_EXPERTS, 1, D_MODEL)
    g1 = norm1_g[0].reshape(1, -1)
    g2 = norm2_g[0].reshape(1, -1)

    n_c = batch + n_seq
    c_all = jnp.pad(jnp.concatenate([c_prompt, c_sample], axis=0), ((0, -n_c % 8), (0, 0)))
    m_all = _ada(c_all, w_ada[0], b_ada[0])
    mp = [m_all[:batch, i * d:(i + 1) * d].reshape(batch, 1, d) for i in range(6)]
    ms = [m_all[batch:n_c, i * d:(i + 1) * d] for i in range(6)]

    xp = x_prompt.reshape(batch * seq, d)
    xs = jnp.pad(x_sample, ((0, 0), (0, R - n_real), (0, 0))).reshape(n_seq * R, d)
    cos_p, sin_p = _rope_tables(jnp.arange(seq))
    cos_s, sin_s = _rope_tables(PAST_LEN + jnp.tile(jnp.arange(R), n_seq))
    sink_rows = jnp.repeat(sinks, R).reshape(SWA_HEADS * R, 1)

    yp, glr_p = _inproj(xp, mp[0], mp[1], g1, w_main, w_glr, cos_p, sin_p, gq, gk, seg,
                        BF16, False, tiles_per_seq)
    ys, glr_s = _inproj(xs, ms[0], ms[1], g1, w_main, w_glr, cos_s, sin_s, gq, gk, seg, F32, True, 1)
    og_p, st_p = _gla_prompt(yp, glr_p, wgk, bgk, gnorm, batch, seq)
    og_s, st_s = _gla_sample(ys, glr_s, wgk, bgk, gnorm, state_gla[0], n_seq, n_real)
    os_p, kc_p, vc_p = _swa_prompt(yp, sinks, batch, seq)
    os_s, kc_s, vc_s = _swa_sample(ys, sink_rows, cache_swa_k[0].reshape(n_seq, WINDOW, kvw),
                                   cache_swa_v[0].reshape(n_seq, WINDOW, kvw), n_seq, n_real)
    x1_p, h2_p, sl_p, gt_p, cnt_p = _post(yp, og_p, os_p, xp, mp[2], mp[3], mp[4], g2, wo, wr, br,
                                          False, tiles_per_seq, R)
    x1_s, h2_s, sl_s, gt_s, cnt_s = _post(ys, og_s, os_s, xs, ms[2], ms[3], ms[4], g2, wo, wr, br,
                                          True, 1, n_real)

    cnt = jnp.concatenate([cnt_p, cnt_s], axis=0)[:, 0, :N_EXPERTS].astype(jnp.int32)
    n_tiles = cnt.shape[0]
    rows_bound = (TOP_K * (batch * seq + n_seq * n_real) + n_tiles * N_EXPERTS * (ROW_UNIT - 1)
                  + N_EXPERTS * (EXPERT_TILE - 1))
    rows_max = -(-rows_bound // EXPERT_TILE) * EXPERT_TILE
    (n_units, seg, n_tail, tail_hbm,
     tile_group, group_expert, n_groups, n_used, tile_rows) = _route_tables(cnt, rows_max)
    xg = _dispatch(h2_p, sl_p, h2_s, sl_s, n_units, seg, n_tail, tail_hbm, rows_max)
    yg = _experts(xg, tile_group, group_expert, n_groups, n_used, tile_rows,
                  w_gate_up[0], bgu, w_down[0], bd)
    p_tiles = batch * seq // TOKEN_TILE
    out_p = _combine(yg, sl_p, gt_p, x1_p, mp[5], n_units, seg, 0, False, tiles_per_seq)
    out_s = _combine(yg, sl_s, gt_s, x1_s, ms[5], n_units, seg, p_tiles, True, 1)

    cache_shape = (WINDOW, SWA_KV_HEADS, HEAD_DIM)
    return (out_p.reshape(batch, seq, d),
            out_s.reshape(n_seq, R, d)[:, :n_real],
            st_p[None],
            kc_p.reshape(1, batch, *cache_shape),
            vc_p.reshape(1, batch, *cache_shape),
            st_s[None],
            kc_s.reshape(1, n_seq, *cache_shape),
            vc_s.reshape(1, n_seq, *cache_shape))
```
